```python
import math
import jax
import jax.numpy as jnp
from jax import lax
import numpy as np

D_MODEL = 1024
BATCH = 16
SEQ = 2048
DEPTH = 2

GRID_W = 64
CTX_LEN = 256

DA_HEADS = 4
DA_QK_DIM = 64
DA_V_DIM = 2 * DA_QK_DIM
DA_WIDTH = DA_HEADS * DA_V_DIM
DA_QK_COLS = DA_HEADS * 2 * DA_QK_DIM
Q_BLOCK = 128
ROPE_BASE = 10000.0
DA_EPS = 1e-5

RW_HEAD = 64
RW_HEADS = 8
RW_WIDTH = RW_HEADS * RW_HEAD
RW_DECAY_LORA = 64
RW_AAA_LORA = 64
RW_GATE_LORA = 128
RW_COLS = 3 * RW_WIDTH + RW_DECAY_LORA + RW_AAA_LORA + RW_GATE_LORA
RW_SPLITS = (RW_WIDTH, 2 * RW_WIDTH, 3 * RW_WIDTH, 3 * RW_WIDTH + RW_DECAY_LORA,
             3 * RW_WIDTH + RW_DECAY_LORA + RW_AAA_LORA)
RW_GN_EPS = 64e-5

SG_CHUNK = 128
SG_GROUPS = 4
SG_WIDTH = 512
SG_GROUP_DIM = SG_WIDTH // SG_GROUPS

N_BRANCH = 3
MIX_WIDTH = DA_WIDTH + RW_WIDTH + SG_WIDTH
DA_Q0 = 0
DA_K0 = DA_Q0 + DA_QK_COLS
DA_V0 = DA_K0 + DA_QK_COLS
RW_0 = DA_V0 + DA_WIDTH
SG_0 = RW_0 + RW_COLS
GATE_0 = SG_0 + 2 * SG_WIDTH
IN_COLS = GATE_0 + N_BRANCH * D_MODEL

N_EXPERTS = 16
EXPERT_FF = 2048
EC_CAPACITY = 2

LN_EPS = 1e-5
DN_ALPHA = (2 * DEPTH) ** 0.25
DN_BETA = (8 * DEPTH) ** -0.25

kernel_name = 'hybrid_diffattn_rwkv7_chunkgmlp_ecmoe_trunk'


def layer_norm(z, g, b, eps=LN_EPS):
    zf = z.astype(jnp.float32)
    mu = jnp.mean(zf, -1, keepdims=True)
    var = jnp.mean(jnp.square(zf - mu), -1, keepdims=True)
    return ((zf - mu) * lax.rsqrt(var + eps) * g + b).astype(z.dtype)


def rms_norm(z, g, eps):
    zf = z.astype(jnp.float32)
    return (zf * lax.rsqrt(jnp.mean(jnp.square(zf), -1, keepdims=True) + eps) * g).astype(z.dtype)


def axial_rope_tables(n_tokens):
    rows = n_tokens // GRID_W
    row = jnp.repeat(jnp.arange(rows, dtype=jnp.float32), GRID_W)
    col = jnp.tile(jnp.arange(GRID_W, dtype=jnp.float32), rows)
    half = DA_QK_DIM // 2
    inv_freq = ROPE_BASE ** (-jnp.arange(0, half, 2, dtype=jnp.float32) / half)
    ar = row[:, None] * inv_freq
    ac = col[:, None] * inv_freq
    ang = jnp.concatenate([ar, ar, ac, ac], axis=-1)
    return jnp.cos(ang), jnp.sin(ang)


def rotate_half(z):
    z1, z2 = jnp.split(z, 2, axis=-1)
    return jnp.concatenate([-z2, z1], axis=-1)


def apply_axial_rope(z, cos, sin):
    zr, zc = jnp.split(z, 2, axis=-1)
    rot = jnp.concatenate([rotate_half(zr), rotate_half(zc)], axis=-1)
    cb = cos[None, :, None, None, :]
    sb = sin[None, :, None, None, :]
    return (z * cb + rot * sb).astype(z.dtype)


def diff_lambda(lam, lam_init):
    lf = lam.astype(jnp.float32)
    return jnp.exp(jnp.sum(lf[0] * lf[1])) - jnp.exp(jnp.sum(lf[2] * lf[3])) + lam_init


def diff_attn_probs(q, k, lam):
    s = jnp.einsum('bqhmd,bkhmd->bhmqk', q, k).astype(jnp.float32)
    p = jax.nn.softmax(s, axis=-1)
    return p[:, :, 0] - lam * p[:, :, 1]


def diff_attn_finish(o, norm_g, lam_init):
    B, T = o.shape[:2]
    return (rms_norm(o, norm_g, DA_EPS) * (1.0 - lam_init)).reshape(B, T, DA_WIDTH)


def diff_attention_latent(q, k, v, kc, vc, lam, norm_g, lam_init):
    B, T = q.shape[:2]
    k_all = jnp.concatenate([kc, k], axis=1)
    v_all = jnp.concatenate([vc, v], axis=1)
    nb = T // Q_BLOCK
    qb = q.reshape(B, nb, Q_BLOCK, DA_HEADS, 2, DA_QK_DIM).transpose(1, 0, 2, 3, 4, 5)

    def block(q_blk):
        a = diff_attn_probs(q_blk, k_all, lam)
        return jnp.einsum('bhqk,bkhe->bqhe', a.astype(v_all.dtype), v_all)

    o = lax.map(block, qb)
    o = o.transpose(1, 0, 2, 3, 4).reshape(B, T, DA_HEADS, DA_V_DIM)
    return diff_attn_finish(o, norm_g, lam_init)


def diff_attention_context(qc, kc, vc, lam, norm_g, lam_init):
    a = diff_attn_probs(qc, kc, lam)
    o = jnp.einsum('bhqk,bkhe->bqhe', a.astype(vc.dtype), vc)
    return diff_attn_finish(o, norm_g, lam_init)


def centred_token_shift(p, mu):
    prev = jnp.pad(p[:, :-1], ((0, 0), (1, 0), (0, 0)))
    nxt = jnp.pad(p[:, 1:], ((0, 0), (0, 1), (0, 0)))
    return p + mu[0] * (prev - p) + mu[1] * (nxt - p)


def heads_l2_normalize(z):
    B, T, C = z.shape
    zh = z.reshape(B, T, RW_HEADS, RW_HEAD).astype(jnp.float32)
    n = jnp.sqrt(jnp.sum(jnp.square(zh), -1, keepdims=True))
    return (zh / jnp.maximum(n, 1e-12)).reshape(B, T, C)


def rwkv7_features(p, mu, w0, w2, a0, a2, k_k, k_a):
    p = centred_token_shift(p, mu)
    r, k, v, xw, xa, xg = jnp.split(p, RW_SPLITS, axis=-1)
    w_pre = w0[:, None, None, :] + jnp.einsum('btr,zrc->zbtc', jnp.tanh(xw), w2)
    w_log = -jax.nn.softplus(-w_pre.astype(jnp.float32)) - 0.5
    decay = jnp.exp(-jnp.exp(w_log))
    a = jax.nn.sigmoid((a0[:, None, None, :] + jnp.einsum('btr,zrc->zbtc', xa, a2)).astype(jnp.float32))
    kk = heads_l2_normalize(k * k_k)
    k_dir = k[None].astype(jnp.float32) * (1.0 + (a - 1.0) * k_a)
    return r, k_dir, v, xg, decay, a, kk


def to_heads_tm(z):
    B, T, _ = z.shape
    return z.reshape(B, T, RW_HEADS, RW_HEAD).transpose(1, 0, 2, 3).astype(jnp.float32)


def from_heads_tm(z):
    T, B = z.shape[:2]
    return z.transpose(1, 0, 2, 3).reshape(B, T, RW_WIDTH)


def wkv7_scan(s0, xs):
    def step(S, inp):
        w_t, k_t, v_t, kk_t, a_t = inp[:5]
        sa = jnp.einsum('bhij,bhj->bhi', S, -kk_t)
        S = (S * w_t[:, :, None, :] + sa[..., None] * (kk_t * a_t)[:, :, None, :]
             + v_t[..., None] * k_t[:, :, None, :])
        y = jnp.einsum('bhij,bhj->bhi', S, inp[5]) if len(inp) == 6 else None
        return S, y
    return lax.scan(step, s0, xs)


def rwkv7_direction(s0, feats, z, reverse, emit):
    r, k_dir, v, xg, decay, a, kk = feats
    xs = [decay[z], k_dir[z], v, kk, a[z]] + ([r] if emit else [])
    xs = [to_heads_tm(t) for t in xs]
    if reverse:
        xs = [jnp.flip(t, axis=0) for t in xs]
    s, ys = wkv7_scan(s0, tuple(xs))
    if not emit:
        return s, None
    if reverse:
        ys = jnp.flip(ys, axis=0)
    return s, from_heads_tm(ys)


def rwkv7_output(y, feats, ln_g, ln_b, r_k, g2, dtype):
    r, k_dir, v, xg, decay, a, kk = feats
    B, T, C = y.shape
    yh = y.reshape(B, T, RW_HEADS, RW_HEAD)
    mu = jnp.mean(yh, -1, keepdims=True)
    var = jnp.mean(jnp.square(yh - mu), -1, keepdims=True)
    gn = ((yh - mu) * lax.rsqrt(var + RW_GN_EPS)).reshape(B, T, C) * ln_g + ln_b
    k_b = 0.5 * (k_dir[0] + k_dir[1])
    bonus = (jnp.sum((r * k_b).reshape(B, T, RW_HEADS, RW_HEAD) * r_k, -1, keepdims=True)
             * v.reshape(B, T, RW_HEADS, RW_HEAD))
    g = jax.nn.sigmoid(xg) @ g2
    return ((gn + bonus.reshape(B, T, C)) * g).astype(dtype)


def rwkv7_mixer(p_lat, p_ctx, mu, w0, w2, a0, a2, k_k, k_a, r_k, ln_g, ln_b, g2, emit_ctx):
    B = p_lat.shape[0]
    f_lat = rwkv7_features(p_lat, mu, w0, w2, a0, a2, k_k, k_a)
    f_ctx = rwkv7_features(p_ctx, mu, w0, w2, a0, a2, k_k, k_a)
    s_zero = jnp.zeros((B, RW_HEADS, RW_HEAD, RW_HEAD), jnp.float32)
    y_lat, y_ctx = [], []
    for z, reverse in ((0, False), (1, True)):
        s_ctx, yc = rwkv7_direction(s_zero, f_ctx, z, reverse, emit_ctx)
        _, yl = rwkv7_direction(s_ctx, f_lat, z, reverse, True)
        y_lat.append(yl)
        y_ctx.append(yc)
    out_lat = rwkv7_output(y_lat[0] + y_lat[1], f_lat, ln_g, ln_b, r_k, g2, p_lat.dtype)
    out_ctx = rwkv7_output(y_ctx[0] + y_ctx[1], f_ctx, ln_g, ln_b, r_k, g2, p_ctx.dtype) if emit_ctx else None
    return out_lat, out_ctx


def spatial_gating(p, norm_g, norm_b, w_s, b_s):
    B, T, _ = p.shape
    u, v = jnp.split(jax.nn.gelu(p), 2, axis=-1)
    v = layer_norm(v, norm_g, norm_b)
    vc = v.reshape(B, T // SG_CHUNK, SG_CHUNK, SG_GROUPS, SG_GROUP_DIM)
    vm = jnp.einsum('gpq,bnqgc->bnpgc', w_s, vc) + b_s.T[None, None, :, :, None]
    return u * vm.reshape(B, T, SG_WIDTH)


def gated_merge(y_da, y_rw, y_sg, gate_logits, w_branch, w_out):
    g_da, g_rw, g_sg = jnp.split(jax.nn.sigmoid(gate_logits), N_BRANCH, axis=-1)
    wb_da, wb_rw, wb_sg = jnp.split(w_branch, (DA_WIDTH, DA_WIDTH + RW_WIDTH), axis=0)
    m = g_da * (y_da @ wb_da) + g_rw * (y_rw @ wb_rw) + g_sg * (y_sg @ wb_sg)
    return m @ w_out


def expert_choice_ffn(h, w_router, w_gate, w_up, w_down):
    B, T, D = h.shape
    cap = EC_CAPACITY * T // N_EXPERTS
    aff = jax.nn.softmax(jnp.einsum('btd,de->bte', h, w_router).astype(jnp.float32), axis=-1)
    gate, idx = lax.top_k(jnp.swapaxes(aff, 1, 2), cap)
    xe = jax.vmap(lambda hb, ib: hb[ib])(h, idx)
    hid = jax.nn.silu(jnp.einsum('becd,edf->becf', xe, w_gate)) * jnp.einsum('becd,edf->becf', xe, w_up)
    ye = jnp.einsum('becf,efd->becd', hid, w_down) * gate[..., None].astype(h.dtype)

    def scatter(ib, yb):
        return jnp.zeros((T, D), yb.dtype).at[ib.reshape(-1)].add(yb.reshape(-1, D))

    return jax.vmap(scatter)(idx, ye)


def setup_inputs(seed: int = 0) -> dict:
    key = jax.random.key(seed)
    ks = iter(jax.random.split(key, 40))
    L, D = DEPTH, D_MODEL

    def nrm(shape, scale):
        return jax.random.normal(next(ks), shape, jnp.float32) * scale

    def uni(shape, lo, hi):
        return jax.random.uniform(next(ks), shape, jnp.float32, lo, hi)

    return {
        'x': nrm((BATCH, SEQ, D), 1.0),
        'c': nrm((BATCH, D), 1.0),
        'ctx': nrm((BATCH, CTX_LEN, D), 1.0),
        'c_ctx': nrm((D,), 1.0),
        'w_mod': nrm((L, D, 6 * D), 0.5 * D ** -0.5),
        'b_mod': nrm((L, 6 * D), 0.02),
        'w_in': nrm((L, D, IN_COLS), D ** -0.5),
        'da_lambda': nrm((L, 4, DA_QK_DIM), 0.1),
        'da_norm_g': 1.0 + nrm((L, DA_V_DIM), 0.05),
        'rw_shift_mu': uni((L, 2, RW_COLS), 0.0, 0.5),
        'rw_w0': uni((L, 2, RW_WIDTH), -6.5, -1.5),
        'rw_w2': nrm((L, 2, RW_DECAY_LORA, RW_WIDTH), 0.1 * RW_DECAY_LORA ** -0.5),
        'rw_a0': nrm((L, 2, RW_WIDTH), 0.1),
        'rw_a2': nrm((L, 2, RW_AAA_LORA, RW_WIDTH), 0.5 * RW_AAA_LORA ** -0.5),
        'rw_k_k': 0.85 + nrm((L, RW_WIDTH), 0.02),
        'rw_k_a': 1.0 + nrm((L, RW_WIDTH), 0.02),
        'rw_r_k': nrm((L, RW_HEADS, RW_HEAD), 0.1),
        'rw_ln_g': 1.0 + nrm((L, RW_WIDTH), 0.05),
        'rw_ln_b': nrm((L, RW_WIDTH), 0.02),
        'rw_g2': nrm((L, RW_GATE_LORA, RW_WIDTH), RW_GATE_LORA ** -0.5),
        'sg_norm_g': 1.0 + nrm((L, SG_WIDTH), 0.05),
        'sg_norm_b': nrm((L, SG_WIDTH), 0.02),
        'sg_w': nrm((L, SG_GROUPS, SG_CHUNK, SG_CHUNK), 0.5 * SG_CHUNK ** -0.5),
        'sg_b': 1.0 + nrm((L, SG_GROUPS, SG_CHUNK), 0.1),
        'w_branch': nrm((L, MIX_WIDTH, D), DA_WIDTH ** -0.5),
        'w_out': nrm((L, D, D), DN_BETA * D ** -0.5),
        'ln1_g': 1.0 + nrm((L, D), 0.05),
        'ln1_b': nrm((L, D), 0.02),
        'w_router': nrm((L, D, N_EXPERTS), D ** -0.5),
        'w_e_gate': nrm((L, N_EXPERTS, D, EXPERT_FF), D ** -0.5),
        'w_e_up': nrm((L, N_EXPERTS, D, EXPERT_FF), D ** -0.5),
        'w_e_down': nrm((L, N_EXPERTS, EXPERT_FF, D), DN_BETA * EXPERT_FF ** -0.5),
        'ln2_g': 1.0 + nrm((L, D), 0.05),
        'ln2_b': nrm((L, D), 0.02),
    }


def reference(x, c, ctx, c_ctx, w_mod, b_mod, w_in, da_lambda, da_norm_g,
              rw_shift_mu, rw_w0, rw_w2, rw_a0, rw_a2, rw_k_k, rw_k_a, rw_r_k,
              rw_ln_g, rw_ln_b, rw_g2, sg_norm_g, sg_norm_b, sg_w, sg_b,
              w_branch, w_out, ln1_g, ln1_b, w_router, w_e_gate, w_e_up, w_e_down,
              ln2_g, ln2_b):
    B, T, _ = x.shape
    cos, sin = axial_rope_tables(T)
    xc = ctx
    for l in range(DEPTH):
        last = l == DEPTH - 1
        Tc = xc.shape[1]
        lam_init = 0.8 - 0.6 * math.exp(-0.3 * l)
        lam = diff_lambda(da_lambda[l], lam_init)
        w_in_l = w_in[l]

        mod = (jax.nn.silu(c) @ w_mod[l] + b_mod[l])[:, None, :]
        sh1, sc1, g1, sh2, sc2, g2 = jnp.split(mod, 6, axis=-1)
        modc = jax.nn.silu(c_ctx) @ w_mod[l] + b_mod[l]
        sh1c, sc1c, g1c, sh2c, sc2c, g2c = jnp.split(modc, 6)

        h = x * (1.0 + sc1) + sh1
        hc = xc * (1.0 + sc1c) + sh1c
        p = h @ w_in_l

        q = apply_axial_rope(p[..., DA_Q0:DA_K0].reshape(B, T, DA_HEADS, 2, DA_QK_DIM) * DA_QK_DIM ** -0.5, cos, sin)
        k = apply_axial_rope(p[..., DA_K0:DA_V0].reshape(B, T, DA_HEADS, 2, DA_QK_DIM), cos, sin)
        v = p[..., DA_V0:RW_0].reshape(B, T, DA_HEADS, DA_V_DIM)
        pc_kv = hc @ w_in_l[:, DA_K0:RW_0]
        kc = pc_kv[..., :DA_QK_COLS].reshape(B, Tc, DA_HEADS, 2, DA_QK_DIM)
        vc = pc_kv[..., DA_QK_COLS:].reshape(B, Tc, DA_HEADS, DA_V_DIM)
        y_da = diff_attention_latent(q, k, v, kc, vc, lam, da_norm_g[l], lam_init)

        pc_rw = hc @ w_in_l[:, RW_0:SG_0]
        y_rw, y_rw_c = rwkv7_mixer(p[..., RW_0:SG_0], pc_rw, rw_shift_mu[l], rw_w0[l], rw_w2[l],
                                   rw_a0[l], rw_a2[l], rw_k_k[l], rw_k_a[l], rw_r_k[l],
                                   rw_ln_g[l], rw_ln_b[l], rw_g2[l], not last)

        y_sg = spatial_gating(p[..., SG_0:GATE_0], sg_norm_g[l], sg_norm_b[l], sg_w[l], sg_b[l])

        mix = gated_merge(y_da, y_rw, y_sg, p[..., GATE_0:], w_branch[l], w_out[l])
        x_mid = layer_norm(DN_ALPHA * x + g1 * mix, ln1_g[l], ln1_b[l])
        f = expert_choice_ffn(x_mid * (1.0 + sc2) + sh2, w_router[l], w_e_gate[l], w_e_up[l], w_e_down[l])
        x_new = layer_norm(DN_ALPHA * x_mid + g2 * f, ln2_g[l], ln2_b[l])

        if not last:
            qc = (hc @ w_in_l[:, DA_Q0:DA_K0]).reshape(B, Tc, DA_HEADS, 2, DA_QK_DIM) * DA_QK_DIM ** -0.5
            y_da_c = diff_attention_context(qc, kc, vc, lam, da_norm_g[l], lam_init)
            pc_rest = hc @ w_in_l[:, SG_0:]
            y_sg_c = spatial_gating(pc_rest[..., :2 * SG_WIDTH], sg_norm_g[l], sg_norm_b[l], sg_w[l], sg_b[l])
            mix_c = gated_merge(y_da_c, y_rw_c, y_sg_c, pc_rest[..., 2 * SG_WIDTH:], w_branch[l], w_out[l])
            xc_mid = layer_norm(DN_ALPHA * xc + g1c * mix_c, ln1_g[l], ln1_b[l])
            fc = expert_choice_ffn(xc_mid * (1.0 + sc2c) + sh2c, w_router[l], w_e_gate[l], w_e_up[l], w_e_down[l])
            xc = layer_norm(DN_ALPHA * xc_mid + g2c * fc, ln2_g[l], ln2_b[l])
        x = x_new
    return x
```

```python
import functools
import math

import jax
import jax.numpy as jnp
from jax import lax
from jax.experimental import pallas as pl
from jax.experimental.pallas import tpu as pltpu

F32 = jnp.float32
BF16 = jnp.bfloat16
HIGHEST = lax.Precision.HIGHEST

D_MODEL = 1024
GRID_W = 64
DA_HEADS = 4
DA_QK_DIM = 64
DA_V_DIM = 128
DA_WIDTH = 512
DA_QK_COLS = 512
ROPE_BASE = 10000.0
DA_EPS = 1e-5
RW_HEAD = 64
RW_HEADS = 8
RW_WIDTH = 512
RW_COLS = 1792
RW_GN_EPS = 64e-5
SG_CHUNK = 128
SG_GROUPS = 4
SG_WIDTH = 512
MIX_WIDTH = 1536
DA_K0 = 512
DA_V0 = 1024
RW_0 = 1536
SG_0 = RW_0 + RW_COLS
GATE_0 = SG_0 + 2 * SG_WIDTH
IN_COLS = GATE_0 + 3 * D_MODEL
N_EXPERTS = 16
EC_CAPACITY = 2
LN_EPS = 1e-5

TOKEN_TILE = 256
SCAN_CHUNK = 64
VMEM_LIMIT_BYTES = 58 * 1024 * 1024


def _dot(a, b):
    return jnp.dot(a, b, preferred_element_type=F32)


def _dot_hi(a, b):
    return jnp.dot(a, b, preferred_element_type=F32, precision=HIGHEST)


def _dot_nt(a, b, precision=None):
    return lax.dot_general(a, b, (((1,), (1,)), ((), ())), preferred_element_type=F32,
                           precision=precision)


def _dot_tn(a, b, precision=None):
    return lax.dot_general(a, b, (((0,), (0,)), ((), ())), preferred_element_type=F32,
                           precision=precision)


def _sigmoid(z):
    return 1.0 / (1.0 + jnp.exp(-z))


def _seg_sum(z, ones_bd):
    hi = z.astype(BF16)
    lo = (z - hi.astype(F32)).astype(BF16)
    return _dot(hi, ones_bd) + _dot(lo, ones_bd)


def _mod_kernel(c_ref, w_ref, b_ref, o_ref):
    cc = c_ref[...]
    o_ref[...] = _dot_hi(cc * _sigmoid(cc), w_ref[...]) + b_ref[...]


def _modulation(cc, w_mod, b_mod):
    rows, d = cc.shape
    n = w_mod.shape[1]
    tn = 1024
    return pl.pallas_call(
        _mod_kernel,
        grid=(n // tn,),
        in_specs=[pl.BlockSpec((rows, d), lambda j: (0, 0)),
                  pl.BlockSpec((d, tn), lambda j: (0, j)),
                  pl.BlockSpec((1, tn), lambda j: (0, j))],
        out_specs=pl.BlockSpec((rows, tn), lambda j: (0, j)),
        out_shape=jax.ShapeDtypeStruct((rows, n), F32),
        name="adaln_mod",
    )(cc, w_mod, b_mod.reshape(1, n))


def _inproj_kernel(x_ref, sh_ref, sc_ref, w_ref, cos_ref, sin_ref, sgg_ref, sgb_ref, sgw_ref,
                   sgbias_ref, q_ref, k_ref, v_ref, rw_ref, sg_ref, gate_ref):
    tm = x_ref.shape[1]
    h = (x_ref[0] * (1.0 + sc_ref[0, 0]) + sh_ref[0, 0]).astype(BF16)

    def proj(c0, c1):
        return _dot(h, w_ref[:, c0:c1])

    cos = cos_ref[...]
    sin = sin_ref[...]
    lane = lax.broadcasted_iota(jnp.int32, (tm, DA_QK_COLS), 1)
    first = (lane % 32) < 16

    def rope(z):
        zr = jnp.where(first, pltpu.roll(z, DA_QK_COLS - 16, 1), pltpu.roll(z, 16, 1))
        return z * cos + zr * sin

    q_ref[0] = (rope(proj(0, DA_K0)) * DA_QK_DIM ** -0.5).astype(BF16)
    k_ref[0] = rope(proj(DA_K0, DA_V0)).astype(BF16)
    v_ref[0] = proj(DA_V0, RW_0).astype(BF16)
    for c0 in range(0, RW_COLS, 896):
        rw_ref[0, :, c0:c0 + 896] = proj(RW_0 + c0, RW_0 + c0 + 896)
    for j in range(3):
        gate_ref[0, :, j * D_MODEL:(j + 1) * D_MODEL] = _sigmoid(
            proj(GATE_0 + j * D_MODEL, GATE_0 + (j + 1) * D_MODEL)).astype(BF16)

    ps = proj(SG_0, GATE_0)
    gl = ps * (0.5 * (1.0 + jnp.tanh(math.sqrt(2.0 / math.pi) * (ps + 0.044715 * (ps * ps * ps)))))
    u = gl[:, :SG_WIDTH]
    vv = gl[:, SG_WIDTH:]
    mu = jnp.mean(vv, axis=-1, keepdims=True)
    dv = vv - mu
    var = jnp.mean(dv * dv, axis=-1, keepdims=True)
    vn = (dv * lax.rsqrt(var + LN_EPS) * sgg_ref[...] + sgb_ref[...]).astype(BF16)
    gd = SG_WIDTH // SG_GROUPS
    for n in range(tm // SG_CHUNK):
        r0 = n * SG_CHUNK
        for g in range(SG_GROUPS):
            c0 = g * gd
            vm = _dot(sgw_ref[g], vn[r0:r0 + SG_CHUNK, c0:c0 + gd]) + sgbias_ref[:, c0:c0 + gd]
            sg_ref[0, r0:r0 + SG_CHUNK, c0:c0 + gd] = (u[r0:r0 + SG_CHUNK, c0:c0 + gd] * vm).astype(BF16)


def _input_projection(x_all, modall, w_in, cos, sin, sgg, sgb, sgw, sgbias, ntl):
    B, TT, D = x_all.shape
    tm = TOKEN_TILE
    nt = TT // tm
    tok = lambda w: pl.BlockSpec((1, tm, w), lambda b, i: (b, i, 0))
    modspec = lambda j: pl.BlockSpec((1, 1, 1, D), lambda b, i: (b, i // ntl, 0, j))
    const2 = lambda a: pl.BlockSpec(a.shape, lambda b, i: (0, 0))
    return pl.pallas_call(
        _inproj_kernel,
        grid=(B, nt),
        in_specs=[tok(D), modspec(0), modspec(1),
                  pl.BlockSpec(w_in.shape, lambda b, i: (0, 0), pipeline_mode=pl.Buffered(1)),
                  pl.BlockSpec((tm, DA_QK_COLS), lambda b, i: (i, 0)),
                  pl.BlockSpec((tm, DA_QK_COLS), lambda b, i: (i, 0)),
                  const2(sgg), const2(sgb),
                  pl.BlockSpec(sgw.shape, lambda b, i: (0, 0, 0)),
                  const2(sgbias)],
        out_specs=[tok(DA_QK_COLS), tok(DA_QK_COLS), tok(DA_WIDTH), tok(RW_COLS), tok(SG_WIDTH),
                   tok(3 * D)],
        out_shape=[jax.ShapeDtypeStruct((B, TT, DA_QK_COLS), BF16),
                   jax.ShapeDtypeStruct((B, TT, DA_QK_COLS), BF16),
                   jax.ShapeDtypeStruct((B, TT, DA_WIDTH), BF16),
                   jax.ShapeDtypeStruct((B, TT, RW_COLS), F32),
                   jax.ShapeDtypeStruct((B, TT, SG_WIDTH), BF16),
                   jax.ShapeDtypeStruct((B, TT, 3 * D), BF16)],
        compiler_params=pltpu.CompilerParams(
            dimension_semantics=("parallel", "parallel"), vmem_limit_bytes=VMEM_LIMIT_BYTES),
        name="in_proj",
    )(x_all, modall, modall, w_in, cos, sin, sgg, sgb, sgw, sgbias)


def _attn_kernel(q_ref, k_ref, v_ref, lam_ref, g_ref, o_ref, *, ntl, t_lat, lam_init):
    i = pl.program_id(2)
    lp = lam_ref[...]
    lam = (jnp.exp(jnp.sum(lp[0:1] * lp[1:2], axis=-1, keepdims=True))
           - jnp.exp(jnp.sum(lp[2:3] * lp[3:4], axis=-1, keepdims=True)) + lam_init)
    q = q_ref[0]
    lane = lax.broadcasted_iota(jnp.int32, q.shape, 1)
    zero = jnp.zeros_like(q)
    q0 = jnp.where(lane < DA_QK_DIM, q, zero)
    q1 = jnp.where(lane >= DA_QK_DIM, q, zero)

    def attend(k, v):
        def one(qm):
            s = _dot_nt(qm, k)
            e = jnp.exp(s - jnp.max(s, axis=-1, keepdims=True))
            return _dot(e.astype(BF16), v) / jnp.sum(e, axis=-1, keepdims=True)
        o = one(q0) - lam * one(q1)
        o = o * lax.rsqrt(jnp.mean(o * o, axis=-1, keepdims=True) + DA_EPS) * g_ref[...]
        o_ref[0] = (o * (1.0 - lam_init)).astype(BF16)

    @pl.when(i < ntl)
    def _():
        attend(k_ref[0], v_ref[0])

    @pl.when(i >= ntl)
    def _():
        attend(k_ref[0, t_lat:, :], v_ref[0, t_lat:, :])


def _diff_attention(q, k, v, lam_p, norm_g, ntl, lam_init):
    B, TT, _ = q.shape
    tm = TOKEN_TILE
    nt = TT // tm
    kv = pl.BlockSpec((1, TT, DA_V_DIM), lambda b, h, i: (b, 0, h))
    qo = pl.BlockSpec((1, tm, DA_V_DIM), lambda b, h, i: (b, i, h))
    return pl.pallas_call(
        functools.partial(_attn_kernel, ntl=ntl, t_lat=ntl * tm, lam_init=lam_init),
        grid=(B, DA_HEADS, nt),
        in_specs=[qo, kv, kv,
                  pl.BlockSpec(lam_p.shape, lambda b, h, i: (0, 0)),
                  pl.BlockSpec((1, DA_V_DIM), lambda b, h, i: (0, 0))],
        out_specs=qo,
        out_shape=jax.ShapeDtypeStruct((B, TT, DA_WIDTH), BF16),
        compiler_params=pltpu.CompilerParams(
            dimension_semantics=("parallel", "parallel", "parallel"),
            vmem_limit_bytes=VMEM_LIMIT_BYTES),
        name="diff_attn",
    )(q, k, v, lam_p, norm_g.reshape(1, DA_V_DIM))


def _rwfeat_kernel(p_ref, pp_ref, pn_ref, mu_ref, w0_ref, w2_ref, a0_ref, a2_ref, kk_ref, ka_ref,
                   rk_ref, g2_ref, ones_ref,
                   lw0_ref, lw1_ref, kd0_ref, kd1_ref, b0_ref, b1_ref, kko_ref, v_ref, r_ref,
                   bonus_ref, g_ref, *, ntl, nt):
    i = pl.program_id(1)
    tm = p_ref.shape[1]
    p = p_ref[0]
    prev_ok = jnp.logical_and(i != 0, i != ntl)
    next_ok = jnp.logical_and(i != ntl - 1, i != nt - 1)
    prev_row = jnp.where(prev_ok, pp_ref[0, 7:8, :], 0.0)
    next_row = jnp.where(next_ok, pn_ref[0, 0:1, :], 0.0)
    row = lax.broadcasted_iota(jnp.int32, p.shape, 0)
    prev = jnp.where(row == 0, prev_row, pltpu.roll(p, 1, 0))
    nxt = jnp.where(row == tm - 1, next_row, pltpu.roll(p, tm - 1, 0))
    ps = p + mu_ref[0:1, :] * (prev - p) + mu_ref[1:2, :] * (nxt - p)

    W = RW_WIDTH
    r = ps[:, 0:W]
    k = ps[:, W:2 * W]
    v = ps[:, 2 * W:3 * W]
    xw = ps[:, 3 * W:3 * W + 64]
    xa = ps[:, 3 * W + 64:3 * W + 128]
    xg = ps[:, 3 * W + 128:3 * W + 256]
    ones_bd = ones_ref[...]

    w_pre = _dot(jnp.tanh(xw).astype(BF16), w2_ref[...]) + w0_ref[...]
    sp = jnp.maximum(-w_pre, 0.0) + jnp.log(1.0 + jnp.exp(-jnp.abs(w_pre)))
    logw = -jnp.exp(-sp - 0.5)
    a = _sigmoid(_dot(xa.astype(BF16), a2_ref[...]) + a0_ref[...])

    kx = k * kk_ref[...]
    nrm = jnp.sqrt(_seg_sum(kx * kx, ones_bd))
    kk = kx / jnp.maximum(nrm, 1e-12)
    ka = ka_ref[...]
    kd0 = k * (1.0 + (a[:, :W] - 1.0) * ka)
    kd1 = k * (1.0 + (a[:, W:] - 1.0) * ka)
    g = _dot(_sigmoid(xg).astype(BF16), g2_ref[...])
    k_b = 0.5 * (kd0 + kd1)
    bonus = _seg_sum(r * k_b * rk_ref[...], ones_bd) * v

    lw0_ref[0] = logw[:, :W]
    lw1_ref[0] = logw[:, W:]
    kd0_ref[0] = kd0
    kd1_ref[0] = kd1
    b0_ref[0] = kk * a[:, :W]
    b1_ref[0] = kk * a[:, W:]
    kko_ref[0] = kk
    v_ref[0] = v
    r_ref[0] = r
    bonus_ref[0] = bonus
    g_ref[0] = g


def _rwkv_features(prw, mu, w0, w2, a0, a2, k_k, k_a, r_k, g2, ones_bd, ntl):
    B, TT, _ = prw.shape
    tm = TOKEN_TILE
    nt = TT // tm
    r8 = tm // 8
    last8 = TT // 8 - 1
    W = RW_WIDTH
    const = lambda a: pl.BlockSpec(a.shape, lambda b, i: (0,) * a.ndim)
    tok = pl.BlockSpec((1, tm, W), lambda b, i: (b, i, 0))
    small = [mu, w0, w2, a0, a2, k_k, k_a, r_k, g2, ones_bd]
    return pl.pallas_call(
        functools.partial(_rwfeat_kernel, ntl=ntl, nt=nt),
        grid=(B, nt),
        in_specs=[pl.BlockSpec((1, tm, RW_COLS), lambda b, i: (b, i, 0)),
                  pl.BlockSpec((1, 8, RW_COLS), lambda b, i: (b, jnp.maximum(i * r8 - 1, 0), 0)),
                  pl.BlockSpec((1, 8, RW_COLS), lambda b, i: (b, jnp.minimum((i + 1) * r8, last8), 0))]
                 + [const(a) for a in small],
        out_specs=[tok] * 11,
        out_shape=[jax.ShapeDtypeStruct((B, TT, W), F32)] * 11,
        compiler_params=pltpu.CompilerParams(
            dimension_semantics=("parallel", "parallel"), vmem_limit_bytes=VMEM_LIMIT_BYTES),
        name="rwkv_features",
    )(prw, prw, prw, *small)


def _scan_chunk(logw, kd, b, kk, v, r, h0, reverse):
    C = logw.shape[0]
    ti = lax.broadcasted_iota(jnp.int32, (C, C), 0)
    si = lax.broadcasted_iota(jnp.int32, (C, C), 1)
    incl = (si >= ti) if reverse else (si <= ti)
    strict = (si > ti) if reverse else (si < ti)
    eye = si == ti
    cum = _dot_hi(jnp.where(incl, 1.0, 0.0), logw)
    tot = jnp.sum(logw, axis=0, keepdims=True)
    p_inv = jnp.exp(-cum)
    p_end = jnp.exp(tot - cum)
    at = -kk * jnp.exp(cum - logw)
    rt = r * jnp.exp(cum)
    lhs = jnp.concatenate([at, rt], axis=0)
    rhs = jnp.concatenate([b * p_inv, kd * p_inv], axis=0)
    aa = _dot_nt(lhs, rhs, HIGHEST)
    a_ab = jnp.where(strict, aa[:C, :C], 0.0)
    a_ak = jnp.where(strict, aa[:C, C:], 0.0)
    a_rb = jnp.where(incl, aa[C:, :C], 0.0)
    a_rk = jnp.where(incl, aa[C:, C:], 0.0)
    inv = jnp.where(eye, 1.0, 0.0) + a_ab
    apow = a_ab
    for _ in range(int(math.log2(C)) - 1):
        apow = _dot_hi(apow, apow)
        inv = inv + _dot_hi(inv, apow)
    gh = _dot_hi(lhs, h0)
    u = _dot_hi(inv, gh[:C] + _dot_hi(a_ak, v))
    y = gh[C:] + _dot_hi(a_rb, u) + _dot_hi(a_rk, v)
    K = h0.shape[0]
    ki = lax.broadcasted_iota(jnp.int32, (K, K), 0)
    kj = lax.broadcasted_iota(jnp.int32, (K, K), 1)
    dg = jnp.where(ki == kj, jnp.exp(tot), 0.0)
    h1 = _dot_hi(dg, h0) + _dot_tn(b * p_end, u, HIGHEST) + _dot_tn(kd * p_end, v, HIGHEST)
    return y, h1


def _scan_kernel(lw0, kd0, b0, kkf, vf, rf, lw1, kd1, b1, kkr, vr, rr, y0_ref, y1_ref, h_ref):
    s = pl.program_id(1)

    @pl.when(s == 0)
    def _():
        h_ref[...] = jnp.zeros_like(h_ref)

    for z, (lw, kd, b, kk, v, r, y_ref) in enumerate(
            ((lw0, kd0, b0, kkf, vf, rf, y0_ref), (lw1, kd1, b1, kkr, vr, rr, y1_ref))):
        ys = []
        for h in range(RW_HEADS):
            c0, c1 = h * RW_HEAD, (h + 1) * RW_HEAD
            y, h1 = _scan_chunk(lw[0, :, c0:c1], kd[0, :, c0:c1], b[0, :, c0:c1], kk[0, :, c0:c1],
                                v[0, :, c0:c1], r[0, :, c0:c1], h_ref[z, h], reverse=(z == 1))
            h_ref[z, h] = h1
            ys.append(y)
        y_ref[0] = jnp.concatenate(ys, axis=-1)


def _rwkv_scan(lw0, lw1, kd0, kd1, b0, b1, kk, v, r, t_lat):
    B, TT, W = kk.shape
    C = SCAN_CHUNK
    ncl = t_lat // C
    nch = TT // C
    ncc = nch - ncl

    def fwd(s):
        return jnp.where(s < ncc, ncl + s, s - ncc)

    def rev(s):
        return jnp.where(s < ncc, nch - 1 - s, nch - 1 - s)

    def rev_pos(s):
        return jnp.where(s < ncc, nch - 1 - s, ncl - 1 - (s - ncc))

    fs = pl.BlockSpec((1, C, W), lambda b, s: (b, fwd(s), 0))
    rs = pl.BlockSpec((1, C, W), lambda b, s: (b, rev_pos(s), 0))
    return pl.pallas_call(
        _scan_kernel,
        grid=(B, nch),
        in_specs=[fs] * 6 + [rs] * 6,
        out_specs=[fs, rs],
        out_shape=[jax.ShapeDtypeStruct((B, TT, W), F32)] * 2,
        scratch_shapes=[pltpu.VMEM((2, RW_HEADS, RW_HEAD, RW_HEAD), F32)],
        compiler_params=pltpu.CompilerParams(
            dimension_semantics=("parallel", "arbitrary"), vmem_limit_bytes=VMEM_LIMIT_BYTES),
        name="rwkv_scan",
    )(lw0, kd0, b0, kk, v, r, lw1, kd1, b1, kk, v, r)


def _merge_kernel(x_ref, yda_ref, y0_ref, y1_ref, bonus_ref, g_ref, ysg_ref, gate_ref,
                  g1_ref, sh2_ref, sc2_ref, wb_ref, wo_ref, ln1g_ref, ln1b_ref, rlng_ref, rlnb_ref,
                  ones_ref, wr_ref, xmid_ref, h2_ref, aff_ref, *, alpha):
    D = D_MODEL
    ones_bd = ones_ref[...]
    y = y0_ref[0] + y1_ref[0]
    mu = _seg_sum(y, ones_bd) * (1.0 / RW_HEAD)
    dy = y - mu
    var = _seg_sum(dy * dy, ones_bd) * (1.0 / RW_HEAD)
    gn = dy * lax.rsqrt(var + RW_GN_EPS) * rlng_ref[...] + rlnb_ref[...]
    yrw = ((gn + bonus_ref[0]) * g_ref[0]).astype(BF16)

    m = gate_ref[0, :, 0:D].astype(F32) * _dot(yda_ref[0], wb_ref[0:DA_WIDTH, :])
    m = m + gate_ref[0, :, D:2 * D].astype(F32) * _dot(yrw, wb_ref[DA_WIDTH:DA_WIDTH + RW_WIDTH, :])
    m = m + gate_ref[0, :, 2 * D:3 * D].astype(F32) * _dot(ysg_ref[0], wb_ref[DA_WIDTH + RW_WIDTH:, :])
    mix = _dot(m.astype(BF16), wo_ref[...])

    z = alpha * x_ref[0] + g1_ref[0, 0] * mix
    zm = jnp.mean(z, axis=-1, keepdims=True)
    dz = z - zm
    zv = jnp.mean(dz * dz, axis=-1, keepdims=True)
    xmid = dz * lax.rsqrt(zv + LN_EPS) * ln1g_ref[...] + ln1b_ref[...]
    xmid_ref[0] = xmid
    h2 = xmid * (1.0 + sc2_ref[0, 0]) + sh2_ref[0, 0]
    h2_ref[0] = h2.astype(BF16)
    logits = _dot_nt(wr_ref[...], h2, HIGHEST)
    e = jnp.exp(logits - jnp.max(logits, axis=0, keepdims=True))
    aff_ref[0] = e / jnp.sum(e, axis=0, keepdims=True)


def _merge(x_all, yda, y0, y1, bonus, g, ysg, gates, modall, w_branch, w_out, ln1g, ln1b,
           rlng, rlnb, ones_bd, w_router_t, ntl, alpha):
    B, TT, D = x_all.shape
    tm = TOKEN_TILE
    nt = TT // tm
    tok = lambda w: pl.BlockSpec((1, tm, w), lambda b, i: (b, i, 0))
    modspec = lambda j: pl.BlockSpec((1, 1, 1, D), lambda b, i: (b, i // ntl, 0, j))
    const = lambda a: pl.BlockSpec(a.shape, lambda b, i: (0,) * a.ndim)
    consts = [w_branch, w_out, ln1g, ln1b, rlng, rlnb, ones_bd, w_router_t]
    return pl.pallas_call(
        functools.partial(_merge_kernel, alpha=alpha),
        grid=(B, nt),
        in_specs=[tok(D), tok(DA_WIDTH), tok(RW_WIDTH), tok(RW_WIDTH), tok(RW_WIDTH), tok(RW_WIDTH),
                  tok(SG_WIDTH), tok(3 * D), modspec(2), modspec(3), modspec(4)]
                 + [const(a) for a in consts],
        out_specs=[tok(D), tok(D), pl.BlockSpec((1, N_EXPERTS, tm), lambda b, i: (b, 0, i))],
        out_shape=[jax.ShapeDtypeStruct((B, TT, D), F32),
                   jax.ShapeDtypeStruct((B, TT, D), BF16),
                   jax.ShapeDtypeStruct((B, N_EXPERTS, TT), F32)],
        compiler_params=pltpu.CompilerParams(
            dimension_semantics=("parallel", "parallel"), vmem_limit_bytes=VMEM_LIMIT_BYTES),
        name="merge_ln1_router",
    )(x_all, yda, y0, y1, bonus, g, ysg, gates, modall, modall, modall, *consts)


def _topk_kernel(aff_ref, tri_ref, rank_ref, *, cap):
    a = aff_ref[0]
    bits = pltpu.bitcast(a, jnp.int32)
    thr = jnp.zeros((a.shape[0], 1), jnp.int32)
    for bit in range(30, -1, -1):
        cand = thr | (1 << bit)
        cnt = jnp.sum(jnp.where(bits >= cand, 1.0, 0.0), axis=-1, keepdims=True)
        thr = jnp.where(cnt >= cap, cand, thr)
    gt = bits > thr
    eq = bits == thr
    need = cap - jnp.sum(jnp.where(gt, 1.0, 0.0), axis=-1, keepdims=True)
    tri = tri_ref[...]
    tie_rank = _dot(jnp.where(eq, 1.0, 0.0).astype(BF16), tri)
    sel = jnp.logical_or(gt, jnp.logical_and(eq, tie_rank < need))
    rank = _dot(jnp.where(sel, 1.0, 0.0).astype(BF16), tri)
    rank_ref[0] = jnp.where(sel, rank, -1.0).astype(jnp.int32)


def _topk_ranks(aff, tri, t_off, t_len, cap):
    B, E, _ = aff.shape
    blk = t_off // t_len
    return pl.pallas_call(
        functools.partial(_topk_kernel, cap=cap),
        grid=(B,),
        in_specs=[pl.BlockSpec((1, E, t_len), lambda b: (b, 0, blk)),
                  pl.BlockSpec(tri.shape, lambda b: (0, 0))],
        out_specs=pl.BlockSpec((1, E, t_len), lambda b: (b, 0, 0)),
        out_shape=jax.ShapeDtypeStruct((B, E, t_len), jnp.int32),
        compiler_params=pltpu.CompilerParams(
            dimension_semantics=("parallel",), vmem_limit_bytes=VMEM_LIMIT_BYTES),
        name="expert_choice_ranks",
    )(aff, tri)


def _moe_kernel(h_ref, rank_ref, aff_ref, wg_ref, wu_ref, wd_ref, f_ref, *, cap):
    e = pl.program_id(1)

    @pl.when(e == 0)
    def _():
        f_ref[...] = jnp.zeros_like(f_ref)

    rank = rank_ref[0, pl.ds(e, 1), :]
    aff = aff_ref[0, pl.ds(e, 1), :]
    T = rank.shape[1]
    slot = lax.broadcasted_iota(jnp.int32, (cap, T), 0)
    hit = rank == slot
    onehot = jnp.where(hit, 1.0, 0.0).astype(BF16)
    gate = jnp.sum(jnp.where(hit, aff, 0.0), axis=-1, keepdims=True)
    xe = _dot(onehot, h_ref[0]).astype(BF16)
    hg = _dot(xe, wg_ref[0])
    hid = (hg * _sigmoid(hg)) * _dot(xe, wu_ref[0])
    ye = _dot(hid.astype(BF16), wd_ref[0]) * gate
    f_ref[0] += _dot_tn(onehot, ye.astype(BF16))


def _expert_ffn(h2, rank, aff, wg, wu, wd, t_off, t_len, cap):
    B, TT, D = h2.shape
    E = rank.shape[1]
    blk = t_off // t_len
    F = wg.shape[2]
    return pl.pallas_call(
        functools.partial(_moe_kernel, cap=cap),
        grid=(B, E),
        in_specs=[pl.BlockSpec((1, t_len, D), lambda b, e: (b, blk, 0)),
                  pl.BlockSpec((1, E, t_len), lambda b, e: (b, 0, 0)),
                  pl.BlockSpec((1, E, t_len), lambda b, e: (b, 0, blk)),
                  pl.BlockSpec((1, D, F), lambda b, e: (e, 0, 0)),
                  pl.BlockSpec((1, D, F), lambda b, e: (e, 0, 0)),
                  pl.BlockSpec((1, F, D), lambda b, e: (e, 0, 0))],
        out_specs=pl.BlockSpec((1, t_len, D), lambda b, e: (b, 0, 0)),
        out_shape=jax.ShapeDtypeStruct((B, t_len, D), F32),
        compiler_params=pltpu.CompilerParams(
            dimension_semantics=("parallel", "arbitrary"), vmem_limit_bytes=VMEM_LIMIT_BYTES),
        name="expert_ffn",
    )(h2, rank, aff, wg, wu, wd)


def _ln2_kernel(x_ref, f_ref, g2_ref, lng_ref, lnb_ref, o_ref, *, alpha):
    z = alpha * x_ref[0] + g2_ref[0, 0] * f_ref[0]
    zm = jnp.mean(z, axis=-1, keepdims=True)
    dz = z - zm
    zv = jnp.mean(dz * dz, axis=-1, keepdims=True)
    o_ref[0] = dz * lax.rsqrt(zv + LN_EPS) * lng_ref[...] + lnb_ref[...]


def _final_norm(xmid, f, modall, lng, lnb, ntl, alpha):
    B, TT, D = xmid.shape
    tm = TOKEN_TILE
    tok = pl.BlockSpec((1, tm, D), lambda b, i: (b, i, 0))
    const = lambda a: pl.BlockSpec(a.shape, lambda b, i: (0,) * a.ndim)
    return pl.pallas_call(
        functools.partial(_ln2_kernel, alpha=alpha),
        grid=(B, TT // tm),
        in_specs=[tok, tok, pl.BlockSpec((1, 1, 1, D), lambda b, i: (b, i // ntl, 0, 5)),
                  const(lng), const(lnb)],
        out_specs=tok,
        out_shape=jax.ShapeDtypeStruct((B, TT, D), F32),
        compiler_params=pltpu.CompilerParams(dimension_semantics=("parallel", "parallel")),
        name="ln2",
    )(xmid, f, modall, lng, lnb)


def _rope_tables(t_lat, t_ctx):
    rows = t_lat // GRID_W
    row = jnp.repeat(jnp.arange(rows, dtype=F32), GRID_W)
    col = jnp.tile(jnp.arange(GRID_W, dtype=F32), rows)
    half = DA_QK_DIM // 2
    inv_freq = ROPE_BASE ** (-jnp.arange(0, half, 2, dtype=F32) / half)
    ar = row[:, None] * inv_freq
    ac = col[:, None] * inv_freq
    ang = jnp.concatenate([ar, ar, ac, ac], axis=-1)
    sign = jnp.where((jnp.arange(DA_QK_DIM) % 32) < 16, -1.0, 1.0).astype(F32)
    reps = DA_QK_COLS // DA_QK_DIM
    cos = jnp.tile(jnp.cos(ang), (1, reps))
    sin = jnp.tile(jnp.sin(ang) * sign, (1, reps))
    cos = jnp.concatenate([cos, jnp.ones((t_ctx, DA_QK_COLS), F32)], axis=0)
    sin = jnp.concatenate([sin, jnp.zeros((t_ctx, DA_QK_COLS), F32)], axis=0)
    return cos, sin


def kernel(x, c, ctx, c_ctx, w_mod, b_mod, w_in, da_lambda, da_norm_g, rw_shift_mu, rw_w0, rw_w2, rw_a0, rw_a2, rw_k_k, rw_k_a, rw_r_k, rw_ln_g, rw_ln_b, rw_g2, sg_norm_g, sg_norm_b, sg_w, sg_b, w_branch, w_out, ln1_g, ln1_b, w_router, w_e_gate, w_e_up, w_e_down, ln2_g, ln2_b):
    B, T, D = x.shape
    Tc = ctx.shape[1]
    depth = w_mod.shape[0]
    tm = TOKEN_TILE
    assert D == D_MODEL and T % tm == 0 and Tc % tm == 0 and T % Tc == 0 and Tc <= T
    ntl = T // tm
    alpha = (2 * depth) ** 0.25
    cap_lat = EC_CAPACITY * T // N_EXPERTS
    cap_ctx = EC_CAPACITY * Tc // N_EXPERTS

    cos, sin = _rope_tables(T, Tc)
    lane = jnp.arange(RW_WIDTH)
    ones_bd = (lane[:, None] // RW_HEAD == lane[None, :] // RW_HEAD).astype(BF16)
    tri_lat = (jnp.arange(T)[:, None] < jnp.arange(T)[None, :]).astype(BF16)
    tri_ctx = tri_lat[:Tc, :Tc]
    rows = ((B + 1 + 7) // 8) * 8
    cc = jnp.concatenate([c, c_ctx[None, :], jnp.zeros((rows - B - 1, D), F32)], axis=0)
    row2 = lambda a: a.reshape(1, -1)

    x_all = jnp.concatenate([x, ctx], axis=1)
    for l in range(depth):
        last = l == depth - 1
        lam_init = 0.8 - 0.6 * math.exp(-0.3 * l)
        mod = _modulation(cc, w_mod[l], b_mod[l])
        modall = jnp.stack([mod[:B], jnp.broadcast_to(mod[B], (B, 6 * D))], axis=1)
        modall = modall.reshape(B, 2, 1, 6 * D)

        sgbias = jnp.repeat(sg_b[l].T, SG_WIDTH // SG_GROUPS, axis=1)
        q, k, v, prw, ysg, gates = _input_projection(
            x_all, modall, w_in[l].astype(BF16), cos, sin, row2(sg_norm_g[l]), row2(sg_norm_b[l]),
            sg_w[l].astype(BF16), sgbias, ntl)

        yda = _diff_attention(q, k, v, da_lambda[l], da_norm_g[l], ntl, lam_init)

        cat2 = lambda a: jnp.transpose(a, (1, 0, 2)).reshape(a.shape[1], 2 * RW_WIDTH)
        feats = _rwkv_features(
            prw, rw_shift_mu[l], row2(rw_w0[l]), cat2(rw_w2[l]).astype(BF16), row2(rw_a0[l]),
            cat2(rw_a2[l]).astype(BF16), row2(rw_k_k[l]), row2(rw_k_a[l]), row2(rw_r_k[l]),
            rw_g2[l].astype(BF16), ones_bd, ntl)
        lw0, lw1, kd0, kd1, b0, b1, kk, vv, rr, bonus, gg = feats
        y0, y1 = _rwkv_scan(lw0, lw1, kd0, kd1, b0, b1, kk, vv, rr, T)

        xmid, h2, aff = _merge(
            x_all, yda, y0, y1, bonus, gg, ysg, gates, modall, w_branch[l].astype(BF16),
            w_out[l].astype(BF16), row2(ln1_g[l]), row2(ln1_b[l]), row2(rw_ln_g[l]),
            row2(rw_ln_b[l]), ones_bd, w_router[l].T, ntl, alpha)

        wg = w_e_gate[l].astype(BF16)
        wu = w_e_up[l].astype(BF16)
        wd = w_e_down[l].astype(BF16)
        rank_lat = _topk_ranks(aff, tri_lat, 0, T, cap_lat)
        f_lat = _expert_ffn(h2, rank_lat, aff, wg, wu, wd, 0, T, cap_lat)
        if last:
            f_ctx = jnp.zeros((B, Tc, D), F32)
        else:
            rank_ctx = _topk_ranks(aff, tri_ctx, T, Tc, cap_ctx)
            f_ctx = _expert_ffn(h2, rank_ctx, aff, wg, wu, wd, T, Tc, cap_ctx)
        f = jnp.concatenate([f_lat, f_ctx], axis=1)
        x_all = _final_norm(xmid, f, modall, row2(ln2_g[l]), row2(ln2_b[l]), ntl, alpha)
    return x_all[:, :T]
```

```python
import functools
import math

import jax
import jax.numpy as jnp
from jax import lax
from jax.experimental import pallas as pl
from jax.experimental.pallas import tpu as pltpu

F32 = jnp.float32
BF16 = jnp.bfloat16
HIGHEST = lax.Precision.HIGHEST

D_MODEL = 1024
GRID_W = 64
DA_HEADS = 4
DA_QK_DIM = 64
DA_V_DIM = 128
DA_WIDTH = 512
DA_QK_COLS = 512
ROPE_BASE = 10000.0
DA_EPS = 1e-5
RW_HEAD = 64
RW_HEADS = 8
RW_WIDTH = 512
RW_COLS = 1792
RW_GN_EPS = 64e-5
SG_CHUNK = 128
SG_GROUPS = 4
SG_WIDTH = 512
MIX_WIDTH = 1536
DA_K0 = 512
DA_V0 = 1024
RW_0 = 1536
SG_0 = RW_0 + RW_COLS
GATE_0 = SG_0 + 2 * SG_WIDTH
IN_COLS = GATE_0 + 3 * D_MODEL
N_EXPERTS = 16
EC_CAPACITY = 2
LN_EPS = 1e-5

TOKEN_TILE = 256
SCAN_CHUNK = 64
VMEM_LIMIT_BYTES = 58 * 1024 * 1024


def _dot(a, b):
    return jnp.dot(a, b, preferred_element_type=F32)


def _dot_hi(a, b):
    return jnp.dot(a, b, preferred_element_type=F32, precision=HIGHEST)


def _dot_nt(a, b, precision=None):
    return lax.dot_general(a, b, (((1,), (1,)), ((), ())), preferred_element_type=F32,
                           precision=precision)


def _dot_tn(a, b, precision=None):
    return lax.dot_general(a, b, (((0,), (0,)), ((), ())), preferred_element_type=F32,
                           precision=precision)


def _sigmoid(z):
    return 1.0 / (1.0 + jnp.exp(-z))


def _seg_sum(z, ones_bd):
    hi = z.astype(BF16)
    lo = (z - hi.astype(F32)).astype(BF16)
    return _dot(hi, ones_bd) + _dot(lo, ones_bd)


def _mod_kernel(c_ref, w_ref, b_ref, o_ref):
    cc = c_ref[...]
    o_ref[...] = _dot_hi(cc * _sigmoid(cc), w_ref[...]) + b_ref[...]


def _modulation(cc, w_mod, b_mod):
    rows, d = cc.shape
    n = w_mod.shape[1]
    tn = 1024
    return pl.pallas_call(
        _mod_kernel,
        grid=(n // tn,),
        in_specs=[pl.BlockSpec((rows, d), lambda j: (0, 0)),
                  pl.BlockSpec((d, tn), lambda j: (0, j)),
                  pl.BlockSpec((1, tn), lambda j: (0, j))],
        out_specs=pl.BlockSpec((rows, tn), lambda j: (0, j)),
        out_shape=jax.ShapeDtypeStruct((rows, n), F32),
        name="adaln_mod",
    )(cc, w_mod, b_mod.reshape(1, n))


def _inproj_kernel(x_ref, sh_ref, sc_ref, w_ref, cos_ref, sin_ref, sgg_ref, sgb_ref, sgw_ref,
                   sgbias_ref, q_ref, k_ref, v_ref, rw_ref, sg_ref, gate_ref):
    tm = x_ref.shape[1]
    h = (x_ref[0] * (1.0 + sc_ref[0, 0]) + sh_ref[0, 0]).astype(BF16)

    def proj(c0, c1):
        return _dot(h, w_ref[:, c0:c1])

    cos = cos_ref[...]
    sin = sin_ref[...]
    lane = lax.broadcasted_iota(jnp.int32, (tm, DA_QK_COLS), 1)
    first = (lane % 32) < 16

    def rope(z):
        zr = jnp.where(first, pltpu.roll(z, DA_QK_COLS - 16, 1), pltpu.roll(z, 16, 1))
        return z * cos + zr * sin

    q_ref[0] = (rope(proj(0, DA_K0)) * DA_QK_DIM ** -0.5).astype(BF16)
    k_ref[0] = rope(proj(DA_K0, DA_V0)).astype(BF16)
    v_ref[0] = proj(DA_V0, RW_0).astype(BF16)
    for c0 in range(0, RW_COLS, 896):
        rw_ref[0, :, c0:c0 + 896] = proj(RW_0 + c0, RW_0 + c0 + 896)
    for j in range(3):
        gate_ref[0, :, j * D_MODEL:(j + 1) * D_MODEL] = _sigmoid(
            proj(GATE_0 + j * D_MODEL, GATE_0 + (j + 1) * D_MODEL)).astype(BF16)

    ps = proj(SG_0, GATE_0)
    gl = ps * (0.5 * (1.0 + jnp.tanh(math.sqrt(2.0 / math.pi) * (ps + 0.044715 * (ps * ps * ps)))))
    u = gl[:, :SG_WIDTH]
    vv = gl[:, SG_WIDTH:]
    mu = jnp.mean(vv, axis=-1, keepdims=True)
    dv = vv - mu
    var = jnp.mean(dv * dv, axis=-1, keepdims=True)
    vn = (dv * lax.rsqrt(var + LN_EPS) * sgg_ref[...] + sgb_ref[...]).astype(BF16)
    gd = SG_WIDTH // SG_GROUPS
    for n in range(tm // SG_CHUNK):
        r0 = n * SG_CHUNK
        for g in range(SG_GROUPS):
            c0 = g * gd
            vm = _dot(sgw_ref[g], vn[r0:r0 + SG_CHUNK, c0:c0 + gd]) + sgbias_ref[:, c0:c0 + gd]
            sg_ref[0, r0:r0 + SG_CHUNK, c0:c0 + gd] = (u[r0:r0 + SG_CHUNK, c0:c0 + gd] * vm).astype(BF16)


def _input_projection(x_all, modall, w_in, cos, sin, sgg, sgb, sgw, sgbias, ntl):
    B, TT, D = x_all.shape
    tm = TOKEN_TILE
    nt = TT // tm
    tok = lambda w: pl.BlockSpec((1, tm, w), lambda b, i: (b, i, 0))
    modspec = lambda j: pl.BlockSpec((1, 1, 1, D), lambda b, i: (b, i // ntl, 0, j))
    const2 = lambda a: pl.BlockSpec(a.shape, lambda b, i: (0, 0))
    return pl.pallas_call(
        _inproj_kernel,
        grid=(B, nt),
        in_specs=[tok(D), modspec(0), modspec(1),
                  pl.BlockSpec(w_in.shape, lambda b, i: (0, 0), pipeline_mode=pl.Buffered(1)),
                  pl.BlockSpec((tm, DA_QK_COLS), lambda b, i: (i, 0)),
                  pl.BlockSpec((tm, DA_QK_COLS), lambda b, i: (i, 0)),
                  const2(sgg), const2(sgb),
                  pl.BlockSpec(sgw.shape, lambda b, i: (0, 0, 0)),
                  const2(sgbias)],
        out_specs=[tok(DA_QK_COLS), tok(DA_QK_COLS), tok(DA_WIDTH), tok(RW_COLS), tok(SG_WIDTH),
                   tok(3 * D)],
        out_shape=[jax.ShapeDtypeStruct((B, TT, DA_QK_COLS), BF16),
                   jax.ShapeDtypeStruct((B, TT, DA_QK_COLS), BF16),
                   jax.ShapeDtypeStruct((B, TT, DA_WIDTH), BF16),
                   jax.ShapeDtypeStruct((B, TT, RW_COLS), F32),
                   jax.ShapeDtypeStruct((B, TT, SG_WIDTH), BF16),
                   jax.ShapeDtypeStruct((B, TT, 3 * D), BF16)],
        compiler_params=pltpu.CompilerParams(
            dimension_semantics=("parallel", "parallel"), vmem_limit_bytes=VMEM_LIMIT_BYTES),
        name="in_proj",
    )(x_all, modall, modall, w_in, cos, sin, sgg, sgb, sgw, sgbias)


def _attn_kernel(q_ref, k_ref, v_ref, lam_ref, g_ref, o_ref, *, ntl, t_lat, lam_init):
    i = pl.program_id(2)
    lp = lam_ref[...]
    lam = (jnp.exp(jnp.sum(lp[0:1] * lp[1:2], axis=-1, keepdims=True))
           - jnp.exp(jnp.sum(lp[2:3] * lp[3:4], axis=-1, keepdims=True)) + lam_init)
    q = q_ref[0]
    lane = lax.broadcasted_iota(jnp.int32, q.shape, 1)
    zero = jnp.zeros_like(q)
    q0 = jnp.where(lane < DA_QK_DIM, q, zero)
    q1 = jnp.where(lane >= DA_QK_DIM, q, zero)

    def attend(k, v):
        def one(qm):
            s = _dot_nt(qm, k)
            e = jnp.exp(s - jnp.max(s, axis=-1, keepdims=True))
            return _dot(e.astype(BF16), v) / jnp.sum(e, axis=-1, keepdims=True)
        o = one(q0) - lam * one(q1)
        o = o * lax.rsqrt(jnp.mean(o * o, axis=-1, keepdims=True) + DA_EPS) * g_ref[...]
        o_ref[0] = (o * (1.0 - lam_init)).astype(BF16)

    @pl.when(i < ntl)
    def _():
        attend(k_ref[0], v_ref[0])

    @pl.when(i >= ntl)
    def _():
        attend(k_ref[0, t_lat:, :], v_ref[0, t_lat:, :])


def _diff_attention(q, k, v, lam_p, norm_g, ntl, lam_init):
    B, TT, _ = q.shape
    tm = TOKEN_TILE
    nt = TT // tm
    kv = pl.BlockSpec((1, TT, DA_V_DIM), lambda b, h, i: (b, 0, h))
    qo = pl.BlockSpec((1, tm, DA_V_DIM), lambda b, h, i: (b, i, h))
    return pl.pallas_call(
        functools.partial(_attn_kernel, ntl=ntl, t_lat=ntl * tm, lam_init=lam_init),
        grid=(B, DA_HEADS, nt),
        in_specs=[qo, kv, kv,
                  pl.BlockSpec(lam_p.shape, lambda b, h, i: (0, 0)),
                  pl.BlockSpec((1, DA_V_DIM), lambda b, h, i: (0, 0))],
        out_specs=qo,
        out_shape=jax.ShapeDtypeStruct((B, TT, DA_WIDTH), BF16),
        compiler_params=pltpu.CompilerParams(
            dimension_semantics=("parallel", "parallel", "parallel"),
            vmem_limit_bytes=VMEM_LIMIT_BYTES),
        name="diff_attn",
    )(q, k, v, lam_p, norm_g.reshape(1, DA_V_DIM))


def _rwfeat_kernel(p_ref, pp_ref, pn_ref, mu_ref, w0_ref, w2_ref, a0_ref, a2_ref, kk_ref, ka_ref,
                   rk_ref, g2_ref, ones_ref,
                   lw0_ref, lw1_ref, kd0_ref, kd1_ref, b0_ref, b1_ref, kko_ref, v_ref, r_ref,
                   bonus_ref, g_ref, *, ntl, nt):
    i = pl.program_id(1)
    tm = p_ref.shape[1]
    p = p_ref[0]
    prev_ok = jnp.logical_and(i != 0, i != ntl)
    next_ok = jnp.logical_and(i != ntl - 1, i != nt - 1)
    prev_row = jnp.where(prev_ok, pp_ref[0, 7:8, :], 0.0)
    next_row = jnp.where(next_ok, pn_ref[0, 0:1, :], 0.0)
    row = lax.broadcasted_iota(jnp.int32, p.shape, 0)
    prev = jnp.where(row == 0, prev_row, pltpu.roll(p, 1, 0))
    nxt = jnp.where(row == tm - 1, next_row, pltpu.roll(p, tm - 1, 0))
    ps = p + mu_ref[0:1, :] * (prev - p) + mu_ref[1:2, :] * (nxt - p)

    W = RW_WIDTH
    r = ps[:, 0:W]
    k = ps[:, W:2 * W]
    v = ps[:, 2 * W:3 * W]
    xw = ps[:, 3 * W:3 * W + 64]
    xa = ps[:, 3 * W + 64:3 * W + 128]
    xg = ps[:, 3 * W + 128:3 * W + 256]
    ones_bd = ones_ref[...]

    w_pre = _dot(jnp.tanh(xw).astype(BF16), w2_ref[...]) + w0_ref[...]
    sp = jnp.maximum(-w_pre, 0.0) + jnp.log(1.0 + jnp.exp(-jnp.abs(w_pre)))
    logw = -jnp.exp(-sp - 0.5)
    a = _sigmoid(_dot(xa.astype(BF16), a2_ref[...]) + a0_ref[...])

    kx = k * kk_ref[...]
    nrm = jnp.sqrt(_seg_sum(kx * kx, ones_bd))
    kk = kx / jnp.maximum(nrm, 1e-12)
    ka = ka_ref[...]
    kd0 = k * (1.0 + (a[:, :W] - 1.0) * ka)
    kd1 = k * (1.0 + (a[:, W:] - 1.0) * ka)
    g = _dot(_sigmoid(xg).astype(BF16), g2_ref[...])
    k_b = 0.5 * (kd0 + kd1)
    bonus = _seg_sum(r * k_b * rk_ref[...], ones_bd) * v

    lw0_ref[0] = logw[:, :W]
    lw1_ref[0] = logw[:, W:]
    kd0_ref[0] = kd0
    kd1_ref[0] = kd1
    b0_ref[0] = kk * a[:, :W]
    b1_ref[0] = kk * a[:, W:]
    kko_ref[0] = kk
    v_ref[0] = v
    r_ref[0] = r
    bonus_ref[0] = bonus
    g_ref[0] = g


def _rwkv_features(prw, mu, w0, w2, a0, a2, k_k, k_a, r_k, g2, ones_bd, ntl):
    B, TT, _ = prw.shape
    tm = TOKEN_TILE
    nt = TT // tm
    r8 = tm // 8
    last8 = TT // 8 - 1
    W = RW_WIDTH
    const = lambda a: pl.BlockSpec(a.shape, lambda b, i: (0,) * a.ndim)
    tok = pl.BlockSpec((1, tm, W), lambda b, i: (b, i, 0))
    small = [mu, w0, w2, a0, a2, k_k, k_a, r_k, g2, ones_bd]
    return pl.pallas_call(
        functools.partial(_rwfeat_kernel, ntl=ntl, nt=nt),
        grid=(B, nt),
        in_specs=[pl.BlockSpec((1, tm, RW_COLS), lambda b, i: (b, i, 0)),
                  pl.BlockSpec((1, 8, RW_COLS), lambda b, i: (b, jnp.maximum(i * r8 - 1, 0), 0)),
                  pl.BlockSpec((1, 8, RW_COLS), lambda b, i: (b, jnp.minimum((i + 1) * r8, last8), 0))]
                 + [const(a) for a in small],
        out_specs=[tok] * 11,
        out_shape=[jax.ShapeDtypeStruct((B, TT, W), F32)] * 11,
        compiler_params=pltpu.CompilerParams(
            dimension_semantics=("parallel", "parallel"), vmem_limit_bytes=VMEM_LIMIT_BYTES),
        name="rwkv_features",
    )(prw, prw, prw, *small)


def _split_bf16(x, pieces):
    out = []
    for _ in range(pieces - 1):
        hi = x.astype(BF16)
        out.append(hi)
        x = x - hi.astype(F32)
    out.append(x.astype(BF16))
    return out


_NN = ((1,), (0,))
_NT = ((1,), (1,))
_TN = ((0,), (0,))


def _mm(a, b, passes, dims=_NN):
    dn = (dims, ((), ()))
    if passes == 6:
        return lax.dot_general(a, b, dn, preferred_element_type=F32, precision=HIGHEST)
    if passes == 1:
        return lax.dot_general(a.astype(BF16), b.astype(BF16), dn, preferred_element_type=F32)
    a_hi, a_lo = _split_bf16(a, 2)
    b_hi, b_lo = _split_bf16(b, 2)
    d = lambda p, q: lax.dot_general(p, q, dn, preferred_element_type=F32)
    return d(a_hi, b_hi) + (d(a_hi, b_lo) + d(a_lo, b_hi))


def _scan_kernel(lw0, kd0, b0, kkf, vf, rf, lw1, kd1, b1, kkr, vr, rr, y0_ref, y1_ref, s_ref, *, prec):
    p_aa, p_inv, p_gh, p_w, p_u, p_y, p_s = prec
    step = pl.program_id(1)

    @pl.when(step == 0)
    def _():
        s_ref[...] = jnp.zeros_like(s_ref)

    C = lw0.shape[1]
    N = RW_HEAD
    ti = lax.broadcasted_iota(jnp.int32, (C, C), 0)
    si = lax.broadcasted_iota(jnp.int32, (C, C), 1)
    ti2 = lax.broadcasted_iota(jnp.int32, (C, 2 * C), 0)
    si2 = lax.broadcasted_iota(jnp.int32, (C, 2 * C), 1) % C

    heads = []
    for z, (lw, kd, b, kk, v, r) in enumerate(((lw0, kd0, b0, kkf, vf, rf), (lw1, kd1, b1, kkr, vr, rr))):
        rev = z == 1
        incl = (si >= ti) if rev else (si <= ti)
        strict = (si > ti) if rev else (si < ti)
        incl2 = (si2 >= ti2) if rev else (si2 <= ti2)
        logw = lw[0]
        tri = jnp.where(incl, 1.0, 0.0).astype(BF16)
        cum = sum(_dot(tri, piece) for piece in _split_bf16(logw, 3))
        tot = jnp.sum(logw, axis=0, keepdims=True)
        p_in = jnp.exp(-cum)
        p_end = jnp.exp(tot - cum)
        lhs = jnp.concatenate([-kk[0] * jnp.exp(cum - logw), r[0] * jnp.exp(cum)], axis=0)
        rhs = jnp.concatenate([b[0] * p_in, kd[0] * p_in], axis=0)
        upd = jnp.concatenate([b[0] * p_end, kd[0] * p_end], axis=0)
        dec = jnp.exp(tot)
        vz = v[0]
        for h in range(RW_HEADS):
            c = slice(h * N, (h + 1) * N)
            heads.append(dict(z=z, h=h, lhs=lhs[:, c], rhs=rhs[:, c], upd=upd[:, c], dec=dec[:, c],
                              v=vz[:, c], strict=strict, incl2=incl2))

    for g in heads:
        aa = _mm(g["lhs"], g["rhs"], p_aa, _NT)
        g["a_ab"] = jnp.where(g["strict"], aa[:C, :C], 0.0)
        g["a_ak"] = jnp.where(g["strict"], aa[:C, C:], 0.0)
        g["a_r"] = jnp.where(g["incl2"], aa[C:, :], 0.0)
        g["n"] = g["a_ab"]
        g["apow"] = g["a_ab"]
    for _ in range(int(math.log2(C)) - 1):
        for g in heads:
            g["apow"] = _mm(g["apow"], g["apow"], p_inv)
        for g in heads:
            g["n"] = g["n"] + g["apow"] + _mm(g["n"], g["apow"], p_inv)
    for g in heads:
        g["s0"] = s_ref[g["z"], g["h"]]
        g["gh"] = _mm(g["lhs"], g["s0"], p_gh, _NT)
    for g in heads:
        g["w"] = g["gh"][:C] + _mm(g["a_ak"], g["v"], p_w)
    for g in heads:
        g["uv"] = jnp.concatenate([g["w"] + _mm(g["n"], g["w"], p_u), g["v"]], axis=0)
    for g in heads:
        g["y"] = g["gh"][C:] + _mm(g["a_r"], g["uv"], p_y)
    for g in heads:
        s_ref[g["z"], g["h"]] = g["s0"] * g["dec"] + _mm(g["uv"], g["upd"], p_s, _TN)
    y0_ref[0] = jnp.concatenate([g["y"] for g in heads[:RW_HEADS]], axis=-1)
    y1_ref[0] = jnp.concatenate([g["y"] for g in heads[RW_HEADS:]], axis=-1)


SCAN_PRECISION = (1, 1, 1, 1, 1, 1, 1)


def _rwkv_scan(lw0, lw1, kd0, kd1, b0, b1, kk, v, r, t_lat, prec=SCAN_PRECISION):
    B, TT, W = kk.shape
    C = SCAN_CHUNK
    ncl = t_lat // C
    nch = TT // C
    ncc = nch - ncl

    def fwd(s):
        return jnp.where(s < ncc, ncl + s, s - ncc)

    fs = pl.BlockSpec((1, C, W), lambda b, s: (b, fwd(s), 0))
    rs = pl.BlockSpec((1, C, W), lambda b, s: (b, nch - 1 - s, 0))
    return pl.pallas_call(
        functools.partial(_scan_kernel, prec=prec),
        grid=(B, nch),
        in_specs=[fs] * 6 + [rs] * 6,
        out_specs=[fs, rs],
        out_shape=[jax.ShapeDtypeStruct((B, TT, W), F32)] * 2,
        scratch_shapes=[pltpu.VMEM((2, RW_HEADS, RW_HEAD, RW_HEAD), F32)],
        compiler_params=pltpu.CompilerParams(
            dimension_semantics=("parallel", "arbitrary"), vmem_limit_bytes=VMEM_LIMIT_BYTES),
        name="rwkv_scan",
    )(lw0, kd0, b0, kk, v, r, lw1, kd1, b1, kk, v, r)


def _merge_kernel(x_ref, yda_ref, y0_ref, y1_ref, bonus_ref, g_ref, ysg_ref, gate_ref,
                  g1_ref, sh2_ref, sc2_ref, wb_ref, wo_ref, ln1g_ref, ln1b_ref, rlng_ref, rlnb_ref,
                  ones_ref, wr_ref, xmid_ref, h2_ref, aff_ref, *, alpha):
    D = D_MODEL
    ones_bd = ones_ref[...]
    y = y0_ref[0] + y1_ref[0]
    mu = _seg_sum(y, ones_bd) * (1.0 / RW_HEAD)
    dy = y - mu
    var = _seg_sum(dy * dy, ones_bd) * (1.0 / RW_HEAD)
    gn = dy * lax.rsqrt(var + RW_GN_EPS) * rlng_ref[...] + rlnb_ref[...]
    yrw = ((gn + bonus_ref[0]) * g_ref[0]).astype(BF16)

    m = gate_ref[0, :, 0:D].astype(F32) * _dot(yda_ref[0], wb_ref[0:DA_WIDTH, :])
    m = m + gate_ref[0, :, D:2 * D].astype(F32) * _dot(yrw, wb_ref[DA_WIDTH:DA_WIDTH + RW_WIDTH, :])
    m = m + gate_ref[0, :, 2 * D:3 * D].astype(F32) * _dot(ysg_ref[0], wb_ref[DA_WIDTH + RW_WIDTH:, :])
    mix = _dot(m.astype(BF16), wo_ref[...])

    z = alpha * x_ref[0] + g1_ref[0, 0] * mix
    zm = jnp.mean(z, axis=-1, keepdims=True)
    dz = z - zm
    zv = jnp.mean(dz * dz, axis=-1, keepdims=True)
    xmid = dz * lax.rsqrt(zv + LN_EPS) * ln1g_ref[...] + ln1b_ref[...]
    xmid_ref[0] = xmid
    h2 = xmid * (1.0 + sc2_ref[0, 0]) + sh2_ref[0, 0]
    h2_ref[0] = h2.astype(BF16)
    logits = _dot_nt(wr_ref[...], h2, HIGHEST)
    e = jnp.exp(logits - jnp.max(logits, axis=0, keepdims=True))
    aff_ref[0] = e / jnp.sum(e, axis=0, keepdims=True)


def _merge(x_all, yda, y0, y1, bonus, g, ysg, gates, modall, w_branch, w_out, ln1g, ln1b,
           rlng, rlnb, ones_bd, w_router_t, ntl, alpha):
    B, TT, D = x_all.shape
    tm = TOKEN_TILE
    nt = TT // tm
    tok = lambda w: pl.BlockSpec((1, tm, w), lambda b, i: (b, i, 0))
    modspec = lambda j: pl.BlockSpec((1, 1, 1, D), lambda b, i: (b, i // ntl, 0, j))
    const = lambda a: pl.BlockSpec(a.shape, lambda b, i: (0,) * a.ndim)
    consts = [w_branch, w_out, ln1g, ln1b, rlng, rlnb, ones_bd, w_router_t]
    return pl.pallas_call(
        functools.partial(_merge_kernel, alpha=alpha),
        grid=(B, nt),
        in_specs=[tok(D), tok(DA_WIDTH), tok(RW_WIDTH), tok(RW_WIDTH), tok(RW_WIDTH), tok(RW_WIDTH),
                  tok(SG_WIDTH), tok(3 * D), modspec(2), modspec(3), modspec(4)]
                 + [const(a) for a in consts],
        out_specs=[tok(D), tok(D), pl.BlockSpec((1, N_EXPERTS, tm), lambda b, i: (b, 0, i))],
        out_shape=[jax.ShapeDtypeStruct((B, TT, D), F32),
                   jax.ShapeDtypeStruct((B, TT, D), BF16),
                   jax.ShapeDtypeStruct((B, N_EXPERTS, TT), F32)],
        compiler_params=pltpu.CompilerParams(
            dimension_semantics=("parallel", "parallel"), vmem_limit_bytes=VMEM_LIMIT_BYTES),
        name="merge_ln1_router",
    )(x_all, yda, y0, y1, bonus, g, ysg, gates, modall, modall, modall, *consts)


def _topk_kernel(aff_ref, tri_ref, rank_ref, *, cap):
    a = aff_ref[0]
    bits = pltpu.bitcast(a, jnp.int32)
    thr = jnp.zeros((a.shape[0], 1), jnp.int32)
    for bit in range(30, -1, -1):
        cand = thr | (1 << bit)
        cnt = jnp.sum(jnp.where(bits >= cand, 1.0, 0.0), axis=-1, keepdims=True)
        thr = jnp.where(cnt >= cap, cand, thr)
    gt = bits > thr
    eq = bits == thr
    need = cap - jnp.sum(jnp.where(gt, 1.0, 0.0), axis=-1, keepdims=True)
    tri = tri_ref[...]
    tie_rank = _dot(jnp.where(eq, 1.0, 0.0).astype(BF16), tri)
    sel = jnp.logical_or(gt, jnp.logical_and(eq, tie_rank < need))
    rank = _dot(jnp.where(sel, 1.0, 0.0).astype(BF16), tri)
    rank_ref[0] = jnp.where(sel, rank, -1.0).astype(jnp.int32)


def _topk_ranks(aff, tri, t_off, t_len, cap):
    B, E, _ = aff.shape
    blk = t_off // t_len
    return pl.pallas_call(
        functools.partial(_topk_kernel, cap=cap),
        grid=(B,),
        in_specs=[pl.BlockSpec((1, E, t_len), lambda b: (b, 0, blk)),
                  pl.BlockSpec(tri.shape, lambda b: (0, 0))],
        out_specs=pl.BlockSpec((1, E, t_len), lambda b: (b, 0, 0)),
        out_shape=jax.ShapeDtypeStruct((B, E, t_len), jnp.int32),
        compiler_params=pltpu.CompilerParams(
            dimension_semantics=("parallel",), vmem_limit_bytes=VMEM_LIMIT_BYTES),
        name="expert_choice_ranks",
    )(aff, tri)


def _moe_kernel(*refs, sets):
    n = len(sets)
    h_ref, aff_ref = refs[0], refs[1]
    rank_refs = refs[2:2 + n]
    wg_ref, wu_ref, wd_ref, f_ref = refs[2 + n:]
    e = pl.program_id(1)

    @pl.when(e == 0)
    def _():
        f_ref[...] = jnp.zeros_like(f_ref)

    onehots, gates, xs = [], [], []
    for (t0, tn, cap), rank_ref in zip(sets, rank_refs):
        rank = rank_ref[0, pl.ds(e, 1), :]
        aff = aff_ref[0, pl.ds(e, 1), t0:t0 + tn]
        slot = lax.broadcasted_iota(jnp.int32, (cap, tn), 0)
        hit = rank == slot
        onehot = jnp.where(hit, 1.0, 0.0).astype(BF16)
        onehots.append(onehot)
        gates.append(jnp.sum(jnp.where(hit, aff, 0.0), axis=-1, keepdims=True))
        xs.append(_dot(onehot, h_ref[0, t0:t0 + tn, :]).astype(BF16))
    xe = jnp.concatenate(xs, axis=0) if n > 1 else xs[0]
    gate = jnp.concatenate(gates, axis=0) if n > 1 else gates[0]
    hg = _dot(xe, wg_ref[0])
    hid = (hg * _sigmoid(hg)) * _dot(xe, wu_ref[0])
    ye = (_dot(hid.astype(BF16), wd_ref[0]) * gate).astype(BF16)
    r0 = 0
    for (t0, tn, cap), onehot in zip(sets, onehots):
        f_ref[0, t0:t0 + tn, :] += _dot_tn(onehot, ye[r0:r0 + cap])
        r0 += cap


def _expert_ffn(h2, aff, ranks, wg, wu, wd, sets):
    B, TT, D = h2.shape
    E = aff.shape[1]
    F = wg.shape[2]
    return pl.pallas_call(
        functools.partial(_moe_kernel, sets=sets),
        grid=(B, E),
        in_specs=[pl.BlockSpec((1, TT, D), lambda b, e: (b, 0, 0), pipeline_mode=pl.Buffered(1)),
                  pl.BlockSpec((1, E, TT), lambda b, e: (b, 0, 0))]
                 + [pl.BlockSpec((1, E, r.shape[2]), lambda b, e: (b, 0, 0)) for r in ranks]
                 + [pl.BlockSpec((1, D, F), lambda b, e: (e, 0, 0)),
                    pl.BlockSpec((1, D, F), lambda b, e: (e, 0, 0)),
                    pl.BlockSpec((1, F, D), lambda b, e: (e, 0, 0))],
        out_specs=pl.BlockSpec((1, TT, D), lambda b, e: (b, 0, 0)),
        out_shape=jax.ShapeDtypeStruct((B, TT, D), F32),
        compiler_params=pltpu.CompilerParams(
            dimension_semantics=("parallel", "arbitrary"), vmem_limit_bytes=VMEM_LIMIT_BYTES),
        name="expert_ffn",
    )(h2, aff, *ranks, wg, wu, wd)


def _ln2_kernel(x_ref, f_ref, g2_ref, lng_ref, lnb_ref, o_ref, *, alpha):
    z = alpha * x_ref[0] + g2_ref[0, 0] * f_ref[0]
    zm = jnp.mean(z, axis=-1, keepdims=True)
    dz = z - zm
    zv = jnp.mean(dz * dz, axis=-1, keepdims=True)
    o_ref[0] = dz * lax.rsqrt(zv + LN_EPS) * lng_ref[...] + lnb_ref[...]


def _final_norm(xmid, f, modall, lng, lnb, ntl, alpha):
    B, TT, D = xmid.shape
    tm = TOKEN_TILE
    tok = pl.BlockSpec((1, tm, D), lambda b, i: (b, i, 0))
    const = lambda a: pl.BlockSpec(a.shape, lambda b, i: (0,) * a.ndim)
    return pl.pallas_call(
        functools.partial(_ln2_kernel, alpha=alpha),
        grid=(B, TT // tm),
        in_specs=[tok, tok, pl.BlockSpec((1, 1, 1, D), lambda b, i: (b, i // ntl, 0, 5)),
                  const(lng), const(lnb)],
        out_specs=tok,
        out_shape=jax.ShapeDtypeStruct((B, TT, D), F32),
        compiler_params=pltpu.CompilerParams(dimension_semantics=("parallel", "parallel")),
        name="ln2",
    )(xmid, f, modall, lng, lnb)


def _rope_tables(t_lat, t_ctx):
    rows = t_lat // GRID_W
    row = jnp.repeat(jnp.arange(rows, dtype=F32), GRID_W)
    col = jnp.tile(jnp.arange(GRID_W, dtype=F32), rows)
    half = DA_QK_DIM // 2
    inv_freq = ROPE_BASE ** (-jnp.arange(0, half, 2, dtype=F32) / half)
    ar = row[:, None] * inv_freq
    ac = col[:, None] * inv_freq
    ang = jnp.concatenate([ar, ar, ac, ac], axis=-1)
    sign = jnp.where((jnp.arange(DA_QK_DIM) % 32) < 16, -1.0, 1.0).astype(F32)
    reps = DA_QK_COLS // DA_QK_DIM
    cos = jnp.tile(jnp.cos(ang), (1, reps))
    sin = jnp.tile(jnp.sin(ang) * sign, (1, reps))
    cos = jnp.concatenate([cos, jnp.ones((t_ctx, DA_QK_COLS), F32)], axis=0)
    sin = jnp.concatenate([sin, jnp.zeros((t_ctx, DA_QK_COLS), F32)], axis=0)
    return cos, sin


def kernel(x, c, ctx, c_ctx, w_mod, b_mod, w_in, da_lambda, da_norm_g, rw_shift_mu, rw_w0, rw_w2, rw_a0, rw_a2, rw_k_k, rw_k_a, rw_r_k, rw_ln_g, rw_ln_b, rw_g2, sg_norm_g, sg_norm_b, sg_w, sg_b, w_branch, w_out, ln1_g, ln1_b, w_router, w_e_gate, w_e_up, w_e_down, ln2_g, ln2_b):
    B, T, D = x.shape
    Tc = ctx.shape[1]
    depth = w_mod.shape[0]
    tm = TOKEN_TILE
    assert D == D_MODEL and T % tm == 0 and Tc % tm == 0 and T % Tc == 0 and Tc <= T
    ntl = T // tm
    alpha = (2 * depth) ** 0.25
    cap_lat = EC_CAPACITY * T // N_EXPERTS
    cap_ctx = EC_CAPACITY * Tc // N_EXPERTS

    cos, sin = _rope_tables(T, Tc)
    lane = jnp.arange(RW_WIDTH)
    ones_bd = (lane[:, None] // RW_HEAD == lane[None, :] // RW_HEAD).astype(BF16)
    tri_lat = (jnp.arange(T)[:, None] < jnp.arange(T)[None, :]).astype(BF16)
    tri_ctx = tri_lat[:Tc, :Tc]
    rows = ((B + 1 + 7) // 8) * 8
    cc = jnp.concatenate([c, c_ctx[None, :], jnp.zeros((rows - B - 1, D), F32)], axis=0)
    row2 = lambda a: a.reshape(1, -1)

    x_all = jnp.concatenate([x, ctx], axis=1)
    for l in range(depth):
        last = l == depth - 1
        lam_init = 0.8 - 0.6 * math.exp(-0.3 * l)
        mod = _modulation(cc, w_mod[l], b_mod[l])
        modall = jnp.stack([mod[:B], jnp.broadcast_to(mod[B], (B, 6 * D))], axis=1)
        modall = modall.reshape(B, 2, 1, 6 * D)

        sgbias = jnp.repeat(sg_b[l].T, SG_WIDTH // SG_GROUPS, axis=1)
        q, k, v, prw, ysg, gates = _input_projection(
            x_all, modall, w_in[l].astype(BF16), cos, sin, row2(sg_norm_g[l]), row2(sg_norm_b[l]),
            sg_w[l].astype(BF16), sgbias, ntl)

        yda = _diff_attention(q, k, v, da_lambda[l], da_norm_g[l], ntl, lam_init)

        cat2 = lambda a: jnp.transpose(a, (1, 0, 2)).reshape(a.shape[1], 2 * RW_WIDTH)
        feats = _rwkv_features(
            prw, rw_shift_mu[l], row2(rw_w0[l]), cat2(rw_w2[l]).astype(BF16), row2(rw_a0[l]),
            cat2(rw_a2[l]).astype(BF16), row2(rw_k_k[l]), row2(rw_k_a[l]), row2(rw_r_k[l]),
            rw_g2[l].astype(BF16), ones_bd, ntl)
        lw0, lw1, kd0, kd1, b0, b1, kk, vv, rr, bonus, gg = feats
        y0, y1 = _rwkv_scan(lw0, lw1, kd0, kd1, b0, b1, kk, vv, rr, T)

        xmid, h2, aff = _merge(
            x_all, yda, y0, y1, bonus, gg, ysg, gates, modall, w_branch[l].astype(BF16),
            w_out[l].astype(BF16), row2(ln1_g[l]), row2(ln1_b[l]), row2(rw_ln_g[l]),
            row2(rw_ln_b[l]), ones_bd, w_router[l].T, ntl, alpha)

        wg = w_e_gate[l].astype(BF16)
        wu = w_e_up[l].astype(BF16)
        wd = w_e_down[l].astype(BF16)
        sets = ((0, T, cap_lat),) if last else ((0, T, cap_lat), (T, Tc, cap_ctx))
        ranks = [_topk_ranks(aff, tri_lat, 0, T, cap_lat)]
        if not last:
            ranks.append(_topk_ranks(aff, tri_ctx, T, Tc, cap_ctx))
        f = _expert_ffn(h2, aff, ranks, wg, wu, wd, sets)
        x_all = _final_norm(xmid, f, modall, row2(ln2_g[l]), row2(ln2_b[l]), ntl, alpha)
    return x_all[:, :T]
```

```python
import functools
import math

import jax
import jax.numpy as jnp
from jax import lax
from jax.experimental import pallas as pl
from jax.experimental.pallas import tpu as pltpu

F32 = jnp.float32
BF16 = jnp.bfloat16
HIGHEST = lax.Precision.HIGHEST

D_MODEL = 1024
GRID_W = 64
DA_HEADS = 4
DA_QK_DIM = 64
DA_V_DIM = 128
DA_WIDTH = 512
DA_QK_COLS = 512
ROPE_BASE = 10000.0
DA_EPS = 1e-5
RW_HEAD = 64
RW_HEADS = 8
RW_WIDTH = 512
RW_COLS = 1792
RW_GN_EPS = 64e-5
SG_CHUNK = 128
SG_GROUPS = 4
SG_WIDTH = 512
MIX_WIDTH = 1536
DA_K0 = 512
DA_V0 = 1024
RW_0 = 1536
SG_0 = RW_0 + RW_COLS
GATE_0 = SG_0 + 2 * SG_WIDTH
IN_COLS = GATE_0 + 3 * D_MODEL
N_EXPERTS = 16
EC_CAPACITY = 2
LN_EPS = 1e-5
LOG2_E = math.log2(math.e)

TOKEN_TILE = 256
SCAN_CHUNK = 64
ATTN_KEY_TILE = 256
VMEM_LIMIT_BYTES = 58 * 1024 * 1024


def _dot(a, b):
    return jnp.dot(a, b, preferred_element_type=F32)


def _dot_hi(a, b):
    return jnp.dot(a, b, preferred_element_type=F32, precision=HIGHEST)


def _dot_nt(a, b, precision=None):
    return lax.dot_general(a, b, (((1,), (1,)), ((), ())), preferred_element_type=F32,
                           precision=precision)


def _dot_tn(a, b, precision=None):
    return lax.dot_general(a, b, (((0,), (0,)), ((), ())), preferred_element_type=F32,
                           precision=precision)


def _sigmoid(z):
    return 1.0 / (1.0 + jnp.exp(-z))


def _seg_sum(z, ones_bd):
    hi = z.astype(BF16)
    lo = (z - hi.astype(F32)).astype(BF16)
    return _dot(hi, ones_bd) + _dot(lo, ones_bd)


def _mod_kernel(c_ref, w_ref, b_ref, o_ref):
    cc = c_ref[...]
    o_ref[...] = _dot_hi(cc * _sigmoid(cc), w_ref[...]) + b_ref[...]


def _modulation(cc, w_mod, b_mod):
    rows, d = cc.shape
    n = w_mod.shape[1]
    tn = 1024
    return pl.pallas_call(
        _mod_kernel,
        grid=(n // tn,),
        in_specs=[pl.BlockSpec((rows, d), lambda j: (0, 0)),
                  pl.BlockSpec((d, tn), lambda j: (0, j)),
                  pl.BlockSpec((1, tn), lambda j: (0, j))],
        out_specs=pl.BlockSpec((rows, tn), lambda j: (0, j)),
        out_shape=jax.ShapeDtypeStruct((rows, n), F32),
        name="adaln_mod",
    )(cc, w_mod, b_mod.reshape(1, n))


def _inproj_kernel(x_ref, sh_ref, sc_ref, w_ref, cos_ref, sin_ref, sgg_ref, sgb_ref, sgw_ref,
                   sgbias_ref, q_ref, k_ref, v_ref, rw_ref, sg_ref, gate_ref):
    tm = x_ref.shape[1]
    h = (x_ref[0] * (1.0 + sc_ref[0, 0]) + sh_ref[0, 0]).astype(BF16)

    def proj(c0, c1):
        return _dot(h, w_ref[:, c0:c1])

    cos = cos_ref[...]
    sin = sin_ref[...]
    lane = lax.broadcasted_iota(jnp.int32, (tm, DA_QK_COLS), 1)
    first = (lane % 32) < 16

    def rope(z):
        zr = jnp.where(first, pltpu.roll(z, DA_QK_COLS - 16, 1), pltpu.roll(z, 16, 1))
        return z * cos + zr * sin

    q_ref[0] = (rope(proj(0, DA_K0)) * (DA_QK_DIM ** -0.5 * LOG2_E)).astype(BF16)
    k_ref[0] = rope(proj(DA_K0, DA_V0)).astype(BF16)
    v_ref[0] = proj(DA_V0, RW_0).astype(BF16)
    for c0 in range(0, RW_COLS, 896):
        rw_ref[0, :, c0:c0 + 896] = proj(RW_0 + c0, RW_0 + c0 + 896)
    for j in range(3):
        gate_ref[0, :, j * D_MODEL:(j + 1) * D_MODEL] = _sigmoid(
            proj(GATE_0 + j * D_MODEL, GATE_0 + (j + 1) * D_MODEL)).astype(BF16)

    ps = proj(SG_0, GATE_0)
    gl = ps * (0.5 * (1.0 + jnp.tanh(math.sqrt(2.0 / math.pi) * (ps + 0.044715 * (ps * ps * ps)))))
    u = gl[:, :SG_WIDTH]
    vv = gl[:, SG_WIDTH:]
    mu = jnp.mean(vv, axis=-1, keepdims=True)
    dv = vv - mu
    var = jnp.mean(dv * dv, axis=-1, keepdims=True)
    vn = (dv * lax.rsqrt(var + LN_EPS) * sgg_ref[...] + sgb_ref[...]).astype(BF16)
    gd = SG_WIDTH // SG_GROUPS
    for n in range(tm // SG_CHUNK):
        r0 = n * SG_CHUNK
        for g in range(SG_GROUPS):
            c0 = g * gd
            vm = _dot(sgw_ref[g], vn[r0:r0 + SG_CHUNK, c0:c0 + gd]) + sgbias_ref[:, c0:c0 + gd]
            sg_ref[0, r0:r0 + SG_CHUNK, c0:c0 + gd] = (u[r0:r0 + SG_CHUNK, c0:c0 + gd] * vm).astype(BF16)


def _input_projection(x_all, modall, w_in, cos, sin, sgg, sgb, sgw, sgbias, ntl):
    B, TT, D = x_all.shape
    tm = TOKEN_TILE
    nt = TT // tm
    tok = lambda w: pl.BlockSpec((1, tm, w), lambda b, i: (b, i, 0))
    modspec = lambda j: pl.BlockSpec((1, 1, 1, D), lambda b, i: (b, i // ntl, 0, j))
    const2 = lambda a: pl.BlockSpec(a.shape, lambda b, i: (0, 0))
    return pl.pallas_call(
        _inproj_kernel,
        grid=(B, nt),
        in_specs=[tok(D), modspec(0), modspec(1),
                  pl.BlockSpec(w_in.shape, lambda b, i: (0, 0), pipeline_mode=pl.Buffered(1)),
                  pl.BlockSpec((tm, DA_QK_COLS), lambda b, i: (i, 0)),
                  pl.BlockSpec((tm, DA_QK_COLS), lambda b, i: (i, 0)),
                  const2(sgg), const2(sgb),
                  pl.BlockSpec(sgw.shape, lambda b, i: (0, 0, 0)),
                  const2(sgbias)],
        out_specs=[tok(DA_QK_COLS), tok(DA_QK_COLS), tok(DA_WIDTH), tok(RW_COLS), tok(SG_WIDTH),
                   tok(3 * D)],
        out_shape=[jax.ShapeDtypeStruct((B, TT, DA_QK_COLS), BF16),
                   jax.ShapeDtypeStruct((B, TT, DA_QK_COLS), BF16),
                   jax.ShapeDtypeStruct((B, TT, DA_WIDTH), BF16),
                   jax.ShapeDtypeStruct((B, TT, RW_COLS), F32),
                   jax.ShapeDtypeStruct((B, TT, SG_WIDTH), BF16),
                   jax.ShapeDtypeStruct((B, TT, 3 * D), BF16)],
        compiler_params=pltpu.CompilerParams(
            dimension_semantics=("parallel", "parallel"), vmem_limit_bytes=VMEM_LIMIT_BYTES),
        name="in_proj",
    )(x_all, modall, modall, w_in, cos, sin, sgg, sgb, sgw, sgbias)


def _attn_kernel(q_ref, k_ref, v_ref, lam_ref, g_ref, o_ref, *, ntl, t_lat, lam_init):
    i = pl.program_id(2)
    lp = lam_ref[...]
    lam = (jnp.exp(jnp.sum(lp[0:1] * lp[1:2], axis=-1, keepdims=True))
           - jnp.exp(jnp.sum(lp[2:3] * lp[3:4], axis=-1, keepdims=True)) + lam_init)
    q = q_ref[0]
    lane = lax.broadcasted_iota(jnp.int32, q.shape, 1)
    zero = jnp.zeros_like(q)
    q0 = jnp.where(lane < DA_QK_DIM, q, zero)
    q1 = jnp.where(lane >= DA_QK_DIM, q, zero)

    def attend(k0, nk):
        kt = ATTN_KEY_TILE
        tiles = [slice(k0 + t * kt, k0 + (t + 1) * kt) for t in range(nk // kt)]
        ks = [k_ref[0, sl, :] for sl in tiles]

        def row_max(ss):
            m = ss[0]
            for s in ss[1:]:
                m = jnp.maximum(m, s)
            return jnp.max(m, axis=-1, keepdims=True)

        def pv(es):
            acc = None
            for e, sl in zip(es, tiles):
                v = v_ref[0, sl, :]
                d = _dot(e, jnp.concatenate([v, jnp.ones_like(v)], axis=1))
                acc = d if acc is None else acc + d
            return acc[:, :DA_V_DIM] / acc[:, DA_V_DIM:]

        s0 = [_dot_nt(q0, kk) for kk in ks]
        m0 = row_max(s0)
        s1, e0 = [], []
        for kk, s in zip(ks, s0):
            s1.append(_dot_nt(q1, kk))
            e0.append(jnp.exp2(s - m0).astype(BF16))
        m1 = row_max(s1)
        o0 = pv(e0)
        e1 = [jnp.exp2(s - m1).astype(BF16) for s in s1]
        o = o0 - lam * pv(e1)
        o = o * lax.rsqrt(jnp.mean(o * o, axis=-1, keepdims=True) + DA_EPS) * g_ref[...]
        o_ref[0] = (o * (1.0 - lam_init)).astype(BF16)

    @pl.when(i < ntl)
    def _():
        attend(0, k_ref.shape[1])

    @pl.when(i >= ntl)
    def _():
        attend(t_lat, k_ref.shape[1] - t_lat)


def _diff_attention(q, k, v, lam_p, norm_g, ntl, lam_init):
    B, TT, _ = q.shape
    tm = TOKEN_TILE
    nt = TT // tm
    kv = pl.BlockSpec((1, TT, DA_V_DIM), lambda b, h, i: (b, 0, h))
    qo = pl.BlockSpec((1, tm, DA_V_DIM), lambda b, h, i: (b, i, h))
    return pl.pallas_call(
        functools.partial(_attn_kernel, ntl=ntl, t_lat=ntl * tm, lam_init=lam_init),
        grid=(B, DA_HEADS, nt),
        in_specs=[qo, kv, kv,
                  pl.BlockSpec(lam_p.shape, lambda b, h, i: (0, 0)),
                  pl.BlockSpec((1, DA_V_DIM), lambda b, h, i: (0, 0))],
        out_specs=qo,
        out_shape=jax.ShapeDtypeStruct((B, TT, DA_WIDTH), BF16),
        compiler_params=pltpu.CompilerParams(
            dimension_semantics=("parallel", "parallel", "parallel"),
            vmem_limit_bytes=VMEM_LIMIT_BYTES),
        name="diff_attn",
    )(q, k, v, lam_p, norm_g.reshape(1, DA_V_DIM))


def _rwfeat_kernel(p_ref, pp_ref, pn_ref, mu_ref, w0_ref, w2_ref, a0_ref, a2_ref, kk_ref, ka_ref,
                   rk_ref, g2_ref, ones_ref,
                   lw0_ref, lw1_ref, kd0_ref, kd1_ref, b0_ref, b1_ref, kko_ref, v_ref, r_ref,
                   bonus_ref, g_ref, *, ntl, nt):
    i = pl.program_id(1)
    tm = p_ref.shape[1]
    p = p_ref[0]
    prev_ok = jnp.logical_and(i != 0, i != ntl)
    next_ok = jnp.logical_and(i != ntl - 1, i != nt - 1)
    prev_row = jnp.where(prev_ok, pp_ref[0, 7:8, :], 0.0)
    next_row = jnp.where(next_ok, pn_ref[0, 0:1, :], 0.0)
    row = lax.broadcasted_iota(jnp.int32, p.shape, 0)
    prev = jnp.where(row == 0, prev_row, pltpu.roll(p, 1, 0))
    nxt = jnp.where(row == tm - 1, next_row, pltpu.roll(p, tm - 1, 0))
    ps = p + mu_ref[0:1, :] * (prev - p) + mu_ref[1:2, :] * (nxt - p)

    W = RW_WIDTH
    r = ps[:, 0:W]
    k = ps[:, W:2 * W]
    v = ps[:, 2 * W:3 * W]
    xw = ps[:, 3 * W:3 * W + 64]
    xa = ps[:, 3 * W + 64:3 * W + 128]
    xg = ps[:, 3 * W + 128:3 * W + 256]
    ones_bd = ones_ref[...]

    w_pre = _dot(jnp.tanh(xw).astype(BF16), w2_ref[...]) + w0_ref[...]
    sp = jnp.maximum(-w_pre, 0.0) + jnp.log(1.0 + jnp.exp(-jnp.abs(w_pre)))
    logw = -jnp.exp(-sp - 0.5)
    a = _sigmoid(_dot(xa.astype(BF16), a2_ref[...]) + a0_ref[...])

    kx = k * kk_ref[...]
    nrm = jnp.sqrt(_seg_sum(kx * kx, ones_bd))
    kk = kx / jnp.maximum(nrm, 1e-12)
    ka = ka_ref[...]
    kd0 = k * (1.0 + (a[:, :W] - 1.0) * ka)
    kd1 = k * (1.0 + (a[:, W:] - 1.0) * ka)
    g = _dot(_sigmoid(xg).astype(BF16), g2_ref[...])
    k_b = 0.5 * (kd0 + kd1)
    bonus = _seg_sum(r * k_b * rk_ref[...], ones_bd) * v

    lw0_ref[0] = logw[:, :W]
    lw1_ref[0] = logw[:, W:]
    kd0_ref[0] = kd0.astype(BF16)
    kd1_ref[0] = kd1.astype(BF16)
    b0_ref[0] = (kk * a[:, :W]).astype(BF16)
    b1_ref[0] = (kk * a[:, W:]).astype(BF16)
    kko_ref[0] = kk.astype(BF16)
    v_ref[0] = v.astype(BF16)
    r_ref[0] = r.astype(BF16)
    bonus_ref[0] = bonus.astype(BF16)
    g_ref[0] = g.astype(BF16)


def _rwkv_features(prw, mu, w0, w2, a0, a2, k_k, k_a, r_k, g2, ones_bd, ntl):
    B, TT, _ = prw.shape
    tm = TOKEN_TILE
    nt = TT // tm
    r8 = tm // 8
    last8 = TT // 8 - 1
    W = RW_WIDTH
    const = lambda a: pl.BlockSpec(a.shape, lambda b, i: (0,) * a.ndim)
    tok = pl.BlockSpec((1, tm, W), lambda b, i: (b, i, 0))
    small = [mu, w0, w2, a0, a2, k_k, k_a, r_k, g2, ones_bd]
    return pl.pallas_call(
        functools.partial(_rwfeat_kernel, ntl=ntl, nt=nt),
        grid=(B, nt),
        in_specs=[pl.BlockSpec((1, tm, RW_COLS), lambda b, i: (b, i, 0)),
                  pl.BlockSpec((1, 8, RW_COLS), lambda b, i: (b, jnp.maximum(i * r8 - 1, 0), 0)),
                  pl.BlockSpec((1, 8, RW_COLS), lambda b, i: (b, jnp.minimum((i + 1) * r8, last8), 0))]
                 + [const(a) for a in small],
        out_specs=[tok] * 11,
        out_shape=[jax.ShapeDtypeStruct((B, TT, W), F32)] * 2 + [jax.ShapeDtypeStruct((B, TT, W), BF16)] * 9,
        compiler_params=pltpu.CompilerParams(
            dimension_semantics=("parallel", "parallel"), vmem_limit_bytes=VMEM_LIMIT_BYTES),
        name="rwkv_features",
    )(prw, prw, prw, *small)


def _split_bf16(x, pieces):
    out = []
    for _ in range(pieces - 1):
        hi = x.astype(BF16)
        out.append(hi)
        x = x - hi.astype(F32)
    out.append(x.astype(BF16))
    return out


_NN = ((1,), (0,))
_NT = ((1,), (1,))
_TN = ((0,), (0,))


def _mm(a, b, passes, dims=_NN):
    dn = (dims, ((), ()))
    if passes == 6:
        return lax.dot_general(a, b, dn, preferred_element_type=F32, precision=HIGHEST)
    if passes == 1:
        return lax.dot_general(a.astype(BF16), b.astype(BF16), dn, preferred_element_type=F32)
    a_hi, a_lo = _split_bf16(a, 2)
    b_hi, b_lo = _split_bf16(b, 2)
    d = lambda p, q: lax.dot_general(p, q, dn, preferred_element_type=F32)
    return d(a_hi, b_hi) + (d(a_hi, b_lo) + d(a_lo, b_hi))


def _scan_kernel(lw0, kd0, b0, kkf, vf, rf, lw1, kd1, b1, kkr, vr, rr, y0_ref, y1_ref, s_ref, *, prec):
    p_aa, p_inv, p_gh, p_w, p_u, p_y, p_s = prec
    step = pl.program_id(1)

    @pl.when(step == 0)
    def _():
        s_ref[...] = jnp.zeros_like(s_ref)

    C = lw0.shape[1]
    N = RW_HEAD
    ti = lax.broadcasted_iota(jnp.int32, (C, C), 0)
    si = lax.broadcasted_iota(jnp.int32, (C, C), 1)
    ti2 = lax.broadcasted_iota(jnp.int32, (C, 2 * C), 0)
    si2 = lax.broadcasted_iota(jnp.int32, (C, 2 * C), 1) % C

    heads = []
    for z, (lw, kd, b, kk, v, r) in enumerate(((lw0, kd0, b0, kkf, vf, rf), (lw1, kd1, b1, kkr, vr, rr))):
        rev = z == 1
        incl = (si >= ti) if rev else (si <= ti)
        strict = (si > ti) if rev else (si < ti)
        incl2 = (si2 >= ti2) if rev else (si2 <= ti2)
        logw = lw[0]
        tri = jnp.where(incl, 1.0, 0.0).astype(BF16)
        cum = sum(_dot(tri, piece) for piece in _split_bf16(logw, 3))
        tot = jnp.sum(logw, axis=0, keepdims=True)
        p_in = jnp.exp(-cum)
        p_end = jnp.exp(tot - cum)
        kkz, rz, bz, kdz = (t[0].astype(F32) for t in (kk, r, b, kd))
        lhs = jnp.concatenate([-kkz * jnp.exp(cum - logw), rz * jnp.exp(cum)], axis=0)
        rhs = jnp.concatenate([bz * p_in, kdz * p_in], axis=0)
        upd = jnp.concatenate([bz * p_end, kdz * p_end], axis=0)
        dec = jnp.exp(tot)
        vz = v[0].astype(F32)
        for h in range(RW_HEADS):
            c = slice(h * N, (h + 1) * N)
            heads.append(dict(z=z, h=h, lhs=lhs[:, c], rhs=rhs[:, c], upd=upd[:, c], dec=dec[:, c],
                              v=vz[:, c], strict=strict, incl2=incl2))

    for g in heads:
        aa = _mm(g["lhs"], g["rhs"], p_aa, _NT)
        g["a_ab"] = jnp.where(g["strict"], aa[:C, :C], 0.0)
        g["a_ak"] = jnp.where(g["strict"], aa[:C, C:], 0.0)
        g["a_r"] = jnp.where(g["incl2"], aa[C:, :], 0.0)
        g["n"] = g["a_ab"]
        g["apow"] = g["a_ab"]
    for _ in range(int(math.log2(C)) - 1):
        for g in heads:
            g["apow"] = _mm(g["apow"], g["apow"], p_inv)
        for g in heads:
            g["n"] = g["n"] + g["apow"] + _mm(g["n"], g["apow"], p_inv)
    for g in heads:
        g["s0"] = s_ref[g["z"], g["h"]]
        g["gh"] = _mm(g["lhs"], g["s0"], p_gh, _NT)
    for g in heads:
        g["w"] = g["gh"][:C] + _mm(g["a_ak"], g["v"], p_w)
    for g in heads:
        g["uv"] = jnp.concatenate([g["w"] + _mm(g["n"], g["w"], p_u), g["v"]], axis=0)
    for g in heads:
        g["y"] = g["gh"][C:] + _mm(g["a_r"], g["uv"], p_y)
    for g in heads:
        s_ref[g["z"], g["h"]] = g["s0"] * g["dec"] + _mm(g["uv"], g["upd"], p_s, _TN)
    y0_ref[0] = jnp.concatenate([g["y"] for g in heads[:RW_HEADS]], axis=-1)
    y1_ref[0] = jnp.concatenate([g["y"] for g in heads[RW_HEADS:]], axis=-1)


SCAN_PRECISION = (1, 1, 1, 1, 1, 1, 1)


def _rwkv_scan(lw0, lw1, kd0, kd1, b0, b1, kk, v, r, t_lat, prec=SCAN_PRECISION):
    B, TT, W = kk.shape
    C = SCAN_CHUNK
    ncl = t_lat // C
    nch = TT // C
    ncc = nch - ncl

    def fwd(s):
        return jnp.where(s < ncc, ncl + s, s - ncc)

    fs = pl.BlockSpec((1, C, W), lambda b, s: (b, fwd(s), 0))
    rs = pl.BlockSpec((1, C, W), lambda b, s: (b, nch - 1 - s, 0))
    return pl.pallas_call(
        functools.partial(_scan_kernel, prec=prec),
        grid=(B, nch),
        in_specs=[fs] * 6 + [rs] * 6,
        out_specs=[fs, rs],
        out_shape=[jax.ShapeDtypeStruct((B, TT, W), F32)] * 2,
        scratch_shapes=[pltpu.VMEM((2, RW_HEADS, RW_HEAD, RW_HEAD), F32)],
        compiler_params=pltpu.CompilerParams(
            dimension_semantics=("parallel", "arbitrary"), vmem_limit_bytes=VMEM_LIMIT_BYTES),
        name="rwkv_scan",
    )(lw0, kd0, b0, kk, v, r, lw1, kd1, b1, kk, v, r)


def _merge_kernel(x_ref, yda_ref, y0_ref, y1_ref, bonus_ref, g_ref, ysg_ref, gate_ref,
                  g1_ref, sh2_ref, sc2_ref, wb_ref, wo_ref, ln1g_ref, ln1b_ref, rlng_ref, rlnb_ref,
                  ones_ref, wr_ref, xmid_ref, h2_ref, aff_ref, *, alpha):
    D = D_MODEL
    ones_bd = ones_ref[...]
    y = y0_ref[0] + y1_ref[0]
    mu = _seg_sum(y, ones_bd) * (1.0 / RW_HEAD)
    dy = y - mu
    var = _seg_sum(dy * dy, ones_bd) * (1.0 / RW_HEAD)
    gn = dy * lax.rsqrt(var + RW_GN_EPS) * rlng_ref[...] + rlnb_ref[...]
    yrw = ((gn + bonus_ref[0].astype(F32)) * g_ref[0].astype(F32)).astype(BF16)

    m = gate_ref[0, :, 0:D].astype(F32) * _dot(yda_ref[0], wb_ref[0:DA_WIDTH, :])
    m = m + gate_ref[0, :, D:2 * D].astype(F32) * _dot(yrw, wb_ref[DA_WIDTH:DA_WIDTH + RW_WIDTH, :])
    m = m + gate_ref[0, :, 2 * D:3 * D].astype(F32) * _dot(ysg_ref[0], wb_ref[DA_WIDTH + RW_WIDTH:, :])
    mix = _dot(m.astype(BF16), wo_ref[...])

    z = alpha * x_ref[0] + g1_ref[0, 0] * mix
    zm = jnp.mean(z, axis=-1, keepdims=True)
    dz = z - zm
    zv = jnp.mean(dz * dz, axis=-1, keepdims=True)
    xmid = dz * lax.rsqrt(zv + LN_EPS) * ln1g_ref[...] + ln1b_ref[...]
    xmid_ref[0] = xmid
    h2 = xmid * (1.0 + sc2_ref[0, 0]) + sh2_ref[0, 0]
    h2_ref[0] = h2.astype(BF16)
    logits = _dot_nt(wr_ref[...], h2, HIGHEST)
    e = jnp.exp(logits - jnp.max(logits, axis=0, keepdims=True))
    aff_ref[0] = e / jnp.sum(e, axis=0, keepdims=True)


def _merge(x_all, yda, y0, y1, bonus, g, ysg, gates, modall, w_branch, w_out, ln1g, ln1b,
           rlng, rlnb, ones_bd, w_router_t, ntl, alpha):
    B, TT, D = x_all.shape
    tm = TOKEN_TILE
    nt = TT // tm
    tok = lambda w: pl.BlockSpec((1, tm, w), lambda b, i: (b, i, 0))
    modspec = lambda j: pl.BlockSpec((1, 1, 1, D), lambda b, i: (b, i // ntl, 0, j))
    const = lambda a: pl.BlockSpec(a.shape, lambda b, i: (0,) * a.ndim)
    consts = [w_branch, w_out, ln1g, ln1b, rlng, rlnb, ones_bd, w_router_t]
    return pl.pallas_call(
        functools.partial(_merge_kernel, alpha=alpha),
        grid=(B, nt),
        in_specs=[tok(D), tok(DA_WIDTH), tok(RW_WIDTH), tok(RW_WIDTH), tok(RW_WIDTH), tok(RW_WIDTH),
                  tok(SG_WIDTH), tok(3 * D), modspec(2), modspec(3), modspec(4)]
                 + [const(a) for a in consts],
        out_specs=[tok(D), tok(D), pl.BlockSpec((1, N_EXPERTS, tm), lambda b, i: (b, 0, i))],
        out_shape=[jax.ShapeDtypeStruct((B, TT, D), F32),
                   jax.ShapeDtypeStruct((B, TT, D), BF16),
                   jax.ShapeDtypeStruct((B, N_EXPERTS, TT), F32)],
        compiler_params=pltpu.CompilerParams(
            dimension_semantics=("parallel", "parallel"), vmem_limit_bytes=VMEM_LIMIT_BYTES),
        name="merge_ln1_router",
    )(x_all, yda, y0, y1, bonus, g, ysg, gates, modall, modall, modall, *consts)


def _topk_kernel(aff_ref, tri_ref, rank_ref, *, cap):
    a = aff_ref[0]
    bits = pltpu.bitcast(a, jnp.int32)
    thr = jnp.zeros((a.shape[0], 1), jnp.int32)
    for bit in range(30, -1, -1):
        cand = thr | (1 << bit)
        cnt = jnp.sum(jnp.where(bits >= cand, 1.0, 0.0), axis=-1, keepdims=True)
        thr = jnp.where(cnt >= cap, cand, thr)
    gt = bits > thr
    eq = bits == thr
    need = cap - jnp.sum(jnp.where(gt, 1.0, 0.0), axis=-1, keepdims=True)
    tri = tri_ref[...]
    tie_rank = _dot(jnp.where(eq, 1.0, 0.0).astype(BF16), tri)
    sel = jnp.logical_or(gt, jnp.logical_and(eq, tie_rank < need))
    rank = _dot(jnp.where(sel, 1.0, 0.0).astype(BF16), tri)
    rank_ref[0] = jnp.where(sel, rank, -1.0).astype(jnp.int32)


def _topk_ranks(aff, tri, t_off, t_len, cap):
    B, E, _ = aff.shape
    blk = t_off // t_len
    return pl.pallas_call(
        functools.partial(_topk_kernel, cap=cap),
        grid=(B,),
        in_specs=[pl.BlockSpec((1, E, t_len), lambda b: (b, 0, blk)),
                  pl.BlockSpec(tri.shape, lambda b: (0, 0))],
        out_specs=pl.BlockSpec((1, E, t_len), lambda b: (b, 0, 0)),
        out_shape=jax.ShapeDtypeStruct((B, E, t_len), jnp.int32),
        compiler_params=pltpu.CompilerParams(
            dimension_semantics=("parallel",), vmem_limit_bytes=VMEM_LIMIT_BYTES),
        name="expert_choice_ranks",
    )(aff, tri)


def _moe_kernel(*refs, sets):
    n = len(sets)
    h_ref, aff_ref = refs[0], refs[1]
    rank_refs = refs[2:2 + n]
    wg_ref, wu_ref, wd_ref, f_ref = refs[2 + n:]
    e = pl.program_id(1)

    @pl.when(e == 0)
    def _():
        f_ref[...] = jnp.zeros_like(f_ref)

    onehots, gates, xs = [], [], []
    for (t0, tn, cap), rank_ref in zip(sets, rank_refs):
        rank = rank_ref[0, pl.ds(e, 1), :]
        aff = aff_ref[0, pl.ds(e, 1), t0:t0 + tn]
        slot = lax.broadcasted_iota(jnp.int32, (cap, tn), 0)
        hit = rank == slot
        onehot = jnp.where(hit, 1.0, 0.0).astype(BF16)
        onehots.append(onehot)
        gates.append(jnp.sum(jnp.where(hit, aff, 0.0), axis=-1, keepdims=True))
        xs.append(_dot(onehot, h_ref[0, t0:t0 + tn, :]).astype(BF16))
    xe = jnp.concatenate(xs, axis=0) if n > 1 else xs[0]
    gate = jnp.concatenate(gates, axis=0) if n > 1 else gates[0]
    hg = _dot(xe, wg_ref[0])
    hid = (hg * _sigmoid(hg)) * _dot(xe, wu_ref[0])
    ye = (_dot(hid.astype(BF16), wd_ref[0]) * gate).astype(BF16)
    r0 = 0
    for (t0, tn, cap), onehot in zip(sets, onehots):
        f_ref[0, t0:t0 + tn, :] += _dot_tn(onehot, ye[r0:r0 + cap])
        r0 += cap


def _expert_ffn(h2, aff, ranks, wg, wu, wd, sets):
    B, TT, D = h2.shape
    E = aff.shape[1]
    F = wg.shape[2]
    return pl.pallas_call(
        functools.partial(_moe_kernel, sets=sets),
        grid=(B, E),
        in_specs=[pl.BlockSpec((1, TT, D), lambda b, e: (b, 0, 0), pipeline_mode=pl.Buffered(1)),
                  pl.BlockSpec((1, E, TT), lambda b, e: (b, 0, 0))]
                 + [pl.BlockSpec((1, E, r.shape[2]), lambda b, e: (b, 0, 0)) for r in ranks]
                 + [pl.BlockSpec((1, D, F), lambda b, e: (e, 0, 0)),
                    pl.BlockSpec((1, D, F), lambda b, e: (e, 0, 0)),
                    pl.BlockSpec((1, F, D), lambda b, e: (e, 0, 0))],
        out_specs=pl.BlockSpec((1, TT, D), lambda b, e: (b, 0, 0)),
        out_shape=jax.ShapeDtypeStruct((B, TT, D), F32),
        compiler_params=pltpu.CompilerParams(
            dimension_semantics=("parallel", "arbitrary"), vmem_limit_bytes=VMEM_LIMIT_BYTES),
        name="expert_ffn",
    )(h2, aff, *ranks, wg, wu, wd)


def _ln2_kernel(x_ref, f_ref, g2_ref, lng_ref, lnb_ref, o_ref, *, alpha):
    z = alpha * x_ref[0] + g2_ref[0, 0] * f_ref[0]
    zm = jnp.mean(z, axis=-1, keepdims=True)
    dz = z - zm
    zv = jnp.mean(dz * dz, axis=-1, keepdims=True)
    o_ref[0] = dz * lax.rsqrt(zv + LN_EPS) * lng_ref[...] + lnb_ref[...]


def _final_norm(xmid, f, modall, lng, lnb, ntl, alpha, rows):
    B, _, D = xmid.shape
    tm = TOKEN_TILE
    tok = pl.BlockSpec((1, tm, D), lambda b, i: (b, i, 0))
    const = lambda a: pl.BlockSpec(a.shape, lambda b, i: (0,) * a.ndim)
    return pl.pallas_call(
        functools.partial(_ln2_kernel, alpha=alpha),
        grid=(B, rows // tm),
        in_specs=[tok, tok, pl.BlockSpec((1, 1, 1, D), lambda b, i: (b, i // ntl, 0, 5)),
                  const(lng), const(lnb)],
        out_specs=tok,
        out_shape=jax.ShapeDtypeStruct((B, rows, D), F32),
        compiler_params=pltpu.CompilerParams(dimension_semantics=("parallel", "parallel")),
        name="ln2",
    )(xmid, f, modall, lng, lnb)


def _rope_tables(t_lat, t_ctx):
    rows = t_lat // GRID_W
    row = jnp.repeat(jnp.arange(rows, dtype=F32), GRID_W)
    col = jnp.tile(jnp.arange(GRID_W, dtype=F32), rows)
    half = DA_QK_DIM // 2
    inv_freq = ROPE_BASE ** (-jnp.arange(0, half, 2, dtype=F32) / half)
    ar = row[:, None] * inv_freq
    ac = col[:, None] * inv_freq
    ang = jnp.concatenate([ar, ar, ac, ac], axis=-1)
    sign = jnp.where((jnp.arange(DA_QK_DIM) % 32) < 16, -1.0, 1.0).astype(F32)
    reps = DA_QK_COLS // DA_QK_DIM
    cos = jnp.tile(jnp.cos(ang), (1, reps))
    sin = jnp.tile(jnp.sin(ang) * sign, (1, reps))
    cos = jnp.concatenate([cos, jnp.ones((t_ctx, DA_QK_COLS), F32)], axis=0)
    sin = jnp.concatenate([sin, jnp.zeros((t_ctx, DA_QK_COLS), F32)], axis=0)
    return cos, sin


def kernel(x, c, ctx, c_ctx, w_mod, b_mod, w_in, da_lambda, da_norm_g, rw_shift_mu, rw_w0, rw_w2, rw_a0, rw_a2, rw_k_k, rw_k_a, rw_r_k, rw_ln_g, rw_ln_b, rw_g2, sg_norm_g, sg_norm_b, sg_w, sg_b, w_branch, w_out, ln1_g, ln1_b, w_router, w_e_gate, w_e_up, w_e_down, ln2_g, ln2_b):
    B, T, D = x.shape
    Tc = ctx.shape[1]
    depth = w_mod.shape[0]
    tm = TOKEN_TILE
    assert D == D_MODEL and T % tm == 0 and Tc % tm == 0 and T % Tc == 0 and Tc <= T
    ntl = T // tm
    alpha = (2 * depth) ** 0.25
    cap_lat = EC_CAPACITY * T // N_EXPERTS
    cap_ctx = EC_CAPACITY * Tc // N_EXPERTS

    cos, sin = _rope_tables(T, Tc)
    lane = jnp.arange(RW_WIDTH)
    ones_bd = (lane[:, None] // RW_HEAD == lane[None, :] // RW_HEAD).astype(BF16)
    tri_lat = (jnp.arange(T)[:, None] < jnp.arange(T)[None, :]).astype(BF16)
    tri_ctx = tri_lat[:Tc, :Tc]
    rows = ((B + 1 + 7) // 8) * 8
    cc = jnp.concatenate([c, c_ctx[None, :], jnp.zeros((rows - B - 1, D), F32)], axis=0)
    row2 = lambda a: a.reshape(1, -1)

    x_all = jnp.concatenate([x, ctx], axis=1)
    for l in range(depth):
        last = l == depth - 1
        lam_init = 0.8 - 0.6 * math.exp(-0.3 * l)
        mod = _modulation(cc, w_mod[l], b_mod[l])
        modall = jnp.stack([mod[:B], jnp.broadcast_to(mod[B], (B, 6 * D))], axis=1)
        modall = modall.reshape(B, 2, 1, 6 * D)

        sgbias = jnp.repeat(sg_b[l].T, SG_WIDTH // SG_GROUPS, axis=1)
        q, k, v, prw, ysg, gates = _input_projection(
            x_all, modall, w_in[l].astype(BF16), cos, sin, row2(sg_norm_g[l]), row2(sg_norm_b[l]),
            sg_w[l].astype(BF16), sgbias, ntl)

        yda = _diff_attention(q, k, v, da_lambda[l], da_norm_g[l], ntl, lam_init)

        cat2 = lambda a: jnp.transpose(a, (1, 0, 2)).reshape(a.shape[1], 2 * RW_WIDTH)
        feats = _rwkv_features(
            prw, rw_shift_mu[l], row2(rw_w0[l]), cat2(rw_w2[l]).astype(BF16), row2(rw_a0[l]),
            cat2(rw_a2[l]).astype(BF16), row2(rw_k_k[l]), row2(rw_k_a[l]), row2(rw_r_k[l]),
            rw_g2[l].astype(BF16), ones_bd, ntl)
        lw0, lw1, kd0, kd1, b0, b1, kk, vv, rr, bonus, gg = feats
        y0, y1 = _rwkv_scan(lw0, lw1, kd0, kd1, b0, b1, kk, vv, rr, T)

        xmid, h2, aff = _merge(
            x_all, yda, y0, y1, bonus, gg, ysg, gates, modall, w_branch[l].astype(BF16),
            w_out[l].astype(BF16), row2(ln1_g[l]), row2(ln1_b[l]), row2(rw_ln_g[l]),
            row2(rw_ln_b[l]), ones_bd, w_router[l].T, ntl, alpha)

        wg = w_e_gate[l].astype(BF16)
        wu = w_e_up[l].astype(BF16)
        wd = w_e_down[l].astype(BF16)
        sets = ((0, T, cap_lat),) if last else ((0, T, cap_lat), (T, Tc, cap_ctx))
        ranks = [_topk_ranks(aff, tri_lat, 0, T, cap_lat)]
        if not last:
            ranks.append(_topk_ranks(aff, tri_ctx, T, Tc, cap_ctx))
        f = _expert_ffn(h2, aff, ranks, wg, wu, wd, sets)
        x_all = _final_norm(xmid, f, modall, row2(ln2_g[l]), row2(ln2_b[l]), ntl, alpha,
                            T if last else T + Tc)
    return x_all
```

```python
import functools
import math

import jax
import jax.numpy as jnp
from jax import lax
from jax.experimental import pallas as pl
from jax.experimental.pallas import tpu as pltpu

F32 = jnp.float32
BF16 = jnp.bfloat16
HIGHEST = lax.Precision.HIGHEST

D_MODEL = 1024
GRID_W = 64
DA_HEADS = 4
DA_QK_DIM = 64
DA_V_DIM = 128
DA_WIDTH = 512
DA_QK_COLS = 512
ROPE_BASE = 10000.0
DA_EPS = 1e-5
RW_HEAD = 64
RW_HEADS = 8
RW_WIDTH = 512
RW_COLS = 1792
RW_GN_EPS = 64e-5
SG_CHUNK = 128
SG_GROUPS = 4
SG_WIDTH = 512
MIX_WIDTH = 1536
DA_K0 = 512
DA_V0 = 1024
RW_0 = 1536
SG_0 = RW_0 + RW_COLS
GATE_0 = SG_0 + 2 * SG_WIDTH
IN_COLS = GATE_0 + 3 * D_MODEL
N_EXPERTS = 16
EC_CAPACITY = 2
LN_EPS = 1e-5
LOG2_E = math.log2(math.e)

TOKEN_TILE = 256
SCAN_CHUNK = 64
ATTN_KEY_TILE = 256
SCAN_HEADS_PER_PASS = 4
SCAN_BATCH = 4
VMEM_LIMIT_BYTES = 58 * 1024 * 1024


def _dot(a, b):
    return jnp.dot(a, b, preferred_element_type=F32)


def _dot_hi(a, b):
    return jnp.dot(a, b, preferred_element_type=F32, precision=HIGHEST)


def _dot_nt(a, b, precision=None):
    return lax.dot_general(a, b, (((1,), (1,)), ((), ())), preferred_element_type=F32,
                           precision=precision)


def _dot_tn(a, b, precision=None):
    return lax.dot_general(a, b, (((0,), (0,)), ((), ())), preferred_element_type=F32,
                           precision=precision)


def _sigmoid(z):
    return 1.0 / (1.0 + jnp.exp(-z))


def _seg_sum(z, ones_bd):
    hi = z.astype(BF16)
    lo = (z - hi.astype(F32)).astype(BF16)
    return _dot(hi, ones_bd) + _dot(lo, ones_bd)


def _mod_kernel(c_ref, w_ref, b_ref, o_ref):
    cc = c_ref[...]
    o_ref[...] = _dot_hi(cc * _sigmoid(cc), w_ref[...]) + b_ref[...]


def _modulation(cc, w_mod, b_mod):
    rows, d = cc.shape
    n = w_mod.shape[1]
    tn = 1024
    return pl.pallas_call(
        _mod_kernel,
        grid=(n // tn,),
        in_specs=[pl.BlockSpec((rows, d), lambda j: (0, 0)),
                  pl.BlockSpec((d, tn), lambda j: (0, j)),
                  pl.BlockSpec((1, tn), lambda j: (0, j))],
        out_specs=pl.BlockSpec((rows, tn), lambda j: (0, j)),
        out_shape=jax.ShapeDtypeStruct((rows, n), F32),
        name="adaln_mod",
    )(cc, w_mod, b_mod.reshape(1, n))


def _inproj_kernel(x_ref, sh_ref, sc_ref, w_ref, cos_ref, sin_ref, sgg_ref, sgb_ref, sgw_ref,
                   sgbias_ref, q_ref, k_ref, v_ref, rw_ref, sg_ref, gate_ref):
    tm = x_ref.shape[1]
    h = (x_ref[0] * (1.0 + sc_ref[0, 0]) + sh_ref[0, 0]).astype(BF16)

    def proj(c0, c1):
        return _dot(h, w_ref[0, :, c0:c1])

    cos = cos_ref[...]
    sin = sin_ref[...]
    lane = lax.broadcasted_iota(jnp.int32, (tm, DA_QK_COLS), 1)
    first = (lane % 32) < 16

    def rope(z):
        zr = jnp.where(first, pltpu.roll(z, DA_QK_COLS - 16, 1), pltpu.roll(z, 16, 1))
        return z * cos + zr * sin

    q_ref[0] = (rope(proj(0, DA_K0)) * (DA_QK_DIM ** -0.5 * LOG2_E)).astype(BF16)
    k_ref[0] = rope(proj(DA_K0, DA_V0)).astype(BF16)
    v_ref[0] = proj(DA_V0, RW_0).astype(BF16)
    for c0 in range(0, RW_COLS, 896):
        rw_ref[0, :, c0:c0 + 896] = proj(RW_0 + c0, RW_0 + c0 + 896)
    for j in range(3):
        gate_ref[0, :, j * D_MODEL:(j + 1) * D_MODEL] = _sigmoid(
            proj(GATE_0 + j * D_MODEL, GATE_0 + (j + 1) * D_MODEL)).astype(BF16)

    ps = proj(SG_0, GATE_0)
    gl = ps * (0.5 * (1.0 + jnp.tanh(math.sqrt(2.0 / math.pi) * (ps + 0.044715 * (ps * ps * ps)))))
    u = gl[:, :SG_WIDTH]
    vv = gl[:, SG_WIDTH:]
    mu = jnp.mean(vv, axis=-1, keepdims=True)
    dv = vv - mu
    var = jnp.mean(dv * dv, axis=-1, keepdims=True)
    vn = (dv * lax.rsqrt(var + LN_EPS) * sgg_ref[...] + sgb_ref[...]).astype(BF16)
    gd = SG_WIDTH // SG_GROUPS
    for n in range(tm // SG_CHUNK):
        r0 = n * SG_CHUNK
        for g in range(SG_GROUPS):
            c0 = g * gd
            vm = _dot(sgw_ref[g], vn[r0:r0 + SG_CHUNK, c0:c0 + gd]) + sgbias_ref[:, c0:c0 + gd]
            sg_ref[0, r0:r0 + SG_CHUNK, c0:c0 + gd] = (u[r0:r0 + SG_CHUNK, c0:c0 + gd] * vm).astype(BF16)


def _input_projection(x_all, modall, w_in, layer, cos, sin, sgg, sgb, sgw, sgbias, ntl):
    B, TT, D = x_all.shape
    tm = TOKEN_TILE
    nt = TT // tm
    tok = lambda w: pl.BlockSpec((1, tm, w), lambda b, i: (b, i, 0))
    modspec = lambda j: pl.BlockSpec((1, 1, 1, D), lambda b, i: (b, i // ntl, 0, j))
    const2 = lambda a: pl.BlockSpec(a.shape, lambda b, i: (0, 0))
    return pl.pallas_call(
        _inproj_kernel,
        grid=(B, nt),
        in_specs=[tok(D), modspec(0), modspec(1),
                  pl.BlockSpec((1,) + w_in.shape[1:], lambda b, i: (layer, 0, 0),
                               pipeline_mode=pl.Buffered(1)),
                  pl.BlockSpec((tm, DA_QK_COLS), lambda b, i: (i, 0)),
                  pl.BlockSpec((tm, DA_QK_COLS), lambda b, i: (i, 0)),
                  const2(sgg), const2(sgb),
                  pl.BlockSpec(sgw.shape, lambda b, i: (0, 0, 0)),
                  const2(sgbias)],
        out_specs=[tok(DA_QK_COLS), tok(DA_QK_COLS), tok(DA_WIDTH), tok(RW_COLS), tok(SG_WIDTH),
                   tok(3 * D)],
        out_shape=[jax.ShapeDtypeStruct((B, TT, DA_QK_COLS), BF16),
                   jax.ShapeDtypeStruct((B, TT, DA_QK_COLS), BF16),
                   jax.ShapeDtypeStruct((B, TT, DA_WIDTH), BF16),
                   jax.ShapeDtypeStruct((B, TT, RW_COLS), F32),
                   jax.ShapeDtypeStruct((B, TT, SG_WIDTH), BF16),
                   jax.ShapeDtypeStruct((B, TT, 3 * D), BF16)],
        compiler_params=pltpu.CompilerParams(
            dimension_semantics=("parallel", "parallel"), vmem_limit_bytes=VMEM_LIMIT_BYTES),
        name="in_proj",
    )(x_all, modall, modall, w_in, cos, sin, sgg, sgb, sgw, sgbias)


def _attn_kernel(q_ref, k_ref, v_ref, lam_ref, g_ref, o_ref, *, ntl, t_lat, lam_init):
    i = pl.program_id(2)
    lp = lam_ref[...]
    lam = (jnp.exp(jnp.sum(lp[0:1] * lp[1:2], axis=-1, keepdims=True))
           - jnp.exp(jnp.sum(lp[2:3] * lp[3:4], axis=-1, keepdims=True)) + lam_init)
    q = q_ref[0]
    lane = lax.broadcasted_iota(jnp.int32, q.shape, 1)
    zero = jnp.zeros_like(q)
    q0 = jnp.where(lane < DA_QK_DIM, q, zero)
    q1 = jnp.where(lane >= DA_QK_DIM, q, zero)

    def attend(k0, nk):
        kt = ATTN_KEY_TILE
        tiles = [slice(k0 + t * kt, k0 + (t + 1) * kt) for t in range(nk // kt)]
        ks = [k_ref[0, sl, :] for sl in tiles]

        def row_max(ss):
            m = ss[0]
            for s in ss[1:]:
                m = jnp.maximum(m, s)
            return jnp.max(m, axis=-1, keepdims=True)

        def pv(es):
            acc = None
            for e, sl in zip(es, tiles):
                v = v_ref[0, sl, :]
                d = _dot(e, jnp.concatenate([v, jnp.ones_like(v)], axis=1))
                acc = d if acc is None else acc + d
            return acc[:, :DA_V_DIM] / acc[:, DA_V_DIM:]

        s0 = [_dot_nt(q0, kk) for kk in ks]
        m0 = row_max(s0)
        s1, e0 = [], []
        for kk, s in zip(ks, s0):
            s1.append(_dot_nt(q1, kk))
            e0.append(jnp.exp2(s - m0).astype(BF16))
        m1 = row_max(s1)
        o0 = pv(e0)
        e1 = [jnp.exp2(s - m1).astype(BF16) for s in s1]
        o = o0 - lam * pv(e1)
        o = o * lax.rsqrt(jnp.mean(o * o, axis=-1, keepdims=True) + DA_EPS) * g_ref[...]
        o_ref[0] = (o * (1.0 - lam_init)).astype(BF16)

    @pl.when(i < ntl)
    def _():
        attend(0, k_ref.shape[1])

    @pl.when(i >= ntl)
    def _():
        attend(t_lat, k_ref.shape[1] - t_lat)


def _diff_attention(q, k, v, lam_p, norm_g, ntl, lam_init):
    B, TT, _ = q.shape
    tm = TOKEN_TILE
    nt = TT // tm
    kv = pl.BlockSpec((1, TT, DA_V_DIM), lambda b, h, i: (b, 0, h))
    qo = pl.BlockSpec((1, tm, DA_V_DIM), lambda b, h, i: (b, i, h))
    return pl.pallas_call(
        functools.partial(_attn_kernel, ntl=ntl, t_lat=ntl * tm, lam_init=lam_init),
        grid=(B, DA_HEADS, nt),
        in_specs=[qo, kv, kv,
                  pl.BlockSpec(lam_p.shape, lambda b, h, i: (0, 0)),
                  pl.BlockSpec((1, DA_V_DIM), lambda b, h, i: (0, 0))],
        out_specs=qo,
        out_shape=jax.ShapeDtypeStruct((B, TT, DA_WIDTH), BF16),
        compiler_params=pltpu.CompilerParams(
            dimension_semantics=("parallel", "parallel", "parallel"),
            vmem_limit_bytes=VMEM_LIMIT_BYTES),
        name="diff_attn",
    )(q, k, v, lam_p, norm_g.reshape(1, DA_V_DIM))


def _rwfeat_kernel(p_ref, pp_ref, pn_ref, mu_ref, w0_ref, w2_ref, a0_ref, a2_ref, kk_ref, ka_ref,
                   rk_ref, g2_ref, ones_ref,
                   lw0_ref, lw1_ref, kd0_ref, kd1_ref, b0_ref, b1_ref, kko_ref, v_ref, r_ref,
                   bonus_ref, g_ref, *, ntl, nt):
    i = pl.program_id(1)
    tm = p_ref.shape[1]
    p = p_ref[0]
    prev_ok = jnp.logical_and(i != 0, i != ntl)
    next_ok = jnp.logical_and(i != ntl - 1, i != nt - 1)
    prev_row = jnp.where(prev_ok, pp_ref[0, 7:8, :], 0.0)
    next_row = jnp.where(next_ok, pn_ref[0, 0:1, :], 0.0)
    row = lax.broadcasted_iota(jnp.int32, p.shape, 0)
    prev = jnp.where(row == 0, prev_row, pltpu.roll(p, 1, 0))
    nxt = jnp.where(row == tm - 1, next_row, pltpu.roll(p, tm - 1, 0))
    ps = p + mu_ref[0:1, :] * (prev - p) + mu_ref[1:2, :] * (nxt - p)

    W = RW_WIDTH
    r = ps[:, 0:W]
    k = ps[:, W:2 * W]
    v = ps[:, 2 * W:3 * W]
    xw = ps[:, 3 * W:3 * W + 64]
    xa = ps[:, 3 * W + 64:3 * W + 128]
    xg = ps[:, 3 * W + 128:3 * W + 256]
    ones_bd = ones_ref[...]

    w_pre = _dot(jnp.tanh(xw).astype(BF16), w2_ref[...]) + w0_ref[...]
    sp = jnp.maximum(-w_pre, 0.0) + jnp.log(1.0 + jnp.exp(-jnp.abs(w_pre)))
    logw = -jnp.exp(-sp - 0.5)
    a = _sigmoid(_dot(xa.astype(BF16), a2_ref[...]) + a0_ref[...])

    kx = k * kk_ref[...]
    nrm = jnp.sqrt(_seg_sum(kx * kx, ones_bd))
    kk = kx / jnp.maximum(nrm, 1e-12)
    ka = ka_ref[...]
    kd0 = k * (1.0 + (a[:, :W] - 1.0) * ka)
    kd1 = k * (1.0 + (a[:, W:] - 1.0) * ka)
    g = _dot(_sigmoid(xg).astype(BF16), g2_ref[...])
    k_b = 0.5 * (kd0 + kd1)
    bonus = _seg_sum(r * k_b * rk_ref[...], ones_bd) * v

    lw0_ref[0] = logw[:, :W]
    lw1_ref[0] = logw[:, W:]
    kd0_ref[0] = kd0.astype(BF16)
    kd1_ref[0] = kd1.astype(BF16)
    b0_ref[0] = (kk * a[:, :W]).astype(BF16)
    b1_ref[0] = (kk * a[:, W:]).astype(BF16)
    kko_ref[0] = kk.astype(BF16)
    v_ref[0] = v.astype(BF16)
    r_ref[0] = r.astype(BF16)
    bonus_ref[0] = bonus.astype(BF16)
    g_ref[0] = g.astype(BF16)


def _rwkv_features(prw, mu, w0, w2, a0, a2, k_k, k_a, r_k, g2, ones_bd, ntl):
    B, TT, _ = prw.shape
    tm = TOKEN_TILE
    nt = TT // tm
    r8 = tm // 8
    last8 = TT // 8 - 1
    W = RW_WIDTH
    const = lambda a: pl.BlockSpec(a.shape, lambda b, i: (0,) * a.ndim)
    tok = pl.BlockSpec((1, tm, W), lambda b, i: (b, i, 0))
    small = [mu, w0, w2, a0, a2, k_k, k_a, r_k, g2, ones_bd]
    return pl.pallas_call(
        functools.partial(_rwfeat_kernel, ntl=ntl, nt=nt),
        grid=(B, nt),
        in_specs=[pl.BlockSpec((1, tm, RW_COLS), lambda b, i: (b, i, 0)),
                  pl.BlockSpec((1, 8, RW_COLS), lambda b, i: (b, jnp.maximum(i * r8 - 1, 0), 0)),
                  pl.BlockSpec((1, 8, RW_COLS), lambda b, i: (b, jnp.minimum((i + 1) * r8, last8), 0))]
                 + [const(a) for a in small],
        out_specs=[tok] * 11,
        out_shape=[jax.ShapeDtypeStruct((B, TT, W), F32)] * 2 + [jax.ShapeDtypeStruct((B, TT, W), BF16)] * 9,
        compiler_params=pltpu.CompilerParams(
            dimension_semantics=("parallel", "parallel"), vmem_limit_bytes=VMEM_LIMIT_BYTES),
        name="rwkv_features",
    )(prw, prw, prw, *small)


def _split_bf16(x, pieces):
    out = []
    for _ in range(pieces - 1):
        hi = x.astype(BF16)
        out.append(hi)
        x = x - hi.astype(F32)
    out.append(x.astype(BF16))
    return out


def _block_diag(x, groups):
    xb = x.astype(BF16)
    rows, lanes = xb.shape
    t = jnp.concatenate([xb] * groups, axis=0)
    ri = lax.broadcasted_iota(jnp.int32, t.shape, 0) // rows
    li = lax.broadcasted_iota(jnp.int32, t.shape, 1) // (lanes // groups)
    return jnp.where(ri == li, t, jnp.zeros_like(t))


def _scan_kernel(lw0, kd0, b0, kkf, vf, rf, lw1, kd1, b1, kkr, vr, rr, y0_ref, y1_ref, s_ref):
    step = pl.program_id(1)

    @pl.when(step == 0)
    def _():
        s_ref[...] = jnp.zeros_like(s_ref)

    C = lw0.shape[1]
    G = SCAN_HEADS_PER_PASS
    L = G * RW_HEAD
    assert C == RW_HEAD
    ti = lax.broadcasted_iota(jnp.int32, (C, L), 0)
    si = lax.broadcasted_iota(jnp.int32, (C, L), 1) % C
    tc = lax.broadcasted_iota(jnp.int32, (C, C), 0)
    sc = lax.broadcasted_iota(jnp.int32, (C, C), 1)
    same_head = (lax.broadcasted_iota(jnp.int32, (L, L), 0) // RW_HEAD
                 == lax.broadcasted_iota(jnp.int32, (L, L), 1) // RW_HEAD)

    groups = []
    for z, (lw, kd, b, kk, v, r) in enumerate(((lw0, kd0, b0, kkf, vf, rf), (lw1, kd1, b1, kkr, vr, rr))):
        rev = z == 1
        incl = (si >= ti) if rev else (si <= ti)
        strict = (si > ti) if rev else (si < ti)
        tri = jnp.where((sc >= tc) if rev else (sc <= tc), 1.0, 0.0).astype(BF16)
        for n in range(lw.shape[0]):
            logw = lw[n]
            cum = sum(_dot(tri, piece) for piece in _split_bf16(logw, 3))
            tot = jnp.sum(logw, axis=0, keepdims=True)
            p_in = jnp.exp(-cum)
            p_end = jnp.exp(tot - cum)
            kkz, rz, bz, kdz = (t[n].astype(F32) for t in (kk, r, b, kd))
            lhs = jnp.concatenate([-kkz * jnp.exp(cum - logw), rz * jnp.exp(cum)], axis=0)
            rhs_b = bz * p_in
            rhs_k = kdz * p_in
            upd = jnp.concatenate([bz * p_end, kdz * p_end], axis=0)
            dec = jnp.exp(tot)
            vz = v[n]
            for j in range(RW_HEADS // G):
                c = slice(j * L, (j + 1) * L)
                groups.append(dict(z=z, smp=n, j=j, lhs=lhs[:, c].astype(BF16), rhs_b=rhs_b[:, c],
                                   rhs_k=rhs_k[:, c], upd=upd[:, c].astype(BF16), dec=dec[:, c],
                                   v=vz[:, c], strict=strict, incl=incl))

    for g in groups:
        aa_b = _dot_nt(g["lhs"], _block_diag(g["rhs_b"], G))
        aa_k = _dot_nt(g["lhs"], _block_diag(g["rhs_k"], G))
        g["a_k"] = jnp.concatenate([jnp.where(g["strict"], aa_k[:C], 0.0),
                                    jnp.where(g["incl"], aa_k[C:], 0.0)], axis=0).astype(BF16)
        g["a_rb"] = jnp.where(g["incl"], aa_b[C:], 0.0).astype(BF16)
        g["n"] = jnp.where(g["strict"], aa_b[:C], 0.0)
        g["apow"] = _dot(g["n"].astype(BF16), _block_diag(g["n"], G))
    levels = int(math.log2(C)) - 1
    for lv in range(levels):
        for g in groups:
            pbd = _block_diag(g["apow"], G)
            if lv < levels - 1:
                st = _dot(jnp.concatenate([g["n"], g["apow"]], axis=0).astype(BF16), pbd)
                g["n"] = g["n"] + g["apow"] + st[:C]
                g["apow"] = st[C:]
            else:
                g["n"] = g["n"] + g["apow"] + _dot(g["n"].astype(BF16), pbd)
    for g in groups:
        g["s0"] = s_ref[g["smp"], g["z"], g["j"]]
        g["gh"] = _dot_nt(g["lhs"], g["s0"].astype(BF16))
    for g in groups:
        st = _dot(g["a_k"], _block_diag(g["v"], G))
        g["w"] = g["gh"][:C] + st[:C]
        g["yk"] = st[C:]
    for g in groups:
        g["u"] = g["w"] + _dot(g["n"].astype(BF16), _block_diag(g["w"], G))
    for g in groups:
        g["y"] = g["gh"][C:] + g["yk"] + _dot(g["a_rb"], _block_diag(g["u"], G))
    for g in groups:
        uv = jnp.concatenate([g["u"].astype(BF16), g["v"]], axis=0)
        s_ref[g["smp"], g["z"], g["j"]] = g["s0"] * g["dec"] + jnp.where(same_head, _dot_tn(uv, g["upd"]), 0.0)
    for z, y_ref in enumerate((y0_ref, y1_ref)):
        for n in range(y_ref.shape[0]):
            y_ref[n] = jnp.concatenate([g["y"] for g in groups if g["z"] == z and g["smp"] == n], axis=-1)


def _rwkv_scan(lw0, lw1, kd0, kd1, b0, b1, kk, v, r, t_lat):
    B, TT, W = kk.shape
    C = SCAN_CHUNK
    ncl = t_lat // C
    nch = TT // C
    ncc = nch - ncl

    def fwd(s):
        return jnp.where(s < ncc, ncl + s, s - ncc)

    nb = math.gcd(B, SCAN_BATCH)
    fs =pl.BlockSpec((nb, C, W), lambda b, s: (b, fwd(s), 0))
    rs = pl.BlockSpec((nb, C, W), lambda b, s: (b, nch - 1 - s, 0))
    return pl.pallas_call(
        _scan_kernel,
        grid=(B // nb, nch),
        in_specs=[fs] * 6 + [rs] * 6,
        out_specs=[fs, rs],
        out_shape=[jax.ShapeDtypeStruct((B, TT, W), F32)] * 2,
        scratch_shapes=[pltpu.VMEM((nb, 2, RW_HEADS // SCAN_HEADS_PER_PASS, SCAN_HEADS_PER_PASS * RW_HEAD,
                                    SCAN_HEADS_PER_PASS * RW_HEAD), F32)],
        compiler_params=pltpu.CompilerParams(
            dimension_semantics=("parallel", "arbitrary"), vmem_limit_bytes=VMEM_LIMIT_BYTES),
        name="rwkv_scan",
    )(lw0, kd0, b0, kk, v, r, lw1, kd1, b1, kk, v, r)


def _merge_kernel(x_ref, yda_ref, y0_ref, y1_ref, bonus_ref, g_ref, ysg_ref, gate_ref,
                  g1_ref, sh2_ref, sc2_ref, wb_ref, wo_ref, ln1g_ref, ln1b_ref, rlng_ref, rlnb_ref,
                  ones_ref, wr_ref, xmid_ref, h2_ref, aff_ref, *, alpha):
    D = D_MODEL
    ones_bd = ones_ref[...]
    y = y0_ref[0] + y1_ref[0]
    mu = _seg_sum(y, ones_bd) * (1.0 / RW_HEAD)
    dy = y - mu
    var = _seg_sum(dy * dy, ones_bd) * (1.0 / RW_HEAD)
    gn = dy * lax.rsqrt(var + RW_GN_EPS) * rlng_ref[...] + rlnb_ref[...]
    yrw = ((gn + bonus_ref[0].astype(F32)) * g_ref[0].astype(F32)).astype(BF16)

    m = gate_ref[0, :, 0:D].astype(F32) * _dot(yda_ref[0], wb_ref[0:DA_WIDTH, :])
    m = m + gate_ref[0, :, D:2 * D].astype(F32) * _dot(yrw, wb_ref[DA_WIDTH:DA_WIDTH + RW_WIDTH, :])
    m = m + gate_ref[0, :, 2 * D:3 * D].astype(F32) * _dot(ysg_ref[0], wb_ref[DA_WIDTH + RW_WIDTH:, :])
    mix = _dot(m.astype(BF16), wo_ref[...])

    z = alpha * x_ref[0] + g1_ref[0, 0] * mix
    zm = jnp.mean(z, axis=-1, keepdims=True)
    dz = z - zm
    zv = jnp.mean(dz * dz, axis=-1, keepdims=True)
    xmid = dz * lax.rsqrt(zv + LN_EPS) * ln1g_ref[...] + ln1b_ref[...]
    xmid_ref[0] = xmid
    h2 = xmid * (1.0 + sc2_ref[0, 0]) + sh2_ref[0, 0]
    h2_ref[0] = h2.astype(BF16)
    logits = _dot_nt(wr_ref[...], h2, HIGHEST)
    e = jnp.exp(logits - jnp.max(logits, axis=0, keepdims=True))
    aff_ref[0] = e / jnp.sum(e, axis=0, keepdims=True)


def _merge(x_all, yda, y0, y1, bonus, g, ysg, gates, modall, w_branch, w_out, ln1g, ln1b,
           rlng, rlnb, ones_bd, w_router_t, ntl, alpha):
    B, TT, D = x_all.shape
    tm = TOKEN_TILE
    nt = TT // tm
    tok = lambda w: pl.BlockSpec((1, tm, w), lambda b, i: (b, i, 0))
    modspec = lambda j: pl.BlockSpec((1, 1, 1, D), lambda b, i: (b, i // ntl, 0, j))
    const = lambda a: pl.BlockSpec(a.shape, lambda b, i: (0,) * a.ndim)
    consts = [w_branch, w_out, ln1g, ln1b, rlng, rlnb, ones_bd, w_router_t]
    return pl.pallas_call(
        functools.partial(_merge_kernel, alpha=alpha),
        grid=(B, nt),
        in_specs=[tok(D), tok(DA_WIDTH), tok(RW_WIDTH), tok(RW_WIDTH), tok(RW_WIDTH), tok(RW_WIDTH),
                  tok(SG_WIDTH), tok(3 * D), modspec(2), modspec(3), modspec(4)]
                 + [const(a) for a in consts],
        out_specs=[tok(D), tok(D), pl.BlockSpec((1, N_EXPERTS, tm), lambda b, i: (b, 0, i))],
        out_shape=[jax.ShapeDtypeStruct((B, TT, D), F32),
                   jax.ShapeDtypeStruct((B, TT, D), BF16),
                   jax.ShapeDtypeStruct((B, N_EXPERTS, TT), F32)],
        compiler_params=pltpu.CompilerParams(
            dimension_semantics=("parallel", "parallel"), vmem_limit_bytes=VMEM_LIMIT_BYTES),
        name="merge_ln1_router",
    )(x_all, yda, y0, y1, bonus, g, ysg, gates, modall, modall, modall, *consts)


def _topk_kernel(aff_ref, tri_ref, rank_ref, *, cap):
    a = aff_ref[0]
    bits = pltpu.bitcast(a, jnp.int32)
    thr = jnp.zeros((a.shape[0], 1), jnp.int32)
    for bit in range(30, -1, -1):
        cand = thr | (1 << bit)
        cnt = jnp.sum(jnp.where(bits >= cand, 1.0, 0.0), axis=-1, keepdims=True)
        thr = jnp.where(cnt >= cap, cand, thr)
    gt = bits > thr
    eq = bits == thr
    need = cap - jnp.sum(jnp.where(gt, 1.0, 0.0), axis=-1, keepdims=True)
    tri = tri_ref[...]
    tie_rank = _dot(jnp.where(eq, 1.0, 0.0).astype(BF16), tri)
    sel = jnp.logical_or(gt, jnp.logical_and(eq, tie_rank < need))
    rank = _dot(jnp.where(sel, 1.0, 0.0).astype(BF16), tri)
    rank_ref[0] = jnp.where(sel, rank, -1.0).astype(jnp.int32)


def _topk_ranks(aff, tri, t_off, t_len, cap):
    B, E, _ = aff.shape
    blk = t_off // t_len
    return pl.pallas_call(
        functools.partial(_topk_kernel, cap=cap),
        grid=(B,),
        in_specs=[pl.BlockSpec((1, E, t_len), lambda b: (b, 0, blk)),
                  pl.BlockSpec(tri.shape, lambda b: (0, 0))],
        out_specs=pl.BlockSpec((1, E, t_len), lambda b: (b, 0, 0)),
        out_shape=jax.ShapeDtypeStruct((B, E, t_len), jnp.int32),
        compiler_params=pltpu.CompilerParams(
            dimension_semantics=("parallel",), vmem_limit_bytes=VMEM_LIMIT_BYTES),
        name="expert_choice_ranks",
    )(aff, tri)


def _moe_kernel(*refs, sets):
    n = len(sets)
    h_ref, aff_ref = refs[0], refs[1]
    rank_refs = refs[2:2 + n]
    wg_ref, wu_ref, wd_ref, f_ref = refs[2 + n:]
    e = pl.program_id(1)

    @pl.when(e == 0)
    def _():
        f_ref[...] = jnp.zeros_like(f_ref)

    onehots, gates, xs = [], [], []
    for (t0, tn, cap), rank_ref in zip(sets, rank_refs):
        rank = rank_ref[0, pl.ds(e, 1), :]
        aff = aff_ref[0, pl.ds(e, 1), t0:t0 + tn]
        slot = lax.broadcasted_iota(jnp.int32, (cap, tn), 0)
        hit = rank == slot
        onehot = jnp.where(hit, 1.0, 0.0).astype(BF16)
        onehots.append(onehot)
        gates.append(jnp.sum(jnp.where(hit, aff, 0.0), axis=-1, keepdims=True))
        xs.append(_dot(onehot, h_ref[0, t0:t0 + tn, :]).astype(BF16))
    xe = jnp.concatenate(xs, axis=0) if n > 1 else xs[0]
    gate = jnp.concatenate(gates, axis=0) if n > 1 else gates[0]
    hg = _dot(xe, wg_ref[0, 0])
    hid = (hg * _sigmoid(hg)) * _dot(xe, wu_ref[0, 0])
    ye = (_dot(hid.astype(BF16), wd_ref[0, 0]) * gate).astype(BF16)
    r0 = 0
    for (t0, tn, cap), onehot in zip(sets, onehots):
        f_ref[0, t0:t0 + tn, :] += _dot_tn(onehot, ye[r0:r0 + cap])
        r0 += cap


def _expert_ffn(h2, aff, ranks, wg, wu, wd, layer, sets):
    B, TT, D = h2.shape
    E = aff.shape[1]
    F = wg.shape[3]
    return pl.pallas_call(
        functools.partial(_moe_kernel, sets=sets),
        grid=(B, E),
        in_specs=[pl.BlockSpec((1, TT, D), lambda b, e: (b, 0, 0), pipeline_mode=pl.Buffered(1)),
                  pl.BlockSpec((1, E, TT), lambda b, e: (b, 0, 0))]
                 + [pl.BlockSpec((1, E, r.shape[2]), lambda b, e: (b, 0, 0)) for r in ranks]
                 + [pl.BlockSpec((1, 1, D, F), lambda b, e: (layer, e, 0, 0)),
                    pl.BlockSpec((1, 1, D, F), lambda b, e: (layer, e, 0, 0)),
                    pl.BlockSpec((1, 1, F, D), lambda b, e: (layer, e, 0, 0))],
        out_specs=pl.BlockSpec((1, TT, D), lambda b, e: (b, 0, 0)),
        out_shape=jax.ShapeDtypeStruct((B, TT, D), F32),
        compiler_params=pltpu.CompilerParams(
            dimension_semantics=("parallel", "arbitrary"), vmem_limit_bytes=VMEM_LIMIT_BYTES),
        name="expert_ffn",
    )(h2, aff, *ranks, wg, wu, wd)


def _ln2_kernel(x_ref, f_ref, g2_ref, lng_ref, lnb_ref, o_ref, *, alpha):
    z = alpha * x_ref[0] + g2_ref[0, 0] * f_ref[0]
    zm = jnp.mean(z, axis=-1, keepdims=True)
    dz = z - zm
    zv = jnp.mean(dz * dz, axis=-1, keepdims=True)
    o_ref[0] = dz * lax.rsqrt(zv + LN_EPS) * lng_ref[...] + lnb_ref[...]


def _final_norm(xmid, f, modall, lng, lnb, ntl, alpha, rows):
    B, _, D = xmid.shape
    tm = TOKEN_TILE
    tok = pl.BlockSpec((1, tm, D), lambda b, i: (b, i, 0))
    const = lambda a: pl.BlockSpec(a.shape, lambda b, i: (0,) * a.ndim)
    return pl.pallas_call(
        functools.partial(_ln2_kernel, alpha=alpha),
        grid=(B, rows // tm),
        in_specs=[tok, tok, pl.BlockSpec((1, 1, 1, D), lambda b, i: (b, i // ntl, 0, 5)),
                  const(lng), const(lnb)],
        out_specs=tok,
        out_shape=jax.ShapeDtypeStruct((B, rows, D), F32),
        compiler_params=pltpu.CompilerParams(dimension_semantics=("parallel", "parallel")),
        name="ln2",
    )(xmid, f, modall, lng, lnb)


def _rope_tables(t_lat, t_ctx):
    rows = t_lat // GRID_W
    row = jnp.repeat(jnp.arange(rows, dtype=F32), GRID_W)
    col = jnp.tile(jnp.arange(GRID_W, dtype=F32), rows)
    half = DA_QK_DIM // 2
    inv_freq = ROPE_BASE ** (-jnp.arange(0, half, 2, dtype=F32) / half)
    ar = row[:, None] * inv_freq
    ac = col[:, None] * inv_freq
    ang = jnp.concatenate([ar, ar, ac, ac], axis=-1)
    sign = jnp.where((jnp.arange(DA_QK_DIM) % 32) < 16, -1.0, 1.0).astype(F32)
    reps = DA_QK_COLS // DA_QK_DIM
    cos = jnp.tile(jnp.cos(ang), (1, reps))
    sin = jnp.tile(jnp.sin(ang) * sign, (1, reps))
    cos = jnp.concatenate([cos, jnp.ones((t_ctx, DA_QK_COLS), F32)], axis=0)
    sin = jnp.concatenate([sin, jnp.zeros((t_ctx, DA_QK_COLS), F32)], axis=0)
    return cos, sin


def kernel(x, c, ctx, c_ctx, w_mod, b_mod, w_in, da_lambda, da_norm_g, rw_shift_mu, rw_w0, rw_w2, rw_a0, rw_a2, rw_k_k, rw_k_a, rw_r_k, rw_ln_g, rw_ln_b, rw_g2, sg_norm_g, sg_norm_b, sg_w, sg_b, w_branch, w_out, ln1_g, ln1_b, w_router, w_e_gate, w_e_up, w_e_down, ln2_g, ln2_b):
    B, T, D = x.shape
    Tc = ctx.shape[1]
    depth = w_mod.shape[0]
    tm = TOKEN_TILE
    assert D == D_MODEL and T % tm == 0 and Tc % tm == 0 and T % Tc == 0 and Tc <= T
    ntl = T // tm
    alpha = (2 * depth) ** 0.25
    cap_lat = EC_CAPACITY * T // N_EXPERTS
    cap_ctx = EC_CAPACITY * Tc // N_EXPERTS

    cos, sin = _rope_tables(T, Tc)
    lane = jnp.arange(RW_WIDTH)
    ones_bd = (lane[:, None] // RW_HEAD == lane[None, :] // RW_HEAD).astype(BF16)
    tri_lat = (jnp.arange(T)[:, None] < jnp.arange(T)[None, :]).astype(BF16)
    tri_ctx = tri_lat[:Tc, :Tc]
    rows = ((B + 1 + 7) // 8) * 8
    cc = jnp.concatenate([c, c_ctx[None, :], jnp.zeros((rows - B - 1, D), F32)], axis=0)
    row2 = lambda a: a.reshape(1, -1)

    w_in_bf = w_in.astype(BF16)
    wg, wu, wd = w_e_gate.astype(BF16), w_e_up.astype(BF16), w_e_down.astype(BF16)
    x_all = jnp.concatenate([x, ctx], axis=1)
    for l in range(depth):
        last = l == depth - 1
        lam_init = 0.8 - 0.6 * math.exp(-0.3 * l)
        mod = _modulation(cc, w_mod[l], b_mod[l])
        modall = jnp.stack([mod[:B], jnp.broadcast_to(mod[B], (B, 6 * D))], axis=1)
        modall = modall.reshape(B, 2, 1, 6 * D)

        sgbias = jnp.repeat(sg_b[l].T, SG_WIDTH // SG_GROUPS, axis=1)
        q, k, v, prw, ysg, gates = _input_projection(
            x_all, modall, w_in_bf, l, cos, sin, row2(sg_norm_g[l]), row2(sg_norm_b[l]),
            sg_w[l].astype(BF16), sgbias, ntl)

        yda = _diff_attention(q, k, v, da_lambda[l], da_norm_g[l], ntl, lam_init)

        cat2 = lambda a: jnp.transpose(a, (1, 0, 2)).reshape(a.shape[1], 2 * RW_WIDTH)
        feats = _rwkv_features(
            prw, rw_shift_mu[l], row2(rw_w0[l]), cat2(rw_w2[l]).astype(BF16), row2(rw_a0[l]),
            cat2(rw_a2[l]).astype(BF16), row2(rw_k_k[l]), row2(rw_k_a[l]), row2(rw_r_k[l]),
            rw_g2[l].astype(BF16), ones_bd, ntl)
        lw0, lw1, kd0, kd1, b0, b1, kk, vv, rr, bonus, gg = feats
        y0, y1 = _rwkv_scan(lw0, lw1, kd0, kd1, b0, b1, kk, vv, rr, T)

        xmid, h2, aff = _merge(
            x_all, yda, y0, y1, bonus, gg, ysg, gates, modall, w_branch[l].astype(BF16),
            w_out[l].astype(BF16), row2(ln1_g[l]), row2(ln1_b[l]), row2(rw_ln_g[l]),
            row2(rw_ln_b[l]), ones_bd, w_router[l].T, ntl, alpha)

        sets = ((0, T, cap_lat),) if last else ((0, T, cap_lat), (T, Tc, cap_ctx))
        ranks = [_topk_ranks(aff, tri_lat, 0, T, cap_lat)]
        if not last:
            ranks.append(_topk_ranks(aff, tri_ctx, T, Tc, cap_ctx))
        f = _expert_ffn(h2, aff, ranks, wg, wu, wd, l, sets)
        x_all = _final_norm(xmid, f, modall, row2(ln2_g[l]), row2(ln2_b[l]), ntl, alpha,
                            T if last else T + Tc)
    return x_all
```

```python
import functools
import math

import jax
import jax.numpy as jnp
from jax import lax
from jax.experimental import pallas as pl
from jax.experimental.pallas import tpu as pltpu

F32 = jnp.float32
BF16 = jnp.bfloat16
HIGHEST = lax.Precision.HIGHEST

D_MODEL = 1024
GRID_W = 64
DA_HEADS = 4
DA_QK_DIM = 64
DA_V_DIM = 128
DA_WIDTH = 512
DA_QK_COLS = 512
ROPE_BASE = 10000.0
DA_EPS = 1e-5
RW_HEAD = 64
RW_HEADS = 8
RW_WIDTH = 512
RW_COLS = 1792
RW_GN_EPS = 64e-5
SG_CHUNK = 128
SG_GROUPS = 4
SG_WIDTH = 512
MIX_WIDTH = 1536
DA_K0 = 512
DA_V0 = 1024
RW_0 = 1536
SG_0 = RW_0 + RW_COLS
GATE_0 = SG_0 + 2 * SG_WIDTH
IN_COLS = GATE_0 + 3 * D_MODEL
N_EXPERTS = 16
EC_CAPACITY = 2
LN_EPS = 1e-5
LOG2_E = math.log2(math.e)

TOKEN_TILE = 256
SCAN_CHUNK = 64
ATTN_KEY_TILE = 256
SCAN_HEADS_PER_PASS = 4
SCAN_BATCH = 4
MERGE_ROWS = 128
RANK_BLOCK = 256
VMEM_LIMIT_BYTES = 58 * 1024 * 1024


def _dot(a, b):
    return jnp.dot(a, b, preferred_element_type=F32)


def _dot_hi(a, b):
    return jnp.dot(a, b, preferred_element_type=F32, precision=HIGHEST)


def _dot_nt(a, b, precision=None):
    return lax.dot_general(a, b, (((1,), (1,)), ((), ())), preferred_element_type=F32,
                           precision=precision)


def _dot_tn(a, b, precision=None):
    return lax.dot_general(a, b, (((0,), (0,)), ((), ())), preferred_element_type=F32,
                           precision=precision)


def _sigmoid(z):
    return 1.0 / (1.0 + jnp.exp(-z))


def _seg_sum(z, ones_bd):
    hi = z.astype(BF16)
    lo = (z - hi.astype(F32)).astype(BF16)
    return _dot(hi, ones_bd) + _dot(lo, ones_bd)


def _mod_kernel(c_ref, w_ref, b_ref, o_ref):
    cc = c_ref[...]
    o_ref[...] = _dot_hi(cc * _sigmoid(cc), w_ref[...]) + b_ref[...]


def _modulation(cc, w_mod, b_mod):
    rows, d = cc.shape
    n = w_mod.shape[1]
    tn = 1024
    return pl.pallas_call(
        _mod_kernel,
        grid=(n // tn,),
        in_specs=[pl.BlockSpec((rows, d), lambda j: (0, 0)),
                  pl.BlockSpec((d, tn), lambda j: (0, j)),
                  pl.BlockSpec((1, tn), lambda j: (0, j))],
        out_specs=pl.BlockSpec((rows, tn), lambda j: (0, j)),
        out_shape=jax.ShapeDtypeStruct((rows, n), F32),
        name="adaln_mod",
    )(cc, w_mod, b_mod.reshape(1, n))


def _inproj_kernel(x_ref, sh_ref, sc_ref, w_ref, cos_ref, sin_ref, sgg_ref, sgb_ref, sgw_ref,
                   sgbias_ref, q_ref, k_ref, v_ref, rw_ref, sg_ref, gate_ref):
    tm = x_ref.shape[1]
    h = (x_ref[0] * (1.0 + sc_ref[0, 0]) + sh_ref[0, 0]).astype(BF16)

    def proj(c0, c1):
        return _dot(h, w_ref[0, :, c0:c1])

    cos = cos_ref[...]
    sin = sin_ref[...]
    lane = lax.broadcasted_iota(jnp.int32, (tm, DA_QK_COLS), 1)
    first = (lane % 32) < 16

    def rope(z):
        zr = jnp.where(first, pltpu.roll(z, DA_QK_COLS - 16, 1), pltpu.roll(z, 16, 1))
        return z * cos + zr * sin

    q_ref[0] = (rope(proj(0, DA_K0)) * (DA_QK_DIM ** -0.5 * LOG2_E)).astype(BF16)
    k_ref[0] = rope(proj(DA_K0, DA_V0)).astype(BF16)
    v_ref[0] = proj(DA_V0, RW_0).astype(BF16)
    for c0 in range(0, RW_COLS, 896):
        rw_ref[0, :, c0:c0 + 896] = proj(RW_0 + c0, RW_0 + c0 + 896)
    for j in range(3):
        gate_ref[0, :, j * D_MODEL:(j + 1) * D_MODEL] = _sigmoid(
            proj(GATE_0 + j * D_MODEL, GATE_0 + (j + 1) * D_MODEL)).astype(BF16)

    ps = proj(SG_0, GATE_0)
    gl = ps * (0.5 * (1.0 + jnp.tanh(math.sqrt(2.0 / math.pi) * (ps + 0.044715 * (ps * ps * ps)))))
    u = gl[:, :SG_WIDTH]
    vv = gl[:, SG_WIDTH:]
    mu = jnp.mean(vv, axis=-1, keepdims=True)
    dv = vv - mu
    var = jnp.mean(dv * dv, axis=-1, keepdims=True)
    vn = (dv * lax.rsqrt(var + LN_EPS) * sgg_ref[...] + sgb_ref[...]).astype(BF16)
    gd = SG_WIDTH // SG_GROUPS
    for n in range(tm // SG_CHUNK):
        r0 = n * SG_CHUNK
        for g in range(SG_GROUPS):
            c0 = g * gd
            vm = _dot(sgw_ref[g], vn[r0:r0 + SG_CHUNK, c0:c0 + gd]) + sgbias_ref[:, c0:c0 + gd]
            sg_ref[0, r0:r0 + SG_CHUNK, c0:c0 + gd] = (u[r0:r0 + SG_CHUNK, c0:c0 + gd] * vm).astype(BF16)


def _input_projection(x_all, modall, w_in, layer, cos, sin, sgg, sgb, sgw, sgbias, ntl):
    B, TT, D = x_all.shape
    tm = TOKEN_TILE
    nt = TT // tm
    tok = lambda w: pl.BlockSpec((1, tm, w), lambda b, i: (b, i, 0))
    modspec = lambda j: pl.BlockSpec((1, 1, 1, D), lambda b, i: (b, i // ntl, 0, j))
    const2 = lambda a: pl.BlockSpec(a.shape, lambda b, i: (0, 0))
    return pl.pallas_call(
        _inproj_kernel,
        grid=(B, nt),
        in_specs=[tok(D), modspec(0), modspec(1),
                  pl.BlockSpec((1,) + w_in.shape[1:], lambda b, i: (layer, 0, 0),
                               pipeline_mode=pl.Buffered(1)),
                  pl.BlockSpec((tm, DA_QK_COLS), lambda b, i: (i, 0)),
                  pl.BlockSpec((tm, DA_QK_COLS), lambda b, i: (i, 0)),
                  const2(sgg), const2(sgb),
                  pl.BlockSpec(sgw.shape, lambda b, i: (0, 0, 0)),
                  const2(sgbias)],
        out_specs=[tok(DA_QK_COLS), tok(DA_QK_COLS), tok(DA_WIDTH), tok(RW_COLS), tok(SG_WIDTH),
                   tok(3 * D)],
        out_shape=[jax.ShapeDtypeStruct((B, TT, DA_QK_COLS), BF16),
                   jax.ShapeDtypeStruct((B, TT, DA_QK_COLS), BF16),
                   jax.ShapeDtypeStruct((B, TT, DA_WIDTH), BF16),
                   jax.ShapeDtypeStruct((B, TT, RW_COLS), F32),
                   jax.ShapeDtypeStruct((B, TT, SG_WIDTH), BF16),
                   jax.ShapeDtypeStruct((B, TT, 3 * D), BF16)],
        compiler_params=pltpu.CompilerParams(
            dimension_semantics=("parallel", "parallel"), vmem_limit_bytes=VMEM_LIMIT_BYTES),
        name="in_proj",
    )(x_all, modall, modall, w_in, cos, sin, sgg, sgb, sgw, sgbias)


def _attn_kernel(q_ref, k_ref, v_ref, lam_ref, g_ref, o_ref, *, ntl, t_lat, lam_init):
    i = pl.program_id(2)
    lp = lam_ref[...]
    lam = (jnp.exp(jnp.sum(lp[0:1] * lp[1:2], axis=-1, keepdims=True))
           - jnp.exp(jnp.sum(lp[2:3] * lp[3:4], axis=-1, keepdims=True)) + lam_init)
    q = q_ref[0]
    lane = lax.broadcasted_iota(jnp.int32, q.shape, 1)
    zero = jnp.zeros_like(q)
    q0 = jnp.where(lane < DA_QK_DIM, q, zero)
    q1 = jnp.where(lane >= DA_QK_DIM, q, zero)

    def attend(k0, nk):
        kt = ATTN_KEY_TILE
        tiles = [slice(k0 + t * kt, k0 + (t + 1) * kt) for t in range(nk // kt)]
        ks = [k_ref[0, sl, :] for sl in tiles]

        def row_max(ss):
            m = ss[0]
            for s in ss[1:]:
                m = jnp.maximum(m, s)
            return jnp.max(m, axis=-1, keepdims=True)

        def pv(es):
            acc = None
            for e, sl in zip(es, tiles):
                v = v_ref[0, sl, :]
                d = _dot(e, jnp.concatenate([v, jnp.ones_like(v)], axis=1))
                acc = d if acc is None else acc + d
            return acc[:, :DA_V_DIM] / acc[:, DA_V_DIM:]

        s0 = [_dot_nt(q0, kk) for kk in ks]
        m0 = row_max(s0)
        s1, e0 = [], []
        for kk, s in zip(ks, s0):
            s1.append(_dot_nt(q1, kk))
            e0.append(jnp.exp2(s - m0).astype(BF16))
        m1 = row_max(s1)
        o0 = pv(e0)
        e1 = [jnp.exp2(s - m1).astype(BF16) for s in s1]
        o = o0 - lam * pv(e1)
        o = o * lax.rsqrt(jnp.mean(o * o, axis=-1, keepdims=True) + DA_EPS) * g_ref[...]
        o_ref[0] = (o * (1.0 - lam_init)).astype(BF16)

    @pl.when(i < ntl)
    def _():
        attend(0, k_ref.shape[1])

    @pl.when(i >= ntl)
    def _():
        attend(t_lat, k_ref.shape[1] - t_lat)


def _diff_attention(q, k, v, lam_p, norm_g, ntl, lam_init):
    B, TT, _ = q.shape
    tm = TOKEN_TILE
    nt = TT // tm
    kv = pl.BlockSpec((1, TT, DA_V_DIM), lambda b, h, i: (b, 0, h))
    qo = pl.BlockSpec((1, tm, DA_V_DIM), lambda b, h, i: (b, i, h))
    return pl.pallas_call(
        functools.partial(_attn_kernel, ntl=ntl, t_lat=ntl * tm, lam_init=lam_init),
        grid=(B, DA_HEADS, nt),
        in_specs=[qo, kv, kv,
                  pl.BlockSpec(lam_p.shape, lambda b, h, i: (0, 0)),
                  pl.BlockSpec((1, DA_V_DIM), lambda b, h, i: (0, 0))],
        out_specs=qo,
        out_shape=jax.ShapeDtypeStruct((B, TT, DA_WIDTH), BF16),
        compiler_params=pltpu.CompilerParams(
            dimension_semantics=("parallel", "parallel", "parallel"),
            vmem_limit_bytes=VMEM_LIMIT_BYTES),
        name="diff_attn",
    )(q, k, v, lam_p, norm_g.reshape(1, DA_V_DIM))


def _rwfeat_kernel(p_ref, pp_ref, pn_ref, mu_ref, w0_ref, w2_ref, a0_ref, a2_ref, kk_ref, ka_ref,
                   rk_ref, g2_ref, ones_ref,
                   lw0_ref, lw1_ref, kd0_ref, kd1_ref, b0_ref, b1_ref, kko_ref, v_ref, r_ref,
                   bonus_ref, g_ref, *, ntl, nt):
    i = pl.program_id(1)
    tm = p_ref.shape[1]
    p = p_ref[0]
    prev_ok = jnp.logical_and(i != 0, i != ntl)
    next_ok = jnp.logical_and(i != ntl - 1, i != nt - 1)
    prev_row = jnp.where(prev_ok, pp_ref[0, 7:8, :], 0.0)
    next_row = jnp.where(next_ok, pn_ref[0, 0:1, :], 0.0)
    row = lax.broadcasted_iota(jnp.int32, p.shape, 0)
    prev = jnp.where(row == 0, prev_row, pltpu.roll(p, 1, 0))
    nxt = jnp.where(row == tm - 1, next_row, pltpu.roll(p, tm - 1, 0))
    ps = p + mu_ref[0:1, :] * (prev - p) + mu_ref[1:2, :] * (nxt - p)

    W = RW_WIDTH
    r = ps[:, 0:W]
    k = ps[:, W:2 * W]
    v = ps[:, 2 * W:3 * W]
    xw = ps[:, 3 * W:3 * W + 64]
    xa = ps[:, 3 * W + 64:3 * W + 128]
    xg = ps[:, 3 * W + 128:3 * W + 256]
    ones_bd = ones_ref[...]

    w_pre = _dot(jnp.tanh(xw).astype(BF16), w2_ref[...]) + w0_ref[...]
    sp = jnp.maximum(-w_pre, 0.0) + jnp.log(1.0 + jnp.exp(-jnp.abs(w_pre)))
    logw = -jnp.exp(-sp - 0.5)
    a = _sigmoid(_dot(xa.astype(BF16), a2_ref[...]) + a0_ref[...])

    kx = k * kk_ref[...]
    nrm = jnp.sqrt(_seg_sum(kx * kx, ones_bd))
    kk = kx / jnp.maximum(nrm, 1e-12)
    ka = ka_ref[...]
    kd0 = k * (1.0 + (a[:, :W] - 1.0) * ka)
    kd1 = k * (1.0 + (a[:, W:] - 1.0) * ka)
    g = _dot(_sigmoid(xg).astype(BF16), g2_ref[...])
    k_b = 0.5 * (kd0 + kd1)
    bonus = _seg_sum(r * k_b * rk_ref[...], ones_bd) * v

    lw0_ref[0] = logw[:, :W]
    lw1_ref[0] = logw[:, W:]
    kd0_ref[0] = kd0.astype(BF16)
    kd1_ref[0] = kd1.astype(BF16)
    b0_ref[0] = (kk * a[:, :W]).astype(BF16)
    b1_ref[0] = (kk * a[:, W:]).astype(BF16)
    kko_ref[0] = kk.astype(BF16)
    v_ref[0] = v.astype(BF16)
    r_ref[0] = r.astype(BF16)
    bonus_ref[0] = bonus.astype(BF16)
    g_ref[0] = g.astype(BF16)


def _rwkv_features(prw, mu, w0, w2, a0, a2, k_k, k_a, r_k, g2, ones_bd, ntl):
    B, TT, _ = prw.shape
    tm = TOKEN_TILE
    nt = TT // tm
    r8 = tm // 8
    last8 = TT // 8 - 1
    W = RW_WIDTH
    const = lambda a: pl.BlockSpec(a.shape, lambda b, i: (0,) * a.ndim)
    tok = pl.BlockSpec((1, tm, W), lambda b, i: (b, i, 0))
    small = [mu, w0, w2, a0, a2, k_k, k_a, r_k, g2, ones_bd]
    return pl.pallas_call(
        functools.partial(_rwfeat_kernel, ntl=ntl, nt=nt),
        grid=(B, nt),
        in_specs=[pl.BlockSpec((1, tm, RW_COLS), lambda b, i: (b, i, 0)),
                  pl.BlockSpec((1, 8, RW_COLS), lambda b, i: (b, jnp.maximum(i * r8 - 1, 0), 0)),
                  pl.BlockSpec((1, 8, RW_COLS), lambda b, i: (b, jnp.minimum((i + 1) * r8, last8), 0))]
                 + [const(a) for a in small],
        out_specs=[tok] * 11,
        out_shape=[jax.ShapeDtypeStruct((B, TT, W), F32)] * 2 + [jax.ShapeDtypeStruct((B, TT, W), BF16)] * 9,
        compiler_params=pltpu.CompilerParams(
            dimension_semantics=("parallel", "parallel"), vmem_limit_bytes=VMEM_LIMIT_BYTES),
        name="rwkv_features",
    )(prw, prw, prw, *small)


def _split_bf16(x, pieces):
    out = []
    for _ in range(pieces - 1):
        hi = x.astype(BF16)
        out.append(hi)
        x = x - hi.astype(F32)
    out.append(x.astype(BF16))
    return out


def _block_diag(x, groups):
    xb = x.astype(BF16)
    rows, lanes = xb.shape
    t = jnp.concatenate([xb] * groups, axis=0)
    ri = lax.broadcasted_iota(jnp.int32, t.shape, 0) // rows
    li = lax.broadcasted_iota(jnp.int32, t.shape, 1) // (lanes // groups)
    return jnp.where(ri == li, t, jnp.zeros_like(t))


def _scan_kernel(lw0, kd0, b0, kkf, vf, rf, lw1, kd1, b1, kkr, vr, rr, y0_ref, y1_ref, s_ref):
    step = pl.program_id(1)

    @pl.when(step == 0)
    def _():
        s_ref[...] = jnp.zeros_like(s_ref)

    C = lw0.shape[1]
    G = SCAN_HEADS_PER_PASS
    L = G * RW_HEAD
    assert C == RW_HEAD
    ti = lax.broadcasted_iota(jnp.int32, (C, L), 0)
    si = lax.broadcasted_iota(jnp.int32, (C, L), 1) % C
    tc = lax.broadcasted_iota(jnp.int32, (C, C), 0)
    sc = lax.broadcasted_iota(jnp.int32, (C, C), 1)
    same_head = (lax.broadcasted_iota(jnp.int32, (L, L), 0) // RW_HEAD
                 == lax.broadcasted_iota(jnp.int32, (L, L), 1) // RW_HEAD)

    groups = []
    for z, (lw, kd, b, kk, v, r) in enumerate(((lw0, kd0, b0, kkf, vf, rf), (lw1, kd1, b1, kkr, vr, rr))):
        rev = z == 1
        incl = (si >= ti) if rev else (si <= ti)
        strict = (si > ti) if rev else (si < ti)
        tri = jnp.where((sc >= tc) if rev else (sc <= tc), 1.0, 0.0).astype(BF16)
        for n in range(lw.shape[0]):
            logw = lw[n]
            cum = sum(_dot(tri, piece) for piece in _split_bf16(logw, 3))
            tot = jnp.sum(logw, axis=0, keepdims=True)
            p_in = jnp.exp(-cum)
            p_end = jnp.exp(tot - cum)
            kkz, rz, bz, kdz = (t[n].astype(F32) for t in (kk, r, b, kd))
            lhs = jnp.concatenate([-kkz * jnp.exp(cum - logw), rz * jnp.exp(cum)], axis=0)
            rhs_b = bz * p_in
            rhs_k = kdz * p_in
            upd = jnp.concatenate([bz * p_end, kdz * p_end], axis=0)
            dec = jnp.exp(tot)
            vz = v[n]
            for j in range(RW_HEADS // G):
                c = slice(j * L, (j + 1) * L)
                groups.append(dict(z=z, smp=n, j=j, lhs=lhs[:, c].astype(BF16), rhs_b=rhs_b[:, c],
                                   rhs_k=rhs_k[:, c], upd=upd[:, c].astype(BF16), dec=dec[:, c],
                                   v=vz[:, c], strict=strict, incl=incl))

    for g in groups:
        aa_b = _dot_nt(g["lhs"], _block_diag(g["rhs_b"], G))
        aa_k = _dot_nt(g["lhs"], _block_diag(g["rhs_k"], G))
        g["a_k"] = jnp.concatenate([jnp.where(g["strict"], aa_k[:C], 0.0),
                                    jnp.where(g["incl"], aa_k[C:], 0.0)], axis=0).astype(BF16)
        g["a_rb"] = jnp.where(g["incl"], aa_b[C:], 0.0).astype(BF16)
        g["n"] = jnp.where(g["strict"], aa_b[:C], 0.0)
        g["apow"] = _dot(g["n"].astype(BF16), _block_diag(g["n"], G))
    levels = int(math.log2(C)) - 1
    for lv in range(levels):
        for g in groups:
            pbd = _block_diag(g["apow"], G)
            if lv < levels - 1:
                st = _dot(jnp.concatenate([g["n"], g["apow"]], axis=0).astype(BF16), pbd)
                g["n"] = g["n"] + g["apow"] + st[:C]
                g["apow"] = st[C:]
            else:
                g["n"] = g["n"] + g["apow"] + _dot(g["n"].astype(BF16), pbd)
    for g in groups:
        g["s0"] = s_ref[g["smp"], g["z"], g["j"]]
        g["gh"] = _dot_nt(g["lhs"], g["s0"].astype(BF16))
    for g in groups:
        st = _dot(g["a_k"], _block_diag(g["v"], G))
        g["w"] = g["gh"][:C] + st[:C]
        g["yk"] = st[C:]
    for g in groups:
        g["u"] = g["w"] + _dot(g["n"].astype(BF16), _block_diag(g["w"], G))
    for g in groups:
        g["y"] = g["gh"][C:] + g["yk"] + _dot(g["a_rb"], _block_diag(g["u"], G))
    for g in groups:
        uv = jnp.concatenate([g["u"].astype(BF16), g["v"]], axis=0)
        s_ref[g["smp"], g["z"], g["j"]] = g["s0"] * g["dec"] + jnp.where(same_head, _dot_tn(uv, g["upd"]), 0.0)
    for z, y_ref in enumerate((y0_ref, y1_ref)):
        for n in range(y_ref.shape[0]):
            y_ref[n] = jnp.concatenate([g["y"] for g in groups if g["z"] == z and g["smp"] == n],
                                       axis=-1).astype(y_ref.dtype)


def _rwkv_scan(lw0, lw1, kd0, kd1, b0, b1, kk, v, r, t_lat):
    B, TT, W = kk.shape
    C = SCAN_CHUNK
    ncl = t_lat // C
    nch = TT // C
    ncc = nch - ncl

    def fwd(s):
        return jnp.where(s < ncc, ncl + s, s - ncc)

    nb = math.gcd(B, SCAN_BATCH)
    fs =pl.BlockSpec((nb, C, W), lambda b, s: (b, fwd(s), 0))
    rs = pl.BlockSpec((nb, C, W), lambda b, s: (b, nch - 1 - s, 0))
    return pl.pallas_call(
        _scan_kernel,
        grid=(B // nb, nch),
        in_specs=[fs] * 6 + [rs] * 6,
        out_specs=[fs, rs],
        out_shape=[jax.ShapeDtypeStruct((B, TT, W), BF16)] * 2,
        scratch_shapes=[pltpu.VMEM((nb, 2, RW_HEADS // SCAN_HEADS_PER_PASS, SCAN_HEADS_PER_PASS * RW_HEAD,
                                    SCAN_HEADS_PER_PASS * RW_HEAD), F32)],
        compiler_params=pltpu.CompilerParams(
            dimension_semantics=("parallel", "arbitrary"), vmem_limit_bytes=VMEM_LIMIT_BYTES),
        name="rwkv_scan",
    )(lw0, kd0, b0, kk, v, r, lw1, kd1, b1, kk, v, r)


def _merge_kernel(x_ref, yda_ref, y0_ref, y1_ref, bonus_ref, g_ref, ysg_ref, gate_ref,
                  g1_ref, sh2_ref, sc2_ref, wb_ref, wo_ref, ln1g_ref, ln1b_ref, rlng_ref, rlnb_ref,
                  ones_ref, wr_ref, xmid_ref, h2_ref, aff_ref, *, alpha):
    D = D_MODEL
    tm = x_ref.shape[1]
    subs = [slice(r, r + MERGE_ROWS) for r in range(0, tm, MERGE_ROWS)]
    ones_bd = ones_ref[...]
    wr_hi, wr_lo = _split_bf16(wr_ref[...], 2)
    wr_hl = jnp.concatenate([wr_hi, wr_lo], axis=0)

    yrw = []
    for sl in subs:
        y = y0_ref[0, sl, :].astype(F32) + y1_ref[0, sl, :].astype(F32)
        mu = _seg_sum(y, ones_bd) * (1.0 / RW_HEAD)
        dy = y - mu
        var = _seg_sum(dy * dy, ones_bd) * (1.0 / RW_HEAD)
        gn = dy * lax.rsqrt(var + RW_GN_EPS) * rlng_ref[...] + rlnb_ref[...]
        yrw.append(((gn + bonus_ref[0, sl, :].astype(F32)) * g_ref[0, sl, :].astype(F32)).astype(BF16))
    ms = []
    for sl, yr in zip(subs, yrw):
        m = gate_ref[0, sl, 0:D].astype(F32) * _dot(yda_ref[0, sl, :], wb_ref[0:DA_WIDTH, :])
        m = m + gate_ref[0, sl, D:2 * D].astype(F32) * _dot(yr, wb_ref[DA_WIDTH:DA_WIDTH + RW_WIDTH, :])
        m = m + gate_ref[0, sl, 2 * D:3 * D].astype(F32) * _dot(ysg_ref[0, sl, :], wb_ref[DA_WIDTH + RW_WIDTH:, :])
        ms.append(m.astype(BF16))
    mixes = [_dot(m, wo_ref[...]) for m in ms]
    h2s = []
    for sl, mix in zip(subs, mixes):
        z = alpha * x_ref[0, sl, :] + g1_ref[0, 0] * mix
        zm = jnp.mean(z, axis=-1, keepdims=True)
        dz = z - zm
        zv = jnp.mean(dz * dz, axis=-1, keepdims=True)
        xmid = dz * lax.rsqrt(zv + LN_EPS) * ln1g_ref[...] + ln1b_ref[...]
        xmid_ref[0, sl, :] = xmid
        h2 = xmid * (1.0 + sc2_ref[0, 0]) + sh2_ref[0, 0]
        h2_ref[0, sl, :] = h2.astype(BF16)
        h2s.append(h2)
    ne = wr_hi.shape[0]
    for sl, h2 in zip(subs, h2s):
        h_hi, h_lo = _split_bf16(h2, 2)
        part = _dot_nt(wr_hl, h_hi)
        logits = part[:ne] + part[ne:] + _dot_nt(wr_hi, h_lo)
        e = jnp.exp(logits - jnp.max(logits, axis=0, keepdims=True))
        aff_ref[0, :, sl] = e / jnp.sum(e, axis=0, keepdims=True)


def _merge(x_all, yda, y0, y1, bonus, g, ysg, gates, modall, w_branch, w_out, ln1g, ln1b,
           rlng, rlnb, ones_bd, w_router_t, ntl, alpha):
    B, TT, D = x_all.shape
    tm = TOKEN_TILE
    nt = TT // tm
    tok = lambda w: pl.BlockSpec((1, tm, w), lambda b, i: (b, i, 0))
    modspec = lambda j: pl.BlockSpec((1, 1, 1, D), lambda b, i: (b, i // ntl, 0, j))
    const = lambda a: pl.BlockSpec(a.shape, lambda b, i: (0,) * a.ndim)
    consts = [w_branch, w_out, ln1g, ln1b, rlng, rlnb, ones_bd, w_router_t]
    return pl.pallas_call(
        functools.partial(_merge_kernel, alpha=alpha),
        grid=(B, nt),
        in_specs=[tok(D), tok(DA_WIDTH), tok(RW_WIDTH), tok(RW_WIDTH), tok(RW_WIDTH), tok(RW_WIDTH),
                  tok(SG_WIDTH), tok(3 * D), modspec(2), modspec(3), modspec(4)]
                 + [const(a) for a in consts],
        out_specs=[tok(D), tok(D), pl.BlockSpec((1, N_EXPERTS, tm), lambda b, i: (b, 0, i))],
        out_shape=[jax.ShapeDtypeStruct((B, TT, D), F32),
                   jax.ShapeDtypeStruct((B, TT, D), BF16),
                   jax.ShapeDtypeStruct((B, N_EXPERTS, TT), F32)],
        compiler_params=pltpu.CompilerParams(
            dimension_semantics=("parallel", "parallel"), vmem_limit_bytes=VMEM_LIMIT_BYTES),
        name="merge_ln1_router",
    )(x_all, yda, y0, y1, bonus, g, ysg, gates, modall, modall, modall, *consts)


def _topk_kernel(aff_ref, tri_ref, rank_ref, *, cap):
    a = aff_ref[0]
    bits = pltpu.bitcast(a, jnp.int32)
    thr = jnp.zeros((a.shape[0], 1), jnp.int32)
    for bit in range(30, -1, -1):
        cand = thr | (1 << bit)
        cnt = jnp.sum(jnp.where(bits >= cand, 1.0, 0.0), axis=-1, keepdims=True)
        thr = jnp.where(cnt >= cap, cand, thr)
    gt = bits > thr
    eq = bits == thr
    need = cap - jnp.sum(jnp.where(gt, 1.0, 0.0), axis=-1, keepdims=True)
    tri = tri_ref[...]
    kb = tri.shape[0]

    def excl_cumsum(mask):
        m = jnp.where(mask, 1.0, 0.0)
        outs = []
        carry = jnp.zeros((m.shape[0], 1), F32)
        for j in range(m.shape[1] // kb):
            blk = m[:, j * kb:(j + 1) * kb]
            outs.append(_dot(blk.astype(BF16), tri) + carry)
            carry = carry + jnp.sum(blk, axis=-1, keepdims=True)
        return jnp.concatenate(outs, axis=1)

    sel = jnp.logical_or(gt, jnp.logical_and(eq, excl_cumsum(eq) < need))
    rank_ref[0] = jnp.where(sel, excl_cumsum(sel), -1.0).astype(jnp.int32)


def _topk_ranks(aff, tri, t_off, t_len, cap):
    B, E, _ = aff.shape
    blk = t_off // t_len
    return pl.pallas_call(
        functools.partial(_topk_kernel, cap=cap),
        grid=(B,),
        in_specs=[pl.BlockSpec((1, E, t_len), lambda b: (b, 0, blk)),
                  pl.BlockSpec(tri.shape, lambda b: (0, 0))],
        out_specs=pl.BlockSpec((1, E, t_len), lambda b: (b, 0, 0)),
        out_shape=jax.ShapeDtypeStruct((B, E, t_len), jnp.int32),
        compiler_params=pltpu.CompilerParams(
            dimension_semantics=("parallel",), vmem_limit_bytes=VMEM_LIMIT_BYTES),
        name="expert_choice_ranks",
    )(aff, tri)


def _moe_kernel(*refs, sets):
    n = len(sets)
    h_ref, aff_ref = refs[0], refs[1]
    rank_refs = refs[2:2 + n]
    wg_ref, wu_ref, wd_ref, f_ref = refs[2 + n:]
    e = pl.program_id(1)

    @pl.when(e == 0)
    def _():
        f_ref[...] = jnp.zeros_like(f_ref)

    onehots, gates, xs = [], [], []
    for (t0, tn, cap), rank_ref in zip(sets, rank_refs):
        rank = rank_ref[0, pl.ds(e, 1), :]
        aff = aff_ref[0, pl.ds(e, 1), t0:t0 + tn]
        slot = lax.broadcasted_iota(jnp.int32, (cap, tn), 0)
        hit = rank == slot
        onehot = jnp.where(hit, 1.0, 0.0).astype(BF16)
        onehots.append(onehot)
        gates.append(jnp.sum(jnp.where(hit, aff, 0.0), axis=-1, keepdims=True))
        xs.append(_dot(onehot, h_ref[0, t0:t0 + tn, :]).astype(BF16))
    xe = jnp.concatenate(xs, axis=0) if n > 1 else xs[0]
    gate = jnp.concatenate(gates, axis=0) if n > 1 else gates[0]
    hg = _dot(xe, wg_ref[0, 0])
    hid = (hg * _sigmoid(hg)) * _dot(xe, wu_ref[0, 0])
    ye = (_dot(hid.astype(BF16), wd_ref[0, 0]) * gate).astype(BF16)
    r0 = 0
    for (t0, tn, cap), onehot in zip(sets, onehots):
        f_ref[0, t0:t0 + tn, :] += _dot_tn(onehot, ye[r0:r0 + cap])
        r0 += cap


def _expert_ffn(h2, aff, ranks, wg, wu, wd, layer, sets):
    B, TT, D = h2.shape
    E = aff.shape[1]
    F = wg.shape[3]
    return pl.pallas_call(
        functools.partial(_moe_kernel, sets=sets),
        grid=(B, E),
        in_specs=[pl.BlockSpec((1, TT, D), lambda b, e: (b, 0, 0), pipeline_mode=pl.Buffered(1)),
                  pl.BlockSpec((1, E, TT), lambda b, e: (b, 0, 0))]
                 + [pl.BlockSpec((1, E, r.shape[2]), lambda b, e: (b, 0, 0)) for r in ranks]
                 + [pl.BlockSpec((1, 1, D, F), lambda b, e: (layer, e, 0, 0)),
                    pl.BlockSpec((1, 1, D, F), lambda b, e: (layer, e, 0, 0)),
                    pl.BlockSpec((1, 1, F, D), lambda b, e: (layer, e, 0, 0))],
        out_specs=pl.BlockSpec((1, TT, D), lambda b, e: (b, 0, 0)),
        out_shape=jax.ShapeDtypeStruct((B, TT, D), F32),
        compiler_params=pltpu.CompilerParams(
            dimension_semantics=("parallel", "arbitrary"), vmem_limit_bytes=VMEM_LIMIT_BYTES),
        name="expert_ffn",
    )(h2, aff, *ranks, wg, wu, wd)


def _ln2_kernel(x_ref, f_ref, g2_ref, lng_ref, lnb_ref, o_ref, *, alpha):
    z = alpha * x_ref[0] + g2_ref[0, 0] * f_ref[0]
    zm = jnp.mean(z, axis=-1, keepdims=True)
    dz = z - zm
    zv = jnp.mean(dz * dz, axis=-1, keepdims=True)
    o_ref[0] = dz * lax.rsqrt(zv + LN_EPS) * lng_ref[...] + lnb_ref[...]


def _final_norm(xmid, f, modall, lng, lnb, ntl, alpha, rows):
    B, _, D = xmid.shape
    tm = TOKEN_TILE
    tok = pl.BlockSpec((1, tm, D), lambda b, i: (b, i, 0))
    const = lambda a: pl.BlockSpec(a.shape, lambda b, i: (0,) * a.ndim)
    return pl.pallas_call(
        functools.partial(_ln2_kernel, alpha=alpha),
        grid=(B, rows // tm),
        in_specs=[tok, tok, pl.BlockSpec((1, 1, 1, D), lambda b, i: (b, i // ntl, 0, 5)),
                  const(lng), const(lnb)],
        out_specs=tok,
        out_shape=jax.ShapeDtypeStruct((B, rows, D), F32),
        compiler_params=pltpu.CompilerParams(dimension_semantics=("parallel", "parallel")),
        name="ln2",
    )(xmid, f, modall, lng, lnb)


def _rope_tables(t_lat, t_ctx):
    rows = t_lat // GRID_W
    row = jnp.repeat(jnp.arange(rows, dtype=F32), GRID_W)
    col = jnp.tile(jnp.arange(GRID_W, dtype=F32), rows)
    half = DA_QK_DIM // 2
    inv_freq = ROPE_BASE ** (-jnp.arange(0, half, 2, dtype=F32) / half)
    ar = row[:, None] * inv_freq
    ac = col[:, None] * inv_freq
    ang = jnp.concatenate([ar, ar, ac, ac], axis=-1)
    sign = jnp.where((jnp.arange(DA_QK_DIM) % 32) < 16, -1.0, 1.0).astype(F32)
    reps = DA_QK_COLS // DA_QK_DIM
    cos = jnp.tile(jnp.cos(ang), (1, reps))
    sin = jnp.tile(jnp.sin(ang) * sign, (1, reps))
    cos = jnp.concatenate([cos, jnp.ones((t_ctx, DA_QK_COLS), F32)], axis=0)
    sin = jnp.concatenate([sin, jnp.zeros((t_ctx, DA_QK_COLS), F32)], axis=0)
    return cos, sin


def kernel(x, c, ctx, c_ctx, w_mod, b_mod, w_in, da_lambda, da_norm_g, rw_shift_mu, rw_w0, rw_w2, rw_a0, rw_a2, rw_k_k, rw_k_a, rw_r_k, rw_ln_g, rw_ln_b, rw_g2, sg_norm_g, sg_norm_b, sg_w, sg_b, w_branch, w_out, ln1_g, ln1_b, w_router, w_e_gate, w_e_up, w_e_down, ln2_g, ln2_b):
    B, T, D = x.shape
    Tc = ctx.shape[1]
    depth = w_mod.shape[0]
    tm = TOKEN_TILE
    assert D == D_MODEL and T % tm == 0 and Tc % tm == 0 and T % Tc == 0 and Tc <= T
    ntl = T // tm
    alpha = (2 * depth) ** 0.25
    cap_lat = EC_CAPACITY * T // N_EXPERTS
    cap_ctx = EC_CAPACITY * Tc // N_EXPERTS

    cos, sin = _rope_tables(T, Tc)
    lane = jnp.arange(RW_WIDTH)
    ones_bd = (lane[:, None] // RW_HEAD == lane[None, :] // RW_HEAD).astype(BF16)
    kb = math.gcd(Tc, RANK_BLOCK)
    tri = (jnp.arange(kb)[:, None] < jnp.arange(kb)[None, :]).astype(BF16)
    rows = ((B + 1 + 7) // 8) * 8
    cc = jnp.concatenate([c, c_ctx[None, :], jnp.zeros((rows - B - 1, D), F32)], axis=0)
    row2 = lambda a: a.reshape(1, -1)

    w_in_bf = w_in.astype(BF16)
    wg, wu, wd = w_e_gate.astype(BF16), w_e_up.astype(BF16), w_e_down.astype(BF16)
    x_all = jnp.concatenate([x, ctx], axis=1)
    for l in range(depth):
        last = l == depth - 1
        lam_init = 0.8 - 0.6 * math.exp(-0.3 * l)
        mod = _modulation(cc, w_mod[l], b_mod[l])
        modall = jnp.stack([mod[:B], jnp.broadcast_to(mod[B], (B, 6 * D))], axis=1)
        modall = modall.reshape(B, 2, 1, 6 * D)

        sgbias = jnp.repeat(sg_b[l].T, SG_WIDTH // SG_GROUPS, axis=1)
        q, k, v, prw, ysg, gates = _input_projection(
            x_all, modall, w_in_bf, l, cos, sin, row2(sg_norm_g[l]), row2(sg_norm_b[l]),
            sg_w[l].astype(BF16), sgbias, ntl)

        yda = _diff_attention(q, k, v, da_lambda[l], da_norm_g[l], ntl, lam_init)

        cat2 = lambda a: jnp.transpose(a, (1, 0, 2)).reshape(a.shape[1], 2 * RW_WIDTH)
        feats = _rwkv_features(
            prw, rw_shift_mu[l], row2(rw_w0[l]), cat2(rw_w2[l]).astype(BF16), row2(rw_a0[l]),
            cat2(rw_a2[l]).astype(BF16), row2(rw_k_k[l]), row2(rw_k_a[l]), row2(rw_r_k[l]),
            rw_g2[l].astype(BF16), ones_bd, ntl)
        lw0, lw1, kd0, kd1, b0, b1, kk, vv, rr, bonus, gg = feats
        y0, y1 = _rwkv_scan(lw0, lw1, kd0, kd1, b0, b1, kk, vv, rr, T)

        xmid, h2, aff = _merge(
            x_all, yda, y0, y1, bonus, gg, ysg, gates, modall, w_branch[l].astype(BF16),
            w_out[l].astype(BF16), row2(ln1_g[l]), row2(ln1_b[l]), row2(rw_ln_g[l]),
            row2(rw_ln_b[l]), ones_bd, w_router[l].T, ntl, alpha)

        sets = ((0, T, cap_lat),) if last else ((0, T, cap_lat), (T, Tc, cap_ctx))
        ranks = [_topk_ranks(aff, tri, 0, T, cap_lat)]
        if not last:
            ranks.append(_topk_ranks(aff, tri, T, Tc, cap_ctx))
        f = _expert_ffn(h2, aff, ranks, wg, wu, wd, l, sets)
        x_all = _final_norm(xmid, f, modall, row2(ln2_g[l]), row2(ln2_b[l]), ntl, alpha,
                            T if last else T + Tc)
    return x_all
```

```python
import functools
import math

import jax
import jax.numpy as jnp
from jax import lax
from jax.experimental import pallas as pl
from jax.experimental.pallas import tpu as pltpu

F32 = jnp.float32
BF16 = jnp.bfloat16
HIGHEST = lax.Precision.HIGHEST

D_MODEL = 1024
GRID_W = 64
DA_HEADS = 4
DA_QK_DIM = 64
DA_V_DIM = 128
DA_WIDTH = 512
DA_QK_COLS = 512
ROPE_BASE = 10000.0
DA_EPS = 1e-5
RW_HEAD = 64
RW_HEADS = 8
RW_WIDTH = 512
RW_COLS = 1792
RW_GN_EPS = 64e-5
SG_CHUNK = 128
SG_GROUPS = 4
SG_WIDTH = 512
MIX_WIDTH = 1536
DA_K0 = 512
DA_V0 = 1024
RW_0 = 1536
SG_0 = RW_0 + RW_COLS
GATE_0 = SG_0 + 2 * SG_WIDTH
IN_COLS = GATE_0 + 3 * D_MODEL
N_EXPERTS = 16
EC_CAPACITY = 2
LN_EPS = 1e-5
LOG2_E = math.log2(math.e)

TOKEN_TILE = 256
SCAN_CHUNK = 64
ATTN_KEY_TILE = 256
ATTN_HEADS_PER_STEP = 4
SCAN_HEADS_PER_PASS = 4
SCAN_BATCH = 4
MERGE_ROWS = 128
RANK_BLOCK = 256
VMEM_LIMIT_BYTES = 58 * 1024 * 1024


def _dot(a, b):
    return jnp.dot(a, b, preferred_element_type=F32)


def _dot_hi(a, b):
    return jnp.dot(a, b, preferred_element_type=F32, precision=HIGHEST)


def _dot_nt(a, b, precision=None):
    return lax.dot_general(a, b, (((1,), (1,)), ((), ())), preferred_element_type=F32,
                           precision=precision)


def _dot_tn(a, b, precision=None):
    return lax.dot_general(a, b, (((0,), (0,)), ((), ())), preferred_element_type=F32,
                           precision=precision)


def _sigmoid(z):
    return 1.0 / (1.0 + jnp.exp(-z))


def _seg_sum(z, ones_bd):
    hi = z.astype(BF16)
    lo = (z - hi.astype(F32)).astype(BF16)
    return _dot(hi, ones_bd) + _dot(lo, ones_bd)


def _mod_kernel(c_ref, w_ref, b_ref, o_ref):
    cc = c_ref[...]
    o_ref[...] = _dot_hi(cc * _sigmoid(cc), w_ref[...]) + b_ref[...]


def _modulation(cc, w_mod, b_mod):
    rows, d = cc.shape
    n = w_mod.shape[1]
    tn = 1024
    return pl.pallas_call(
        _mod_kernel,
        grid=(n // tn,),
        in_specs=[pl.BlockSpec((rows, d), lambda j: (0, 0)),
                  pl.BlockSpec((d, tn), lambda j: (0, j)),
                  pl.BlockSpec((1, tn), lambda j: (0, j))],
        out_specs=pl.BlockSpec((rows, tn), lambda j: (0, j)),
        out_shape=jax.ShapeDtypeStruct((rows, n), F32),
        name="adaln_mod",
    )(cc, w_mod, b_mod.reshape(1, n))


def _inproj_kernel(xl_ref, xc_ref, sh_ref, sc_ref, w_ref, cos_ref, sin_ref, sgg_ref, sgb_ref, sgw_ref,
                   sgbias_ref, q_ref, k_ref, v_ref, rw_ref, sg_ref, gate_ref, *, ntl):
    tm = xl_ref.shape[1]
    x = jnp.where(pl.program_id(1) < ntl, xl_ref[0], xc_ref[0])
    h = (x * (1.0 + sc_ref[0, 0]) + sh_ref[0, 0]).astype(BF16)

    def proj(c0, c1):
        return _dot(h, w_ref[0, :, c0:c1])

    cos = cos_ref[...]
    sin = sin_ref[...]
    lane = lax.broadcasted_iota(jnp.int32, (tm, DA_QK_COLS), 1)
    first = (lane % 32) < 16

    def rope(z):
        zr = jnp.where(first, pltpu.roll(z, DA_QK_COLS - 16, 1), pltpu.roll(z, 16, 1))
        return z * cos + zr * sin

    q_ref[0] = (rope(proj(0, DA_K0)) * (DA_QK_DIM ** -0.5 * LOG2_E)).astype(BF16)
    k_ref[0] = rope(proj(DA_K0, DA_V0)).astype(BF16)
    v_ref[0] = proj(DA_V0, RW_0).astype(BF16)
    for c0 in range(0, RW_COLS, 896):
        rw_ref[0, :, c0:c0 + 896] = proj(RW_0 + c0, RW_0 + c0 + 896)
    for j in range(3):
        gate_ref[0, :, j * D_MODEL:(j + 1) * D_MODEL] = _sigmoid(
            proj(GATE_0 + j * D_MODEL, GATE_0 + (j + 1) * D_MODEL)).astype(BF16)

    ps = proj(SG_0, GATE_0)
    gl = ps * (0.5 * (1.0 + jnp.tanh(math.sqrt(2.0 / math.pi) * (ps + 0.044715 * (ps * ps * ps)))))
    u = gl[:, :SG_WIDTH]
    vv = gl[:, SG_WIDTH:]
    mu = jnp.mean(vv, axis=-1, keepdims=True)
    dv = vv - mu
    var = jnp.mean(dv * dv, axis=-1, keepdims=True)
    vn = (dv * lax.rsqrt(var + LN_EPS) * sgg_ref[...] + sgb_ref[...]).astype(BF16)
    gd = SG_WIDTH // SG_GROUPS
    for n in range(tm // SG_CHUNK):
        r0 = n * SG_CHUNK
        for g in range(SG_GROUPS):
            c0 = g * gd
            vm = _dot(sgw_ref[g], vn[r0:r0 + SG_CHUNK, c0:c0 + gd]) + sgbias_ref[:, c0:c0 + gd]
            sg_ref[0, r0:r0 + SG_CHUNK, c0:c0 + gd] = (u[r0:r0 + SG_CHUNK, c0:c0 + gd] * vm).astype(BF16)


def _stream_specs(d, ntl, ctx_block0):
    tm = TOKEN_TILE
    lat = pl.BlockSpec((1, tm, d), lambda b, i: (b, jnp.minimum(i, ntl - 1), 0))
    ctx = pl.BlockSpec((1, tm, d), lambda b, i: (b, ctx_block0 + jnp.maximum(i - ntl, 0), 0))
    return [lat, ctx]


def _input_projection(x_lat, x_ctx, ctx_block0, TT, modall, w_in, layer, cos, sin, sgg, sgb, sgw, sgbias, ntl):
    B, _, D = x_lat.shape
    tm = TOKEN_TILE
    nt = TT // tm
    tok = lambda w: pl.BlockSpec((1, tm, w), lambda b, i: (b, i, 0))
    modspec = lambda j: pl.BlockSpec((1, 1, 1, D), lambda b, i: (b, i // ntl, 0, j))
    const2 = lambda a: pl.BlockSpec(a.shape, lambda b, i: (0, 0))
    return pl.pallas_call(
        functools.partial(_inproj_kernel, ntl=ntl),
        grid=(B, nt),
        in_specs=_stream_specs(D, ntl, ctx_block0) + [modspec(0), modspec(1),
                  pl.BlockSpec((1,) + w_in.shape[1:], lambda b, i: (layer, 0, 0),
                               pipeline_mode=pl.Buffered(1)),
                  pl.BlockSpec((tm, DA_QK_COLS), lambda b, i: (i, 0)),
                  pl.BlockSpec((tm, DA_QK_COLS), lambda b, i: (i, 0)),
                  const2(sgg), const2(sgb),
                  pl.BlockSpec(sgw.shape, lambda b, i: (0, 0, 0)),
                  const2(sgbias)],
        out_specs=[tok(DA_QK_COLS), tok(DA_QK_COLS), tok(DA_WIDTH), tok(RW_COLS), tok(SG_WIDTH),
                   tok(3 * D)],
        out_shape=[jax.ShapeDtypeStruct((B, TT, DA_QK_COLS), BF16),
                   jax.ShapeDtypeStruct((B, TT, DA_QK_COLS), BF16),
                   jax.ShapeDtypeStruct((B, TT, DA_WIDTH), BF16),
                   jax.ShapeDtypeStruct((B, TT, RW_COLS), F32),
                   jax.ShapeDtypeStruct((B, TT, SG_WIDTH), BF16),
                   jax.ShapeDtypeStruct((B, TT, 3 * D), BF16)],
        compiler_params=pltpu.CompilerParams(
            dimension_semantics=("parallel", "parallel"), vmem_limit_bytes=VMEM_LIMIT_BYTES),
        name="in_proj",
    )(x_lat, x_ctx, modall, modall, w_in, cos, sin, sgg, sgb, sgw, sgbias)


def _attn_kernel(q_ref, k_ref, v_ref, lam_ref, g_ref, o_ref, *, ntl, t_lat, lam_init):
    i = pl.program_id(2)
    lp = lam_ref[...]
    lam = (jnp.exp(jnp.sum(lp[0:1] * lp[1:2], axis=-1, keepdims=True))
           - jnp.exp(jnp.sum(lp[2:3] * lp[3:4], axis=-1, keepdims=True)) + lam_init)
    dv = DA_V_DIM
    heads = [slice(h * dv, (h + 1) * dv) for h in range(q_ref.shape[2] // dv)]
    lane = lax.broadcasted_iota(jnp.int32, (q_ref.shape[1], dv), 1)
    qs = []
    for hs in heads:
        q = q_ref[0, :, hs]
        zero = jnp.zeros_like(q)
        qs.append((jnp.where(lane < DA_QK_DIM, q, zero), jnp.where(lane >= DA_QK_DIM, q, zero)))

    def attend(k0, nk):
        kt = ATTN_KEY_TILE
        tiles = [slice(k0 + t * kt, k0 + (t + 1) * kt) for t in range(nk // kt)]

        def row_max(ss):
            m = ss[0]
            for s in ss[1:]:
                m = jnp.maximum(m, s)
            return jnp.max(m, axis=-1, keepdims=True)

        def pv(es, hs):
            acc = None
            for e, sl in zip(es, tiles):
                v = v_ref[0, sl, hs]
                d = _dot(e, jnp.concatenate([v, jnp.ones_like(v)], axis=1))
                acc = d if acc is None else acc + d
            return acc[:, :dv] / acc[:, dv:]

        ks = [[k_ref[0, sl, hs] for sl in tiles] for hs in heads]
        s0 = [[_dot_nt(q0, kk) for kk in kh] for (q0, _), kh in zip(qs, ks)]
        m0 = [row_max(s) for s in s0]
        s1, e0 = [], []
        for (_, q1), kh, sh, mh in zip(qs, ks, s0, m0):
            s1.append([])
            e0.append([])
            for kk, s in zip(kh, sh):
                s1[-1].append(_dot_nt(q1, kk))
                e0[-1].append(jnp.exp2(s - mh).astype(BF16))
        m1 = [row_max(s) for s in s1]
        o0 = [pv(e, hs) for e, hs in zip(e0, heads)]
        e1 = [[jnp.exp2(s - mh).astype(BF16) for s in sh] for sh, mh in zip(s1, m1)]
        for hs, oa, e in zip(heads, o0, e1):
            o = oa - lam * pv(e, hs)
            o = o * lax.rsqrt(jnp.mean(o * o, axis=-1, keepdims=True) + DA_EPS) * g_ref[...]
            o_ref[0, :, hs] = (o * (1.0 - lam_init)).astype(BF16)

    @pl.when(i < ntl)
    def _():
        attend(0, k_ref.shape[1])

    @pl.when(i >= ntl)
    def _():
        attend(t_lat, k_ref.shape[1] - t_lat)


def _diff_attention(q, k, v, lam_p, norm_g, ntl, lam_init):
    B, TT, _ = q.shape
    tm = TOKEN_TILE
    nt = TT // tm
    w = ATTN_HEADS_PER_STEP * DA_V_DIM
    kv = pl.BlockSpec((1, TT, w), lambda b, h, i: (b, 0, h))
    qo = pl.BlockSpec((1, tm, w), lambda b, h, i: (b, i, h))
    return pl.pallas_call(
        functools.partial(_attn_kernel, ntl=ntl, t_lat=ntl * tm, lam_init=lam_init),
        grid=(B, DA_HEADS // ATTN_HEADS_PER_STEP, nt),
        in_specs=[qo, kv, kv,
                  pl.BlockSpec(lam_p.shape, lambda b, h, i: (0, 0)),
                  pl.BlockSpec((1, DA_V_DIM), lambda b, h, i: (0, 0))],
        out_specs=qo,
        out_shape=jax.ShapeDtypeStruct((B, TT, DA_WIDTH), BF16),
        compiler_params=pltpu.CompilerParams(
            dimension_semantics=("parallel", "parallel", "parallel"),
            vmem_limit_bytes=VMEM_LIMIT_BYTES),
        name="diff_attn",
    )(q, k, v, lam_p, norm_g.reshape(1, DA_V_DIM))


def _rwfeat_kernel(p_ref, pp_ref, pn_ref, mu_ref, w0_ref, w2_ref, a0_ref, a2_ref, kk_ref, ka_ref,
                   rk_ref, g2_ref, ones_ref,
                   lw0_ref, lw1_ref, kd0_ref, kd1_ref, b0_ref, b1_ref, kko_ref, v_ref, r_ref,
                   bonus_ref, g_ref, *, ntl, nt):
    i = pl.program_id(1)
    tm = p_ref.shape[1]
    p = p_ref[0]
    prev_ok = jnp.logical_and(i != 0, i != ntl)
    next_ok = jnp.logical_and(i != ntl - 1, i != nt - 1)
    prev_row = jnp.where(prev_ok, pp_ref[0, 7:8, :], 0.0)
    next_row = jnp.where(next_ok, pn_ref[0, 0:1, :], 0.0)
    row = lax.broadcasted_iota(jnp.int32, p.shape, 0)
    prev = jnp.where(row == 0, prev_row, pltpu.roll(p, 1, 0))
    nxt = jnp.where(row == tm - 1, next_row, pltpu.roll(p, tm - 1, 0))
    ps = p + mu_ref[0:1, :] * (prev - p) + mu_ref[1:2, :] * (nxt - p)

    W = RW_WIDTH
    r = ps[:, 0:W]
    k = ps[:, W:2 * W]
    v = ps[:, 2 * W:3 * W]
    xw = ps[:, 3 * W:3 * W + 64]
    xa = ps[:, 3 * W + 64:3 * W + 128]
    xg = ps[:, 3 * W + 128:3 * W + 256]
    ones_bd = ones_ref[...]

    w_pre = _dot(jnp.tanh(xw).astype(BF16), w2_ref[...]) + w0_ref[...]
    sp = jnp.maximum(-w_pre, 0.0) + jnp.log(1.0 + jnp.exp(-jnp.abs(w_pre)))
    logw = -jnp.exp(-sp - 0.5)
    a = _sigmoid(_dot(xa.astype(BF16), a2_ref[...]) + a0_ref[...])

    kx = k * kk_ref[...]
    nrm = jnp.sqrt(_seg_sum(kx * kx, ones_bd))
    kk = kx / jnp.maximum(nrm, 1e-12)
    ka = ka_ref[...]
    kd0 = k * (1.0 + (a[:, :W] - 1.0) * ka)
    kd1 = k * (1.0 + (a[:, W:] - 1.0) * ka)
    g = _dot(_sigmoid(xg).astype(BF16), g2_ref[...])
    k_b = 0.5 * (kd0 + kd1)
    bonus = _seg_sum(r * k_b * rk_ref[...], ones_bd) * v

    lw0_ref[0] = logw[:, :W]
    lw1_ref[0] = logw[:, W:]
    kd0_ref[0] = kd0.astype(BF16)
    kd1_ref[0] = kd1.astype(BF16)
    b0_ref[0] = (kk * a[:, :W]).astype(BF16)
    b1_ref[0] = (kk * a[:, W:]).astype(BF16)
    kko_ref[0] = kk.astype(BF16)
    v_ref[0] = v.astype(BF16)
    r_ref[0] = r.astype(BF16)
    bonus_ref[0] = bonus.astype(BF16)
    g_ref[0] = g.astype(BF16)


def _rwkv_features(prw, mu, w0, w2, a0, a2, k_k, k_a, r_k, g2, ones_bd, ntl):
    B, TT, _ = prw.shape
    tm = TOKEN_TILE
    nt = TT // tm
    r8 = tm // 8
    last8 = TT // 8 - 1
    W = RW_WIDTH
    const = lambda a: pl.BlockSpec(a.shape, lambda b, i: (0,) * a.ndim)
    tok = pl.BlockSpec((1, tm, W), lambda b, i: (b, i, 0))
    small = [mu, w0, w2, a0, a2, k_k, k_a, r_k, g2, ones_bd]
    return pl.pallas_call(
        functools.partial(_rwfeat_kernel, ntl=ntl, nt=nt),
        grid=(B, nt),
        in_specs=[pl.BlockSpec((1, tm, RW_COLS), lambda b, i: (b, i, 0)),
                  pl.BlockSpec((1, 8, RW_COLS), lambda b, i: (b, jnp.maximum(i * r8 - 1, 0), 0)),
                  pl.BlockSpec((1, 8, RW_COLS), lambda b, i: (b, jnp.minimum((i + 1) * r8, last8), 0))]
                 + [const(a) for a in small],
        out_specs=[tok] * 11,
        out_shape=[jax.ShapeDtypeStruct((B, TT, W), F32)] * 2 + [jax.ShapeDtypeStruct((B, TT, W), BF16)] * 9,
        compiler_params=pltpu.CompilerParams(
            dimension_semantics=("parallel", "parallel"), vmem_limit_bytes=VMEM_LIMIT_BYTES),
        name="rwkv_features",
    )(prw, prw, prw, *small)


def _split_bf16(x, pieces):
    out = []
    for _ in range(pieces - 1):
        hi = x.astype(BF16)
        out.append(hi)
        x = x - hi.astype(F32)
    out.append(x.astype(BF16))
    return out


def _block_diag(x, groups):
    xb = x.astype(BF16)
    rows, lanes = xb.shape
    t = jnp.concatenate([xb] * groups, axis=0)
    ri = lax.broadcasted_iota(jnp.int32, t.shape, 0) // rows
    li = lax.broadcasted_iota(jnp.int32, t.shape, 1) // (lanes // groups)
    return jnp.where(ri == li, t, jnp.zeros_like(t))


def _scan_kernel(lw0, kd0, b0, kkf, vf, rf, lw1, kd1, b1, kkr, vr, rr, y0_ref, y1_ref, s_ref):
    step = pl.program_id(1)

    @pl.when(step == 0)
    def _():
        s_ref[...] = jnp.zeros_like(s_ref)

    C = lw0.shape[1]
    G = SCAN_HEADS_PER_PASS
    L = G * RW_HEAD
    assert C == RW_HEAD
    ti = lax.broadcasted_iota(jnp.int32, (C, L), 0)
    si = lax.broadcasted_iota(jnp.int32, (C, L), 1) % C
    tc = lax.broadcasted_iota(jnp.int32, (C, C), 0)
    sc = lax.broadcasted_iota(jnp.int32, (C, C), 1)
    same_head = (lax.broadcasted_iota(jnp.int32, (L, L), 0) // RW_HEAD
                 == lax.broadcasted_iota(jnp.int32, (L, L), 1) // RW_HEAD)

    groups = []
    for z, (lw, kd, b, kk, v, r) in enumerate(((lw0, kd0, b0, kkf, vf, rf), (lw1, kd1, b1, kkr, vr, rr))):
        rev = z == 1
        incl = (si >= ti) if rev else (si <= ti)
        strict = (si > ti) if rev else (si < ti)
        tri = jnp.where((sc >= tc) if rev else (sc <= tc), 1.0, 0.0).astype(BF16)
        for n in range(lw.shape[0]):
            logw = lw[n]
            cum = sum(_dot(tri, piece) for piece in _split_bf16(logw, 3))
            tot = jnp.sum(logw, axis=0, keepdims=True)
            p_in = jnp.exp(-cum)
            p_end = jnp.exp(tot - cum)
            kkz, rz, bz, kdz = (t[n].astype(F32) for t in (kk, r, b, kd))
            lhs = jnp.concatenate([-kkz * jnp.exp(cum - logw), rz * jnp.exp(cum)], axis=0)
            rhs_b = bz * p_in
            rhs_k = kdz * p_in
            upd = jnp.concatenate([bz * p_end, kdz * p_end], axis=0)
            dec = jnp.exp(tot)
            vz = v[n]
            for j in range(RW_HEADS // G):
                c = slice(j * L, (j + 1) * L)
                groups.append(dict(z=z, smp=n, j=j, lhs=lhs[:, c].astype(BF16), rhs_b=rhs_b[:, c],
                                   rhs_k=rhs_k[:, c], upd=upd[:, c].astype(BF16), dec=dec[:, c],
                                   v=vz[:, c], strict=strict, incl=incl))

    for g in groups:
        aa_b = _dot_nt(g["lhs"], _block_diag(g["rhs_b"], G))
        aa_k = _dot_nt(g["lhs"], _block_diag(g["rhs_k"], G))
        g["a_k"] = jnp.concatenate([jnp.where(g["strict"], aa_k[:C], 0.0),
                                    jnp.where(g["incl"], aa_k[C:], 0.0)], axis=0).astype(BF16)
        g["a_rb"] = jnp.where(g["incl"], aa_b[C:], 0.0).astype(BF16)
        g["n"] = jnp.where(g["strict"], aa_b[:C], 0.0)
        g["apow"] = _dot(g["n"].astype(BF16), _block_diag(g["n"], G))
    levels = int(math.log2(C)) - 1
    for lv in range(levels):
        for g in groups:
            pbd = _block_diag(g["apow"], G)
            if lv < levels - 1:
                st = _dot(jnp.concatenate([g["n"], g["apow"]], axis=0).astype(BF16), pbd)
                g["n"] = g["n"] + g["apow"] + st[:C]
                g["apow"] = st[C:]
            else:
                g["n"] = g["n"] + g["apow"] + _dot(g["n"].astype(BF16), pbd)
    for g in groups:
        g["s0"] = s_ref[g["smp"], g["z"], g["j"]]
        g["gh"] = _dot_nt(g["lhs"], g["s0"].astype(BF16))
    for g in groups:
        st = _dot(g["a_k"], _block_diag(g["v"], G))
        g["w"] = g["gh"][:C] + st[:C]
        g["yk"] = st[C:]
    for g in groups:
        g["u"] = g["w"] + _dot(g["n"].astype(BF16), _block_diag(g["w"], G))
    for g in groups:
        g["y"] = g["gh"][C:] + g["yk"] + _dot(g["a_rb"], _block_diag(g["u"], G))
    for g in groups:
        uv = jnp.concatenate([g["u"].astype(BF16), g["v"]], axis=0)
        s_ref[g["smp"], g["z"], g["j"]] = g["s0"] * g["dec"] + jnp.where(same_head, _dot_tn(uv, g["upd"]), 0.0)
    for z, y_ref in enumerate((y0_ref, y1_ref)):
        for n in range(y_ref.shape[0]):
            y_ref[n] = jnp.concatenate([g["y"] for g in groups if g["z"] == z and g["smp"] == n],
                                       axis=-1).astype(y_ref.dtype)


def _rwkv_scan(lw0, lw1, kd0, kd1, b0, b1, kk, v, r, t_lat):
    B, TT, W = kk.shape
    C = SCAN_CHUNK
    ncl = t_lat // C
    nch = TT // C
    ncc = nch - ncl

    def fwd(s):
        return jnp.where(s < ncc, ncl + s, s - ncc)

    nb = math.gcd(B, SCAN_BATCH)
    fs =pl.BlockSpec((nb, C, W), lambda b, s: (b, fwd(s), 0))
    rs = pl.BlockSpec((nb, C, W), lambda b, s: (b, nch - 1 - s, 0))
    return pl.pallas_call(
        _scan_kernel,
        grid=(B // nb, nch),
        in_specs=[fs] * 6 + [rs] * 6,
        out_specs=[fs, rs],
        out_shape=[jax.ShapeDtypeStruct((B, TT, W), BF16)] * 2,
        scratch_shapes=[pltpu.VMEM((nb, 2, RW_HEADS // SCAN_HEADS_PER_PASS, SCAN_HEADS_PER_PASS * RW_HEAD,
                                    SCAN_HEADS_PER_PASS * RW_HEAD), F32)],
        compiler_params=pltpu.CompilerParams(
            dimension_semantics=("parallel", "arbitrary"), vmem_limit_bytes=VMEM_LIMIT_BYTES),
        name="rwkv_scan",
    )(lw0, kd0, b0, kk, v, r, lw1, kd1, b1, kk, v, r)


def _merge_kernel(xl_ref, xc_ref, yda_ref, y0_ref, y1_ref, bonus_ref, g_ref, ysg_ref, gate_ref,
                  g1_ref, sh2_ref, sc2_ref, wb_ref, wo_ref, ln1g_ref, ln1b_ref, rlng_ref, rlnb_ref,
                  ones_ref, wr_ref, xmid_ref, h2_ref, aff_ref, *, alpha, ntl):
    D = D_MODEL
    tm = xl_ref.shape[1]
    is_lat = pl.program_id(1) < ntl
    subs = [slice(r, r + MERGE_ROWS) for r in range(0, tm, MERGE_ROWS)]
    ones_bd = ones_ref[...]
    wr_hi, wr_lo = _split_bf16(wr_ref[...], 2)
    wr_hl = jnp.concatenate([wr_hi, wr_lo], axis=0)

    yrw = []
    for sl in subs:
        y = y0_ref[0, sl, :].astype(F32) + y1_ref[0, sl, :].astype(F32)
        mu = _seg_sum(y, ones_bd) * (1.0 / RW_HEAD)
        dy = y - mu
        var = _seg_sum(dy * dy, ones_bd) * (1.0 / RW_HEAD)
        gn = dy * lax.rsqrt(var + RW_GN_EPS) * rlng_ref[...] + rlnb_ref[...]
        yrw.append(((gn + bonus_ref[0, sl, :].astype(F32)) * g_ref[0, sl, :].astype(F32)).astype(BF16))
    ms = []
    for sl, yr in zip(subs, yrw):
        m = gate_ref[0, sl, 0:D].astype(F32) * _dot(yda_ref[0, sl, :], wb_ref[0:DA_WIDTH, :])
        m = m + gate_ref[0, sl, D:2 * D].astype(F32) * _dot(yr, wb_ref[DA_WIDTH:DA_WIDTH + RW_WIDTH, :])
        m = m + gate_ref[0, sl, 2 * D:3 * D].astype(F32) * _dot(ysg_ref[0, sl, :], wb_ref[DA_WIDTH + RW_WIDTH:, :])
        ms.append(m.astype(BF16))
    mixes = [_dot(m, wo_ref[...]) for m in ms]
    h2s = []
    for sl, mix in zip(subs, mixes):
        z = alpha * jnp.where(is_lat, xl_ref[0, sl, :], xc_ref[0, sl, :]) + g1_ref[0, 0] * mix
        zm = jnp.mean(z, axis=-1, keepdims=True)
        dz = z - zm
        zv = jnp.mean(dz * dz, axis=-1, keepdims=True)
        xmid = dz * lax.rsqrt(zv + LN_EPS) * ln1g_ref[...] + ln1b_ref[...]
        xmid_ref[0, sl, :] = xmid
        h2 = xmid * (1.0 + sc2_ref[0, 0]) + sh2_ref[0, 0]
        h2_ref[0, sl, :] = h2.astype(BF16)
        h2s.append(h2)
    ne = wr_hi.shape[0]
    for sl, h2 in zip(subs, h2s):
        h_hi, h_lo = _split_bf16(h2, 2)
        part = _dot_nt(wr_hl, h_hi)
        logits = part[:ne] + part[ne:] + _dot_nt(wr_hi, h_lo)
        e = jnp.exp(logits - jnp.max(logits, axis=0, keepdims=True))
        aff_ref[0, :, sl] = e / jnp.sum(e, axis=0, keepdims=True)


def _merge(x_lat, x_ctx, ctx_block0, yda, y0, y1, bonus, g, ysg, gates, modall, w_branch, w_out, ln1g, ln1b,
           rlng, rlnb, ones_bd, w_router_t, ntl, alpha):
    B, TT, _ = yda.shape
    D = x_lat.shape[2]
    tm = TOKEN_TILE
    nt = TT // tm
    tok = lambda w: pl.BlockSpec((1, tm, w), lambda b, i: (b, i, 0))
    modspec = lambda j: pl.BlockSpec((1, 1, 1, D), lambda b, i: (b, i // ntl, 0, j))
    const = lambda a: pl.BlockSpec(a.shape, lambda b, i: (0,) * a.ndim)
    consts = [w_branch, w_out, ln1g, ln1b, rlng, rlnb, ones_bd, w_router_t]
    return pl.pallas_call(
        functools.partial(_merge_kernel, alpha=alpha, ntl=ntl),
        grid=(B, nt),
        in_specs=_stream_specs(D, ntl, ctx_block0) + [tok(DA_WIDTH), tok(RW_WIDTH), tok(RW_WIDTH), tok(RW_WIDTH), tok(RW_WIDTH),
                  tok(SG_WIDTH), tok(3 * D), modspec(2), modspec(3), modspec(4)]
                 + [const(a) for a in consts],
        out_specs=[tok(D), tok(D), pl.BlockSpec((1, N_EXPERTS, tm), lambda b, i: (b, 0, i))],
        out_shape=[jax.ShapeDtypeStruct((B, TT, D), F32),
                   jax.ShapeDtypeStruct((B, TT, D), BF16),
                   jax.ShapeDtypeStruct((B, N_EXPERTS, TT), F32)],
        compiler_params=pltpu.CompilerParams(
            dimension_semantics=("parallel", "parallel"), vmem_limit_bytes=VMEM_LIMIT_BYTES),
        name="merge_ln1_router",
    )(x_lat, x_ctx, yda, y0, y1, bonus, g, ysg, gates, modall, modall, modall, *consts)


def _topk_kernel(aff_ref, tri_ref, rank_ref, *, cap):
    a = aff_ref[0]
    bits = pltpu.bitcast(a, jnp.int32)
    thr = jnp.zeros((a.shape[0], 1), jnp.int32)
    for bit in range(30, -1, -1):
        cand = thr | (1 << bit)
        cnt = jnp.sum(jnp.where(bits >= cand, 1.0, 0.0), axis=-1, keepdims=True)
        thr = jnp.where(cnt >= cap, cand, thr)
    gt = bits > thr
    eq = bits == thr
    need = cap - jnp.sum(jnp.where(gt, 1.0, 0.0), axis=-1, keepdims=True)
    tri = tri_ref[...]
    kb = tri.shape[0]

    def excl_cumsum(mask):
        m = jnp.where(mask, 1.0, 0.0)
        outs = []
        carry = jnp.zeros((m.shape[0], 1), F32)
        for j in range(m.shape[1] // kb):
            blk = m[:, j * kb:(j + 1) * kb]
            outs.append(_dot(blk.astype(BF16), tri) + carry)
            carry = carry + jnp.sum(blk, axis=-1, keepdims=True)
        return jnp.concatenate(outs, axis=1)

    sel = jnp.logical_or(gt, jnp.logical_and(eq, excl_cumsum(eq) < need))
    rank_ref[0] = jnp.where(sel, excl_cumsum(sel), -1.0).astype(jnp.int32)


def _topk_ranks(aff, tri, t_off, t_len, cap):
    B, E, _ = aff.shape
    blk = t_off // t_len
    return pl.pallas_call(
        functools.partial(_topk_kernel, cap=cap),
        grid=(B,),
        in_specs=[pl.BlockSpec((1, E, t_len), lambda b: (b, 0, blk)),
                  pl.BlockSpec(tri.shape, lambda b: (0, 0))],
        out_specs=pl.BlockSpec((1, E, t_len), lambda b: (b, 0, 0)),
        out_shape=jax.ShapeDtypeStruct((B, E, t_len), jnp.int32),
        compiler_params=pltpu.CompilerParams(
            dimension_semantics=("parallel",), vmem_limit_bytes=VMEM_LIMIT_BYTES),
        name="expert_choice_ranks",
    )(aff, tri)


def _moe_kernel(*refs, sets):
    n = len(sets)
    h_ref, aff_ref = refs[0], refs[1]
    rank_refs = refs[2:2 + n]
    wg_ref, wu_ref, wd_ref, f_ref = refs[2 + n:]
    e = pl.program_id(1)

    @pl.when(e == 0)
    def _():
        f_ref[...] = jnp.zeros_like(f_ref)

    onehots, gates, xs = [], [], []
    for (t0, tn, cap), rank_ref in zip(sets, rank_refs):
        rank = rank_ref[0, pl.ds(e, 1), :]
        aff = aff_ref[0, pl.ds(e, 1), t0:t0 + tn]
        slot = lax.broadcasted_iota(jnp.int32, (cap, tn), 0)
        hit = rank == slot
        onehot = jnp.where(hit, 1.0, 0.0).astype(BF16)
        onehots.append(onehot)
        gates.append(jnp.sum(jnp.where(hit, aff, 0.0), axis=-1, keepdims=True))
        xs.append(_dot(onehot, h_ref[0, t0:t0 + tn, :]).astype(BF16))
    xe = jnp.concatenate(xs, axis=0) if n > 1 else xs[0]
    gate = jnp.concatenate(gates, axis=0) if n > 1 else gates[0]
    hg = _dot(xe, wg_ref[0, 0])
    hid = (hg * _sigmoid(hg)) * _dot(xe, wu_ref[0, 0])
    ye = (_dot(hid.astype(BF16), wd_ref[0, 0]) * gate).astype(BF16)
    r0 = 0
    for (t0, tn, cap), onehot in zip(sets, onehots):
        f_ref[0, t0:t0 + tn, :] += _dot_tn(onehot, ye[r0:r0 + cap])
        r0 += cap


def _expert_ffn(h2, aff, ranks, wg, wu, wd, layer, sets):
    B, TT, D = h2.shape
    E = aff.shape[1]
    F = wg.shape[3]
    return pl.pallas_call(
        functools.partial(_moe_kernel, sets=sets),
        grid=(B, E),
        in_specs=[pl.BlockSpec((1, TT, D), lambda b, e: (b, 0, 0), pipeline_mode=pl.Buffered(1)),
                  pl.BlockSpec((1, E, TT), lambda b, e: (b, 0, 0))]
                 + [pl.BlockSpec((1, E, r.shape[2]), lambda b, e: (b, 0, 0)) for r in ranks]
                 + [pl.BlockSpec((1, 1, D, F), lambda b, e: (layer, e, 0, 0)),
                    pl.BlockSpec((1, 1, D, F), lambda b, e: (layer, e, 0, 0)),
                    pl.BlockSpec((1, 1, F, D), lambda b, e: (layer, e, 0, 0))],
        out_specs=pl.BlockSpec((1, TT, D), lambda b, e: (b, 0, 0)),
        out_shape=jax.ShapeDtypeStruct((B, TT, D), F32),
        compiler_params=pltpu.CompilerParams(
            dimension_semantics=("parallel", "arbitrary"), vmem_limit_bytes=VMEM_LIMIT_BYTES),
        name="expert_ffn",
    )(h2, aff, *ranks, wg, wu, wd)


def _ln2_kernel(x_ref, f_ref, g2_ref, lng_ref, lnb_ref, o_ref, *, alpha):
    z = alpha * x_ref[0] + g2_ref[0, 0] * f_ref[0]
    zm = jnp.mean(z, axis=-1, keepdims=True)
    dz = z - zm
    zv = jnp.mean(dz * dz, axis=-1, keepdims=True)
    o_ref[0] = dz * lax.rsqrt(zv + LN_EPS) * lng_ref[...] + lnb_ref[...]


def _final_norm(xmid, f, modall, lng, lnb, ntl, alpha, rows):
    B, _, D = xmid.shape
    tm = TOKEN_TILE
    tok = pl.BlockSpec((1, tm, D), lambda b, i: (b, i, 0))
    const = lambda a: pl.BlockSpec(a.shape, lambda b, i: (0,) * a.ndim)
    return pl.pallas_call(
        functools.partial(_ln2_kernel, alpha=alpha),
        grid=(B, rows // tm),
        in_specs=[tok, tok, pl.BlockSpec((1, 1, 1, D), lambda b, i: (b, i // ntl, 0, 5)),
                  const(lng), const(lnb)],
        out_specs=tok,
        out_shape=jax.ShapeDtypeStruct((B, rows, D), F32),
        compiler_params=pltpu.CompilerParams(dimension_semantics=("parallel", "parallel")),
        name="ln2",
    )(xmid, f, modall, lng, lnb)


def _rope_tables(t_lat, t_ctx):
    rows = t_lat // GRID_W
    row = jnp.repeat(jnp.arange(rows, dtype=F32), GRID_W)
    col = jnp.tile(jnp.arange(GRID_W, dtype=F32), rows)
    half = DA_QK_DIM // 2
    inv_freq = ROPE_BASE ** (-jnp.arange(0, half, 2, dtype=F32) / half)
    ar = row[:, None] * inv_freq
    ac = col[:, None] * inv_freq
    ang = jnp.concatenate([ar, ar, ac, ac], axis=-1)
    sign = jnp.where((jnp.arange(DA_QK_DIM) % 32) < 16, -1.0, 1.0).astype(F32)
    reps = DA_QK_COLS // DA_QK_DIM
    cos = jnp.tile(jnp.cos(ang), (1, reps))
    sin = jnp.tile(jnp.sin(ang) * sign, (1, reps))
    cos = jnp.concatenate([cos, jnp.ones((t_ctx, DA_QK_COLS), F32)], axis=0)
    sin = jnp.concatenate([sin, jnp.zeros((t_ctx, DA_QK_COLS), F32)], axis=0)
    return cos, sin


def kernel(x, c, ctx, c_ctx, w_mod, b_mod, w_in, da_lambda, da_norm_g, rw_shift_mu, rw_w0, rw_w2, rw_a0, rw_a2, rw_k_k, rw_k_a, rw_r_k, rw_ln_g, rw_ln_b, rw_g2, sg_norm_g, sg_norm_b, sg_w, sg_b, w_branch, w_out, ln1_g, ln1_b, w_router, w_e_gate, w_e_up, w_e_down, ln2_g, ln2_b):
    B, T, D = x.shape
    Tc = ctx.shape[1]
    depth = w_mod.shape[0]
    tm = TOKEN_TILE
    assert D == D_MODEL and T % tm == 0 and Tc % tm == 0 and T % Tc == 0 and Tc <= T
    ntl = T // tm
    alpha = (2 * depth) ** 0.25
    cap_lat = EC_CAPACITY * T // N_EXPERTS
    cap_ctx = EC_CAPACITY * Tc // N_EXPERTS

    cos, sin = _rope_tables(T, Tc)
    lane = jnp.arange(RW_WIDTH)
    ones_bd = (lane[:, None] // RW_HEAD == lane[None, :] // RW_HEAD).astype(BF16)
    kb = math.gcd(Tc, RANK_BLOCK)
    tri = (jnp.arange(kb)[:, None] < jnp.arange(kb)[None, :]).astype(BF16)
    rows = ((B + 1 + 7) // 8) * 8
    cc = jnp.concatenate([c, c_ctx[None, :], jnp.zeros((rows - B - 1, D), F32)], axis=0)
    row2 = lambda a: a.reshape(1, -1)

    w_in_bf = w_in.astype(BF16)
    wg, wu, wd = w_e_gate.astype(BF16), w_e_up.astype(BF16), w_e_down.astype(BF16)
    x_lat, x_ctx, ctx_block0 = x, ctx, 0
    for l in range(depth):
        last = l == depth - 1
        lam_init = 0.8 - 0.6 * math.exp(-0.3 * l)
        mod = _modulation(cc, w_mod[l], b_mod[l])
        modall = jnp.stack([mod[:B], jnp.broadcast_to(mod[B], (B, 6 * D))], axis=1)
        modall = modall.reshape(B, 2, 1, 6 * D)

        sgbias = jnp.repeat(sg_b[l].T, SG_WIDTH // SG_GROUPS, axis=1)
        q, k, v, prw, ysg, gates = _input_projection(
            x_lat, x_ctx, ctx_block0, T + Tc, modall, w_in_bf, l, cos, sin, row2(sg_norm_g[l]), row2(sg_norm_b[l]),
            sg_w[l].astype(BF16), sgbias, ntl)

        yda = _diff_attention(q, k, v, da_lambda[l], da_norm_g[l], ntl, lam_init)

        cat2 = lambda a: jnp.transpose(a, (1, 0, 2)).reshape(a.shape[1], 2 * RW_WIDTH)
        feats = _rwkv_features(
            prw, rw_shift_mu[l], row2(rw_w0[l]), cat2(rw_w2[l]).astype(BF16), row2(rw_a0[l]),
            cat2(rw_a2[l]).astype(BF16), row2(rw_k_k[l]), row2(rw_k_a[l]), row2(rw_r_k[l]),
            rw_g2[l].astype(BF16), ones_bd, ntl)
        lw0, lw1, kd0, kd1, b0, b1, kk, vv, rr, bonus, gg = feats
        y0, y1 = _rwkv_scan(lw0, lw1, kd0, kd1, b0, b1, kk, vv, rr, T)

        xmid, h2, aff = _merge(
            x_lat, x_ctx, ctx_block0, yda, y0, y1, bonus, gg, ysg, gates, modall, w_branch[l].astype(BF16),
            w_out[l].astype(BF16), row2(ln1_g[l]), row2(ln1_b[l]), row2(rw_ln_g[l]),
            row2(rw_ln_b[l]), ones_bd, w_router[l].T, ntl, alpha)

        sets = ((0, T, cap_lat),) if last else ((0, T, cap_lat), (T, Tc, cap_ctx))
        ranks = [_topk_ranks(aff, tri, 0, T, cap_lat)]
        if not last:
            ranks.append(_topk_ranks(aff, tri, T, Tc, cap_ctx))
        f = _expert_ffn(h2, aff, ranks, wg, wu, wd, l, sets)
        x_lat = _final_norm(xmid, f, modall, row2(ln2_g[l]), row2(ln2_b[l]), ntl, alpha,
                            T if last else T + Tc)
        x_ctx, ctx_block0 = x_lat, ntl
    return x_lat
```

```python
import functools
import math

import jax
import jax.numpy as jnp
from jax import lax
from jax.experimental import pallas as pl
from jax.experimental.pallas import tpu as pltpu

F32 = jnp.float32
BF16 = jnp.bfloat16
HIGHEST = lax.Precision.HIGHEST

D_MODEL = 1024
GRID_W = 64
DA_HEADS = 4
DA_QK_DIM = 64
DA_V_DIM = 128
DA_WIDTH = 512
DA_QK_COLS = 512
ROPE_BASE = 10000.0
DA_EPS = 1e-5
RW_HEAD = 64
RW_HEADS = 8
RW_WIDTH = 512
RW_COLS = 1792
RW_GN_EPS = 64e-5
SG_CHUNK = 128
SG_GROUPS = 4
SG_WIDTH = 512
MIX_WIDTH = 1536
DA_K0 = 512
DA_V0 = 1024
RW_0 = 1536
SG_0 = RW_0 + RW_COLS
GATE_0 = SG_0 + 2 * SG_WIDTH
IN_COLS = GATE_0 + 3 * D_MODEL
N_EXPERTS = 16
EC_CAPACITY = 2
LN_EPS = 1e-5
LOG2_E = math.log2(math.e)

TOKEN_TILE = 256
SCAN_CHUNK = 64
ATTN_KEY_TILE = 256
ATTN_HEADS_PER_STEP = 4
SCAN_HEADS_PER_PASS = 4
SCAN_BATCH = 4
MERGE_ROWS = 128
RANK_BLOCK = 256
RANK_BATCH = 4
VMEM_LIMIT_BYTES = 58 * 1024 * 1024


def _dot(a, b):
    return jnp.dot(a, b, preferred_element_type=F32)


def _dot_hi(a, b):
    return jnp.dot(a, b, preferred_element_type=F32, precision=HIGHEST)


def _dot_nt(a, b, precision=None):
    return lax.dot_general(a, b, (((1,), (1,)), ((), ())), preferred_element_type=F32,
                           precision=precision)


def _dot_tn(a, b, precision=None):
    return lax.dot_general(a, b, (((0,), (0,)), ((), ())), preferred_element_type=F32,
                           precision=precision)


def _sigmoid(z):
    return 1.0 / (1.0 + jnp.exp(-z))


def _seg_sum(z, ones_bd):
    hi = z.astype(BF16)
    lo = (z - hi.astype(F32)).astype(BF16)
    return _dot(hi, ones_bd) + _dot(lo, ones_bd)


def _mod_kernel(c_ref, w_ref, b_ref, o_ref):
    cc = c_ref[...]
    o_ref[...] = _dot_hi(cc * _sigmoid(cc), w_ref[...]) + b_ref[...]


def _modulation(cc, w_mod, b_mod):
    rows, d = cc.shape
    n = w_mod.shape[1]
    tn = 1024
    return pl.pallas_call(
        _mod_kernel,
        grid=(n // tn,),
        in_specs=[pl.BlockSpec((rows, d), lambda j: (0, 0)),
                  pl.BlockSpec((d, tn), lambda j: (0, j)),
                  pl.BlockSpec((1, tn), lambda j: (0, j))],
        out_specs=pl.BlockSpec((rows, tn), lambda j: (0, j)),
        out_shape=jax.ShapeDtypeStruct((rows, n), F32),
        name="adaln_mod",
    )(cc, w_mod, b_mod.reshape(1, n))


def _inproj_kernel(xl_ref, xc_ref, sh_ref, sc_ref, w_ref, cos_ref, sin_ref, sgg_ref, sgb_ref, sgw_ref,
                   sgbias_ref, q_ref, k_ref, v_ref, rw_ref, sg_ref, gate_ref, *, ntl):
    tm = xl_ref.shape[1]
    x = jnp.where(pl.program_id(1) < ntl, xl_ref[0], xc_ref[0])
    h = (x * (1.0 + sc_ref[0, 0]) + sh_ref[0, 0]).astype(BF16)

    def proj(c0, c1):
        return _dot(h, w_ref[0, :, c0:c1])

    cos = cos_ref[...]
    sin = sin_ref[...]
    lane = lax.broadcasted_iota(jnp.int32, (tm, DA_QK_COLS), 1)
    first = (lane % 32) < 16

    def rope(z):
        zr = jnp.where(first, pltpu.roll(z, DA_QK_COLS - 16, 1), pltpu.roll(z, 16, 1))
        return z * cos + zr * sin

    q_ref[0] = (rope(proj(0, DA_K0)) * (DA_QK_DIM ** -0.5 * LOG2_E)).astype(BF16)
    k_ref[0] = rope(proj(DA_K0, DA_V0)).astype(BF16)
    v_ref[0] = proj(DA_V0, RW_0).astype(BF16)
    for c0 in range(0, RW_COLS, 896):
        rw_ref[0, :, c0:c0 + 896] = proj(RW_0 + c0, RW_0 + c0 + 896)
    for j in range(3):
        gate_ref[0, :, j * D_MODEL:(j + 1) * D_MODEL] = _sigmoid(
            proj(GATE_0 + j * D_MODEL, GATE_0 + (j + 1) * D_MODEL)).astype(BF16)

    ps = proj(SG_0, GATE_0)
    gl = ps * (0.5 * (1.0 + jnp.tanh(math.sqrt(2.0 / math.pi) * (ps + 0.044715 * (ps * ps * ps)))))
    u = gl[:, :SG_WIDTH]
    vv = gl[:, SG_WIDTH:]
    mu = jnp.mean(vv, axis=-1, keepdims=True)
    dv = vv - mu
    var = jnp.mean(dv * dv, axis=-1, keepdims=True)
    vn = (dv * lax.rsqrt(var + LN_EPS) * sgg_ref[...] + sgb_ref[...]).astype(BF16)
    gd = SG_WIDTH // SG_GROUPS
    for n in range(tm // SG_CHUNK):
        r0 = n * SG_CHUNK
        for g in range(SG_GROUPS):
            c0 = g * gd
            vm = _dot(sgw_ref[g], vn[r0:r0 + SG_CHUNK, c0:c0 + gd]) + sgbias_ref[:, c0:c0 + gd]
            sg_ref[0, r0:r0 + SG_CHUNK, c0:c0 + gd] = (u[r0:r0 + SG_CHUNK, c0:c0 + gd] * vm).astype(BF16)


def _stream_specs(d, ntl, ctx_block0):
    tm = TOKEN_TILE
    lat = pl.BlockSpec((1, tm, d), lambda b, i: (b, jnp.minimum(i, ntl - 1), 0))
    ctx = pl.BlockSpec((1, tm, d), lambda b, i: (b, ctx_block0 + jnp.maximum(i - ntl, 0), 0))
    return [lat, ctx]


def _input_projection(x_lat, x_ctx, ctx_block0, TT, modall, w_in, layer, cos, sin, sgg, sgb, sgw, sgbias, ntl):
    B, _, D = x_lat.shape
    tm = TOKEN_TILE
    nt = TT // tm
    tok = lambda w: pl.BlockSpec((1, tm, w), lambda b, i: (b, i, 0))
    modspec = lambda j: pl.BlockSpec((1, 1, 1, D), lambda b, i: (b, i // ntl, 0, j))
    const2 = lambda a: pl.BlockSpec(a.shape, lambda b, i: (0, 0))
    return pl.pallas_call(
        functools.partial(_inproj_kernel, ntl=ntl),
        grid=(B, nt),
        in_specs=_stream_specs(D, ntl, ctx_block0) + [modspec(0), modspec(1),
                  pl.BlockSpec((1,) + w_in.shape[1:], lambda b, i: (layer, 0, 0),
                               pipeline_mode=pl.Buffered(1)),
                  pl.BlockSpec((tm, DA_QK_COLS), lambda b, i: (i, 0)),
                  pl.BlockSpec((tm, DA_QK_COLS), lambda b, i: (i, 0)),
                  const2(sgg), const2(sgb),
                  pl.BlockSpec(sgw.shape, lambda b, i: (0, 0, 0)),
                  const2(sgbias)],
        out_specs=[tok(DA_QK_COLS), tok(DA_QK_COLS), tok(DA_WIDTH), tok(RW_COLS), tok(SG_WIDTH),
                   tok(3 * D)],
        out_shape=[jax.ShapeDtypeStruct((B, TT, DA_QK_COLS), BF16),
                   jax.ShapeDtypeStruct((B, TT, DA_QK_COLS), BF16),
                   jax.ShapeDtypeStruct((B, TT, DA_WIDTH), BF16),
                   jax.ShapeDtypeStruct((B, TT, RW_COLS), F32),
                   jax.ShapeDtypeStruct((B, TT, SG_WIDTH), BF16),
                   jax.ShapeDtypeStruct((B, TT, 3 * D), BF16)],
        compiler_params=pltpu.CompilerParams(
            dimension_semantics=("parallel", "parallel"), vmem_limit_bytes=VMEM_LIMIT_BYTES),
        name="in_proj",
    )(x_lat, x_ctx, modall, modall, w_in, cos, sin, sgg, sgb, sgw, sgbias)


def _attn_kernel(q_ref, k_ref, v_ref, lam_ref, g_ref, o_ref, *, ntl, t_lat, lam_init):
    i = pl.program_id(2)
    lp = lam_ref[...]
    lam = (jnp.exp(jnp.sum(lp[0:1] * lp[1:2], axis=-1, keepdims=True))
           - jnp.exp(jnp.sum(lp[2:3] * lp[3:4], axis=-1, keepdims=True)) + lam_init)
    dv = DA_V_DIM
    heads = [slice(h * dv, (h + 1) * dv) for h in range(q_ref.shape[2] // dv)]
    lane = lax.broadcasted_iota(jnp.int32, (q_ref.shape[1], dv), 1)
    qs = []
    for hs in heads:
        q = q_ref[0, :, hs]
        zero = jnp.zeros_like(q)
        qs.append((jnp.where(lane < DA_QK_DIM, q, zero), jnp.where(lane >= DA_QK_DIM, q, zero)))

    def attend(k0, nk):
        kt = ATTN_KEY_TILE
        tiles = [slice(k0 + t * kt, k0 + (t + 1) * kt) for t in range(nk // kt)]

        def row_max(ss):
            m = ss[0]
            for s in ss[1:]:
                m = jnp.maximum(m, s)
            return jnp.max(m, axis=-1, keepdims=True)

        def pv(es, hs):
            acc = None
            for e, sl in zip(es, tiles):
                v = v_ref[0, sl, hs]
                d = _dot(e, jnp.concatenate([v, jnp.ones_like(v)], axis=1))
                acc = d if acc is None else acc + d
            return acc[:, :dv] / acc[:, dv:]

        ks = [[k_ref[0, sl, hs] for sl in tiles] for hs in heads]
        s0 = [[_dot_nt(q0, kk) for kk in kh] for (q0, _), kh in zip(qs, ks)]
        m0 = [row_max(s) for s in s0]
        s1, e0 = [], []
        for (_, q1), kh, sh, mh in zip(qs, ks, s0, m0):
            s1.append([])
            e0.append([])
            for kk, s in zip(kh, sh):
                s1[-1].append(_dot_nt(q1, kk))
                e0[-1].append(jnp.exp2(s - mh).astype(BF16))
        m1 = [row_max(s) for s in s1]
        o0 = [pv(e, hs) for e, hs in zip(e0, heads)]
        e1 = [[jnp.exp2(s - mh).astype(BF16) for s in sh] for sh, mh in zip(s1, m1)]
        for hs, oa, e in zip(heads, o0, e1):
            o = oa - lam * pv(e, hs)
            o = o * lax.rsqrt(jnp.mean(o * o, axis=-1, keepdims=True) + DA_EPS) * g_ref[...]
            o_ref[0, :, hs] = (o * (1.0 - lam_init)).astype(BF16)

    @pl.when(i < ntl)
    def _():
        attend(0, k_ref.shape[1])

    @pl.when(i >= ntl)
    def _():
        attend(t_lat, k_ref.shape[1] - t_lat)


def _diff_attention(q, k, v, lam_p, norm_g, ntl, lam_init, nt):
    B, TT, _ = q.shape
    tm = TOKEN_TILE
    w = ATTN_HEADS_PER_STEP * DA_V_DIM
    kv = pl.BlockSpec((1, TT, w), lambda b, h, i: (b, 0, h))
    qo = pl.BlockSpec((1, tm, w), lambda b, h, i: (b, i, h))
    return pl.pallas_call(
        functools.partial(_attn_kernel, ntl=ntl, t_lat=ntl * tm, lam_init=lam_init),
        grid=(B, DA_HEADS // ATTN_HEADS_PER_STEP, nt),
        in_specs=[qo, kv, kv,
                  pl.BlockSpec(lam_p.shape, lambda b, h, i: (0, 0)),
                  pl.BlockSpec((1, DA_V_DIM), lambda b, h, i: (0, 0))],
        out_specs=qo,
        out_shape=jax.ShapeDtypeStruct((B, TT, DA_WIDTH), BF16),
        compiler_params=pltpu.CompilerParams(
            dimension_semantics=("parallel", "parallel", "parallel"),
            vmem_limit_bytes=VMEM_LIMIT_BYTES),
        name="diff_attn",
    )(q, k, v, lam_p, norm_g.reshape(1, DA_V_DIM))


def _rwfeat_kernel(p_ref, pp_ref, pn_ref, mu_ref, w0_ref, w2_ref, a0_ref, a2_ref, kk_ref, ka_ref,
                   rk_ref, g2_ref, ones_ref,
                   lw0_ref, lw1_ref, kd0_ref, kd1_ref, b0_ref, b1_ref, kko_ref, v_ref, r_ref,
                   bonus_ref, g_ref, *, ntl, nt):
    i = pl.program_id(1)
    tm = p_ref.shape[1]
    p = p_ref[0]
    prev_ok = jnp.logical_and(i != 0, i != ntl)
    next_ok = jnp.logical_and(i != ntl - 1, i != nt - 1)
    prev_row = jnp.where(prev_ok, pp_ref[0, 7:8, :], 0.0)
    next_row = jnp.where(next_ok, pn_ref[0, 0:1, :], 0.0)
    row = lax.broadcasted_iota(jnp.int32, p.shape, 0)
    prev = jnp.where(row == 0, prev_row, pltpu.roll(p, 1, 0))
    nxt = jnp.where(row == tm - 1, next_row, pltpu.roll(p, tm - 1, 0))
    ps = p + mu_ref[0:1, :] * (prev - p) + mu_ref[1:2, :] * (nxt - p)

    W = RW_WIDTH
    r = ps[:, 0:W]
    k = ps[:, W:2 * W]
    v = ps[:, 2 * W:3 * W]
    xw = ps[:, 3 * W:3 * W + 64]
    xa = ps[:, 3 * W + 64:3 * W + 128]
    xg = ps[:, 3 * W + 128:3 * W + 256]
    ones_bd = ones_ref[...]

    w_pre = _dot(jnp.tanh(xw).astype(BF16), w2_ref[...]) + w0_ref[...]
    logw = -math.exp(-0.5) * _sigmoid(w_pre)
    a = _sigmoid(_dot(xa.astype(BF16), a2_ref[...]) + a0_ref[...])

    kx = k * kk_ref[...]
    nrm = jnp.sqrt(_seg_sum(kx * kx, ones_bd))
    kk = kx / jnp.maximum(nrm, 1e-12)
    ka = ka_ref[...]
    kd0 = k * (1.0 + (a[:, :W] - 1.0) * ka)
    kd1 = k * (1.0 + (a[:, W:] - 1.0) * ka)
    g = _dot(_sigmoid(xg).astype(BF16), g2_ref[...])
    k_b = 0.5 * (kd0 + kd1)
    bonus = _seg_sum(r * k_b * rk_ref[...], ones_bd) * v

    lw0_ref[0] = logw[:, :W]
    lw1_ref[0] = logw[:, W:]
    kd0_ref[0] = kd0.astype(BF16)
    kd1_ref[0] = kd1.astype(BF16)
    b0_ref[0] = (kk * a[:, :W]).astype(BF16)
    b1_ref[0] = (kk * a[:, W:]).astype(BF16)
    kko_ref[0] = kk.astype(BF16)
    v_ref[0] = v.astype(BF16)
    r_ref[0] = r.astype(BF16)
    bonus_ref[0] = bonus.astype(BF16)
    g_ref[0] = g.astype(BF16)


def _rwkv_features(prw, mu, w0, w2, a0, a2, k_k, k_a, r_k, g2, ones_bd, ntl):
    B, TT, _ = prw.shape
    tm = TOKEN_TILE
    nt = TT // tm
    r8 = tm // 8
    last8 = TT // 8 - 1
    W = RW_WIDTH
    const = lambda a: pl.BlockSpec(a.shape, lambda b, i: (0,) * a.ndim)
    tok = pl.BlockSpec((1, tm, W), lambda b, i: (b, i, 0))
    small = [mu, w0, w2, a0, a2, k_k, k_a, r_k, g2, ones_bd]
    return pl.pallas_call(
        functools.partial(_rwfeat_kernel, ntl=ntl, nt=nt),
        grid=(B, nt),
        in_specs=[pl.BlockSpec((1, tm, RW_COLS), lambda b, i: (b, i, 0)),
                  pl.BlockSpec((1, 8, RW_COLS), lambda b, i: (b, jnp.maximum(i * r8 - 1, 0), 0)),
                  pl.BlockSpec((1, 8, RW_COLS), lambda b, i: (b, jnp.minimum((i + 1) * r8, last8), 0))]
                 + [const(a) for a in small],
        out_specs=[tok] * 11,
        out_shape=[jax.ShapeDtypeStruct((B, TT, W), F32)] * 2 + [jax.ShapeDtypeStruct((B, TT, W), BF16)] * 9,
        compiler_params=pltpu.CompilerParams(
            dimension_semantics=("parallel", "parallel"), vmem_limit_bytes=VMEM_LIMIT_BYTES),
        name="rwkv_features",
    )(prw, prw, prw, *small)


def _split_bf16(x, pieces):
    out = []
    for _ in range(pieces - 1):
        hi = x.astype(BF16)
        out.append(hi)
        x = x - hi.astype(F32)
    out.append(x.astype(BF16))
    return out


def _block_diag(x, groups):
    xb = x.astype(BF16)
    rows, lanes = xb.shape
    t = jnp.concatenate([xb] * groups, axis=0)
    ri = lax.broadcasted_iota(jnp.int32, t.shape, 0) // rows
    li = lax.broadcasted_iota(jnp.int32, t.shape, 1) // (lanes // groups)
    return jnp.where(ri == li, t, jnp.zeros_like(t))


def _scan_kernel(lw0, kd0, b0, kkf, vf, rf, lw1, kd1, b1, kkr, vr, rr, y0_ref, y1_ref, s_ref):
    step = pl.program_id(1)

    @pl.when(step == 0)
    def _():
        s_ref[...] = jnp.zeros_like(s_ref)

    C = lw0.shape[1]
    G = SCAN_HEADS_PER_PASS
    L = G * RW_HEAD
    assert C == RW_HEAD
    ti = lax.broadcasted_iota(jnp.int32, (C, L), 0)
    si = lax.broadcasted_iota(jnp.int32, (C, L), 1) % C
    tc = lax.broadcasted_iota(jnp.int32, (C, C), 0)
    sc = lax.broadcasted_iota(jnp.int32, (C, C), 1)
    same_head = (lax.broadcasted_iota(jnp.int32, (L, L), 0) // RW_HEAD
                 == lax.broadcasted_iota(jnp.int32, (L, L), 1) // RW_HEAD)

    groups = []
    for z, (lw, kd, b, kk, v, r) in enumerate(((lw0, kd0, b0, kkf, vf, rf), (lw1, kd1, b1, kkr, vr, rr))):
        rev = z == 1
        incl = (si >= ti) if rev else (si <= ti)
        strict = (si > ti) if rev else (si < ti)
        tri = jnp.where((sc >= tc) if rev else (sc <= tc), 1.0, 0.0).astype(BF16)
        for n in range(lw.shape[0]):
            logw = lw[n]
            cum = sum(_dot(tri, piece) for piece in _split_bf16(logw, 3))
            tot = jnp.sum(logw, axis=0, keepdims=True)
            p_in = jnp.exp(-cum)
            p_end = jnp.exp(tot - cum)
            kkz, rz, bz, kdz = (t[n].astype(F32) for t in (kk, r, b, kd))
            lhs = jnp.concatenate([-kkz * jnp.exp(cum - logw), rz * jnp.exp(cum)], axis=0)
            rhs_b = bz * p_in
            rhs_k = kdz * p_in
            upd = jnp.concatenate([bz * p_end, kdz * p_end], axis=0)
            dec = jnp.exp(tot)
            vz = v[n]
            for j in range(RW_HEADS // G):
                c = slice(j * L, (j + 1) * L)
                groups.append(dict(z=z, smp=n, j=j, lhs=lhs[:, c].astype(BF16), rhs_b=rhs_b[:, c],
                                   rhs_k=rhs_k[:, c], upd=upd[:, c].astype(BF16), dec=dec[:, c],
                                   v=vz[:, c], strict=strict, incl=incl))

    for g in groups:
        aa_b = _dot_nt(g["lhs"], _block_diag(g["rhs_b"], G))
        aa_k = _dot_nt(g["lhs"], _block_diag(g["rhs_k"], G))
        g["a_k"] = jnp.concatenate([jnp.where(g["strict"], aa_k[:C], 0.0),
                                    jnp.where(g["incl"], aa_k[C:], 0.0)], axis=0).astype(BF16)
        g["a_rb"] = jnp.where(g["incl"], aa_b[C:], 0.0).astype(BF16)
        g["n"] = jnp.where(g["strict"], aa_b[:C], 0.0)
        g["apow"] = _dot(g["n"].astype(BF16), _block_diag(g["n"], G))
    levels = int(math.log2(C)) - 1
    for lv in range(levels):
        for g in groups:
            pbd = _block_diag(g["apow"], G)
            if lv < levels - 1:
                st = _dot(jnp.concatenate([g["n"], g["apow"]], axis=0).astype(BF16), pbd)
                g["n"] = g["n"] + g["apow"] + st[:C]
                g["apow"] = st[C:]
            else:
                g["n"] = g["n"] + g["apow"] + _dot(g["n"].astype(BF16), pbd)
    for g in groups:
        g["s0"] = s_ref[g["smp"], g["z"], g["j"]]
        g["gh"] = _dot_nt(g["lhs"], g["s0"].astype(BF16))
    for g in groups:
        st = _dot(g["a_k"], _block_diag(g["v"], G))
        g["w"] = g["gh"][:C] + st[:C]
        g["yk"] = st[C:]
    for g in groups:
        g["u"] = g["w"] + _dot(g["n"].astype(BF16), _block_diag(g["w"], G))
    for g in groups:
        g["y"] = g["gh"][C:] + g["yk"] + _dot(g["a_rb"], _block_diag(g["u"], G))
    for g in groups:
        uv = jnp.concatenate([g["u"].astype(BF16), g["v"]], axis=0)
        s_ref[g["smp"], g["z"], g["j"]] = g["s0"] * g["dec"] + jnp.where(same_head, _dot_tn(uv, g["upd"]), 0.0)
    for z, y_ref in enumerate((y0_ref, y1_ref)):
        for n in range(y_ref.shape[0]):
            y_ref[n] = jnp.concatenate([g["y"] for g in groups if g["z"] == z and g["smp"] == n],
                                       axis=-1).astype(y_ref.dtype)


def _rwkv_scan(lw0, lw1, kd0, kd1, b0, b1, kk, v, r, t_lat):
    B, TT, W = kk.shape
    C = SCAN_CHUNK
    ncl = t_lat // C
    nch = TT // C
    ncc = nch - ncl

    def fwd(s):
        return jnp.where(s < ncc, ncl + s, s - ncc)

    nb = math.gcd(B, SCAN_BATCH)
    fs =pl.BlockSpec((nb, C, W), lambda b, s: (b, fwd(s), 0))
    rs = pl.BlockSpec((nb, C, W), lambda b, s: (b, nch - 1 - s, 0))
    return pl.pallas_call(
        _scan_kernel,
        grid=(B // nb, nch),
        in_specs=[fs] * 6 + [rs] * 6,
        out_specs=[fs, rs],
        out_shape=[jax.ShapeDtypeStruct((B, TT, W), BF16)] * 2,
        scratch_shapes=[pltpu.VMEM((nb, 2, RW_HEADS // SCAN_HEADS_PER_PASS, SCAN_HEADS_PER_PASS * RW_HEAD,
                                    SCAN_HEADS_PER_PASS * RW_HEAD), F32)],
        compiler_params=pltpu.CompilerParams(
            dimension_semantics=("parallel", "arbitrary"), vmem_limit_bytes=VMEM_LIMIT_BYTES),
        name="rwkv_scan",
    )(lw0, kd0, b0, kk, v, r, lw1, kd1, b1, kk, v, r)


def _merge_kernel(xl_ref, xc_ref, yda_ref, y0_ref, y1_ref, bonus_ref, g_ref, ysg_ref, gate_ref,
                  g1_ref, sh2_ref, sc2_ref, wb_ref, wo_ref, ln1g_ref, ln1b_ref, rlng_ref, rlnb_ref,
                  ones_ref, wr_ref, xmid_ref, h2_ref, aff_ref, *, alpha, ntl):
    D = D_MODEL
    tm = xl_ref.shape[1]
    is_lat = pl.program_id(1) < ntl
    subs = [slice(r, r + MERGE_ROWS) for r in range(0, tm, MERGE_ROWS)]
    ones_bd = ones_ref[...]
    wr_hi, wr_lo = _split_bf16(wr_ref[...], 2)
    wr_hl = jnp.concatenate([wr_hi, wr_lo], axis=0)

    yrw = []
    for sl in subs:
        y = y0_ref[0, sl, :].astype(F32) + y1_ref[0, sl, :].astype(F32)
        mu = _seg_sum(y, ones_bd) * (1.0 / RW_HEAD)
        dy = y - mu
        var = _seg_sum(dy * dy, ones_bd) * (1.0 / RW_HEAD)
        gn = dy * lax.rsqrt(var + RW_GN_EPS) * rlng_ref[...] + rlnb_ref[...]
        yrw.append(((gn + bonus_ref[0, sl, :].astype(F32)) * g_ref[0, sl, :].astype(F32)).astype(BF16))
    ms = []
    for sl, yr in zip(subs, yrw):
        m = gate_ref[0, sl, 0:D].astype(F32) * _dot(yda_ref[0, sl, :], wb_ref[0:DA_WIDTH, :])
        m = m + gate_ref[0, sl, D:2 * D].astype(F32) * _dot(yr, wb_ref[DA_WIDTH:DA_WIDTH + RW_WIDTH, :])
        m = m + gate_ref[0, sl, 2 * D:3 * D].astype(F32) * _dot(ysg_ref[0, sl, :], wb_ref[DA_WIDTH + RW_WIDTH:, :])
        ms.append(m.astype(BF16))
    mixes = [_dot(m, wo_ref[...]) for m in ms]
    h2s = []
    for sl, mix in zip(subs, mixes):
        z = alpha * jnp.where(is_lat, xl_ref[0, sl, :], xc_ref[0, sl, :]) + g1_ref[0, 0] * mix
        zm = jnp.mean(z, axis=-1, keepdims=True)
        dz = z - zm
        zv = jnp.mean(dz * dz, axis=-1, keepdims=True)
        xmid = dz * lax.rsqrt(zv + LN_EPS) * ln1g_ref[...] + ln1b_ref[...]
        xmid_ref[0, sl, :] = xmid
        h2 = xmid * (1.0 + sc2_ref[0, 0]) + sh2_ref[0, 0]
        h2_ref[0, sl, :] = h2.astype(BF16)
        h2s.append(h2)
    ne = wr_hi.shape[0]
    for sl, h2 in zip(subs, h2s):
        h_hi, h_lo = _split_bf16(h2, 2)
        part = _dot_nt(wr_hl, h_hi)
        logits = part[:ne] + part[ne:] + _dot_nt(wr_hi, h_lo)
        e = jnp.exp(logits - jnp.max(logits, axis=0, keepdims=True))
        aff_ref[0, :, sl] = e / jnp.sum(e, axis=0, keepdims=True)


def _merge(x_lat, x_ctx, ctx_block0, yda, y0, y1, bonus, g, ysg, gates, modall, w_branch, w_out, ln1g, ln1b,
           rlng, rlnb, ones_bd, w_router_t, ntl, alpha, nt):
    B, TT, _ = yda.shape
    D = x_lat.shape[2]
    tm = TOKEN_TILE
    tok = lambda w: pl.BlockSpec((1, tm, w), lambda b, i: (b, i, 0))
    modspec = lambda j: pl.BlockSpec((1, 1, 1, D), lambda b, i: (b, i // ntl, 0, j))
    const = lambda a: pl.BlockSpec(a.shape, lambda b, i: (0,) * a.ndim)
    consts = [w_branch, w_out, ln1g, ln1b, rlng, rlnb, ones_bd, w_router_t]
    return pl.pallas_call(
        functools.partial(_merge_kernel, alpha=alpha, ntl=ntl),
        grid=(B, nt),
        in_specs=_stream_specs(D, ntl, ctx_block0) + [tok(DA_WIDTH), tok(RW_WIDTH), tok(RW_WIDTH), tok(RW_WIDTH), tok(RW_WIDTH),
                  tok(SG_WIDTH), tok(3 * D), modspec(2), modspec(3), modspec(4)]
                 + [const(a) for a in consts],
        out_specs=[tok(D), tok(D), pl.BlockSpec((1, N_EXPERTS, tm), lambda b, i: (b, 0, i))],
        out_shape=[jax.ShapeDtypeStruct((B, TT, D), F32),
                   jax.ShapeDtypeStruct((B, TT, D), BF16),
                   jax.ShapeDtypeStruct((B, N_EXPERTS, TT), F32)],
        compiler_params=pltpu.CompilerParams(
            dimension_semantics=("parallel", "parallel"), vmem_limit_bytes=VMEM_LIMIT_BYTES),
        name="merge_ln1_router",
    )(x_lat, x_ctx, yda, y0, y1, bonus, g, ysg, gates, modall, modall, modall, *consts)


def _topk_kernel(aff_ref, tri_ref, rank_ref, *, cap):
    a = aff_ref[...].reshape(-1, aff_ref.shape[2])
    bits = pltpu.bitcast(a, jnp.int32)
    thr = jnp.zeros((a.shape[0], 1), jnp.int32)
    for bit in range(30, -1, -1):
        cand = thr | (1 << bit)
        cnt = jnp.sum(jnp.where(bits >= cand, 1.0, 0.0), axis=-1, keepdims=True)
        thr = jnp.where(cnt >= cap, cand, thr)
    gt = bits > thr
    eq = bits == thr
    need = cap - jnp.sum(jnp.where(gt, 1.0, 0.0), axis=-1, keepdims=True)
    tri = tri_ref[...]
    kb = tri.shape[0]

    def excl_cumsum(mask):
        m = jnp.where(mask, 1.0, 0.0)
        outs = []
        carry = jnp.zeros((m.shape[0], 1), F32)
        for j in range(m.shape[1] // kb):
            blk = m[:, j * kb:(j + 1) * kb]
            outs.append(_dot(blk.astype(BF16), tri) + carry)
            carry = carry + jnp.sum(blk, axis=-1, keepdims=True)
        return jnp.concatenate(outs, axis=1)

    sel = jnp.logical_or(gt, jnp.logical_and(eq, excl_cumsum(eq) < need))
    rank_ref[...] = jnp.where(sel, excl_cumsum(sel), -1.0).astype(jnp.int32).reshape(rank_ref.shape)


def _topk_ranks(aff, tri, t_off, t_len, cap):
    B, E, _ = aff.shape
    blk = t_off // t_len
    nb = math.gcd(B, RANK_BATCH)
    return pl.pallas_call(
        functools.partial(_topk_kernel, cap=cap),
        grid=(B // nb,),
        in_specs=[pl.BlockSpec((nb, E, t_len), lambda b: (b, 0, blk)),
                  pl.BlockSpec(tri.shape, lambda b: (0, 0))],
        out_specs=pl.BlockSpec((nb, E, t_len), lambda b: (b, 0, 0)),
        out_shape=jax.ShapeDtypeStruct((B, E, t_len), jnp.int32),
        compiler_params=pltpu.CompilerParams(
            dimension_semantics=("parallel",), vmem_limit_bytes=VMEM_LIMIT_BYTES),
        name="expert_choice_ranks",
    )(aff, tri)


def _moe_kernel(*refs, sets):
    n = len(sets)
    h_ref, aff_ref = refs[0], refs[1]
    rank_refs = refs[2:2 + n]
    wg_ref, wu_ref, wd_ref, f_ref = refs[2 + n:]
    e = pl.program_id(1)

    @pl.when(e == 0)
    def _():
        f_ref[...] = jnp.zeros_like(f_ref)

    onehots, gates, xs = [], [], []
    for (t0, tn, cap), rank_ref in zip(sets, rank_refs):
        rank = rank_ref[0, pl.ds(e, 1), :]
        aff = aff_ref[0, pl.ds(e, 1), t0:t0 + tn]
        slot = lax.broadcasted_iota(jnp.int32, (cap, tn), 0)
        hit = rank == slot
        onehot = jnp.where(hit, 1.0, 0.0).astype(BF16)
        onehots.append(onehot)
        gates.append(jnp.sum(jnp.where(hit, aff, 0.0), axis=-1, keepdims=True))
        xs.append(_dot(onehot, h_ref[0, t0:t0 + tn, :]).astype(BF16))
    xe = jnp.concatenate(xs, axis=0) if n > 1 else xs[0]
    gate = jnp.concatenate(gates, axis=0) if n > 1 else gates[0]
    hg = _dot(xe, wg_ref[0, 0])
    hid = (hg * _sigmoid(hg)) * _dot(xe, wu_ref[0, 0])
    ye = (_dot(hid.astype(BF16), wd_ref[0, 0]) * gate).astype(BF16)
    r0 = 0
    for (t0, tn, cap), onehot in zip(sets, onehots):
        f_ref[0, t0:t0 + tn, :] += _dot_tn(onehot, ye[r0:r0 + cap])
        r0 += cap


def _expert_ffn(h2, aff, ranks, wg, wu, wd, layer, sets):
    B, TT, D = h2.shape
    E = aff.shape[1]
    F = wg.shape[3]
    return pl.pallas_call(
        functools.partial(_moe_kernel, sets=sets),
        grid=(B, E),
        in_specs=[pl.BlockSpec((1, TT, D), lambda b, e: (b, 0, 0), pipeline_mode=pl.Buffered(1)),
                  pl.BlockSpec((1, E, TT), lambda b, e: (b, 0, 0))]
                 + [pl.BlockSpec((1, E, r.shape[2]), lambda b, e: (b, 0, 0)) for r in ranks]
                 + [pl.BlockSpec((1, 1, D, F), lambda b, e: (layer, e, 0, 0)),
                    pl.BlockSpec((1, 1, D, F), lambda b, e: (layer, e, 0, 0)),
                    pl.BlockSpec((1, 1, F, D), lambda b, e: (layer, e, 0, 0))],
        out_specs=pl.BlockSpec((1, TT, D), lambda b, e: (b, 0, 0)),
        out_shape=jax.ShapeDtypeStruct((B, TT, D), F32),
        compiler_params=pltpu.CompilerParams(
            dimension_semantics=("parallel", "arbitrary"), vmem_limit_bytes=VMEM_LIMIT_BYTES),
        name="expert_ffn",
    )(h2, aff, *ranks, wg, wu, wd)


def _ln2_kernel(x_ref, f_ref, g2_ref, lng_ref, lnb_ref, o_ref, *, alpha):
    z = alpha * x_ref[0] + g2_ref[0, 0] * f_ref[0]
    zm = jnp.mean(z, axis=-1, keepdims=True)
    dz = z - zm
    zv = jnp.mean(dz * dz, axis=-1, keepdims=True)
    o_ref[0] = dz * lax.rsqrt(zv + LN_EPS) * lng_ref[...] + lnb_ref[...]


def _final_norm(xmid, f, modall, lng, lnb, ntl, alpha, rows):
    B, _, D = xmid.shape
    tm = TOKEN_TILE
    tok = pl.BlockSpec((1, tm, D), lambda b, i: (b, i, 0))
    const = lambda a: pl.BlockSpec(a.shape, lambda b, i: (0,) * a.ndim)
    return pl.pallas_call(
        functools.partial(_ln2_kernel, alpha=alpha),
        grid=(B, rows // tm),
        in_specs=[tok, tok, pl.BlockSpec((1, 1, 1, D), lambda b, i: (b, i // ntl, 0, 5)),
                  const(lng), const(lnb)],
        out_specs=tok,
        out_shape=jax.ShapeDtypeStruct((B, rows, D), F32),
        compiler_params=pltpu.CompilerParams(dimension_semantics=("parallel", "parallel")),
        name="ln2",
    )(xmid, f, modall, lng, lnb)


def _rope_tables(t_lat, t_ctx):
    rows = t_lat // GRID_W
    row = jnp.repeat(jnp.arange(rows, dtype=F32), GRID_W)
    col = jnp.tile(jnp.arange(GRID_W, dtype=F32), rows)
    half = DA_QK_DIM // 2
    inv_freq = ROPE_BASE ** (-jnp.arange(0, half, 2, dtype=F32) / half)
    ar = row[:, None] * inv_freq
    ac = col[:, None] * inv_freq
    ang = jnp.concatenate([ar, ar, ac, ac], axis=-1)
    sign = jnp.where((jnp.arange(DA_QK_DIM) % 32) < 16, -1.0, 1.0).astype(F32)
    reps = DA_QK_COLS // DA_QK_DIM
    cos = jnp.tile(jnp.cos(ang), (1, reps))
    sin = jnp.tile(jnp.sin(ang) * sign, (1, reps))
    cos = jnp.concatenate([cos, jnp.ones((t_ctx, DA_QK_COLS), F32)], axis=0)
    sin = jnp.concatenate([sin, jnp.zeros((t_ctx, DA_QK_COLS), F32)], axis=0)
    return cos, sin


def kernel(x, c, ctx, c_ctx, w_mod, b_mod, w_in, da_lambda, da_norm_g, rw_shift_mu, rw_w0, rw_w2, rw_a0, rw_a2, rw_k_k, rw_k_a, rw_r_k, rw_ln_g, rw_ln_b, rw_g2, sg_norm_g, sg_norm_b, sg_w, sg_b, w_branch, w_out, ln1_g, ln1_b, w_router, w_e_gate, w_e_up, w_e_down, ln2_g, ln2_b):
    B, T, D = x.shape
    Tc = ctx.shape[1]
    depth = w_mod.shape[0]
    tm = TOKEN_TILE
    assert D == D_MODEL and T % tm == 0 and Tc % tm == 0 and T % Tc == 0 and Tc <= T
    ntl = T // tm
    alpha = (2 * depth) ** 0.25
    cap_lat = EC_CAPACITY * T // N_EXPERTS
    cap_ctx = EC_CAPACITY * Tc // N_EXPERTS

    cos, sin = _rope_tables(T, Tc)
    lane = jnp.arange(RW_WIDTH)
    ones_bd = (lane[:, None] // RW_HEAD == lane[None, :] // RW_HEAD).astype(BF16)
    kb = math.gcd(Tc, RANK_BLOCK)
    tri = (jnp.arange(kb)[:, None] < jnp.arange(kb)[None, :]).astype(BF16)
    rows = ((B + 1 + 7) // 8) * 8
    cc = jnp.concatenate([c, c_ctx[None, :], jnp.zeros((rows - B - 1, D), F32)], axis=0)
    row2 = lambda a: a.reshape(1, -1)

    w_in_bf = w_in.astype(BF16)
    wg, wu, wd = w_e_gate.astype(BF16), w_e_up.astype(BF16), w_e_down.astype(BF16)
    x_lat, x_ctx, ctx_block0 = x, ctx, 0
    for l in range(depth):
        last = l == depth - 1
        lam_init = 0.8 - 0.6 * math.exp(-0.3 * l)
        mod = _modulation(cc, w_mod[l], b_mod[l])
        modall = jnp.stack([mod[:B], jnp.broadcast_to(mod[B], (B, 6 * D))], axis=1)
        modall = modall.reshape(B, 2, 1, 6 * D)

        sgbias = jnp.repeat(sg_b[l].T, SG_WIDTH // SG_GROUPS, axis=1)
        q, k, v, prw, ysg, gates = _input_projection(
            x_lat, x_ctx, ctx_block0, T + Tc, modall, w_in_bf, l, cos, sin, row2(sg_norm_g[l]), row2(sg_norm_b[l]),
            sg_w[l].astype(BF16), sgbias, ntl)

        nt_out = ntl if last else (T + Tc) // tm
        yda = _diff_attention(q, k, v, da_lambda[l], da_norm_g[l], ntl, lam_init, nt_out)

        cat2 = lambda a: jnp.transpose(a, (1, 0, 2)).reshape(a.shape[1], 2 * RW_WIDTH)
        feats = _rwkv_features(
            prw, rw_shift_mu[l], row2(rw_w0[l]), cat2(rw_w2[l]).astype(BF16), row2(rw_a0[l]),
            cat2(rw_a2[l]).astype(BF16), row2(rw_k_k[l]), row2(rw_k_a[l]), row2(rw_r_k[l]),
            rw_g2[l].astype(BF16), ones_bd, ntl)
        lw0, lw1, kd0, kd1, b0, b1, kk, vv, rr, bonus, gg = feats
        y0, y1 = _rwkv_scan(lw0, lw1, kd0, kd1, b0, b1, kk, vv, rr, T)

        xmid, h2, aff = _merge(
            x_lat, x_ctx, ctx_block0, yda, y0, y1, bonus, gg, ysg, gates, modall, w_branch[l].astype(BF16),
            w_out[l].astype(BF16), row2(ln1_g[l]), row2(ln1_b[l]), row2(rw_ln_g[l]),
            row2(rw_ln_b[l]), ones_bd, w_router[l].T, ntl, alpha, nt_out)

        sets = ((0, T, cap_lat),) if last else ((0, T, cap_lat), (T, Tc, cap_ctx))
        ranks = [_topk_ranks(aff, tri, 0, T, cap_lat)]
        if not last:
            ranks.append(_topk_ranks(aff, tri, T, Tc, cap_ctx))
        f = _expert_ffn(h2, aff, ranks, wg, wu, wd, l, sets)
        x_lat = _final_norm(xmid, f, modall, row2(ln2_g[l]), row2(ln2_b[l]), ntl, alpha,
                            T if last else T + Tc)
        x_ctx, ctx_block0 = x_lat, ntl
    return x_lat
```

```python
import functools
import math

import jax
import jax.numpy as jnp
from jax import lax
from jax.experimental import pallas as pl
from jax.experimental.pallas import tpu as pltpu

F32 = jnp.float32
BF16 = jnp.bfloat16
HIGHEST = lax.Precision.HIGHEST

D_MODEL = 1024
GRID_W = 64
DA_HEADS = 4
DA_QK_DIM = 64
DA_V_DIM = 128
DA_WIDTH = 512
DA_QK_COLS = 512
ROPE_BASE = 10000.0
DA_EPS = 1e-5
RW_HEAD = 64
RW_HEADS = 8
RW_WIDTH = 512
RW_COLS = 1792
RW_GN_EPS = 64e-5
SG_CHUNK = 128
SG_GROUPS = 4
SG_WIDTH = 512
MIX_WIDTH = 1536
DA_K0 = 512
DA_V0 = 1024
RW_0 = 1536
SG_0 = RW_0 + RW_COLS
GATE_0 = SG_0 + 2 * SG_WIDTH
IN_COLS = GATE_0 + 3 * D_MODEL
N_EXPERTS = 16
EC_CAPACITY = 2
LN_EPS = 1e-5
LOG2_E = math.log2(math.e)

TOKEN_TILE = 256
SCAN_CHUNK = 64
ATTN_KEY_TILE = 256
ATTN_HEADS_PER_STEP = 4
SCAN_HEADS_PER_PASS = 4
SCAN_BATCH = 4
MERGE_ROWS = 128
RANK_BLOCK = 256
RANK_BATCH = 4
VMEM_LIMIT_BYTES = 58 * 1024 * 1024


def _dot(a, b):
    return jnp.dot(a, b, preferred_element_type=F32)


def _dot_hi(a, b):
    return jnp.dot(a, b, preferred_element_type=F32, precision=HIGHEST)


def _dot_nt(a, b, precision=None):
    return lax.dot_general(a, b, (((1,), (1,)), ((), ())), preferred_element_type=F32,
                           precision=precision)


def _dot_tn(a, b, precision=None):
    return lax.dot_general(a, b, (((0,), (0,)), ((), ())), preferred_element_type=F32,
                           precision=precision)


def _sigmoid(z):
    return 1.0 / (1.0 + jnp.exp(-z))


def _seg_sum(z, ones_bd):
    hi = z.astype(BF16)
    lo = (z - hi.astype(F32)).astype(BF16)
    return _dot(hi, ones_bd) + _dot(lo, ones_bd)


def _mod_kernel(c_ref, w_ref, b_ref, o_ref):
    cc = c_ref[...]
    o_ref[...] = _dot_hi(cc * _sigmoid(cc), w_ref[...]) + b_ref[...]


def _modulation(cc, w_mod, b_mod):
    rows, d = cc.shape
    n = w_mod.shape[1]
    tn = 1024
    return pl.pallas_call(
        _mod_kernel,
        grid=(n // tn,),
        in_specs=[pl.BlockSpec((rows, d), lambda j: (0, 0)),
                  pl.BlockSpec((d, tn), lambda j: (0, j)),
                  pl.BlockSpec((1, tn), lambda j: (0, j))],
        out_specs=pl.BlockSpec((rows, tn), lambda j: (0, j)),
        out_shape=jax.ShapeDtypeStruct((rows, n), F32),
        name="adaln_mod",
    )(cc, w_mod, b_mod.reshape(1, n))


def _inproj_kernel(xl_ref, xc_ref, lp_ref, ln_ref, sh_ref, sc_ref, w_ref, cos_ref, sin_ref, sgg_ref, sgb_ref,
                   sgw_ref, sgbias_ref, *rest, ntl, nt):
    rw_params, (q_ref, k_ref, v_ref, sg_ref, gate_ref), rw_outs = rest[:10], rest[10:15], rest[15:]
    tm = xl_ref.shape[1]
    i = pl.program_id(1)
    x = jnp.where(i < ntl, xl_ref[0], xc_ref[0])
    h = (x * (1.0 + sc_ref[0, 0]) + sh_ref[0, 0]).astype(BF16)

    def proj(c0, c1):
        return _dot(h, w_ref[0, :, c0:c1])

    cos = cos_ref[...]
    sin = sin_ref[...]
    lane = lax.broadcasted_iota(jnp.int32, (tm, DA_QK_COLS), 1)
    first = (lane % 32) < 16

    def rope(z):
        zr = jnp.where(first, pltpu.roll(z, DA_QK_COLS - 16, 1), pltpu.roll(z, 16, 1))
        return z * cos + zr * sin

    q_ref[0] = (rope(proj(0, DA_K0)) * (DA_QK_DIM ** -0.5 * LOG2_E)).astype(BF16)
    k_ref[0] = rope(proj(DA_K0, DA_V0)).astype(BF16)
    v_ref[0] = proj(DA_V0, RW_0).astype(BF16)
    halo = jnp.concatenate([lp_ref[0], ln_ref[0]], axis=0)
    h_halo = (halo * (1.0 + sc_ref[0, 0]) + sh_ref[0, 0]).astype(BF16)
    p_ext = _dot(jnp.concatenate([h, h_halo], axis=0), w_ref[0, :, RW_0:SG_0])
    prev_ok = jnp.logical_and(i != 0, i != ntl)
    next_ok = jnp.logical_and(i != ntl - 1, i != nt - 1)
    _rwkv_feature_math(p_ext[:tm], jnp.where(prev_ok, p_ext[tm + 7:tm + 8], 0.0),
                       jnp.where(next_ok, p_ext[tm + 8:tm + 9], 0.0), rw_params, rw_outs)
    for j in range(3):
        gate_ref[0, :, j * D_MODEL:(j + 1) * D_MODEL] = _sigmoid(
            proj(GATE_0 + j * D_MODEL, GATE_0 + (j + 1) * D_MODEL)).astype(BF16)

    ps = proj(SG_0, GATE_0)
    gl = ps * (0.5 * (1.0 + jnp.tanh(math.sqrt(2.0 / math.pi) * (ps + 0.044715 * (ps * ps * ps)))))
    u = gl[:, :SG_WIDTH]
    vv = gl[:, SG_WIDTH:]
    mu = jnp.mean(vv, axis=-1, keepdims=True)
    dv = vv - mu
    var = jnp.mean(dv * dv, axis=-1, keepdims=True)
    vn = (dv * lax.rsqrt(var + LN_EPS) * sgg_ref[...] + sgb_ref[...]).astype(BF16)
    gd = SG_WIDTH // SG_GROUPS
    for n in range(tm // SG_CHUNK):
        r0 = n * SG_CHUNK
        for g in range(SG_GROUPS):
            c0 = g * gd
            vm = _dot(sgw_ref[g], vn[r0:r0 + SG_CHUNK, c0:c0 + gd]) + sgbias_ref[:, c0:c0 + gd]
            sg_ref[0, r0:r0 + SG_CHUNK, c0:c0 + gd] = (u[r0:r0 + SG_CHUNK, c0:c0 + gd] * vm).astype(BF16)


def _stream_specs(d, ntl, ctx_block0):
    tm = TOKEN_TILE
    lat = pl.BlockSpec((1, tm, d), lambda b, i: (b, jnp.minimum(i, ntl - 1), 0))
    ctx = pl.BlockSpec((1, tm, d), lambda b, i: (b, ctx_block0 + jnp.maximum(i - ntl, 0), 0))
    return [lat, ctx]


def _input_projection(x_lat, x_ctx, ctx_block0, TT, modall, w_in, layer, cos, sin, sgg, sgb, sgw, sgbias,
                      rw_params, ntl):
    B, lat_rows, D = x_lat.shape
    tm = TOKEN_TILE
    nt = TT // tm
    r8 = tm // 8
    last8 = lat_rows // 8 - 1
    W = RW_WIDTH
    tok = lambda w: pl.BlockSpec((1, tm, w), lambda b, i: (b, i, 0))
    modspec = lambda j: pl.BlockSpec((1, 1, 1, D), lambda b, i: (b, i // ntl, 0, j))
    const = lambda a: pl.BlockSpec(a.shape, lambda b, i: (0,) * a.ndim)
    halo_prev = pl.BlockSpec((1, 8, D), lambda b, i: (b, jnp.maximum(jnp.minimum(i, ntl - 1) * r8 - 1, 0), 0))
    halo_next = pl.BlockSpec((1, 8, D), lambda b, i: (b, jnp.minimum((jnp.minimum(i, ntl - 1) + 1) * r8, last8), 0))
    return pl.pallas_call(
        functools.partial(_inproj_kernel, ntl=ntl, nt=nt),
        grid=(B, nt),
        in_specs=_stream_specs(D, ntl, ctx_block0) + [halo_prev, halo_next, modspec(0), modspec(1),
                  pl.BlockSpec((1,) + w_in.shape[1:], lambda b, i: (layer, 0, 0),
                               pipeline_mode=pl.Buffered(1)),
                  pl.BlockSpec((tm, DA_QK_COLS), lambda b, i: (i, 0)),
                  pl.BlockSpec((tm, DA_QK_COLS), lambda b, i: (i, 0)),
                  const(sgg), const(sgb), const(sgw), const(sgbias)]
                 + [const(a) for a in rw_params],
        out_specs=[tok(DA_QK_COLS), tok(DA_QK_COLS), tok(DA_WIDTH), tok(SG_WIDTH), tok(3 * D)] + [tok(W)] * 11,
        out_shape=[jax.ShapeDtypeStruct((B, TT, DA_QK_COLS), BF16),
                   jax.ShapeDtypeStruct((B, TT, DA_QK_COLS), BF16),
                   jax.ShapeDtypeStruct((B, TT, DA_WIDTH), BF16),
                   jax.ShapeDtypeStruct((B, TT, SG_WIDTH), BF16),
                   jax.ShapeDtypeStruct((B, TT, 3 * D), BF16)]
                  + [jax.ShapeDtypeStruct((B, TT, W), F32)] * 2 + [jax.ShapeDtypeStruct((B, TT, W), BF16)] * 9,
        compiler_params=pltpu.CompilerParams(
            dimension_semantics=("parallel", "parallel"), vmem_limit_bytes=VMEM_LIMIT_BYTES),
        name="in_proj",
    )(x_lat, x_ctx, x_lat, x_lat, modall, modall, w_in, cos, sin, sgg, sgb, sgw, sgbias, *rw_params)


def _attn_kernel(q_ref, k_ref, v_ref, lam_ref, g_ref, o_ref, *, ntl, t_lat, lam_init):
    i = pl.program_id(2)
    lp = lam_ref[...]
    lam = (jnp.exp(jnp.sum(lp[0:1] * lp[1:2], axis=-1, keepdims=True))
           - jnp.exp(jnp.sum(lp[2:3] * lp[3:4], axis=-1, keepdims=True)) + lam_init)
    dv = DA_V_DIM
    heads = [slice(h * dv, (h + 1) * dv) for h in range(q_ref.shape[2] // dv)]
    lane = lax.broadcasted_iota(jnp.int32, (q_ref.shape[1], dv), 1)
    qs = []
    for hs in heads:
        q = q_ref[0, :, hs]
        zero = jnp.zeros_like(q)
        qs.append((jnp.where(lane < DA_QK_DIM, q, zero), jnp.where(lane >= DA_QK_DIM, q, zero)))

    def attend(k0, nk):
        kt = ATTN_KEY_TILE
        tiles = [slice(k0 + t * kt, k0 + (t + 1) * kt) for t in range(nk // kt)]

        def row_max(ss):
            m = ss[0]
            for s in ss[1:]:
                m = jnp.maximum(m, s)
            return jnp.max(m, axis=-1, keepdims=True)

        def pv(es, hs):
            acc = None
            for e, sl in zip(es, tiles):
                v = v_ref[0, sl, hs]
                d = _dot(e, jnp.concatenate([v, jnp.ones_like(v)], axis=1))
                acc = d if acc is None else acc + d
            return acc[:, :dv] / acc[:, dv:]

        ks = [[k_ref[0, sl, hs] for sl in tiles] for hs in heads]
        s0 = [[_dot_nt(q0, kk) for kk in kh] for (q0, _), kh in zip(qs, ks)]
        m0 = [row_max(s) for s in s0]
        s1, e0 = [], []
        for (_, q1), kh, sh, mh in zip(qs, ks, s0, m0):
            s1.append([])
            e0.append([])
            for kk, s in zip(kh, sh):
                s1[-1].append(_dot_nt(q1, kk))
                e0[-1].append(jnp.exp2(s - mh).astype(BF16))
        m1 = [row_max(s) for s in s1]
        o0 = [pv(e, hs) for e, hs in zip(e0, heads)]
        e1 = [[jnp.exp2(s - mh).astype(BF16) for s in sh] for sh, mh in zip(s1, m1)]
        for hs, oa, e in zip(heads, o0, e1):
            o = oa - lam * pv(e, hs)
            o = o * lax.rsqrt(jnp.mean(o * o, axis=-1, keepdims=True) + DA_EPS) * g_ref[...]
            o_ref[0, :, hs] = (o * (1.0 - lam_init)).astype(BF16)

    @pl.when(i < ntl)
    def _():
        attend(0, k_ref.shape[1])

    @pl.when(i >= ntl)
    def _():
        attend(t_lat, k_ref.shape[1] - t_lat)


def _diff_attention(q, k, v, lam_p, norm_g, ntl, lam_init, nt):
    B, TT, _ = q.shape
    tm = TOKEN_TILE
    w = ATTN_HEADS_PER_STEP * DA_V_DIM
    kv = pl.BlockSpec((1, TT, w), lambda b, h, i: (b, 0, h))
    qo = pl.BlockSpec((1, tm, w), lambda b, h, i: (b, i, h))
    return pl.pallas_call(
        functools.partial(_attn_kernel, ntl=ntl, t_lat=ntl * tm, lam_init=lam_init),
        grid=(B, DA_HEADS // ATTN_HEADS_PER_STEP, nt),
        in_specs=[qo, kv, kv,
                  pl.BlockSpec(lam_p.shape, lambda b, h, i: (0, 0)),
                  pl.BlockSpec((1, DA_V_DIM), lambda b, h, i: (0, 0))],
        out_specs=qo,
        out_shape=jax.ShapeDtypeStruct((B, TT, DA_WIDTH), BF16),
        compiler_params=pltpu.CompilerParams(
            dimension_semantics=("parallel", "parallel", "parallel"),
            vmem_limit_bytes=VMEM_LIMIT_BYTES),
        name="diff_attn",
    )(q, k, v, lam_p, norm_g.reshape(1, DA_V_DIM))


def _rwkv_feature_math(p, prev_row, next_row, params, outs):
    mu_ref, w0_ref, w2_ref, a0_ref, a2_ref, kk_ref, ka_ref, rk_ref, g2_ref, ones_ref = params
    lw0_ref, lw1_ref, kd0_ref, kd1_ref, b0_ref, b1_ref, kko_ref, v_ref, r_ref, bonus_ref, g_ref = outs
    tm = p.shape[0]
    row = lax.broadcasted_iota(jnp.int32, p.shape, 0)
    prev = jnp.where(row == 0, prev_row, pltpu.roll(p, 1, 0))
    nxt = jnp.where(row == tm - 1, next_row, pltpu.roll(p, tm - 1, 0))
    ps = p + mu_ref[0:1, :] * (prev - p) + mu_ref[1:2, :] * (nxt - p)

    W = RW_WIDTH
    r = ps[:, 0:W]
    k = ps[:, W:2 * W]
    v = ps[:, 2 * W:3 * W]
    xw = ps[:, 3 * W:3 * W + 64]
    xa = ps[:, 3 * W + 64:3 * W + 128]
    xg = ps[:, 3 * W + 128:3 * W + 256]
    ones_bd = ones_ref[...]

    w_pre = _dot(jnp.tanh(xw).astype(BF16), w2_ref[...]) + w0_ref[...]
    logw = -math.exp(-0.5) * _sigmoid(w_pre)
    a = _sigmoid(_dot(xa.astype(BF16), a2_ref[...]) + a0_ref[...])

    kx = k * kk_ref[...]
    nrm = jnp.sqrt(_seg_sum(kx * kx, ones_bd))
    kk = kx / jnp.maximum(nrm, 1e-12)
    ka = ka_ref[...]
    kd0 = k * (1.0 + (a[:, :W] - 1.0) * ka)
    kd1 = k * (1.0 + (a[:, W:] - 1.0) * ka)
    g = _dot(_sigmoid(xg).astype(BF16), g2_ref[...])
    k_b = 0.5 * (kd0 + kd1)
    bonus = _seg_sum(r * k_b * rk_ref[...], ones_bd) * v

    lw0_ref[0] = logw[:, :W]
    lw1_ref[0] = logw[:, W:]
    kd0_ref[0] = kd0.astype(BF16)
    kd1_ref[0] = kd1.astype(BF16)
    b0_ref[0] = (kk * a[:, :W]).astype(BF16)
    b1_ref[0] = (kk * a[:, W:]).astype(BF16)
    kko_ref[0] = kk.astype(BF16)
    v_ref[0] = v.astype(BF16)
    r_ref[0] = r.astype(BF16)
    bonus_ref[0] = bonus.astype(BF16)
    g_ref[0] = g.astype(BF16)


def _split_bf16(x, pieces):
    out = []
    for _ in range(pieces - 1):
        hi = x.astype(BF16)
        out.append(hi)
        x = x - hi.astype(F32)
    out.append(x.astype(BF16))
    return out


def _block_diag(x, groups):
    xb = x.astype(BF16)
    rows, lanes = xb.shape
    t = jnp.concatenate([xb] * groups, axis=0)
    ri = lax.broadcasted_iota(jnp.int32, t.shape, 0) // rows
    li = lax.broadcasted_iota(jnp.int32, t.shape, 1) // (lanes // groups)
    return jnp.where(ri == li, t, jnp.zeros_like(t))


def _scan_kernel(lw0, kd0, b0, kkf, vf, rf, lw1, kd1, b1, kkr, vr, rr, y0_ref, y1_ref, s_ref):
    step = pl.program_id(1)

    @pl.when(step == 0)
    def _():
        s_ref[...] = jnp.zeros_like(s_ref)

    C = lw0.shape[1]
    G = SCAN_HEADS_PER_PASS
    L = G * RW_HEAD
    assert C == RW_HEAD
    ti = lax.broadcasted_iota(jnp.int32, (C, L), 0)
    si = lax.broadcasted_iota(jnp.int32, (C, L), 1) % C
    tc = lax.broadcasted_iota(jnp.int32, (C, C), 0)
    sc = lax.broadcasted_iota(jnp.int32, (C, C), 1)
    same_head = (lax.broadcasted_iota(jnp.int32, (L, L), 0) // RW_HEAD
                 == lax.broadcasted_iota(jnp.int32, (L, L), 1) // RW_HEAD)

    groups = []
    for z, (lw, kd, b, kk, v, r) in enumerate(((lw0, kd0, b0, kkf, vf, rf), (lw1, kd1, b1, kkr, vr, rr))):
        rev = z == 1
        incl = (si >= ti) if rev else (si <= ti)
        strict = (si > ti) if rev else (si < ti)
        tri = jnp.where((sc >= tc) if rev else (sc <= tc), 1.0, 0.0).astype(BF16)
        for n in range(lw.shape[0]):
            logw = lw[n]
            cum = sum(_dot(tri, piece) for piece in _split_bf16(logw, 3))
            tot = jnp.sum(logw, axis=0, keepdims=True)
            p_in = jnp.exp(-cum)
            p_end = jnp.exp(tot - cum)
            kkz, rz, bz, kdz = (t[n].astype(F32) for t in (kk, r, b, kd))
            lhs = jnp.concatenate([-kkz * jnp.exp(cum - logw), rz * jnp.exp(cum)], axis=0)
            rhs_b = bz * p_in
            rhs_k = kdz * p_in
            upd = jnp.concatenate([bz * p_end, kdz * p_end], axis=0)
            dec = jnp.exp(tot)
            vz = v[n]
            for j in range(RW_HEADS // G):
                c = slice(j * L, (j + 1) * L)
                groups.append(dict(z=z, smp=n, j=j, lhs=lhs[:, c].astype(BF16), rhs_b=rhs_b[:, c],
                                   rhs_k=rhs_k[:, c], upd=upd[:, c].astype(BF16), dec=dec[:, c],
                                   v=vz[:, c], strict=strict, incl=incl))

    for g in groups:
        aa_b = _dot_nt(g["lhs"], _block_diag(g["rhs_b"], G))
        aa_k = _dot_nt(g["lhs"], _block_diag(g["rhs_k"], G))
        g["a_k"] = jnp.concatenate([jnp.where(g["strict"], aa_k[:C], 0.0),
                                    jnp.where(g["incl"], aa_k[C:], 0.0)], axis=0).astype(BF16)
        g["a_rb"] = jnp.where(g["incl"], aa_b[C:], 0.0).astype(BF16)
        g["n"] = jnp.where(g["strict"], aa_b[:C], 0.0)
        g["apow"] = _dot(g["n"].astype(BF16), _block_diag(g["n"], G))
    levels = int(math.log2(C)) - 1
    for lv in range(levels):
        for g in groups:
            pbd = _block_diag(g["apow"], G)
            if lv < levels - 1:
                st = _dot(jnp.concatenate([g["n"], g["apow"]], axis=0).astype(BF16), pbd)
                g["n"] = g["n"] + g["apow"] + st[:C]
                g["apow"] = st[C:]
            else:
                g["n"] = g["n"] + g["apow"] + _dot(g["n"].astype(BF16), pbd)
    for g in groups:
        g["s0"] = s_ref[g["smp"], g["z"], g["j"]]
        g["gh"] = _dot_nt(g["lhs"], g["s0"].astype(BF16))
    for g in groups:
        st = _dot(g["a_k"], _block_diag(g["v"], G))
        g["w"] = g["gh"][:C] + st[:C]
        g["yk"] = st[C:]
    for g in groups:
        g["u"] = g["w"] + _dot(g["n"].astype(BF16), _block_diag(g["w"], G))
    for g in groups:
        g["y"] = g["gh"][C:] + g["yk"] + _dot(g["a_rb"], _block_diag(g["u"], G))
    for g in groups:
        uv = jnp.concatenate([g["u"].astype(BF16), g["v"]], axis=0)
        s_ref[g["smp"], g["z"], g["j"]] = g["s0"] * g["dec"] + jnp.where(same_head, _dot_tn(uv, g["upd"]), 0.0)
    for z, y_ref in enumerate((y0_ref, y1_ref)):
        for n in range(y_ref.shape[0]):
            y_ref[n] = jnp.concatenate([g["y"] for g in groups if g["z"] == z and g["smp"] == n],
                                       axis=-1).astype(y_ref.dtype)


def _rwkv_scan(lw0, lw1, kd0, kd1, b0, b1, kk, v, r, t_lat):
    B, TT, W = kk.shape
    C = SCAN_CHUNK
    ncl = t_lat // C
    nch = TT // C
    ncc = nch - ncl

    def fwd(s):
        return jnp.where(s < ncc, ncl + s, s - ncc)

    nb = math.gcd(B, SCAN_BATCH)
    fs =pl.BlockSpec((nb, C, W), lambda b, s: (b, fwd(s), 0))
    rs = pl.BlockSpec((nb, C, W), lambda b, s: (b, nch - 1 - s, 0))
    return pl.pallas_call(
        _scan_kernel,
        grid=(B // nb, nch),
        in_specs=[fs] * 6 + [rs] * 6,
        out_specs=[fs, rs],
        out_shape=[jax.ShapeDtypeStruct((B, TT, W), BF16)] * 2,
        scratch_shapes=[pltpu.VMEM((nb, 2, RW_HEADS // SCAN_HEADS_PER_PASS, SCAN_HEADS_PER_PASS * RW_HEAD,
                                    SCAN_HEADS_PER_PASS * RW_HEAD), F32)],
        compiler_params=pltpu.CompilerParams(
            dimension_semantics=("parallel", "arbitrary"), vmem_limit_bytes=VMEM_LIMIT_BYTES),
        name="rwkv_scan",
    )(lw0, kd0, b0, kk, v, r, lw1, kd1, b1, kk, v, r)


def _merge_kernel(xl_ref, xc_ref, yda_ref, y0_ref, y1_ref, bonus_ref, g_ref, ysg_ref, gate_ref,
                  g1_ref, sh2_ref, sc2_ref, wb_ref, wo_ref, ln1g_ref, ln1b_ref, rlng_ref, rlnb_ref,
                  ones_ref, wr_ref, xmid_ref, h2_ref, aff_ref, *, alpha, ntl):
    D = D_MODEL
    tm = xl_ref.shape[1]
    is_lat = pl.program_id(1) < ntl
    subs = [slice(r, r + MERGE_ROWS) for r in range(0, tm, MERGE_ROWS)]
    ones_bd = ones_ref[...]
    wr_hi, wr_lo = _split_bf16(wr_ref[...], 2)
    wr_hl = jnp.concatenate([wr_hi, wr_lo], axis=0)

    yrw = []
    for sl in subs:
        y = y0_ref[0, sl, :].astype(F32) + y1_ref[0, sl, :].astype(F32)
        mu = _seg_sum(y, ones_bd) * (1.0 / RW_HEAD)
        dy = y - mu
        var = _seg_sum(dy * dy, ones_bd) * (1.0 / RW_HEAD)
        gn = dy * lax.rsqrt(var + RW_GN_EPS) * rlng_ref[...] + rlnb_ref[...]
        yrw.append(((gn + bonus_ref[0, sl, :].astype(F32)) * g_ref[0, sl, :].astype(F32)).astype(BF16))
    ms = []
    for sl, yr in zip(subs, yrw):
        m = gate_ref[0, sl, 0:D].astype(F32) * _dot(yda_ref[0, sl, :], wb_ref[0:DA_WIDTH, :])
        m = m + gate_ref[0, sl, D:2 * D].astype(F32) * _dot(yr, wb_ref[DA_WIDTH:DA_WIDTH + RW_WIDTH, :])
        m = m + gate_ref[0, sl, 2 * D:3 * D].astype(F32) * _dot(ysg_ref[0, sl, :], wb_ref[DA_WIDTH + RW_WIDTH:, :])
        ms.append(m.astype(BF16))
    mixes = [_dot(m, wo_ref[...]) for m in ms]
    h2s = []
    for sl, mix in zip(subs, mixes):
        z = alpha * jnp.where(is_lat, xl_ref[0, sl, :], xc_ref[0, sl, :]) + g1_ref[0, 0] * mix
        zm = jnp.mean(z, axis=-1, keepdims=True)
        dz = z - zm
        zv = jnp.mean(dz * dz, axis=-1, keepdims=True)
        xmid = dz * lax.rsqrt(zv + LN_EPS) * ln1g_ref[...] + ln1b_ref[...]
        xmid_ref[0, sl, :] = xmid
        h2 = xmid * (1.0 + sc2_ref[0, 0]) + sh2_ref[0, 0]
        h2_ref[0, sl, :] = h2.astype(BF16)
        h2s.append(h2)
    ne = wr_hi.shape[0]
    for sl, h2 in zip(subs, h2s):
        h_hi, h_lo = _split_bf16(h2, 2)
        part = _dot_nt(wr_hl, h_hi)
        logits = part[:ne] + part[ne:] + _dot_nt(wr_hi, h_lo)
        e = jnp.exp(logits - jnp.max(logits, axis=0, keepdims=True))
        aff_ref[0, :, sl] = e / jnp.sum(e, axis=0, keepdims=True)


def _merge(x_lat, x_ctx, ctx_block0, yda, y0, y1, bonus, g, ysg, gates, modall, w_branch, w_out, ln1g, ln1b,
           rlng, rlnb, ones_bd, w_router_t, ntl, alpha, nt):
    B, TT, _ = yda.shape
    D = x_lat.shape[2]
    tm = TOKEN_TILE
    tok = lambda w: pl.BlockSpec((1, tm, w), lambda b, i: (b, i, 0))
    modspec = lambda j: pl.BlockSpec((1, 1, 1, D), lambda b, i: (b, i // ntl, 0, j))
    const = lambda a: pl.BlockSpec(a.shape, lambda b, i: (0,) * a.ndim)
    consts = [w_branch, w_out, ln1g, ln1b, rlng, rlnb, ones_bd, w_router_t]
    return pl.pallas_call(
        functools.partial(_merge_kernel, alpha=alpha, ntl=ntl),
        grid=(B, nt),
        in_specs=_stream_specs(D, ntl, ctx_block0) + [tok(DA_WIDTH), tok(RW_WIDTH), tok(RW_WIDTH), tok(RW_WIDTH), tok(RW_WIDTH),
                  tok(SG_WIDTH), tok(3 * D), modspec(2), modspec(3), modspec(4)]
                 + [const(a) for a in consts],
        out_specs=[tok(D), tok(D), pl.BlockSpec((1, N_EXPERTS, tm), lambda b, i: (b, 0, i))],
        out_shape=[jax.ShapeDtypeStruct((B, TT, D), F32),
                   jax.ShapeDtypeStruct((B, TT, D), BF16),
                   jax.ShapeDtypeStruct((B, N_EXPERTS, TT), F32)],
        compiler_params=pltpu.CompilerParams(
            dimension_semantics=("parallel", "parallel"), vmem_limit_bytes=VMEM_LIMIT_BYTES),
        name="merge_ln1_router",
    )(x_lat, x_ctx, yda, y0, y1, bonus, g, ysg, gates, modall, modall, modall, *consts)


def _topk_kernel(aff_ref, tri_ref, rank_ref, *, cap):
    a = aff_ref[...].reshape(-1, aff_ref.shape[2])
    bits = pltpu.bitcast(a, jnp.int32)
    thr = jnp.zeros((a.shape[0], 1), jnp.int32)
    for bit in range(30, -1, -1):
        cand = thr | (1 << bit)
        cnt = jnp.sum(jnp.where(bits >= cand, 1.0, 0.0), axis=-1, keepdims=True)
        thr = jnp.where(cnt >= cap, cand, thr)
    gt = bits > thr
    eq = bits == thr
    need = cap - jnp.sum(jnp.where(gt, 1.0, 0.0), axis=-1, keepdims=True)
    tri = tri_ref[...]
    kb = tri.shape[0]

    def excl_cumsum(mask):
        m = jnp.where(mask, 1.0, 0.0)
        outs = []
        carry = jnp.zeros((m.shape[0], 1), F32)
        for j in range(m.shape[1] // kb):
            blk = m[:, j * kb:(j + 1) * kb]
            outs.append(_dot(blk.astype(BF16), tri) + carry)
            carry = carry + jnp.sum(blk, axis=-1, keepdims=True)
        return jnp.concatenate(outs, axis=1)

    sel = jnp.logical_or(gt, jnp.logical_and(eq, excl_cumsum(eq) < need))
    rank_ref[...] = jnp.where(sel, excl_cumsum(sel), -1.0).astype(jnp.int32).reshape(rank_ref.shape)


def _topk_ranks(aff, tri, t_off, t_len, cap):
    B, E, _ = aff.shape
    blk = t_off // t_len
    nb = math.gcd(B, RANK_BATCH)
    return pl.pallas_call(
        functools.partial(_topk_kernel, cap=cap),
        grid=(B // nb,),
        in_specs=[pl.BlockSpec((nb, E, t_len), lambda b: (b, 0, blk)),
                  pl.BlockSpec(tri.shape, lambda b: (0, 0))],
        out_specs=pl.BlockSpec((nb, E, t_len), lambda b: (b, 0, 0)),
        out_shape=jax.ShapeDtypeStruct((B, E, t_len), jnp.int32),
        compiler_params=pltpu.CompilerParams(
            dimension_semantics=("parallel",), vmem_limit_bytes=VMEM_LIMIT_BYTES),
        name="expert_choice_ranks",
    )(aff, tri)


def _moe_kernel(*refs, sets):
    n = len(sets)
    h_ref, aff_ref = refs[0], refs[1]
    rank_refs = refs[2:2 + n]
    wg_ref, wu_ref, wd_ref, f_ref = refs[2 + n:]
    e = pl.program_id(1)

    @pl.when(e == 0)
    def _():
        f_ref[...] = jnp.zeros_like(f_ref)

    onehots, gates, xs = [], [], []
    for (t0, tn, cap), rank_ref in zip(sets, rank_refs):
        rank = rank_ref[0, pl.ds(e, 1), :]
        aff = aff_ref[0, pl.ds(e, 1), t0:t0 + tn]
        slot = lax.broadcasted_iota(jnp.int32, (cap, tn), 0)
        hit = rank == slot
        onehot = jnp.where(hit, 1.0, 0.0).astype(BF16)
        onehots.append(onehot)
        gates.append(jnp.sum(jnp.where(hit, aff, 0.0), axis=-1, keepdims=True))
        xs.append(_dot(onehot, h_ref[0, t0:t0 + tn, :]).astype(BF16))
    xe = jnp.concatenate(xs, axis=0) if n > 1 else xs[0]
    gate = jnp.concatenate(gates, axis=0) if n > 1 else gates[0]
    hg = _dot(xe, wg_ref[0, 0])
    hid = (hg * _sigmoid(hg)) * _dot(xe, wu_ref[0, 0])
    ye = (_dot(hid.astype(BF16), wd_ref[0, 0]) * gate).astype(BF16)
    r0 = 0
    for (t0, tn, cap), onehot in zip(sets, onehots):
        f_ref[0, t0:t0 + tn, :] += _dot_tn(onehot, ye[r0:r0 + cap])
        r0 += cap


def _expert_ffn(h2, aff, ranks, wg, wu, wd, layer, sets):
    B, TT, D = h2.shape
    E = aff.shape[1]
    F = wg.shape[3]
    return pl.pallas_call(
        functools.partial(_moe_kernel, sets=sets),
        grid=(B, E),
        in_specs=[pl.BlockSpec((1, TT, D), lambda b, e: (b, 0, 0), pipeline_mode=pl.Buffered(1)),
                  pl.BlockSpec((1, E, TT), lambda b, e: (b, 0, 0))]
                 + [pl.BlockSpec((1, E, r.shape[2]), lambda b, e: (b, 0, 0)) for r in ranks]
                 + [pl.BlockSpec((1, 1, D, F), lambda b, e: (layer, e, 0, 0)),
                    pl.BlockSpec((1, 1, D, F), lambda b, e: (layer, e, 0, 0)),
                    pl.BlockSpec((1, 1, F, D), lambda b, e: (layer, e, 0, 0))],
        out_specs=pl.BlockSpec((1, TT, D), lambda b, e: (b, 0, 0)),
        out_shape=jax.ShapeDtypeStruct((B, TT, D), F32),
        compiler_params=pltpu.CompilerParams(
            dimension_semantics=("parallel", "arbitrary"), vmem_limit_bytes=VMEM_LIMIT_BYTES),
        name="expert_ffn",
    )(h2, aff, *ranks, wg, wu, wd)


def _ln2_kernel(x_ref, f_ref, g2_ref, lng_ref, lnb_ref, o_ref, *, alpha):
    z = alpha * x_ref[0] + g2_ref[0, 0] * f_ref[0]
    zm = jnp.mean(z, axis=-1, keepdims=True)
    dz = z - zm
    zv = jnp.mean(dz * dz, axis=-1, keepdims=True)
    o_ref[0] = dz * lax.rsqrt(zv + LN_EPS) * lng_ref[...] + lnb_ref[...]


def _final_norm(xmid, f, modall, lng, lnb, ntl, alpha, rows):
    B, _, D = xmid.shape
    tm = TOKEN_TILE
    tok = pl.BlockSpec((1, tm, D), lambda b, i: (b, i, 0))
    const = lambda a: pl.BlockSpec(a.shape, lambda b, i: (0,) * a.ndim)
    return pl.pallas_call(
        functools.partial(_ln2_kernel, alpha=alpha),
        grid=(B, rows // tm),
        in_specs=[tok, tok, pl.BlockSpec((1, 1, 1, D), lambda b, i: (b, i // ntl, 0, 5)),
                  const(lng), const(lnb)],
        out_specs=tok,
        out_shape=jax.ShapeDtypeStruct((B, rows, D), F32),
        compiler_params=pltpu.CompilerParams(dimension_semantics=("parallel", "parallel")),
        name="ln2",
    )(xmid, f, modall, lng, lnb)


def _rope_tables(t_lat, t_ctx):
    rows = t_lat // GRID_W
    row = jnp.repeat(jnp.arange(rows, dtype=F32), GRID_W)
    col = jnp.tile(jnp.arange(GRID_W, dtype=F32), rows)
    half = DA_QK_DIM // 2
    inv_freq = ROPE_BASE ** (-jnp.arange(0, half, 2, dtype=F32) / half)
    ar = row[:, None] * inv_freq
    ac = col[:, None] * inv_freq
    ang = jnp.concatenate([ar, ar, ac, ac], axis=-1)
    sign = jnp.where((jnp.arange(DA_QK_DIM) % 32) < 16, -1.0, 1.0).astype(F32)
    reps = DA_QK_COLS // DA_QK_DIM
    cos = jnp.tile(jnp.cos(ang), (1, reps))
    sin = jnp.tile(jnp.sin(ang) * sign, (1, reps))
    cos = jnp.concatenate([cos, jnp.ones((t_ctx, DA_QK_COLS), F32)], axis=0)
    sin = jnp.concatenate([sin, jnp.zeros((t_ctx, DA_QK_COLS), F32)], axis=0)
    return cos, sin


def kernel(x, c, ctx, c_ctx, w_mod, b_mod, w_in, da_lambda, da_norm_g, rw_shift_mu, rw_w0, rw_w2, rw_a0, rw_a2, rw_k_k, rw_k_a, rw_r_k, rw_ln_g, rw_ln_b, rw_g2, sg_norm_g, sg_norm_b, sg_w, sg_b, w_branch, w_out, ln1_g, ln1_b, w_router, w_e_gate, w_e_up, w_e_down, ln2_g, ln2_b):
    B, T, D = x.shape
    Tc = ctx.shape[1]
    depth = w_mod.shape[0]
    tm = TOKEN_TILE
    assert D == D_MODEL and T % tm == 0 and Tc == tm
    ntl = T // tm
    alpha = (2 * depth) ** 0.25
    cap_lat = EC_CAPACITY * T // N_EXPERTS
    cap_ctx = EC_CAPACITY * Tc // N_EXPERTS

    cos, sin = _rope_tables(T, Tc)
    lane = jnp.arange(RW_WIDTH)
    ones_bd = (lane[:, None] // RW_HEAD == lane[None, :] // RW_HEAD).astype(BF16)
    kb = math.gcd(Tc, RANK_BLOCK)
    tri = (jnp.arange(kb)[:, None] < jnp.arange(kb)[None, :]).astype(BF16)
    rows = ((B + 1 + 7) // 8) * 8
    cc = jnp.concatenate([c, c_ctx[None, :], jnp.zeros((rows - B - 1, D), F32)], axis=0)
    row2 = lambda a: a.reshape(1, -1)

    w_in_bf = w_in.astype(BF16)
    wg, wu, wd = w_e_gate.astype(BF16), w_e_up.astype(BF16), w_e_down.astype(BF16)
    x_lat, x_ctx, ctx_block0 = x, ctx, 0
    for l in range(depth):
        last = l == depth - 1
        lam_init = 0.8 - 0.6 * math.exp(-0.3 * l)
        mod = _modulation(cc, w_mod[l], b_mod[l])
        modall = jnp.stack([mod[:B], jnp.broadcast_to(mod[B], (B, 6 * D))], axis=1)
        modall = modall.reshape(B, 2, 1, 6 * D)

        sgbias = jnp.repeat(sg_b[l].T, SG_WIDTH // SG_GROUPS, axis=1)
        cat2 = lambda a: jnp.transpose(a, (1, 0, 2)).reshape(a.shape[1], 2 * RW_WIDTH)
        rw_params = [rw_shift_mu[l], row2(rw_w0[l]), cat2(rw_w2[l]).astype(BF16), row2(rw_a0[l]),
                     cat2(rw_a2[l]).astype(BF16), row2(rw_k_k[l]), row2(rw_k_a[l]), row2(rw_r_k[l]),
                     rw_g2[l].astype(BF16), ones_bd]
        (q, k, v, ysg, gates, lw0, lw1, kd0, kd1, b0, b1, kk, vv, rr, bonus, gg) = _input_projection(
            x_lat, x_ctx, ctx_block0, T + Tc, modall, w_in_bf, l, cos, sin, row2(sg_norm_g[l]),
            row2(sg_norm_b[l]), sg_w[l].astype(BF16), sgbias, rw_params, ntl)

        nt_out = ntl if last else (T + Tc) // tm
        yda = _diff_attention(q, k, v, da_lambda[l], da_norm_g[l], ntl, lam_init, nt_out)

        y0, y1 = _rwkv_scan(lw0, lw1, kd0, kd1, b0, b1, kk, vv, rr, T)

        xmid, h2, aff = _merge(
            x_lat, x_ctx, ctx_block0, yda, y0, y1, bonus, gg, ysg, gates, modall, w_branch[l].astype(BF16),
            w_out[l].astype(BF16), row2(ln1_g[l]), row2(ln1_b[l]), row2(rw_ln_g[l]),
            row2(rw_ln_b[l]), ones_bd, w_router[l].T, ntl, alpha, nt_out)

        sets = ((0, T, cap_lat),) if last else ((0, T, cap_lat), (T, Tc, cap_ctx))
        ranks = [_topk_ranks(aff, tri, 0, T, cap_lat)]
        if not last:
            ranks.append(_topk_ranks(aff, tri, T, Tc, cap_ctx))
        f = _expert_ffn(h2, aff, ranks, wg, wu, wd, l, sets)
        x_lat = _final_norm(xmid, f, modall, row2(ln2_g[l]), row2(ln2_b[l]), ntl, alpha,
                            T if last else T + Tc)
        x_ctx, ctx_block0 = x_lat, ntl
    return x_lat
```

```python
import functools
import math

import jax
import jax.numpy as jnp
from jax import lax
from jax.experimental import pallas as pl
from jax.experimental.pallas import tpu as pltpu

F32 = jnp.float32
BF16 = jnp.bfloat16
HIGHEST = lax.Precision.HIGHEST

D_MODEL = 1024
GRID_W = 64
DA_HEADS = 4
DA_QK_DIM = 64
DA_V_DIM = 128
DA_WIDTH = 512
DA_QK_COLS = 512
ROPE_BASE = 10000.0
DA_EPS = 1e-5
RW_HEAD = 64
RW_HEADS = 8
RW_WIDTH = 512
RW_COLS = 1792
RW_GN_EPS = 64e-5
SG_CHUNK = 128
SG_GROUPS = 4
SG_WIDTH = 512
DA_K0 = 512
DA_V0 = 1024
RW_0 = 1536
SG_0 = RW_0 + RW_COLS
GATE_0 = SG_0 + 2 * SG_WIDTH
N_EXPERTS = 16
EC_CAPACITY = 2
LN_EPS = 1e-5
LOG2_E = math.log2(math.e)

TOKEN_TILE = 256
SCAN_CHUNK = 64
ATTN_KEY_TILE = 256
ATTN_HEADS_PER_STEP = 4
SCAN_HEADS_PER_PASS = 4
SCAN_BATCH = 4
MERGE_TILE = 512
MERGE_ROWS = 128
RANK_BLOCK = 256
RANK_BATCH = 4
VMEM_LIMIT_BYTES = 58 * 1024 * 1024


def _dot(a, b):
    return jnp.dot(a, b, preferred_element_type=F32)


def _dot_hi(a, b):
    return jnp.dot(a, b, preferred_element_type=F32, precision=HIGHEST)


def _dot_nt(a, b):
    return lax.dot_general(a, b, (((1,), (1,)), ((), ())), preferred_element_type=F32)


def _dot_tn(a, b):
    return lax.dot_general(a, b, (((0,), (0,)), ((), ())), preferred_element_type=F32)


def _sigmoid(z):
    return 1.0 / (1.0 + jnp.exp(-z))


def _seg_sum(z, ones_bd):
    hi = z.astype(BF16)
    lo = (z - hi.astype(F32)).astype(BF16)
    return _dot(hi, ones_bd) + _dot(lo, ones_bd)


def _mod_kernel(c_ref, w_ref, b_ref, o_ref):
    cc = c_ref[...]
    o_ref[...] = _dot_hi(cc * _sigmoid(cc), w_ref[...]) + b_ref[...]


def _modulation(cc, w_mod, b_mod):
    rows, d = cc.shape
    n = w_mod.shape[1]
    tn = 1024
    return pl.pallas_call(
        _mod_kernel,
        grid=(n // tn,),
        in_specs=[pl.BlockSpec((rows, d), lambda j: (0, 0)),
                  pl.BlockSpec((d, tn), lambda j: (0, j)),
                  pl.BlockSpec((1, tn), lambda j: (0, j))],
        out_specs=pl.BlockSpec((rows, tn), lambda j: (0, j)),
        out_shape=jax.ShapeDtypeStruct((rows, n), F32),
        name="adaln_mod",
    )(cc, w_mod, b_mod.reshape(1, n))


def _inproj_kernel(xl_ref, xc_ref, lp_ref, ln_ref, sh_ref, sc_ref, w_ref, cos_ref, sin_ref, sgg_ref, sgb_ref,
                   sgw_ref, sgbias_ref, *rest, ntl, nt):
    rw_params, (q_ref, k_ref, v_ref, sg_ref, gate_ref), rw_outs = rest[:10], rest[10:15], rest[15:]
    tm = xl_ref.shape[1]
    i = pl.program_id(1)
    x = jnp.where(i < ntl, xl_ref[0], xc_ref[0])
    h = (x * (1.0 + sc_ref[0, 0]) + sh_ref[0, 0]).astype(BF16)

    def proj(c0, c1):
        return _dot(h, w_ref[0, :, c0:c1])

    cos = cos_ref[...]
    sin = sin_ref[...]
    lane = lax.broadcasted_iota(jnp.int32, (tm, DA_QK_COLS), 1)
    first = (lane % 32) < 16

    def rope(z):
        zr = jnp.where(first, pltpu.roll(z, DA_QK_COLS - 16, 1), pltpu.roll(z, 16, 1))
        return z * cos + zr * sin

    q_ref[0] = (rope(proj(0, DA_K0)) * (DA_QK_DIM ** -0.5 * LOG2_E)).astype(BF16)
    k_ref[0] = rope(proj(DA_K0, DA_V0)).astype(BF16)
    v_ref[0] = proj(DA_V0, RW_0).astype(BF16)
    halo = jnp.concatenate([lp_ref[0], ln_ref[0]], axis=0)
    h_halo = (halo * (1.0 + sc_ref[0, 0]) + sh_ref[0, 0]).astype(BF16)
    p_ext = _dot(jnp.concatenate([h, h_halo], axis=0), w_ref[0, :, RW_0:SG_0])
    prev_ok = jnp.logical_and(i != 0, i != ntl)
    next_ok = jnp.logical_and(i != ntl - 1, i != nt - 1)
    _rwkv_feature_math(p_ext[:tm], jnp.where(prev_ok, p_ext[tm + 7:tm + 8], 0.0),
                       jnp.where(next_ok, p_ext[tm + 8:tm + 9], 0.0), rw_params, rw_outs)
    for j in range(3):
        gate_ref[0, :, j * D_MODEL:(j + 1) * D_MODEL] = _sigmoid(
            proj(GATE_0 + j * D_MODEL, GATE_0 + (j + 1) * D_MODEL)).astype(BF16)

    ps = proj(SG_0, GATE_0)
    gl = ps * (0.5 * (1.0 + jnp.tanh(math.sqrt(2.0 / math.pi) * (ps + 0.044715 * (ps * ps * ps)))))
    u = gl[:, :SG_WIDTH]
    vv = gl[:, SG_WIDTH:]
    mu = jnp.mean(vv, axis=-1, keepdims=True)
    dv = vv - mu
    var = jnp.mean(dv * dv, axis=-1, keepdims=True)
    vn = (dv * lax.rsqrt(var + LN_EPS) * sgg_ref[...] + sgb_ref[...]).astype(BF16)
    gd = SG_WIDTH // SG_GROUPS
    for n in range(tm // SG_CHUNK):
        r0 = n * SG_CHUNK
        for g in range(SG_GROUPS):
            c0 = g * gd
            vm = _dot(sgw_ref[g], vn[r0:r0 + SG_CHUNK, c0:c0 + gd]) + sgbias_ref[:, c0:c0 + gd]
            sg_ref[0, r0:r0 + SG_CHUNK, c0:c0 + gd] = (u[r0:r0 + SG_CHUNK, c0:c0 + gd] * vm).astype(BF16)


def _stream_specs(d, ntl, ctx_block0):
    tm = TOKEN_TILE
    lat = pl.BlockSpec((1, tm, d), lambda b, i: (b, jnp.minimum(i, ntl - 1), 0))
    ctx = pl.BlockSpec((1, tm, d), lambda b, i: (b, ctx_block0 + jnp.maximum(i - ntl, 0), 0))
    return [lat, ctx]


def _input_projection(x_lat, x_ctx, ctx_block0, TT, modall, w_in, layer, cos, sin, sgg, sgb, sgw, sgbias,
                      rw_params, ntl):
    B, lat_rows, D = x_lat.shape
    tm = TOKEN_TILE
    nt = TT // tm
    r8 = tm // 8
    last8 = lat_rows // 8 - 1
    W = RW_WIDTH
    tok = lambda w: pl.BlockSpec((1, tm, w), lambda b, i: (b, i, 0))
    modspec = lambda j: pl.BlockSpec((1, 1, 1, D), lambda b, i: (b, i // ntl, 0, j))
    const = lambda a: pl.BlockSpec(a.shape, lambda b, i: (0,) * a.ndim)
    halo_prev = pl.BlockSpec((1, 8, D), lambda b, i: (b, jnp.maximum(jnp.minimum(i, ntl - 1) * r8 - 1, 0), 0))
    halo_next = pl.BlockSpec((1, 8, D), lambda b, i: (b, jnp.minimum((jnp.minimum(i, ntl - 1) + 1) * r8, last8), 0))
    return pl.pallas_call(
        functools.partial(_inproj_kernel, ntl=ntl, nt=nt),
        grid=(B, nt),
        in_specs=_stream_specs(D, ntl, ctx_block0) + [halo_prev, halo_next, modspec(0), modspec(1),
                  pl.BlockSpec((1,) + w_in.shape[1:], lambda b, i: (layer, 0, 0),
                               pipeline_mode=pl.Buffered(1)),
                  pl.BlockSpec((tm, DA_QK_COLS), lambda b, i: (i, 0)),
                  pl.BlockSpec((tm, DA_QK_COLS), lambda b, i: (i, 0)),
                  const(sgg), const(sgb), const(sgw), const(sgbias)]
                 + [const(a) for a in rw_params],
        out_specs=[tok(DA_QK_COLS), tok(DA_QK_COLS), tok(DA_WIDTH), tok(SG_WIDTH), tok(3 * D)] + [tok(W)] * 11,
        out_shape=[jax.ShapeDtypeStruct((B, TT, DA_QK_COLS), BF16),
                   jax.ShapeDtypeStruct((B, TT, DA_QK_COLS), BF16),
                   jax.ShapeDtypeStruct((B, TT, DA_WIDTH), BF16),
                   jax.ShapeDtypeStruct((B, TT, SG_WIDTH), BF16),
                   jax.ShapeDtypeStruct((B, TT, 3 * D), BF16)]
                  + [jax.ShapeDtypeStruct((B, TT, W), F32)] * 2 + [jax.ShapeDtypeStruct((B, TT, W), BF16)] * 9,
        compiler_params=pltpu.CompilerParams(
            dimension_semantics=("parallel", "parallel"), vmem_limit_bytes=VMEM_LIMIT_BYTES),
        name="in_proj",
    )(x_lat, x_ctx, x_lat, x_lat, modall, modall, w_in, cos, sin, sgg, sgb, sgw, sgbias, *rw_params)


def _attn_kernel(q_ref, k_ref, v_ref, lam_ref, g_ref, o_ref, *, ntl, t_lat, lam_init):
    i = pl.program_id(2)
    lp = lam_ref[...]
    lam = (jnp.exp(jnp.sum(lp[0:1] * lp[1:2], axis=-1, keepdims=True))
           - jnp.exp(jnp.sum(lp[2:3] * lp[3:4], axis=-1, keepdims=True)) + lam_init)
    dv = DA_V_DIM
    heads = [slice(h * dv, (h + 1) * dv) for h in range(q_ref.shape[2] // dv)]
    lane = lax.broadcasted_iota(jnp.int32, (q_ref.shape[1], dv), 1)
    qs = []
    for hs in heads:
        q = q_ref[0, :, hs]
        zero = jnp.zeros_like(q)
        qs.append((jnp.where(lane < DA_QK_DIM, q, zero), jnp.where(lane >= DA_QK_DIM, q, zero)))

    def attend(k0, nk):
        kt = ATTN_KEY_TILE
        tiles = [slice(k0 + t * kt, k0 + (t + 1) * kt) for t in range(nk // kt)]

        def row_max(ss):
            m = ss[0]
            for s in ss[1:]:
                m = jnp.maximum(m, s)
            return jnp.max(m, axis=-1, keepdims=True)

        def pv(es, hs):
            acc = None
            for e, sl in zip(es, tiles):
                v = v_ref[0, sl, hs]
                d = _dot(e, jnp.concatenate([v, jnp.ones_like(v)], axis=1))
                acc = d if acc is None else acc + d
            return acc[:, :dv] / acc[:, dv:]

        ks = [[k_ref[0, sl, hs] for sl in tiles] for hs in heads]
        s0 = [[_dot_nt(q0, kk) for kk in kh] for (q0, _), kh in zip(qs, ks)]
        m0 = [row_max(s) for s in s0]
        s1, e0 = [], []
        for (_, q1), kh, sh, mh in zip(qs, ks, s0, m0):
            s1.append([])
            e0.append([])
            for kk, s in zip(kh, sh):
                s1[-1].append(_dot_nt(q1, kk))
                e0[-1].append(jnp.exp2(s - mh).astype(BF16))
        m1 = [row_max(s) for s in s1]
        o0 = [pv(e, hs) for e, hs in zip(e0, heads)]
        e1 = [[jnp.exp2(s - mh).astype(BF16) for s in sh] for sh, mh in zip(s1, m1)]
        for hs, oa, e in zip(heads, o0, e1):
            o = oa - lam * pv(e, hs)
            o = o * lax.rsqrt(jnp.mean(o * o, axis=-1, keepdims=True) + DA_EPS) * g_ref[...]
            o_ref[0, :, hs] = (o * (1.0 - lam_init)).astype(BF16)

    @pl.when(i < ntl)
    def _():
        attend(0, k_ref.shape[1])

    @pl.when(i >= ntl)
    def _():
        attend(t_lat, k_ref.shape[1] - t_lat)


def _diff_attention(q, k, v, lam_p, norm_g, ntl, lam_init, nt):
    B, TT, _ = q.shape
    tm = TOKEN_TILE
    w = ATTN_HEADS_PER_STEP * DA_V_DIM
    kv = pl.BlockSpec((1, TT, w), lambda b, h, i: (b, 0, h))
    qo = pl.BlockSpec((1, tm, w), lambda b, h, i: (b, i, h))
    return pl.pallas_call(
        functools.partial(_attn_kernel, ntl=ntl, t_lat=ntl * tm, lam_init=lam_init),
        grid=(B, DA_HEADS // ATTN_HEADS_PER_STEP, nt),
        in_specs=[qo, kv, kv,
                  pl.BlockSpec(lam_p.shape, lambda b, h, i: (0, 0)),
                  pl.BlockSpec((1, DA_V_DIM), lambda b, h, i: (0, 0))],
        out_specs=qo,
        out_shape=jax.ShapeDtypeStruct((B, TT, DA_WIDTH), BF16),
        compiler_params=pltpu.CompilerParams(
            dimension_semantics=("parallel", "parallel", "parallel"),
            vmem_limit_bytes=VMEM_LIMIT_BYTES),
        name="diff_attn",
    )(q, k, v, lam_p, norm_g.reshape(1, DA_V_DIM))


def _rwkv_feature_math(p, prev_row, next_row, params, outs):
    mu_ref, w0_ref, w2_ref, a0_ref, a2_ref, kk_ref, ka_ref, rk_ref, g2_ref, ones_ref = params
    lw0_ref, lw1_ref, kd0_ref, kd1_ref, b0_ref, b1_ref, kko_ref, v_ref, r_ref, bonus_ref, g_ref = outs
    tm = p.shape[0]
    row = lax.broadcasted_iota(jnp.int32, p.shape, 0)
    prev = jnp.where(row == 0, prev_row, pltpu.roll(p, 1, 0))
    nxt = jnp.where(row == tm - 1, next_row, pltpu.roll(p, tm - 1, 0))
    ps = p + mu_ref[0:1, :] * (prev - p) + mu_ref[1:2, :] * (nxt - p)

    W = RW_WIDTH
    r = ps[:, 0:W]
    k = ps[:, W:2 * W]
    v = ps[:, 2 * W:3 * W]
    xw = ps[:, 3 * W:3 * W + 64]
    xa = ps[:, 3 * W + 64:3 * W + 128]
    xg = ps[:, 3 * W + 128:3 * W + 256]
    ones_bd = ones_ref[...]

    w_pre = _dot(jnp.tanh(xw).astype(BF16), w2_ref[...]) + w0_ref[...]
    logw = -math.exp(-0.5) * _sigmoid(w_pre)
    a = _sigmoid(_dot(xa.astype(BF16), a2_ref[...]) + a0_ref[...])

    kx = k * kk_ref[...]
    nrm = jnp.sqrt(_seg_sum(kx * kx, ones_bd))
    kk = kx / jnp.maximum(nrm, 1e-12)
    ka = ka_ref[...]
    kd0 = k * (1.0 + (a[:, :W] - 1.0) * ka)
    kd1 = k * (1.0 + (a[:, W:] - 1.0) * ka)
    g = _dot(_sigmoid(xg).astype(BF16), g2_ref[...])
    k_b = 0.5 * (kd0 + kd1)
    bonus = _seg_sum(r * k_b * rk_ref[...], ones_bd) * v

    lw0_ref[0] = logw[:, :W]
    lw1_ref[0] = logw[:, W:]
    kd0_ref[0] = kd0.astype(BF16)
    kd1_ref[0] = kd1.astype(BF16)
    b0_ref[0] = (kk * a[:, :W]).astype(BF16)
    b1_ref[0] = (kk * a[:, W:]).astype(BF16)
    kko_ref[0] = kk.astype(BF16)
    v_ref[0] = v.astype(BF16)
    r_ref[0] = r.astype(BF16)
    bonus_ref[0] = bonus.astype(BF16)
    g_ref[0] = g.astype(BF16)


def _split_bf16(x, pieces):
    out = []
    for _ in range(pieces - 1):
        hi = x.astype(BF16)
        out.append(hi)
        x = x - hi.astype(F32)
    out.append(x.astype(BF16))
    return out


def _block_diag(x, groups):
    xb = x.astype(BF16)
    rows, lanes = xb.shape
    t = jnp.concatenate([xb] * groups, axis=0)
    ri = lax.broadcasted_iota(jnp.int32, t.shape, 0) // rows
    li = lax.broadcasted_iota(jnp.int32, t.shape, 1) // (lanes // groups)
    return jnp.where(ri == li, t, jnp.zeros_like(t))


def _scan_kernel(lw0, kd0, b0, kkf, vf, rf, lw1, kd1, b1, kkr, vr, rr, y0_ref, y1_ref, s_ref):
    step = pl.program_id(1)

    @pl.when(step == 0)
    def _():
        s_ref[...] = jnp.zeros_like(s_ref)

    C = lw0.shape[1]
    G = SCAN_HEADS_PER_PASS
    L = G * RW_HEAD
    assert C == RW_HEAD
    ti = lax.broadcasted_iota(jnp.int32, (C, L), 0)
    si = lax.broadcasted_iota(jnp.int32, (C, L), 1) % C
    tc = lax.broadcasted_iota(jnp.int32, (C, C), 0)
    sc = lax.broadcasted_iota(jnp.int32, (C, C), 1)
    same_head = (lax.broadcasted_iota(jnp.int32, (L, L), 0) // RW_HEAD
                 == lax.broadcasted_iota(jnp.int32, (L, L), 1) // RW_HEAD)

    groups = []
    for z, (lw, kd, b, kk, v, r) in enumerate(((lw0, kd0, b0, kkf, vf, rf), (lw1, kd1, b1, kkr, vr, rr))):
        rev = z == 1
        incl = (si >= ti) if rev else (si <= ti)
        strict = (si > ti) if rev else (si < ti)
        tri = jnp.where((sc >= tc) if rev else (sc <= tc), 1.0, 0.0).astype(BF16)
        for n in range(lw.shape[0]):
            logw = lw[n]
            cum = sum(_dot(tri, piece) for piece in _split_bf16(logw, 3))
            tot = jnp.sum(logw, axis=0, keepdims=True)
            p_in = jnp.exp(-cum)
            p_end = jnp.exp(tot - cum)
            kkz, rz, bz, kdz = (t[n].astype(F32) for t in (kk, r, b, kd))
            lhs = jnp.concatenate([-kkz * jnp.exp(cum - logw), rz * jnp.exp(cum)], axis=0)
            rhs_b = bz * p_in
            rhs_k = kdz * p_in
            upd = jnp.concatenate([bz * p_end, kdz * p_end], axis=0)
            dec = jnp.exp(tot)
            vz = v[n]
            for j in range(RW_HEADS // G):
                c = slice(j * L, (j + 1) * L)
                groups.append(dict(z=z, smp=n, j=j, lhs=lhs[:, c].astype(BF16), rhs_b=rhs_b[:, c],
                                   rhs_k=rhs_k[:, c], upd=upd[:, c].astype(BF16), dec=dec[:, c],
                                   v=vz[:, c], strict=strict, incl=incl))

    for g in groups:
        aa_b = _dot_nt(g["lhs"], _block_diag(g["rhs_b"], G))
        aa_k = _dot_nt(g["lhs"], _block_diag(g["rhs_k"], G))
        g["a_k"] = jnp.concatenate([jnp.where(g["strict"], aa_k[:C], 0.0),
                                    jnp.where(g["incl"], aa_k[C:], 0.0)], axis=0).astype(BF16)
        g["a_rb"] = jnp.where(g["incl"], aa_b[C:], 0.0).astype(BF16)
        g["n"] = jnp.where(g["strict"], aa_b[:C], 0.0)
        g["apow"] = _dot(g["n"].astype(BF16), _block_diag(g["n"], G))
    levels = int(math.log2(C)) - 1
    for lv in range(levels):
        for g in groups:
            pbd = _block_diag(g["apow"], G)
            if lv < levels - 1:
                st = _dot(jnp.concatenate([g["n"], g["apow"]], axis=0).astype(BF16), pbd)
                g["n"] = g["n"] + g["apow"] + st[:C]
                g["apow"] = st[C:]
            else:
                g["n"] = g["n"] + g["apow"] + _dot(g["n"].astype(BF16), pbd)
    for g in groups:
        g["s0"] = s_ref[g["smp"], g["z"], g["j"]]
        g["gh"] = _dot_nt(g["lhs"], g["s0"].astype(BF16))
    for g in groups:
        st = _dot(g["a_k"], _block_diag(g["v"], G))
        g["w"] = g["gh"][:C] + st[:C]
        g["yk"] = st[C:]
    for g in groups:
        g["u"] = g["w"] + _dot(g["n"].astype(BF16), _block_diag(g["w"], G))
    for g in groups:
        g["y"] = g["gh"][C:] + g["yk"] + _dot(g["a_rb"], _block_diag(g["u"], G))
    for g in groups:
        uv = jnp.concatenate([g["u"].astype(BF16), g["v"]], axis=0)
        s_ref[g["smp"], g["z"], g["j"]] = g["s0"] * g["dec"] + jnp.where(same_head, _dot_tn(uv, g["upd"]), 0.0)
    for z, y_ref in enumerate((y0_ref, y1_ref)):
        for n in range(y_ref.shape[0]):
            y_ref[n] = jnp.concatenate([g["y"] for g in groups if g["z"] == z and g["smp"] == n],
                                       axis=-1).astype(y_ref.dtype)


def _rwkv_scan(lw0, lw1, kd0, kd1, b0, b1, kk, v, r, t_lat):
    B, TT, W = kk.shape
    C = SCAN_CHUNK
    ncl = t_lat // C
    nch = TT // C
    ncc = nch - ncl

    def fwd(s):
        return jnp.where(s < ncc, ncl + s, s - ncc)

    nb = math.gcd(B, SCAN_BATCH)
    fs =pl.BlockSpec((nb, C, W), lambda b, s: (b, fwd(s), 0))
    rs = pl.BlockSpec((nb, C, W), lambda b, s: (b, nch - 1 - s, 0))
    return pl.pallas_call(
        _scan_kernel,
        grid=(B // nb, nch),
        in_specs=[fs] * 6 + [rs] * 6,
        out_specs=[fs, rs],
        out_shape=[jax.ShapeDtypeStruct((B, TT, W), BF16)] * 2,
        scratch_shapes=[pltpu.VMEM((nb, 2, RW_HEADS // SCAN_HEADS_PER_PASS, SCAN_HEADS_PER_PASS * RW_HEAD,
                                    SCAN_HEADS_PER_PASS * RW_HEAD), F32)],
        compiler_params=pltpu.CompilerParams(
            dimension_semantics=("parallel", "arbitrary"), vmem_limit_bytes=VMEM_LIMIT_BYTES),
        name="rwkv_scan",
    )(lw0, kd0, b0, kk, v, r, lw1, kd1, b1, kk, v, r)


def _merge_kernel(x_ref, yda_ref, y0_ref, y1_ref, bonus_ref, g_ref, ysg_ref, gate_ref,
                  g1_ref, sh2_ref, sc2_ref, wb_ref, wo_ref, ln1g_ref, ln1b_ref, rlng_ref, rlnb_ref,
                  ones_ref, wr_ref, *rest, alpha):
    xmid_ref, h2_ref, aff_ref = rest[-3:]
    D = D_MODEL
    tm = x_ref.shape[1]
    subs = [slice(r, r + MERGE_ROWS) for r in range(0, tm, MERGE_ROWS)]
    ones_bd = ones_ref[...]
    wr_hi, wr_lo = _split_bf16(wr_ref[...], 2)
    wr_hl = jnp.concatenate([wr_hi, wr_lo], axis=0)

    yrw = []
    for sl in subs:
        y = y0_ref[0, sl, :].astype(F32) + y1_ref[0, sl, :].astype(F32)
        mu = _seg_sum(y, ones_bd) * (1.0 / RW_HEAD)
        dy = y - mu
        var = _seg_sum(dy * dy, ones_bd) * (1.0 / RW_HEAD)
        gn = dy * lax.rsqrt(var + RW_GN_EPS) * rlng_ref[...] + rlnb_ref[...]
        yrw.append(((gn + bonus_ref[0, sl, :].astype(F32)) * g_ref[0, sl, :].astype(F32)).astype(BF16))
    ms = []
    for sl, yr in zip(subs, yrw):
        m = gate_ref[0, sl, 0:D].astype(F32) * _dot(yda_ref[0, sl, :], wb_ref[0:DA_WIDTH, :])
        m = m + gate_ref[0, sl, D:2 * D].astype(F32) * _dot(yr, wb_ref[DA_WIDTH:DA_WIDTH + RW_WIDTH, :])
        m = m + gate_ref[0, sl, 2 * D:3 * D].astype(F32) * _dot(ysg_ref[0, sl, :], wb_ref[DA_WIDTH + RW_WIDTH:, :])
        ms.append(m.astype(BF16))
    mixes = [_dot(m, wo_ref[...]) for m in ms]
    h2s = []
    for sl, mix in zip(subs, mixes):
        z = alpha * x_ref[0, sl, :] + g1_ref[0, 0] * mix
        zm = jnp.mean(z, axis=-1, keepdims=True)
        dz = z - zm
        zv = jnp.mean(dz * dz, axis=-1, keepdims=True)
        xmid = dz * lax.rsqrt(zv + LN_EPS) * ln1g_ref[...] + ln1b_ref[...]
        xmid_ref[0, sl, :] = xmid
        h2 = xmid * (1.0 + sc2_ref[0, 0]) + sh2_ref[0, 0]
        h2_ref[0, sl, :] = h2.astype(BF16)
        h2s.append(h2)
    ne = wr_hi.shape[0]
    for sl, h2 in zip(subs, h2s):
        h_hi, h_lo = _split_bf16(h2, 2)
        part = _dot_nt(wr_hl, h_hi)
        logits = part[:ne] + part[ne:] + _dot_nt(wr_hi, h_lo)
        e = jnp.exp(logits - jnp.max(logits, axis=0, keepdims=True))
        aff_ref[0, :, sl] = e / jnp.sum(e, axis=0, keepdims=True)


def _merge(x_src, x_block0, tile, row0, n_tiles, mod_row, prev_outs, yda, y0, y1, bonus, g, ysg, gates, modall,
           w_branch, w_out, ln1g, ln1b, rlng, rlnb, ones_bd, w_router_t, alpha):
    B, TT, _ = yda.shape
    D = x_src.shape[2]
    blk0 = row0 // tile
    tok = lambda w: pl.BlockSpec((1, tile, w), lambda b, i: (b, blk0 + i, 0))
    modspec = lambda j: pl.BlockSpec((1, 1, 1, D), lambda b, i: (b, mod_row, 0, j))
    const = lambda a: pl.BlockSpec(a.shape, lambda b, i: (0,) * a.ndim)
    consts = [w_branch, w_out, ln1g, ln1b, rlng, rlnb, ones_bd, w_router_t]
    n_in = 11 + len(consts)
    prev = list(prev_outs) if prev_outs is not None else []
    return pl.pallas_call(
        functools.partial(_merge_kernel, alpha=alpha),
        grid=(B, n_tiles),
        in_specs=[pl.BlockSpec((1, tile, D), lambda b, i: (b, x_block0 + i, 0)),
                  tok(DA_WIDTH), tok(RW_WIDTH), tok(RW_WIDTH), tok(RW_WIDTH), tok(RW_WIDTH),
                  tok(SG_WIDTH), tok(3 * D), modspec(2), modspec(3), modspec(4)]
                 + [const(a) for a in consts] + [pl.BlockSpec(memory_space=pl.ANY)] * len(prev),
        out_specs=[tok(D), tok(D), pl.BlockSpec((1, N_EXPERTS, tile), lambda b, i: (b, 0, blk0 + i))],
        out_shape=[jax.ShapeDtypeStruct((B, TT, D), F32),
                   jax.ShapeDtypeStruct((B, TT, D), BF16),
                   jax.ShapeDtypeStruct((B, N_EXPERTS, TT), F32)],
        input_output_aliases={n_in + j: j for j in range(len(prev))},
        compiler_params=pltpu.CompilerParams(
            dimension_semantics=("parallel", "parallel"), vmem_limit_bytes=VMEM_LIMIT_BYTES),
        name="merge_ln1_router",
    )(x_src, yda, y0, y1, bonus, g, ysg, gates, modall, modall, modall, *consts, *prev)


def _topk_kernel(aff_ref, tri_ref, rank_ref, *, cap):
    a = aff_ref[...].reshape(-1, aff_ref.shape[2])
    bits = pltpu.bitcast(a, jnp.int32)
    thr = jnp.zeros((a.shape[0], 1), jnp.int32)
    for bit in range(30, -1, -1):
        cand = thr | (1 << bit)
        cnt = jnp.sum(jnp.where(bits >= cand, 1.0, 0.0), axis=-1, keepdims=True)
        thr = jnp.where(cnt >= cap, cand, thr)
    gt = bits > thr
    eq = bits == thr
    need = cap - jnp.sum(jnp.where(gt, 1.0, 0.0), axis=-1, keepdims=True)
    tri = tri_ref[...]
    kb = tri.shape[0]

    def excl_cumsum(mask):
        m = jnp.where(mask, 1.0, 0.0)
        outs = []
        carry = jnp.zeros((m.shape[0], 1), F32)
        for j in range(m.shape[1] // kb):
            blk = m[:, j * kb:(j + 1) * kb]
            outs.append(_dot(blk.astype(BF16), tri) + carry)
            carry = carry + jnp.sum(blk, axis=-1, keepdims=True)
        return jnp.concatenate(outs, axis=1)

    sel = jnp.logical_or(gt, jnp.logical_and(eq, excl_cumsum(eq) < need))
    rank_ref[...] = jnp.where(sel, excl_cumsum(sel), -1.0).astype(jnp.int32).reshape(rank_ref.shape)


def _topk_ranks(aff, tri, t_off, t_len, cap):
    B, E, _ = aff.shape
    blk = t_off // t_len
    nb = math.gcd(B, RANK_BATCH)
    return pl.pallas_call(
        functools.partial(_topk_kernel, cap=cap),
        grid=(B // nb,),
        in_specs=[pl.BlockSpec((nb, E, t_len), lambda b: (b, 0, blk)),
                  pl.BlockSpec(tri.shape, lambda b: (0, 0))],
        out_specs=pl.BlockSpec((nb, E, t_len), lambda b: (b, 0, 0)),
        out_shape=jax.ShapeDtypeStruct((B, E, t_len), jnp.int32),
        compiler_params=pltpu.CompilerParams(
            dimension_semantics=("parallel",), vmem_limit_bytes=VMEM_LIMIT_BYTES),
        name="expert_choice_ranks",
    )(aff, tri)


def _moe_kernel(*refs, sets):
    n = len(sets)
    h_ref, aff_ref = refs[0], refs[1]
    rank_refs = refs[2:2 + n]
    wg_ref, wu_ref, wd_ref, f_ref = refs[2 + n:]
    e = pl.program_id(1)

    @pl.when(e == 0)
    def _():
        f_ref[...] = jnp.zeros_like(f_ref)

    onehots, gates, xs = [], [], []
    for (t0, tn, cap), rank_ref in zip(sets, rank_refs):
        rank = rank_ref[0, pl.ds(e, 1), :]
        aff = aff_ref[0, pl.ds(e, 1), t0:t0 + tn]
        slot = lax.broadcasted_iota(jnp.int32, (cap, tn), 0)
        hit = rank == slot
        onehot = jnp.where(hit, 1.0, 0.0).astype(BF16)
        onehots.append(onehot)
        gates.append(jnp.sum(jnp.where(hit, aff, 0.0), axis=-1, keepdims=True))
        xs.append(_dot(onehot, h_ref[0, t0:t0 + tn, :]).astype(BF16))
    xe = jnp.concatenate(xs, axis=0) if n > 1 else xs[0]
    gate = jnp.concatenate(gates, axis=0) if n > 1 else gates[0]
    hg = _dot(xe, wg_ref[0, 0])
    hid = (hg * _sigmoid(hg)) * _dot(xe, wu_ref[0, 0])
    ye = (_dot(hid.astype(BF16), wd_ref[0, 0]) * gate).astype(BF16)
    r0 = 0
    for (t0, tn, cap), onehot in zip(sets, onehots):
        f_ref[0, t0:t0 + tn, :] += _dot_tn(onehot, ye[r0:r0 + cap])
        r0 += cap


def _expert_ffn(h2, aff, ranks, wg, wu, wd, layer, sets):
    B, TT, D = h2.shape
    E = aff.shape[1]
    F = wg.shape[3]
    return pl.pallas_call(
        functools.partial(_moe_kernel, sets=sets),
        grid=(B, E),
        in_specs=[pl.BlockSpec((1, TT, D), lambda b, e: (b, 0, 0), pipeline_mode=pl.Buffered(1)),
                  pl.BlockSpec((1, E, TT), lambda b, e: (b, 0, 0))]
                 + [pl.BlockSpec((1, E, r.shape[2]), lambda b, e: (b, 0, 0)) for r in ranks]
                 + [pl.BlockSpec((1, 1, D, F), lambda b, e: (layer, e, 0, 0)),
                    pl.BlockSpec((1, 1, D, F), lambda b, e: (layer, e, 0, 0)),
                    pl.BlockSpec((1, 1, F, D), lambda b, e: (layer, e, 0, 0))],
        out_specs=pl.BlockSpec((1, TT, D), lambda b, e: (b, 0, 0)),
        out_shape=jax.ShapeDtypeStruct((B, TT, D), F32),
        compiler_params=pltpu.CompilerParams(
            dimension_semantics=("parallel", "arbitrary"), vmem_limit_bytes=VMEM_LIMIT_BYTES),
        name="expert_ffn",
    )(h2, aff, *ranks, wg, wu, wd)


def _ln2_kernel(x_ref, f_ref, g2_ref, lng_ref, lnb_ref, o_ref, *, alpha):
    z = alpha * x_ref[0] + g2_ref[0, 0] * f_ref[0]
    zm = jnp.mean(z, axis=-1, keepdims=True)
    dz = z - zm
    zv = jnp.mean(dz * dz, axis=-1, keepdims=True)
    o_ref[0] = dz * lax.rsqrt(zv + LN_EPS) * lng_ref[...] + lnb_ref[...]


def _final_norm(xmid, f, modall, lng, lnb, ntl, alpha, rows):
    B, _, D = xmid.shape
    tm = TOKEN_TILE
    tok = pl.BlockSpec((1, tm, D), lambda b, i: (b, i, 0))
    const = lambda a: pl.BlockSpec(a.shape, lambda b, i: (0,) * a.ndim)
    return pl.pallas_call(
        functools.partial(_ln2_kernel, alpha=alpha),
        grid=(B, rows // tm),
        in_specs=[tok, tok, pl.BlockSpec((1, 1, 1, D), lambda b, i: (b, i // ntl, 0, 5)),
                  const(lng), const(lnb)],
        out_specs=tok,
        out_shape=jax.ShapeDtypeStruct((B, rows, D), F32),
        compiler_params=pltpu.CompilerParams(dimension_semantics=("parallel", "parallel")),
        name="ln2",
    )(xmid, f, modall, lng, lnb)


def _rope_tables(t_lat, t_ctx):
    rows = t_lat // GRID_W
    row = jnp.repeat(jnp.arange(rows, dtype=F32), GRID_W)
    col = jnp.tile(jnp.arange(GRID_W, dtype=F32), rows)
    half = DA_QK_DIM // 2
    inv_freq = ROPE_BASE ** (-jnp.arange(0, half, 2, dtype=F32) / half)
    ar = row[:, None] * inv_freq
    ac = col[:, None] * inv_freq
    ang = jnp.concatenate([ar, ar, ac, ac], axis=-1)
    sign = jnp.where((jnp.arange(DA_QK_DIM) % 32) < 16, -1.0, 1.0).astype(F32)
    reps = DA_QK_COLS // DA_QK_DIM
    cos = jnp.tile(jnp.cos(ang), (1, reps))
    sin = jnp.tile(jnp.sin(ang) * sign, (1, reps))
    cos = jnp.concatenate([cos, jnp.ones((t_ctx, DA_QK_COLS), F32)], axis=0)
    sin = jnp.concatenate([sin, jnp.zeros((t_ctx, DA_QK_COLS), F32)], axis=0)
    return cos, sin


def kernel(x, c, ctx, c_ctx, w_mod, b_mod, w_in, da_lambda, da_norm_g, rw_shift_mu, rw_w0, rw_w2, rw_a0, rw_a2, rw_k_k, rw_k_a, rw_r_k, rw_ln_g, rw_ln_b, rw_g2, sg_norm_g, sg_norm_b, sg_w, sg_b, w_branch, w_out, ln1_g, ln1_b, w_router, w_e_gate, w_e_up, w_e_down, ln2_g, ln2_b):
    B, T, D = x.shape
    Tc = ctx.shape[1]
    depth = w_mod.shape[0]
    tm = TOKEN_TILE
    assert D == D_MODEL and T % tm == 0 and Tc == tm and T % MERGE_TILE == 0
    ntl = T // tm
    alpha = (2 * depth) ** 0.25
    cap_lat = EC_CAPACITY * T // N_EXPERTS
    cap_ctx = EC_CAPACITY * Tc // N_EXPERTS

    cos, sin = _rope_tables(T, Tc)
    lane = jnp.arange(RW_WIDTH)
    ones_bd = (lane[:, None] // RW_HEAD == lane[None, :] // RW_HEAD).astype(BF16)
    kb = math.gcd(Tc, RANK_BLOCK)
    tri = (jnp.arange(kb)[:, None] < jnp.arange(kb)[None, :]).astype(BF16)
    rows = ((B + 1 + 7) // 8) * 8
    cc = jnp.concatenate([c, c_ctx[None, :], jnp.zeros((rows - B - 1, D), F32)], axis=0)
    row2 = lambda a: a.reshape(1, -1)

    w_in_bf = w_in.astype(BF16)
    wg, wu, wd = w_e_gate.astype(BF16), w_e_up.astype(BF16), w_e_down.astype(BF16)
    x_lat, x_ctx, ctx_block0 = x, ctx, 0
    for l in range(depth):
        last = l == depth - 1
        lam_init = 0.8 - 0.6 * math.exp(-0.3 * l)
        mod = _modulation(cc, w_mod[l], b_mod[l])
        modall = jnp.stack([mod[:B], jnp.broadcast_to(mod[B], (B, 6 * D))], axis=1)
        modall = modall.reshape(B, 2, 1, 6 * D)

        sgbias = jnp.repeat(sg_b[l].T, SG_WIDTH // SG_GROUPS, axis=1)
        cat2 = lambda a: jnp.transpose(a, (1, 0, 2)).reshape(a.shape[1], 2 * RW_WIDTH)
        rw_params = [rw_shift_mu[l], row2(rw_w0[l]), cat2(rw_w2[l]).astype(BF16), row2(rw_a0[l]),
                     cat2(rw_a2[l]).astype(BF16), row2(rw_k_k[l]), row2(rw_k_a[l]), row2(rw_r_k[l]),
                     rw_g2[l].astype(BF16), ones_bd]
        (q, k, v, ysg, gates, lw0, lw1, kd0, kd1, b0, b1, kk, vv, rr, bonus, gg) = _input_projection(
            x_lat, x_ctx, ctx_block0, T + Tc, modall, w_in_bf, l, cos, sin, row2(sg_norm_g[l]),
            row2(sg_norm_b[l]), sg_w[l].astype(BF16), sgbias, rw_params, ntl)

        nt_out = ntl if last else (T + Tc) // tm
        yda = _diff_attention(q, k, v, da_lambda[l], da_norm_g[l], ntl, lam_init, nt_out)

        y0, y1 = _rwkv_scan(lw0, lw1, kd0, kd1, b0, b1, kk, vv, rr, T)

        merge_args = (yda, y0, y1, bonus, gg, ysg, gates, modall, w_branch[l].astype(BF16),
                      w_out[l].astype(BF16), row2(ln1_g[l]), row2(ln1_b[l]), row2(rw_ln_g[l]),
                      row2(rw_ln_b[l]), ones_bd, w_router[l].T, alpha)
        outs = _merge(x_lat, 0, MERGE_TILE, 0, T // MERGE_TILE, 0, None, *merge_args)
        if not last:
            outs = _merge(x_ctx, ctx_block0, tm, T, Tc // tm, 1, outs, *merge_args)
        xmid, h2, aff = outs

        sets = ((0, T, cap_lat),) if last else ((0, T, cap_lat), (T, Tc, cap_ctx))
        ranks = [_topk_ranks(aff, tri, 0, T, cap_lat)]
        if not last:
            ranks.append(_topk_ranks(aff, tri, T, Tc, cap_ctx))
        f = _expert_ffn(h2, aff, ranks, wg, wu, wd, l, sets)
        x_lat = _final_norm(xmid, f, modall, row2(ln2_g[l]), row2(ln2_b[l]), ntl, alpha,
                            T if last else T + Tc)
        x_ctx, ctx_block0 = x_lat, ntl
    return x_lat
```

```python
import functools
import math

import jax
import jax.numpy as jnp
from jax import lax
from jax.experimental import pallas as pl
from jax.experimental.pallas import tpu as pltpu

F32 = jnp.float32
BF16 = jnp.bfloat16
HIGHEST = lax.Precision.HIGHEST

D_MODEL = 1024
GRID_W = 64
DA_HEADS = 4
DA_QK_DIM = 64
DA_V_DIM = 128
DA_WIDTH = 512
DA_QK_COLS = 512
ROPE_BASE = 10000.0
DA_EPS = 1e-5
RW_HEAD = 64
RW_HEADS = 8
RW_WIDTH = 512
RW_COLS = 1792
RW_GN_EPS = 64e-5
SG_CHUNK = 128
SG_GROUPS = 4
SG_WIDTH = 512
DA_K0 = 512
DA_V0 = 1024
RW_0 = 1536
SG_0 = RW_0 + RW_COLS
GATE_0 = SG_0 + 2 * SG_WIDTH
N_EXPERTS = 16
EC_CAPACITY = 2
LN_EPS = 1e-5
LOG2_E = math.log2(math.e)

TOKEN_TILE = 256
SCAN_CHUNK = 64
ATTN_KEY_TILE = 256
ATTN_HEADS_PER_STEP = 4
SCAN_HEADS_PER_PASS = 4
SCAN_BATCH = 4
MERGE_TILE = 512
MERGE_ROWS = 128
RANK_BLOCK = 256
RANK_BATCH = 4
VMEM_LIMIT_BYTES = 58 * 1024 * 1024


def _dot(a, b):
    return jnp.dot(a, b, preferred_element_type=F32)


def _dot_hi(a, b):
    return jnp.dot(a, b, preferred_element_type=F32, precision=HIGHEST)


def _dot_nt(a, b):
    return lax.dot_general(a, b, (((1,), (1,)), ((), ())), preferred_element_type=F32)


def _dot_tn(a, b):
    return lax.dot_general(a, b, (((0,), (0,)), ((), ())), preferred_element_type=F32)


def _sigmoid(z):
    return 1.0 / (1.0 + jnp.exp(-z))


def _seg_sum(z, ones_bd):
    rows = z.shape[0]
    w = ones_bd.shape[0]
    hi = z.astype(BF16)
    lo = (z - hi.astype(F32)).astype(BF16)
    cols = []
    for c in range(0, z.shape[1], w):
        st = _dot(jnp.concatenate([hi[:, c:c + w], lo[:, c:c + w]], axis=0), ones_bd)
        cols.append(st[:rows] + st[rows:])
    return jnp.concatenate(cols, axis=1)


def _mod_kernel(c_ref, w_ref, b_ref, o_ref):
    cc = c_ref[...]
    o_ref[...] = _dot_hi(cc * _sigmoid(cc), w_ref[...]) + b_ref[...]


def _modulation(cc, w_mod, b_mod):
    rows, d = cc.shape
    n = w_mod.shape[1]
    tn = 1024
    return pl.pallas_call(
        _mod_kernel,
        grid=(n // tn,),
        in_specs=[pl.BlockSpec((rows, d), lambda j: (0, 0)),
                  pl.BlockSpec((d, tn), lambda j: (0, j)),
                  pl.BlockSpec((1, tn), lambda j: (0, j))],
        out_specs=pl.BlockSpec((rows, tn), lambda j: (0, j)),
        out_shape=jax.ShapeDtypeStruct((rows, n), F32),
        name="adaln_mod",
    )(cc, w_mod, b_mod.reshape(1, n))


def _inproj_kernel(xl_ref, xc_ref, lp_ref, ln_ref, sh_ref, sc_ref, w_ref, cos_ref, sin_ref, sgg_ref, sgb_ref,
                   sgw_ref, sgbias_ref, *rest, ntl, nt):
    rw_params, (q_ref, k_ref, v_ref, sg_ref, gate_ref), rw_outs = rest[:10], rest[10:15], rest[15:]
    tm = xl_ref.shape[1]
    i = pl.program_id(1)
    x = jnp.where(i < ntl, xl_ref[0], xc_ref[0])
    h = (x * (1.0 + sc_ref[0, 0]) + sh_ref[0, 0]).astype(BF16)

    def proj(c0, c1):
        return _dot(h, w_ref[0, :, c0:c1])

    cos = cos_ref[...]
    sin = sin_ref[...]
    lane = lax.broadcasted_iota(jnp.int32, (tm, DA_QK_COLS), 1)
    first = (lane % 32) < 16

    def rope(z):
        zr = jnp.where(first, pltpu.roll(z, DA_QK_COLS - 16, 1), pltpu.roll(z, 16, 1))
        return z * cos + zr * sin

    halo = jnp.concatenate([lp_ref[0], ln_ref[0]], axis=0)
    h_halo = (halo * (1.0 + sc_ref[0, 0]) + sh_ref[0, 0]).astype(BF16)
    p_ext = _dot(jnp.concatenate([h, h_halo], axis=0), w_ref[0, :, RW_0:SG_0])
    prev_ok = jnp.logical_and(i != 0, i != ntl)
    next_ok = jnp.logical_and(i != ntl - 1, i != nt - 1)
    _rwkv_feature_math(p_ext[:tm], jnp.where(prev_ok, p_ext[tm + 7:tm + 8], 0.0),
                       jnp.where(next_ok, p_ext[tm + 8:tm + 9], 0.0), rw_params, rw_outs)
    q_ref[0] = (rope(proj(0, DA_K0)) * (DA_QK_DIM ** -0.5 * LOG2_E)).astype(BF16)
    k_ref[0] = rope(proj(DA_K0, DA_V0)).astype(BF16)
    v_ref[0] = proj(DA_V0, RW_0).astype(BF16)
    for j in range(3):
        gate_ref[0, :, j * D_MODEL:(j + 1) * D_MODEL] = _sigmoid(
            proj(GATE_0 + j * D_MODEL, GATE_0 + (j + 1) * D_MODEL)).astype(BF16)

    ps = proj(SG_0, GATE_0)
    gl = ps * (0.5 * (1.0 + jnp.tanh(math.sqrt(2.0 / math.pi) * (ps + 0.044715 * (ps * ps * ps)))))
    u = gl[:, :SG_WIDTH]
    vv = gl[:, SG_WIDTH:]
    mu = jnp.mean(vv, axis=-1, keepdims=True)
    dv = vv - mu
    var = jnp.mean(dv * dv, axis=-1, keepdims=True)
    vn = (dv * lax.rsqrt(var + LN_EPS) * sgg_ref[...] + sgb_ref[...]).astype(BF16)
    gd = SG_WIDTH // SG_GROUPS
    for n in range(tm // SG_CHUNK):
        r0 = n * SG_CHUNK
        for g in range(SG_GROUPS):
            c0 = g * gd
            vm = _dot(sgw_ref[g], vn[r0:r0 + SG_CHUNK, c0:c0 + gd]) + sgbias_ref[:, c0:c0 + gd]
            sg_ref[0, r0:r0 + SG_CHUNK, c0:c0 + gd] = (u[r0:r0 + SG_CHUNK, c0:c0 + gd] * vm).astype(BF16)


def _stream_specs(d, ntl, ctx_block0):
    tm = TOKEN_TILE
    lat = pl.BlockSpec((1, tm, d), lambda b, i: (b, jnp.minimum(i, ntl - 1), 0))
    ctx = pl.BlockSpec((1, tm, d), lambda b, i: (b, ctx_block0 + jnp.maximum(i - ntl, 0), 0))
    return [lat, ctx]


def _input_projection(x_lat, x_ctx, ctx_block0, TT, modall, w_in, layer, cos, sin, sgg, sgb, sgw, sgbias,
                      rw_params, ntl):
    B, lat_rows, D = x_lat.shape
    tm = TOKEN_TILE
    nt = TT // tm
    r8 = tm // 8
    last8 = lat_rows // 8 - 1
    W = RW_WIDTH
    tok = lambda w: pl.BlockSpec((1, tm, w), lambda b, i: (b, i, 0))
    modspec = lambda j: pl.BlockSpec((1, 1, 1, D), lambda b, i: (b, i // ntl, 0, j))
    const = lambda a: pl.BlockSpec(a.shape, lambda b, i: (0,) * a.ndim)
    halo_prev = pl.BlockSpec((1, 8, D), lambda b, i: (b, jnp.maximum(jnp.minimum(i, ntl - 1) * r8 - 1, 0), 0))
    halo_next = pl.BlockSpec((1, 8, D), lambda b, i: (b, jnp.minimum((jnp.minimum(i, ntl - 1) + 1) * r8, last8), 0))
    return pl.pallas_call(
        functools.partial(_inproj_kernel, ntl=ntl, nt=nt),
        grid=(B, nt),
        in_specs=_stream_specs(D, ntl, ctx_block0) + [halo_prev, halo_next, modspec(0), modspec(1),
                  pl.BlockSpec((1,) + w_in.shape[1:], lambda b, i: (layer, 0, 0),
                               pipeline_mode=pl.Buffered(1)),
                  pl.BlockSpec((tm, DA_QK_COLS), lambda b, i: (i, 0)),
                  pl.BlockSpec((tm, DA_QK_COLS), lambda b, i: (i, 0)),
                  const(sgg), const(sgb), const(sgw), const(sgbias)]
                 + [const(a) for a in rw_params],
        out_specs=[tok(DA_QK_COLS), tok(DA_QK_COLS), tok(DA_WIDTH), tok(SG_WIDTH), tok(3 * D)] + [tok(W)] * 11,
        out_shape=[jax.ShapeDtypeStruct((B, TT, DA_QK_COLS), BF16),
                   jax.ShapeDtypeStruct((B, TT, DA_QK_COLS), BF16),
                   jax.ShapeDtypeStruct((B, TT, DA_WIDTH), BF16),
                   jax.ShapeDtypeStruct((B, TT, SG_WIDTH), BF16),
                   jax.ShapeDtypeStruct((B, TT, 3 * D), BF16)]
                  + [jax.ShapeDtypeStruct((B, TT, W), F32)] * 2 + [jax.ShapeDtypeStruct((B, TT, W), BF16)] * 9,
        compiler_params=pltpu.CompilerParams(
            dimension_semantics=("parallel", "parallel"), vmem_limit_bytes=VMEM_LIMIT_BYTES),
        name="in_proj",
    )(x_lat, x_ctx, x_lat, x_lat, modall, modall, w_in, cos, sin, sgg, sgb, sgw, sgbias, *rw_params)


def _attn_kernel(q_ref, k_ref, v_ref, lam_ref, g_ref, o_ref, *, ntl, t_lat, lam_init):
    i = pl.program_id(2)
    lp = lam_ref[...]
    lam = (jnp.exp(jnp.sum(lp[0:1] * lp[1:2], axis=-1, keepdims=True))
           - jnp.exp(jnp.sum(lp[2:3] * lp[3:4], axis=-1, keepdims=True)) + lam_init)
    dv = DA_V_DIM
    heads = [slice(h * dv, (h + 1) * dv) for h in range(q_ref.shape[2] // dv)]
    lane = lax.broadcasted_iota(jnp.int32, (q_ref.shape[1], dv), 1)
    qs = []
    for hs in heads:
        q = q_ref[0, :, hs]
        zero = jnp.zeros_like(q)
        qs.append((jnp.where(lane < DA_QK_DIM, q, zero), jnp.where(lane >= DA_QK_DIM, q, zero)))

    def attend(k0, nk):
        kt = ATTN_KEY_TILE
        tiles = [slice(k0 + t * kt, k0 + (t + 1) * kt) for t in range(nk // kt)]

        def row_max(ss):
            m = ss[0]
            for s in ss[1:]:
                m = jnp.maximum(m, s)
            return jnp.max(m, axis=-1, keepdims=True)

        def pv(es, hs):
            acc = None
            for e, sl in zip(es, tiles):
                v = v_ref[0, sl, hs]
                d = _dot(e, jnp.concatenate([v, jnp.ones_like(v)], axis=1))
                acc = d if acc is None else acc + d
            return acc[:, :dv] / acc[:, dv:]

        ks = [[k_ref[0, sl, hs] for sl in tiles] for hs in heads]
        s0 = [[_dot_nt(q0, kk) for kk in kh] for (q0, _), kh in zip(qs, ks)]
        m0 = [row_max(s) for s in s0]
        s1, e0 = [], []
        for (_, q1), kh, sh, mh in zip(qs, ks, s0, m0):
            s1.append([])
            e0.append([])
            for kk, s in zip(kh, sh):
                s1[-1].append(_dot_nt(q1, kk))
                e0[-1].append(jnp.exp2(s - mh).astype(BF16))
        m1 = [row_max(s) for s in s1]
        o0 = [pv(e, hs) for e, hs in zip(e0, heads)]
        e1 = [[jnp.exp2(s - mh).astype(BF16) for s in sh] for sh, mh in zip(s1, m1)]
        for hs, oa, e in zip(heads, o0, e1):
            o = oa - lam * pv(e, hs)
            o = o * lax.rsqrt(jnp.mean(o * o, axis=-1, keepdims=True) + DA_EPS) * g_ref[...]
            o_ref[0, :, hs] = (o * (1.0 - lam_init)).astype(BF16)

    @pl.when(i < ntl)
    def _():
        attend(0, k_ref.shape[1])

    @pl.when(i >= ntl)
    def _():
        attend(t_lat, k_ref.shape[1] - t_lat)


def _diff_attention(q, k, v, lam_p, norm_g, ntl, lam_init, nt):
    B, TT, _ = q.shape
    tm = TOKEN_TILE
    w = ATTN_HEADS_PER_STEP * DA_V_DIM
    kv = pl.BlockSpec((1, TT, w), lambda b, h, i: (b, 0, h))
    qo = pl.BlockSpec((1, tm, w), lambda b, h, i: (b, i, h))
    return pl.pallas_call(
        functools.partial(_attn_kernel, ntl=ntl, t_lat=ntl * tm, lam_init=lam_init),
        grid=(B, DA_HEADS // ATTN_HEADS_PER_STEP, nt),
        in_specs=[qo, kv, kv,
                  pl.BlockSpec(lam_p.shape, lambda b, h, i: (0, 0)),
                  pl.BlockSpec((1, DA_V_DIM), lambda b, h, i: (0, 0))],
        out_specs=qo,
        out_shape=jax.ShapeDtypeStruct((B, TT, DA_WIDTH), BF16),
        compiler_params=pltpu.CompilerParams(
            dimension_semantics=("parallel", "parallel", "parallel"),
            vmem_limit_bytes=VMEM_LIMIT_BYTES),
        name="diff_attn",
    )(q, k, v, lam_p, norm_g.reshape(1, DA_V_DIM))


def _rwkv_feature_math(p, prev_row, next_row, params, outs):
    mu_ref, w0_ref, w2_ref, a0_ref, a2_ref, kk_ref, ka_ref, rk_ref, g2_ref, ones_ref = params
    lw0_ref, lw1_ref, kd0_ref, kd1_ref, b0_ref, b1_ref, kko_ref, v_ref, r_ref, bonus_ref, g_ref = outs
    tm = p.shape[0]
    row = lax.broadcasted_iota(jnp.int32, p.shape, 0)
    prev = jnp.where(row == 0, prev_row, pltpu.roll(p, 1, 0))
    nxt = jnp.where(row == tm - 1, next_row, pltpu.roll(p, tm - 1, 0))
    ps = p + mu_ref[0:1, :] * (prev - p) + mu_ref[1:2, :] * (nxt - p)

    W = RW_WIDTH
    r = ps[:, 0:W]
    k = ps[:, W:2 * W]
    v = ps[:, 2 * W:3 * W]
    xw = ps[:, 3 * W:3 * W + 64]
    xa = ps[:, 3 * W + 64:3 * W + 128]
    xg = ps[:, 3 * W + 128:3 * W + 256]
    ones_bd = ones_ref[...]

    w_pre = _dot(jnp.tanh(xw).astype(BF16), w2_ref[...]) + w0_ref[...]
    logw = -math.exp(-0.5) * _sigmoid(w_pre)
    a = _sigmoid(_dot(xa.astype(BF16), a2_ref[...]) + a0_ref[...])

    kx = k * kk_ref[...]
    nrm = jnp.sqrt(_seg_sum(kx * kx, ones_bd))
    kk = kx / jnp.maximum(nrm, 1e-12)
    ka = ka_ref[...]
    kd0 = k * (1.0 + (a[:, :W] - 1.0) * ka)
    kd1 = k * (1.0 + (a[:, W:] - 1.0) * ka)
    g = _dot(_sigmoid(xg).astype(BF16), g2_ref[...])
    k_b = 0.5 * (kd0 + kd1)
    bonus = _seg_sum(r * k_b * rk_ref[...], ones_bd) * v

    lw0_ref[0] = logw[:, :W]
    lw1_ref[0] = logw[:, W:]
    kd0_ref[0] = kd0.astype(BF16)
    kd1_ref[0] = kd1.astype(BF16)
    b0_ref[0] = (kk * a[:, :W]).astype(BF16)
    b1_ref[0] = (kk * a[:, W:]).astype(BF16)
    kko_ref[0] = kk.astype(BF16)
    v_ref[0] = v.astype(BF16)
    r_ref[0] = r.astype(BF16)
    bonus_ref[0] = bonus.astype(BF16)
    g_ref[0] = g.astype(BF16)


def _split_bf16(x, pieces):
    out = []
    for _ in range(pieces - 1):
        hi = x.astype(BF16)
        out.append(hi)
        x = x - hi.astype(F32)
    out.append(x.astype(BF16))
    return out


def _block_diag(x, groups):
    xb = x.astype(BF16)
    rows, lanes = xb.shape
    t = jnp.concatenate([xb] * groups, axis=0)
    ri = lax.broadcasted_iota(jnp.int32, t.shape, 0) // rows
    li = lax.broadcasted_iota(jnp.int32, t.shape, 1) // (lanes // groups)
    return jnp.where(ri == li, t, jnp.zeros_like(t))


def _scan_kernel(lw0, kd0, b0, kkf, vf, rf, lw1, kd1, b1, kkr, vr, rr, y0_ref, y1_ref, s_ref):
    step = pl.program_id(1)

    @pl.when(step == 0)
    def _():
        s_ref[...] = jnp.zeros_like(s_ref)

    C = lw0.shape[1]
    G = SCAN_HEADS_PER_PASS
    L = G * RW_HEAD
    assert C == RW_HEAD
    ti = lax.broadcasted_iota(jnp.int32, (C, L), 0)
    si = lax.broadcasted_iota(jnp.int32, (C, L), 1) % C
    tc = lax.broadcasted_iota(jnp.int32, (C, C), 0)
    sc = lax.broadcasted_iota(jnp.int32, (C, C), 1)
    same_head = (lax.broadcasted_iota(jnp.int32, (L, L), 0) // RW_HEAD
                 == lax.broadcasted_iota(jnp.int32, (L, L), 1) // RW_HEAD)

    groups = []
    for z, (lw, kd, b, kk, v, r) in enumerate(((lw0, kd0, b0, kkf, vf, rf), (lw1, kd1, b1, kkr, vr, rr))):
        rev = z == 1
        incl = (si >= ti) if rev else (si <= ti)
        strict = (si > ti) if rev else (si < ti)
        tri = jnp.where((sc >= tc) if rev else (sc <= tc), 1.0, 0.0).astype(BF16)
        for n in range(lw.shape[0]):
            logw = lw[n]
            cum = sum(_dot(tri, piece) for piece in _split_bf16(logw, 3))
            tot = jnp.sum(logw, axis=0, keepdims=True)
            p_in = jnp.exp(-cum)
            p_end = jnp.exp(tot - cum)
            kkz, rz, bz, kdz = (t[n].astype(F32) for t in (kk, r, b, kd))
            lhs = jnp.concatenate([-kkz * jnp.exp(cum - logw), rz * jnp.exp(cum)], axis=0)
            rhs_b = bz * p_in
            rhs_k = kdz * p_in
            upd = jnp.concatenate([bz * p_end, kdz * p_end], axis=0)
            dec = jnp.exp(tot)
            vz = v[n]
            for j in range(RW_HEADS // G):
                c = slice(j * L, (j + 1) * L)
                groups.append(dict(z=z, smp=n, j=j, lhs=lhs[:, c].astype(BF16), rhs_b=rhs_b[:, c],
                                   rhs_k=rhs_k[:, c], upd=upd[:, c].astype(BF16), dec=dec[:, c],
                                   v=vz[:, c], strict=strict, incl=incl))

    for g in groups:
        aa_b = _dot_nt(g["lhs"], _block_diag(g["rhs_b"], G))
        aa_k = _dot_nt(g["lhs"], _block_diag(g["rhs_k"], G))
        g["a_k"] = jnp.concatenate([jnp.where(g["strict"], aa_k[:C], 0.0),
                                    jnp.where(g["incl"], aa_k[C:], 0.0)], axis=0).astype(BF16)
        g["a_rb"] = jnp.where(g["incl"], aa_b[C:], 0.0).astype(BF16)
        g["n"] = jnp.where(g["strict"], aa_b[:C], 0.0)
        g["apow"] = _dot(g["n"].astype(BF16), _block_diag(g["n"], G))
    levels = int(math.log2(C)) - 1
    for lv in range(levels):
        for g in groups:
            pbd = _block_diag(g["apow"], G)
            if lv < levels - 1:
                st = _dot(jnp.concatenate([g["n"], g["apow"]], axis=0).astype(BF16), pbd)
                g["n"] = g["n"] + g["apow"] + st[:C]
                g["apow"] = st[C:]
            else:
                g["n"] = g["n"] + g["apow"] + _dot(g["n"].astype(BF16), pbd)
    for g in groups:
        g["s0"] = s_ref[g["smp"], g["z"], g["j"]]
        g["gh"] = _dot_nt(g["lhs"], g["s0"].astype(BF16))
    for g in groups:
        st = _dot(g["a_k"], _block_diag(g["v"], G))
        g["w"] = g["gh"][:C] + st[:C]
        g["yk"] = st[C:]
    for g in groups:
        g["u"] = g["w"] + _dot(g["n"].astype(BF16), _block_diag(g["w"], G))
    for g in groups:
        g["y"] = g["gh"][C:] + g["yk"] + _dot(g["a_rb"], _block_diag(g["u"], G))
    for g in groups:
        uv = jnp.concatenate([g["u"].astype(BF16), g["v"]], axis=0)
        s_ref[g["smp"], g["z"], g["j"]] = g["s0"] * g["dec"] + jnp.where(same_head, _dot_tn(uv, g["upd"]), 0.0)
    for z, y_ref in enumerate((y0_ref, y1_ref)):
        for n in range(y_ref.shape[0]):
            y_ref[n] = jnp.concatenate([g["y"] for g in groups if g["z"] == z and g["smp"] == n],
                                       axis=-1).astype(y_ref.dtype)


def _rwkv_scan(lw0, lw1, kd0, kd1, b0, b1, kk, v, r, t_lat):
    B, TT, W = kk.shape
    C = SCAN_CHUNK
    ncl = t_lat // C
    nch = TT // C
    ncc = nch - ncl

    def fwd(s):
        return jnp.where(s < ncc, ncl + s, s - ncc)

    nb = math.gcd(B, SCAN_BATCH)
    fs =pl.BlockSpec((nb, C, W), lambda b, s: (b, fwd(s), 0))
    rs = pl.BlockSpec((nb, C, W), lambda b, s: (b, nch - 1 - s, 0))
    return pl.pallas_call(
        _scan_kernel,
        grid=(B // nb, nch),
        in_specs=[fs] * 6 + [rs] * 6,
        out_specs=[fs, rs],
        out_shape=[jax.ShapeDtypeStruct((B, TT, W), BF16)] * 2,
        scratch_shapes=[pltpu.VMEM((nb, 2, RW_HEADS // SCAN_HEADS_PER_PASS, SCAN_HEADS_PER_PASS * RW_HEAD,
                                    SCAN_HEADS_PER_PASS * RW_HEAD), F32)],
        compiler_params=pltpu.CompilerParams(
            dimension_semantics=("parallel", "arbitrary"), vmem_limit_bytes=VMEM_LIMIT_BYTES),
        name="rwkv_scan",
    )(lw0, kd0, b0, kk, v, r, lw1, kd1, b1, kk, v, r)


def _merge_kernel(x_ref, yda_ref, y0_ref, y1_ref, bonus_ref, g_ref, ysg_ref, gate_ref,
                  g1_ref, sh2_ref, sc2_ref, wb_ref, wo_ref, ln1g_ref, ln1b_ref, rlng_ref, rlnb_ref,
                  ones_ref, wr_ref, *rest, alpha):
    xmid_ref, h2_ref, aff_ref = rest[-3:]
    D = D_MODEL
    tm = x_ref.shape[1]
    subs = [slice(r, r + MERGE_ROWS) for r in range(0, tm, MERGE_ROWS)]
    ones_bd = ones_ref[...]
    wr_hi, wr_lo = _split_bf16(wr_ref[...], 2)
    wr_hl = jnp.concatenate([wr_hi, wr_lo], axis=0)

    yrw = []
    for sl in subs:
        y = y0_ref[0, sl, :].astype(F32) + y1_ref[0, sl, :].astype(F32)
        mu = _seg_sum(y, ones_bd) * (1.0 / RW_HEAD)
        dy = y - mu
        var = _seg_sum(dy * dy, ones_bd) * (1.0 / RW_HEAD)
        gn = dy * lax.rsqrt(var + RW_GN_EPS) * rlng_ref[...] + rlnb_ref[...]
        yrw.append(((gn + bonus_ref[0, sl, :].astype(F32)) * g_ref[0, sl, :].astype(F32)).astype(BF16))
    ms = []
    for sl, yr in zip(subs, yrw):
        m = gate_ref[0, sl, 0:D].astype(F32) * _dot(yda_ref[0, sl, :], wb_ref[0:DA_WIDTH, :])
        m = m + gate_ref[0, sl, D:2 * D].astype(F32) * _dot(yr, wb_ref[DA_WIDTH:DA_WIDTH + RW_WIDTH, :])
        m = m + gate_ref[0, sl, 2 * D:3 * D].astype(F32) * _dot(ysg_ref[0, sl, :], wb_ref[DA_WIDTH + RW_WIDTH:, :])
        ms.append(m.astype(BF16))
    mixes = [_dot(m, wo_ref[...]) for m in ms]
    h2s = []
    for sl, mix in zip(subs, mixes):
        z = alpha * x_ref[0, sl, :] + g1_ref[0, 0] * mix
        zm = jnp.mean(z, axis=-1, keepdims=True)
        dz = z - zm
        zv = jnp.mean(dz * dz, axis=-1, keepdims=True)
        xmid = dz * lax.rsqrt(zv + LN_EPS) * ln1g_ref[...] + ln1b_ref[...]
        xmid_ref[0, sl, :] = xmid
        h2 = xmid * (1.0 + sc2_ref[0, 0]) + sh2_ref[0, 0]
        h2_ref[0, sl, :] = h2.astype(BF16)
        h2s.append(h2)
    ne = wr_hi.shape[0]
    for sl, h2 in zip(subs, h2s):
        h_hi, h_lo = _split_bf16(h2, 2)
        part = _dot_nt(wr_hl, h_hi)
        logits = part[:ne] + part[ne:] + _dot_nt(wr_hi, h_lo)
        e = jnp.exp(logits - jnp.max(logits, axis=0, keepdims=True))
        aff_ref[0, :, sl] = e / jnp.sum(e, axis=0, keepdims=True)


def _merge(x_src, x_block0, tile, row0, n_tiles, mod_row, prev_outs, yda, y0, y1, bonus, g, ysg, gates, modall,
           w_branch, w_out, ln1g, ln1b, rlng, rlnb, ones_bd, w_router_t, alpha):
    B, TT, _ = yda.shape
    D = x_src.shape[2]
    blk0 = row0 // tile
    tok = lambda w: pl.BlockSpec((1, tile, w), lambda b, i: (b, blk0 + i, 0))
    modspec = lambda j: pl.BlockSpec((1, 1, 1, D), lambda b, i: (b, mod_row, 0, j))
    const = lambda a: pl.BlockSpec(a.shape, lambda b, i: (0,) * a.ndim)
    consts = [w_branch, w_out, ln1g, ln1b, rlng, rlnb, ones_bd, w_router_t]
    n_in = 11 + len(consts)
    prev = list(prev_outs) if prev_outs is not None else []
    return pl.pallas_call(
        functools.partial(_merge_kernel, alpha=alpha),
        grid=(B, n_tiles),
        in_specs=[pl.BlockSpec((1, tile, D), lambda b, i: (b, x_block0 + i, 0)),
                  tok(DA_WIDTH), tok(RW_WIDTH), tok(RW_WIDTH), tok(RW_WIDTH), tok(RW_WIDTH),
                  tok(SG_WIDTH), tok(3 * D), modspec(2), modspec(3), modspec(4)]
                 + [const(a) for a in consts] + [pl.BlockSpec(memory_space=pl.ANY)] * len(prev),
        out_specs=[tok(D), tok(D), pl.BlockSpec((1, N_EXPERTS, tile), lambda b, i: (b, 0, blk0 + i))],
        out_shape=[jax.ShapeDtypeStruct((B, TT, D), F32),
                   jax.ShapeDtypeStruct((B, TT, D), BF16),
                   jax.ShapeDtypeStruct((B, N_EXPERTS, TT), F32)],
        input_output_aliases={n_in + j: j for j in range(len(prev))},
        compiler_params=pltpu.CompilerParams(
            dimension_semantics=("parallel", "parallel"), vmem_limit_bytes=VMEM_LIMIT_BYTES),
        name="merge_ln1_router",
    )(x_src, yda, y0, y1, bonus, g, ysg, gates, modall, modall, modall, *consts, *prev)


def _topk_kernel(aff_ref, tri_ref, rank_ref, *, cap):
    a = aff_ref[...].reshape(-1, aff_ref.shape[2])
    bits = pltpu.bitcast(a, jnp.int32)
    thr = jnp.zeros((a.shape[0], 1), jnp.int32)
    for bit in range(30, -1, -1):
        cand = thr | (1 << bit)
        cnt = jnp.sum(jnp.where(bits >= cand, 1.0, 0.0), axis=-1, keepdims=True)
        thr = jnp.where(cnt >= cap, cand, thr)
    gt = bits > thr
    eq = bits == thr
    need = cap - jnp.sum(jnp.where(gt, 1.0, 0.0), axis=-1, keepdims=True)
    tri = tri_ref[...]
    kb = tri.shape[0]

    def excl_cumsum(mask):
        m = jnp.where(mask, 1.0, 0.0)
        outs = []
        carry = jnp.zeros((m.shape[0], 1), F32)
        for j in range(m.shape[1] // kb):
            blk = m[:, j * kb:(j + 1) * kb]
            outs.append(_dot(blk.astype(BF16), tri) + carry)
            carry = carry + jnp.sum(blk, axis=-1, keepdims=True)
        return jnp.concatenate(outs, axis=1)

    sel = jnp.logical_or(gt, jnp.logical_and(eq, excl_cumsum(eq) < need))
    rank_ref[...] = jnp.where(sel, excl_cumsum(sel), -1.0).astype(jnp.int32).reshape(rank_ref.shape)


def _topk_ranks(aff, tri, t_off, t_len, cap):
    B, E, _ = aff.shape
    blk = t_off // t_len
    nb = math.gcd(B, RANK_BATCH)
    return pl.pallas_call(
        functools.partial(_topk_kernel, cap=cap),
        grid=(B // nb,),
        in_specs=[pl.BlockSpec((nb, E, t_len), lambda b: (b, 0, blk)),
                  pl.BlockSpec(tri.shape, lambda b: (0, 0))],
        out_specs=pl.BlockSpec((nb, E, t_len), lambda b: (b, 0, 0)),
        out_shape=jax.ShapeDtypeStruct((B, E, t_len), jnp.int32),
        compiler_params=pltpu.CompilerParams(
            dimension_semantics=("parallel",), vmem_limit_bytes=VMEM_LIMIT_BYTES),
        name="expert_choice_ranks",
    )(aff, tri)


def _moe_kernel(*refs, sets):
    n = len(sets)
    h_ref, aff_ref = refs[0], refs[1]
    rank_refs = refs[2:2 + n]
    wg_ref, wu_ref, wd_ref, f_ref = refs[2 + n:]
    e = pl.program_id(1)

    @pl.when(e == 0)
    def _():
        f_ref[...] = jnp.zeros_like(f_ref)

    onehots, gates, xs = [], [], []
    for (t0, tn, cap), rank_ref in zip(sets, rank_refs):
        rank = rank_ref[0, pl.ds(e, 1), :]
        aff = aff_ref[0, pl.ds(e, 1), t0:t0 + tn]
        slot = lax.broadcasted_iota(jnp.int32, (cap, tn), 0)
        hit = rank == slot
        onehot = jnp.where(hit, 1.0, 0.0).astype(BF16)
        onehots.append(onehot)
        gates.append(jnp.sum(jnp.where(hit, aff, 0.0), axis=-1, keepdims=True))
        xs.append(_dot(onehot, h_ref[0, t0:t0 + tn, :]).astype(BF16))
    xe = jnp.concatenate(xs, axis=0) if n > 1 else xs[0]
    gate = jnp.concatenate(gates, axis=0) if n > 1 else gates[0]
    hg = _dot(xe, wg_ref[0, 0])
    hid = (hg * _sigmoid(hg)) * _dot(xe, wu_ref[0, 0])
    ye = (_dot(hid.astype(BF16), wd_ref[0, 0]) * gate).astype(BF16)
    r0 = 0
    for (t0, tn, cap), onehot in zip(sets, onehots):
        f_ref[0, t0:t0 + tn, :] += _dot_tn(onehot, ye[r0:r0 + cap])
        r0 += cap


def _expert_ffn(h2, aff, ranks, wg, wu, wd, layer, sets):
    B, TT, D = h2.shape
    E = aff.shape[1]
    F = wg.shape[3]
    return pl.pallas_call(
        functools.partial(_moe_kernel, sets=sets),
        grid=(B, E),
        in_specs=[pl.BlockSpec((1, TT, D), lambda b, e: (b, 0, 0), pipeline_mode=pl.Buffered(1)),
                  pl.BlockSpec((1, E, TT), lambda b, e: (b, 0, 0))]
                 + [pl.BlockSpec((1, E, r.shape[2]), lambda b, e: (b, 0, 0)) for r in ranks]
                 + [pl.BlockSpec((1, 1, D, F), lambda b, e: (layer, e, 0, 0)),
                    pl.BlockSpec((1, 1, D, F), lambda b, e: (layer, e, 0, 0)),
                    pl.BlockSpec((1, 1, F, D), lambda b, e: (layer, e, 0, 0))],
        out_specs=pl.BlockSpec((1, TT, D), lambda b, e: (b, 0, 0)),
        out_shape=jax.ShapeDtypeStruct((B, TT, D), F32),
        compiler_params=pltpu.CompilerParams(
            dimension_semantics=("parallel", "arbitrary"), vmem_limit_bytes=VMEM_LIMIT_BYTES),
        name="expert_ffn",
    )(h2, aff, *ranks, wg, wu, wd)


def _ln2_kernel(x_ref, f_ref, g2_ref, lng_ref, lnb_ref, o_ref, *, alpha):
    z = alpha * x_ref[0] + g2_ref[0, 0] * f_ref[0]
    zm = jnp.mean(z, axis=-1, keepdims=True)
    dz = z - zm
    zv = jnp.mean(dz * dz, axis=-1, keepdims=True)
    o_ref[0] = dz * lax.rsqrt(zv + LN_EPS) * lng_ref[...] + lnb_ref[...]


def _final_norm(xmid, f, modall, lng, lnb, ntl, alpha, rows):
    B, _, D = xmid.shape
    tm = TOKEN_TILE
    tok = pl.BlockSpec((1, tm, D), lambda b, i: (b, i, 0))
    const = lambda a: pl.BlockSpec(a.shape, lambda b, i: (0,) * a.ndim)
    return pl.pallas_call(
        functools.partial(_ln2_kernel, alpha=alpha),
        grid=(B, rows // tm),
        in_specs=[tok, tok, pl.BlockSpec((1, 1, 1, D), lambda b, i: (b, i // ntl, 0, 5)),
                  const(lng), const(lnb)],
        out_specs=tok,
        out_shape=jax.ShapeDtypeStruct((B, rows, D), F32),
        compiler_params=pltpu.CompilerParams(dimension_semantics=("parallel", "parallel")),
        name="ln2",
    )(xmid, f, modall, lng, lnb)


def _rope_tables(t_lat, t_ctx):
    rows = t_lat // GRID_W
    row = jnp.repeat(jnp.arange(rows, dtype=F32), GRID_W)
    col = jnp.tile(jnp.arange(GRID_W, dtype=F32), rows)
    half = DA_QK_DIM // 2
    inv_freq = ROPE_BASE ** (-jnp.arange(0, half, 2, dtype=F32) / half)
    ar = row[:, None] * inv_freq
    ac = col[:, None] * inv_freq
    ang = jnp.concatenate([ar, ar, ac, ac], axis=-1)
    sign = jnp.where((jnp.arange(DA_QK_DIM) % 32) < 16, -1.0, 1.0).astype(F32)
    reps = DA_QK_COLS // DA_QK_DIM
    cos = jnp.tile(jnp.cos(ang), (1, reps))
    sin = jnp.tile(jnp.sin(ang) * sign, (1, reps))
    cos = jnp.concatenate([cos, jnp.ones((t_ctx, DA_QK_COLS), F32)], axis=0)
    sin = jnp.concatenate([sin, jnp.zeros((t_ctx, DA_QK_COLS), F32)], axis=0)
    return cos, sin


def kernel(x, c, ctx, c_ctx, w_mod, b_mod, w_in, da_lambda, da_norm_g, rw_shift_mu, rw_w0, rw_w2, rw_a0, rw_a2, rw_k_k, rw_k_a, rw_r_k, rw_ln_g, rw_ln_b, rw_g2, sg_norm_g, sg_norm_b, sg_w, sg_b, w_branch, w_out, ln1_g, ln1_b, w_router, w_e_gate, w_e_up, w_e_down, ln2_g, ln2_b):
    B, T, D = x.shape
    Tc = ctx.shape[1]
    depth = w_mod.shape[0]
    tm = TOKEN_TILE
    assert D == D_MODEL and T % tm == 0 and Tc == tm and T % MERGE_TILE == 0
    ntl = T // tm
    alpha = (2 * depth) ** 0.25
    cap_lat = EC_CAPACITY * T // N_EXPERTS
    cap_ctx = EC_CAPACITY * Tc // N_EXPERTS

    cos, sin = _rope_tables(T, Tc)
    lane = jnp.arange(SCAN_HEADS_PER_PASS * RW_HEAD)
    ones_bd = (lane[:, None] // RW_HEAD == lane[None, :] // RW_HEAD).astype(BF16)
    kb = math.gcd(Tc, RANK_BLOCK)
    tri = (jnp.arange(kb)[:, None] < jnp.arange(kb)[None, :]).astype(BF16)
    rows = ((B + 1 + 7) // 8) * 8
    cc = jnp.concatenate([c, c_ctx[None, :], jnp.zeros((rows - B - 1, D), F32)], axis=0)
    row2 = lambda a: a.reshape(1, -1)

    w_in_bf = w_in.astype(BF16)
    wg, wu, wd = w_e_gate.astype(BF16), w_e_up.astype(BF16), w_e_down.astype(BF16)
    x_lat, x_ctx, ctx_block0 = x, ctx, 0
    for l in range(depth):
        last = l == depth - 1
        lam_init = 0.8 - 0.6 * math.exp(-0.3 * l)
        mod = _modulation(cc, w_mod[l], b_mod[l])
        modall = jnp.stack([mod[:B], jnp.broadcast_to(mod[B], (B, 6 * D))], axis=1)
        modall = modall.reshape(B, 2, 1, 6 * D)

        sgbias = jnp.repeat(sg_b[l].T, SG_WIDTH // SG_GROUPS, axis=1)
        cat2 = lambda a: jnp.transpose(a, (1, 0, 2)).reshape(a.shape[1], 2 * RW_WIDTH)
        rw_params = [rw_shift_mu[l], row2(rw_w0[l]), cat2(rw_w2[l]).astype(BF16), row2(rw_a0[l]),
                     cat2(rw_a2[l]).astype(BF16), row2(rw_k_k[l]), row2(rw_k_a[l]), row2(rw_r_k[l]),
                     rw_g2[l].astype(BF16), ones_bd]
        (q, k, v, ysg, gates, lw0, lw1, kd0, kd1, b0, b1, kk, vv, rr, bonus, gg) = _input_projection(
            x_lat, x_ctx, ctx_block0, T + Tc, modall, w_in_bf, l, cos, sin, row2(sg_norm_g[l]),
            row2(sg_norm_b[l]), sg_w[l].astype(BF16), sgbias, rw_params, ntl)

        nt_out = ntl if last else (T + Tc) // tm
        yda = _diff_attention(q, k, v, da_lambda[l], da_norm_g[l], ntl, lam_init, nt_out)

        y0, y1 = _rwkv_scan(lw0, lw1, kd0, kd1, b0, b1, kk, vv, rr, T)

        merge_args = (yda, y0, y1, bonus, gg, ysg, gates, modall, w_branch[l].astype(BF16),
                      w_out[l].astype(BF16), row2(ln1_g[l]), row2(ln1_b[l]), row2(rw_ln_g[l]),
                      row2(rw_ln_b[l]), ones_bd, w_router[l].T, alpha)
        outs = _merge(x_lat, 0, MERGE_TILE, 0, T // MERGE_TILE, 0, None, *merge_args)
        if not last:
            outs = _merge(x_ctx, ctx_block0, tm, T, Tc // tm, 1, outs, *merge_args)
        xmid, h2, aff = outs

        sets = ((0, T, cap_lat),) if last else ((0, T, cap_lat), (T, Tc, cap_ctx))
        ranks = [_topk_ranks(aff, tri, 0, T, cap_lat)]
        if not last:
            ranks.append(_topk_ranks(aff, tri, T, Tc, cap_ctx))
        f = _expert_ffn(h2, aff, ranks, wg, wu, wd, l, sets)
        x_lat = _final_norm(xmid, f, modall, row2(ln2_g[l]), row2(ln2_b[l]), ntl, alpha,
                            T if last else T + Tc)
        x_ctx, ctx_block0 = x_lat, ntl
    return x_lat
```

```python
import functools
import math

import jax
import jax.numpy as jnp
from jax import lax
from jax.experimental import pallas as pl
from jax.experimental.pallas import tpu as pltpu

F32 = jnp.float32
BF16 = jnp.bfloat16
HIGHEST = lax.Precision.HIGHEST

D_MODEL = 1024
GRID_W = 64
DA_HEADS = 4
DA_QK_DIM = 64
DA_V_DIM = 128
DA_WIDTH = 512
DA_QK_COLS = 512
ROPE_BASE = 10000.0
DA_EPS = 1e-5
RW_HEAD = 64
RW_HEADS = 8
RW_WIDTH = 512
RW_COLS = 1792
RW_GN_EPS = 64e-5
SG_CHUNK = 128
SG_GROUPS = 4
SG_WIDTH = 512
DA_K0 = 512
DA_V0 = 1024
RW_0 = 1536
SG_0 = RW_0 + RW_COLS
GATE_0 = SG_0 + 2 * SG_WIDTH
N_EXPERTS = 16
EC_CAPACITY = 2
LN_EPS = 1e-5
LOG2_E = math.log2(math.e)

TOKEN_TILE = 256
SCAN_CHUNK = 64
ATTN_KEY_TILE = 256
ATTN_HEADS_PER_STEP = 4
SCAN_HEADS_PER_PASS = 4
SCAN_BATCH = 4
MERGE_TILE = 512
MERGE_ROWS = 128
RANK_BLOCK = 256
RANK_BATCH = 4
LN2_TILES = (1024, 768, 512, 256)
VMEM_LIMIT_BYTES = 58 * 1024 * 1024


def _dot(a, b):
    return jnp.dot(a, b, preferred_element_type=F32)


def _dot_hi(a, b):
    return jnp.dot(a, b, preferred_element_type=F32, precision=HIGHEST)


def _dot_nt(a, b):
    return lax.dot_general(a, b, (((1,), (1,)), ((), ())), preferred_element_type=F32)


def _dot_tn(a, b):
    return lax.dot_general(a, b, (((0,), (0,)), ((), ())), preferred_element_type=F32)


def _sigmoid(z):
    return 1.0 / (1.0 + jnp.exp(-z))


def _seg_sum(z, ones_bd):
    rows = z.shape[0]
    w = ones_bd.shape[0]
    hi = z.astype(BF16)
    lo = (z - hi.astype(F32)).astype(BF16)
    cols = []
    for c in range(0, z.shape[1], w):
        st = _dot(jnp.concatenate([hi[:, c:c + w], lo[:, c:c + w]], axis=0), ones_bd)
        cols.append(st[:rows] + st[rows:])
    return jnp.concatenate(cols, axis=1)


def _mod_kernel(c_ref, w_ref, b_ref, o_ref):
    cc = c_ref[...]
    o_ref[...] = _dot_hi(cc * _sigmoid(cc), w_ref[...]) + b_ref[...]


def _modulation(cc, w_mod, b_mod):
    rows, d = cc.shape
    n = w_mod.shape[1]
    tn = 1024
    return pl.pallas_call(
        _mod_kernel,
        grid=(n // tn,),
        in_specs=[pl.BlockSpec((rows, d), lambda j: (0, 0)),
                  pl.BlockSpec((d, tn), lambda j: (0, j)),
                  pl.BlockSpec((1, tn), lambda j: (0, j))],
        out_specs=pl.BlockSpec((rows, tn), lambda j: (0, j)),
        out_shape=jax.ShapeDtypeStruct((rows, n), F32),
        name="adaln_mod",
    )(cc, w_mod, b_mod.reshape(1, n))


def _inproj_kernel(xl_ref, xc_ref, lp_ref, ln_ref, sh_ref, sc_ref, w_ref, cos_ref, sin_ref, sgg_ref, sgb_ref,
                   sgw_ref, sgbias_ref, *rest, ntl, nt):
    rw_params, (q_ref, k_ref, v_ref, sg_ref, gate_ref), rw_outs = rest[:10], rest[10:15], rest[15:]
    tm = xl_ref.shape[1]
    i = pl.program_id(1)
    x = jnp.where(i < ntl, xl_ref[0], xc_ref[0])
    h = (x * (1.0 + sc_ref[0, 0]) + sh_ref[0, 0]).astype(BF16)

    def proj(c0, c1):
        return _dot(h, w_ref[0, :, c0:c1])

    cos = cos_ref[...]
    sin = sin_ref[...]
    lane = lax.broadcasted_iota(jnp.int32, (tm, DA_QK_COLS), 1)
    first = (lane % 32) < 16

    def rope(z):
        zr = jnp.where(first, pltpu.roll(z, DA_QK_COLS - 16, 1), pltpu.roll(z, 16, 1))
        return z * cos + zr * sin

    halo = jnp.concatenate([lp_ref[0], ln_ref[0]], axis=0)
    h_halo = (halo * (1.0 + sc_ref[0, 0]) + sh_ref[0, 0]).astype(BF16)
    p_ext = _dot(jnp.concatenate([h, h_halo], axis=0), w_ref[0, :, RW_0:SG_0])
    prev_ok = jnp.logical_and(i != 0, i != ntl)
    next_ok = jnp.logical_and(i != ntl - 1, i != nt - 1)
    _rwkv_feature_math(p_ext[:tm], jnp.where(prev_ok, p_ext[tm + 7:tm + 8], 0.0),
                       jnp.where(next_ok, p_ext[tm + 8:tm + 9], 0.0), rw_params, rw_outs)
    q_ref[0] = (rope(proj(0, DA_K0)) * (DA_QK_DIM ** -0.5 * LOG2_E)).astype(BF16)
    k_ref[0] = rope(proj(DA_K0, DA_V0)).astype(BF16)
    v_ref[0] = proj(DA_V0, RW_0).astype(BF16)
    for j in range(3):
        gate_ref[0, :, j * D_MODEL:(j + 1) * D_MODEL] = _sigmoid(
            proj(GATE_0 + j * D_MODEL, GATE_0 + (j + 1) * D_MODEL)).astype(BF16)

    ps = proj(SG_0, GATE_0)
    gl = ps * (0.5 * (1.0 + jnp.tanh(math.sqrt(2.0 / math.pi) * (ps + 0.044715 * (ps * ps * ps)))))
    u = gl[:, :SG_WIDTH]
    vv = gl[:, SG_WIDTH:]
    mu = jnp.mean(vv, axis=-1, keepdims=True)
    dv = vv - mu
    var = jnp.mean(dv * dv, axis=-1, keepdims=True)
    vn = (dv * lax.rsqrt(var + LN_EPS) * sgg_ref[...] + sgb_ref[...]).astype(BF16)
    gd = SG_WIDTH // SG_GROUPS
    for n in range(tm // SG_CHUNK):
        r0 = n * SG_CHUNK
        for g in range(SG_GROUPS):
            c0 = g * gd
            vm = _dot(sgw_ref[g], vn[r0:r0 + SG_CHUNK, c0:c0 + gd]) + sgbias_ref[:, c0:c0 + gd]
            sg_ref[0, r0:r0 + SG_CHUNK, c0:c0 + gd] = (u[r0:r0 + SG_CHUNK, c0:c0 + gd] * vm).astype(BF16)


def _stream_specs(d, ntl, ctx_block0):
    tm = TOKEN_TILE
    lat = pl.BlockSpec((1, tm, d), lambda b, i: (b, jnp.minimum(i, ntl - 1), 0))
    ctx = pl.BlockSpec((1, tm, d), lambda b, i: (b, ctx_block0 + jnp.maximum(i - ntl, 0), 0))
    return [lat, ctx]


def _input_projection(x_lat, x_ctx, ctx_block0, TT, modall, w_in, layer, cos, sin, sgg, sgb, sgw, sgbias,
                      rw_params, ntl):
    B, lat_rows, D = x_lat.shape
    tm = TOKEN_TILE
    nt = TT // tm
    r8 = tm // 8
    last8 = lat_rows // 8 - 1
    W = RW_WIDTH
    tok = lambda w: pl.BlockSpec((1, tm, w), lambda b, i: (b, i, 0))
    modspec = lambda j: pl.BlockSpec((1, 1, 1, D), lambda b, i: (b, i // ntl, 0, j))
    const = lambda a: pl.BlockSpec(a.shape, lambda b, i: (0,) * a.ndim)
    halo_prev = pl.BlockSpec((1, 8, D), lambda b, i: (b, jnp.maximum(jnp.minimum(i, ntl - 1) * r8 - 1, 0), 0))
    halo_next = pl.BlockSpec((1, 8, D), lambda b, i: (b, jnp.minimum((jnp.minimum(i, ntl - 1) + 1) * r8, last8), 0))
    return pl.pallas_call(
        functools.partial(_inproj_kernel, ntl=ntl, nt=nt),
        grid=(B, nt),
        in_specs=_stream_specs(D, ntl, ctx_block0) + [halo_prev, halo_next, modspec(0), modspec(1),
                  pl.BlockSpec((1,) + w_in.shape[1:], lambda b, i: (layer, 0, 0),
                               pipeline_mode=pl.Buffered(1)),
                  pl.BlockSpec((tm, DA_QK_COLS), lambda b, i: (i, 0)),
                  pl.BlockSpec((tm, DA_QK_COLS), lambda b, i: (i, 0)),
                  const(sgg), const(sgb), const(sgw), const(sgbias)]
                 + [const(a) for a in rw_params],
        out_specs=[tok(DA_QK_COLS), tok(DA_QK_COLS), tok(DA_WIDTH), tok(SG_WIDTH), tok(3 * D)] + [tok(W)] * 11,
        out_shape=[jax.ShapeDtypeStruct((B, TT, DA_QK_COLS), BF16),
                   jax.ShapeDtypeStruct((B, TT, DA_QK_COLS), BF16),
                   jax.ShapeDtypeStruct((B, TT, DA_WIDTH), BF16),
                   jax.ShapeDtypeStruct((B, TT, SG_WIDTH), BF16),
                   jax.ShapeDtypeStruct((B, TT, 3 * D), BF16)]
                  + [jax.ShapeDtypeStruct((B, TT, W), F32)] * 2 + [jax.ShapeDtypeStruct((B, TT, W), BF16)] * 9,
        compiler_params=pltpu.CompilerParams(
            dimension_semantics=("parallel", "parallel"), vmem_limit_bytes=VMEM_LIMIT_BYTES),
        name="in_proj",
    )(x_lat, x_ctx, x_lat, x_lat, modall, modall, w_in, cos, sin, sgg, sgb, sgw, sgbias, *rw_params)


def _attn_kernel(q_ref, k_ref, v_ref, lam_ref, g_ref, o_ref, *, ntl, t_lat, lam_init):
    i = pl.program_id(2)
    lp = lam_ref[...]
    lam = (jnp.exp(jnp.sum(lp[0:1] * lp[1:2], axis=-1, keepdims=True))
           - jnp.exp(jnp.sum(lp[2:3] * lp[3:4], axis=-1, keepdims=True)) + lam_init)
    dv = DA_V_DIM
    heads = [slice(h * dv, (h + 1) * dv) for h in range(q_ref.shape[2] // dv)]
    lane = lax.broadcasted_iota(jnp.int32, (q_ref.shape[1], dv), 1)
    qs = []
    for hs in heads:
        q = q_ref[0, :, hs]
        zero = jnp.zeros_like(q)
        qs.append((jnp.where(lane < DA_QK_DIM, q, zero), jnp.where(lane >= DA_QK_DIM, q, zero)))

    def attend(k0, nk):
        kt = ATTN_KEY_TILE
        tiles = [slice(k0 + t * kt, k0 + (t + 1) * kt) for t in range(nk // kt)]

        def row_max(ss):
            m = ss[0]
            for s in ss[1:]:
                m = jnp.maximum(m, s)
            return jnp.max(m, axis=-1, keepdims=True)

        def pv(es, hs):
            acc = None
            for e, sl in zip(es, tiles):
                v = v_ref[0, sl, hs]
                d = _dot(e, jnp.concatenate([v, jnp.ones_like(v)], axis=1))
                acc = d if acc is None else acc + d
            return acc[:, :dv] / acc[:, dv:]

        ks = [[k_ref[0, sl, hs] for sl in tiles] for hs in heads]
        s0 = [[_dot_nt(q0, kk) for kk in kh] for (q0, _), kh in zip(qs, ks)]
        m0 = [row_max(s) for s in s0]
        s1, e0 = [], []
        for (_, q1), kh, sh, mh in zip(qs, ks, s0, m0):
            s1.append([])
            e0.append([])
            for kk, s in zip(kh, sh):
                s1[-1].append(_dot_nt(q1, kk))
                e0[-1].append(jnp.exp2(s - mh).astype(BF16))
        m1 = [row_max(s) for s in s1]
        o0 = [pv(e, hs) for e, hs in zip(e0, heads)]
        e1 = [[jnp.exp2(s - mh).astype(BF16) for s in sh] for sh, mh in zip(s1, m1)]
        for hs, oa, e in zip(heads, o0, e1):
            o = oa - lam * pv(e, hs)
            o = o * lax.rsqrt(jnp.mean(o * o, axis=-1, keepdims=True) + DA_EPS) * g_ref[...]
            o_ref[0, :, hs] = (o * (1.0 - lam_init)).astype(BF16)

    @pl.when(i < ntl)
    def _():
        attend(0, k_ref.shape[1])

    @pl.when(i >= ntl)
    def _():
        attend(t_lat, k_ref.shape[1] - t_lat)


def _diff_attention(q, k, v, lam_p, norm_g, ntl, lam_init, nt):
    B, TT, _ = q.shape
    tm = TOKEN_TILE
    w = ATTN_HEADS_PER_STEP * DA_V_DIM
    kv = pl.BlockSpec((1, TT, w), lambda b, h, i: (b, 0, h))
    qo = pl.BlockSpec((1, tm, w), lambda b, h, i: (b, i, h))
    return pl.pallas_call(
        functools.partial(_attn_kernel, ntl=ntl, t_lat=ntl * tm, lam_init=lam_init),
        grid=(B, DA_HEADS // ATTN_HEADS_PER_STEP, nt),
        in_specs=[qo, kv, kv,
                  pl.BlockSpec(lam_p.shape, lambda b, h, i: (0, 0)),
                  pl.BlockSpec((1, DA_V_DIM), lambda b, h, i: (0, 0))],
        out_specs=qo,
        out_shape=jax.ShapeDtypeStruct((B, TT, DA_WIDTH), BF16),
        compiler_params=pltpu.CompilerParams(
            dimension_semantics=("parallel", "parallel", "parallel"),
            vmem_limit_bytes=VMEM_LIMIT_BYTES),
        name="diff_attn",
    )(q, k, v, lam_p, norm_g.reshape(1, DA_V_DIM))


def _rwkv_feature_math(p, prev_row, next_row, params, outs):
    mu_ref, w0_ref, w2_ref, a0_ref, a2_ref, kk_ref, ka_ref, rk_ref, g2_ref, ones_ref = params
    lw0_ref, lw1_ref, kd0_ref, kd1_ref, b0_ref, b1_ref, kko_ref, v_ref, r_ref, bonus_ref, g_ref = outs
    tm = p.shape[0]
    row = lax.broadcasted_iota(jnp.int32, p.shape, 0)
    prev = jnp.where(row == 0, prev_row, pltpu.roll(p, 1, 0))
    nxt = jnp.where(row == tm - 1, next_row, pltpu.roll(p, tm - 1, 0))
    ps = p + mu_ref[0:1, :] * (prev - p) + mu_ref[1:2, :] * (nxt - p)

    W = RW_WIDTH
    r = ps[:, 0:W]
    k = ps[:, W:2 * W]
    v = ps[:, 2 * W:3 * W]
    xw = ps[:, 3 * W:3 * W + 64]
    xa = ps[:, 3 * W + 64:3 * W + 128]
    xg = ps[:, 3 * W + 128:3 * W + 256]
    ones_bd = ones_ref[...]

    w_pre = _dot(jnp.tanh(xw).astype(BF16), w2_ref[...]) + w0_ref[...]
    logw = -math.exp(-0.5) * _sigmoid(w_pre)
    a = _sigmoid(_dot(xa.astype(BF16), a2_ref[...]) + a0_ref[...])

    kx = k * kk_ref[...]
    nrm = jnp.sqrt(_seg_sum(kx * kx, ones_bd))
    kk = kx / jnp.maximum(nrm, 1e-12)
    ka = ka_ref[...]
    kd0 = k * (1.0 + (a[:, :W] - 1.0) * ka)
    kd1 = k * (1.0 + (a[:, W:] - 1.0) * ka)
    g = _dot(_sigmoid(xg).astype(BF16), g2_ref[...])
    k_b = 0.5 * (kd0 + kd1)
    bonus = _seg_sum(r * k_b * rk_ref[...], ones_bd) * v

    lw0_ref[0] = logw[:, :W]
    lw1_ref[0] = logw[:, W:]
    kd0_ref[0] = kd0.astype(BF16)
    kd1_ref[0] = kd1.astype(BF16)
    b0_ref[0] = (kk * a[:, :W]).astype(BF16)
    b1_ref[0] = (kk * a[:, W:]).astype(BF16)
    kko_ref[0] = kk.astype(BF16)
    v_ref[0] = v.astype(BF16)
    r_ref[0] = r.astype(BF16)
    bonus_ref[0] = bonus.astype(BF16)
    g_ref[0] = g.astype(BF16)


def _split_bf16(x, pieces):
    out = []
    for _ in range(pieces - 1):
        hi = x.astype(BF16)
        out.append(hi)
        x = x - hi.astype(F32)
    out.append(x.astype(BF16))
    return out


def _block_diag(x, groups):
    xb = x.astype(BF16)
    rows, lanes = xb.shape
    t = jnp.concatenate([xb] * groups, axis=0)
    ri = lax.broadcasted_iota(jnp.int32, t.shape, 0) // rows
    li = lax.broadcasted_iota(jnp.int32, t.shape, 1) // (lanes // groups)
    return jnp.where(ri == li, t, jnp.zeros_like(t))


def _scan_kernel(lw0, kd0, b0, kkf, vf, rf, lw1, kd1, b1, kkr, vr, rr, y0_ref, y1_ref, s_ref):
    step = pl.program_id(1)

    @pl.when(step == 0)
    def _():
        s_ref[...] = jnp.zeros_like(s_ref)

    C = lw0.shape[1]
    G = SCAN_HEADS_PER_PASS
    L = G * RW_HEAD
    assert C == RW_HEAD
    ti = lax.broadcasted_iota(jnp.int32, (C, L), 0)
    si = lax.broadcasted_iota(jnp.int32, (C, L), 1) % C
    tc = lax.broadcasted_iota(jnp.int32, (C, C), 0)
    sc = lax.broadcasted_iota(jnp.int32, (C, C), 1)
    same_head = (lax.broadcasted_iota(jnp.int32, (L, L), 0) // RW_HEAD
                 == lax.broadcasted_iota(jnp.int32, (L, L), 1) // RW_HEAD)

    groups = []
    for z, (lw, kd, b, kk, v, r) in enumerate(((lw0, kd0, b0, kkf, vf, rf), (lw1, kd1, b1, kkr, vr, rr))):
        rev = z == 1
        incl = (si >= ti) if rev else (si <= ti)
        strict = (si > ti) if rev else (si < ti)
        tri = jnp.where((sc >= tc) if rev else (sc <= tc), 1.0, 0.0).astype(BF16)
        for n in range(lw.shape[0]):
            logw = lw[n]
            cum = sum(_dot(tri, piece) for piece in _split_bf16(logw, 3))
            tot = jnp.sum(logw, axis=0, keepdims=True)
            p_in = jnp.exp(-cum)
            p_end = jnp.exp(tot - cum)
            kkz, rz, bz, kdz = (t[n].astype(F32) for t in (kk, r, b, kd))
            lhs = jnp.concatenate([-kkz * jnp.exp(cum - logw), rz * jnp.exp(cum)], axis=0)
            rhs_b = bz * p_in
            rhs_k = kdz * p_in
            upd = jnp.concatenate([bz * p_end, kdz * p_end], axis=0)
            dec = jnp.exp(tot)
            vz = v[n]
            for j in range(RW_HEADS // G):
                c = slice(j * L, (j + 1) * L)
                groups.append(dict(z=z, smp=n, j=j, lhs=lhs[:, c].astype(BF16), rhs_b=rhs_b[:, c],
                                   rhs_k=rhs_k[:, c], upd=upd[:, c].astype(BF16), dec=dec[:, c],
                                   v=vz[:, c], strict=strict, incl=incl))

    for g in groups:
        aa_b = _dot_nt(g["lhs"], _block_diag(g["rhs_b"], G))
        aa_k = _dot_nt(g["lhs"], _block_diag(g["rhs_k"], G))
        g["a_k"] = jnp.concatenate([jnp.where(g["strict"], aa_k[:C], 0.0),
                                    jnp.where(g["incl"], aa_k[C:], 0.0)], axis=0).astype(BF16)
        g["a_rb"] = jnp.where(g["incl"], aa_b[C:], 0.0).astype(BF16)
        g["n"] = jnp.where(g["strict"], aa_b[:C], 0.0)
        g["apow"] = _dot(g["n"].astype(BF16), _block_diag(g["n"], G))
    levels = int(math.log2(C)) - 1
    for lv in range(levels):
        for g in groups:
            pbd = _block_diag(g["apow"], G)
            if lv < levels - 1:
                st = _dot(jnp.concatenate([g["n"], g["apow"]], axis=0).astype(BF16), pbd)
                g["n"] = g["n"] + g["apow"] + st[:C]
                g["apow"] = st[C:]
            else:
                g["n"] = g["n"] + g["apow"] + _dot(g["n"].astype(BF16), pbd)
    for g in groups:
        g["s0"] = s_ref[g["smp"], g["z"], g["j"]]
        g["gh"] = _dot_nt(g["lhs"], g["s0"].astype(BF16))
    for g in groups:
        st = _dot(g["a_k"], _block_diag(g["v"], G))
        g["w"] = g["gh"][:C] + st[:C]
        g["yk"] = st[C:]
    for g in groups:
        g["u"] = g["w"] + _dot(g["n"].astype(BF16), _block_diag(g["w"], G))
    for g in groups:
        g["y"] = g["gh"][C:] + g["yk"] + _dot(g["a_rb"], _block_diag(g["u"], G))
    for g in groups:
        uv = jnp.concatenate([g["u"].astype(BF16), g["v"]], axis=0)
        s_ref[g["smp"], g["z"], g["j"]] = g["s0"] * g["dec"] + jnp.where(same_head, _dot_tn(uv, g["upd"]), 0.0)
    for z, y_ref in enumerate((y0_ref, y1_ref)):
        for n in range(y_ref.shape[0]):
            y_ref[n] = jnp.concatenate([g["y"] for g in groups if g["z"] == z and g["smp"] == n],
                                       axis=-1).astype(y_ref.dtype)


def _rwkv_scan(lw0, lw1, kd0, kd1, b0, b1, kk, v, r, t_lat):
    B, TT, W = kk.shape
    C = SCAN_CHUNK
    ncl = t_lat // C
    nch = TT // C
    ncc = nch - ncl

    def fwd(s):
        return jnp.where(s < ncc, ncl + s, s - ncc)

    nb = math.gcd(B, SCAN_BATCH)
    fs =pl.BlockSpec((nb, C, W), lambda b, s: (b, fwd(s), 0))
    rs = pl.BlockSpec((nb, C, W), lambda b, s: (b, nch - 1 - s, 0))
    return pl.pallas_call(
        _scan_kernel,
        grid=(B // nb, nch),
        in_specs=[fs] * 6 + [rs] * 6,
        out_specs=[fs, rs],
        out_shape=[jax.ShapeDtypeStruct((B, TT, W), BF16)] * 2,
        scratch_shapes=[pltpu.VMEM((nb, 2, RW_HEADS // SCAN_HEADS_PER_PASS, SCAN_HEADS_PER_PASS * RW_HEAD,
                                    SCAN_HEADS_PER_PASS * RW_HEAD), F32)],
        compiler_params=pltpu.CompilerParams(
            dimension_semantics=("parallel", "arbitrary"), vmem_limit_bytes=VMEM_LIMIT_BYTES),
        name="rwkv_scan",
    )(lw0, kd0, b0, kk, v, r, lw1, kd1, b1, kk, v, r)


def _merge_kernel(x_ref, yda_ref, y0_ref, y1_ref, bonus_ref, g_ref, ysg_ref, gate_ref,
                  g1_ref, sh2_ref, sc2_ref, wb_ref, wo_ref, ln1g_ref, ln1b_ref, rlng_ref, rlnb_ref,
                  ones_ref, wr_ref, *rest, alpha):
    xmid_ref, h2_ref, aff_ref = rest[-3:]
    D = D_MODEL
    tm = x_ref.shape[1]
    subs = [slice(r, r + MERGE_ROWS) for r in range(0, tm, MERGE_ROWS)]
    ones_bd = ones_ref[...]
    wr_hi, wr_lo = _split_bf16(wr_ref[...], 2)
    wr_hl = jnp.concatenate([wr_hi, wr_lo], axis=0)

    yrw = []
    for sl in subs:
        y = y0_ref[0, sl, :].astype(F32) + y1_ref[0, sl, :].astype(F32)
        mu = _seg_sum(y, ones_bd) * (1.0 / RW_HEAD)
        dy = y - mu
        var = _seg_sum(dy * dy, ones_bd) * (1.0 / RW_HEAD)
        gn = dy * lax.rsqrt(var + RW_GN_EPS) * rlng_ref[...] + rlnb_ref[...]
        yrw.append(((gn + bonus_ref[0, sl, :].astype(F32)) * g_ref[0, sl, :].astype(F32)).astype(BF16))
    ms = []
    for sl, yr in zip(subs, yrw):
        m = gate_ref[0, sl, 0:D].astype(F32) * _dot(yda_ref[0, sl, :], wb_ref[0:DA_WIDTH, :])
        m = m + gate_ref[0, sl, D:2 * D].astype(F32) * _dot(yr, wb_ref[DA_WIDTH:DA_WIDTH + RW_WIDTH, :])
        m = m + gate_ref[0, sl, 2 * D:3 * D].astype(F32) * _dot(ysg_ref[0, sl, :], wb_ref[DA_WIDTH + RW_WIDTH:, :])
        ms.append(m.astype(BF16))
    mixes = [_dot(m, wo_ref[...]) for m in ms]
    h2s = []
    for sl, mix in zip(subs, mixes):
        z = alpha * x_ref[0, sl, :] + g1_ref[0, 0] * mix
        zm = jnp.mean(z, axis=-1, keepdims=True)
        dz = z - zm
        zv = jnp.mean(dz * dz, axis=-1, keepdims=True)
        xmid = dz * lax.rsqrt(zv + LN_EPS) * ln1g_ref[...] + ln1b_ref[...]
        xmid_ref[0, sl, :] = xmid
        h2 = xmid * (1.0 + sc2_ref[0, 0]) + sh2_ref[0, 0]
        h2_ref[0, sl, :] = h2.astype(BF16)
        h2s.append(h2)
    ne = wr_hi.shape[0]
    for sl, h2 in zip(subs, h2s):
        h_hi, h_lo = _split_bf16(h2, 2)
        part = _dot_nt(wr_hl, h_hi)
        logits = part[:ne] + part[ne:] + _dot_nt(wr_hi, h_lo)
        e = jnp.exp(logits - jnp.max(logits, axis=0, keepdims=True))
        aff_ref[0, :, sl] = e / jnp.sum(e, axis=0, keepdims=True)


def _merge(x_src, x_block0, tile, row0, n_tiles, mod_row, prev_outs, yda, y0, y1, bonus, g, ysg, gates, modall,
           w_branch, w_out, ln1g, ln1b, rlng, rlnb, ones_bd, w_router_t, alpha):
    B, TT, _ = yda.shape
    D = x_src.shape[2]
    blk0 = row0 // tile
    tok = lambda w: pl.BlockSpec((1, tile, w), lambda b, i: (b, blk0 + i, 0))
    modspec = lambda j: pl.BlockSpec((1, 1, 1, D), lambda b, i: (b, mod_row, 0, j))
    const = lambda a: pl.BlockSpec(a.shape, lambda b, i: (0,) * a.ndim)
    consts = [w_branch, w_out, ln1g, ln1b, rlng, rlnb, ones_bd, w_router_t]
    n_in = 11 + len(consts)
    prev = list(prev_outs) if prev_outs is not None else []
    return pl.pallas_call(
        functools.partial(_merge_kernel, alpha=alpha),
        grid=(B, n_tiles),
        in_specs=[pl.BlockSpec((1, tile, D), lambda b, i: (b, x_block0 + i, 0)),
                  tok(DA_WIDTH), tok(RW_WIDTH), tok(RW_WIDTH), tok(RW_WIDTH), tok(RW_WIDTH),
                  tok(SG_WIDTH), tok(3 * D), modspec(2), modspec(3), modspec(4)]
                 + [const(a) for a in consts] + [pl.BlockSpec(memory_space=pl.ANY)] * len(prev),
        out_specs=[tok(D), tok(D), pl.BlockSpec((1, N_EXPERTS, tile), lambda b, i: (b, 0, blk0 + i))],
        out_shape=[jax.ShapeDtypeStruct((B, TT, D), F32),
                   jax.ShapeDtypeStruct((B, TT, D), BF16),
                   jax.ShapeDtypeStruct((B, N_EXPERTS, TT), F32)],
        input_output_aliases={n_in + j: j for j in range(len(prev))},
        compiler_params=pltpu.CompilerParams(
            dimension_semantics=("parallel", "parallel"), vmem_limit_bytes=VMEM_LIMIT_BYTES),
        name="merge_ln1_router",
    )(x_src, yda, y0, y1, bonus, g, ysg, gates, modall, modall, modall, *consts, *prev)


def _topk_kernel(aff_ref, tri_ref, rank_ref, *, cap):
    a = aff_ref[...].reshape(-1, aff_ref.shape[2])
    bits = pltpu.bitcast(a, jnp.int32)
    thr = jnp.zeros((a.shape[0], 1), jnp.int32)
    for bit in range(30, -1, -1):
        cand = thr | (1 << bit)
        cnt = jnp.sum(jnp.where(bits >= cand, 1.0, 0.0), axis=-1, keepdims=True)
        thr = jnp.where(cnt >= cap, cand, thr)
    gt = bits > thr
    eq = bits == thr
    need = cap - jnp.sum(jnp.where(gt, 1.0, 0.0), axis=-1, keepdims=True)
    tri = tri_ref[...]
    kb = tri.shape[0]

    def excl_cumsum(mask):
        m = jnp.where(mask, 1.0, 0.0)
        outs = []
        carry = jnp.zeros((m.shape[0], 1), F32)
        for j in range(m.shape[1] // kb):
            blk = m[:, j * kb:(j + 1) * kb]
            outs.append(_dot(blk.astype(BF16), tri) + carry)
            carry = carry + jnp.sum(blk, axis=-1, keepdims=True)
        return jnp.concatenate(outs, axis=1)

    sel = jnp.logical_or(gt, jnp.logical_and(eq, excl_cumsum(eq) < need))
    rank_ref[...] = jnp.where(sel, excl_cumsum(sel), -1.0).astype(jnp.int32).reshape(rank_ref.shape)


def _topk_ranks(aff, tri, t_off, t_len, cap):
    B, E, _ = aff.shape
    blk = t_off // t_len
    nb = math.gcd(B, RANK_BATCH)
    return pl.pallas_call(
        functools.partial(_topk_kernel, cap=cap),
        grid=(B // nb,),
        in_specs=[pl.BlockSpec((nb, E, t_len), lambda b: (b, 0, blk)),
                  pl.BlockSpec(tri.shape, lambda b: (0, 0))],
        out_specs=pl.BlockSpec((nb, E, t_len), lambda b: (b, 0, 0)),
        out_shape=jax.ShapeDtypeStruct((B, E, t_len), jnp.int32),
        compiler_params=pltpu.CompilerParams(
            dimension_semantics=("parallel",), vmem_limit_bytes=VMEM_LIMIT_BYTES),
        name="expert_choice_ranks",
    )(aff, tri)


def _moe_kernel(*refs, sets):
    n = len(sets)
    h_ref, aff_ref = refs[0], refs[1]
    rank_refs = refs[2:2 + n]
    wg_ref, wu_ref, wd_ref, f_ref, acc_ref = refs[2 + n:]
    e = pl.program_id(1)

    @pl.when(e == 0)
    def _():
        acc_ref[...] = jnp.zeros_like(acc_ref)

    onehots, gates, xs = [], [], []
    for (t0, tn, cap), rank_ref in zip(sets, rank_refs):
        rank = rank_ref[0, pl.ds(e, 1), :]
        aff = aff_ref[0, pl.ds(e, 1), t0:t0 + tn]
        slot = lax.broadcasted_iota(jnp.int32, (cap, tn), 0)
        hit = rank == slot
        onehot = jnp.where(hit, 1.0, 0.0).astype(BF16)
        onehots.append(onehot)
        gates.append(jnp.sum(jnp.where(hit, aff, 0.0), axis=-1, keepdims=True))
        xs.append(_dot(onehot, h_ref[0, t0:t0 + tn, :]).astype(BF16))
    xe = jnp.concatenate(xs, axis=0) if n > 1 else xs[0]
    gate = jnp.concatenate(gates, axis=0) if n > 1 else gates[0]
    hg = _dot(xe, wg_ref[0, 0])
    hid = (hg * _sigmoid(hg)) * _dot(xe, wu_ref[0, 0])
    ye = (_dot(hid.astype(BF16), wd_ref[0, 0]) * gate).astype(BF16)
    r0 = 0
    for (t0, tn, cap), onehot in zip(sets, onehots):
        acc_ref[t0:t0 + tn, :] += _dot_tn(onehot, ye[r0:r0 + cap])
        r0 += cap

    @pl.when(e == pl.num_programs(1) - 1)
    def _():
        f_ref[0] = acc_ref[...].astype(f_ref.dtype)


def _expert_ffn(h2, aff, ranks, wg, wu, wd, layer, sets):
    B, TT, D = h2.shape
    E = aff.shape[1]
    F = wg.shape[3]
    return pl.pallas_call(
        functools.partial(_moe_kernel, sets=sets),
        grid=(B, E),
        in_specs=[pl.BlockSpec((1, TT, D), lambda b, e: (b, 0, 0), pipeline_mode=pl.Buffered(1)),
                  pl.BlockSpec((1, E, TT), lambda b, e: (b, 0, 0))]
                 + [pl.BlockSpec((1, E, r.shape[2]), lambda b, e: (b, 0, 0)) for r in ranks]
                 + [pl.BlockSpec((1, 1, D, F), lambda b, e: (layer, e, 0, 0)),
                    pl.BlockSpec((1, 1, D, F), lambda b, e: (layer, e, 0, 0)),
                    pl.BlockSpec((1, 1, F, D), lambda b, e: (layer, e, 0, 0))],
        out_specs=pl.BlockSpec((1, TT, D), lambda b, e: (b, 0, 0)),
        out_shape=jax.ShapeDtypeStruct((B, TT, D), BF16),
        scratch_shapes=[pltpu.VMEM((TT, D), F32)],
        compiler_params=pltpu.CompilerParams(
            dimension_semantics=("parallel", "arbitrary"), vmem_limit_bytes=VMEM_LIMIT_BYTES),
        name="expert_ffn",
    )(h2, aff, *ranks, wg, wu, wd)


def _ln2_kernel(x_ref, f_ref, g2_ref, lng_ref, lnb_ref, o_ref, *, alpha, t_lat):
    tile = x_ref.shape[1]
    row0 = pl.program_id(1) * tile
    for r in range(0, tile, TOKEN_TILE):
        sl = slice(r, r + TOKEN_TILE)
        g2 = jnp.where(row0 + r < t_lat, g2_ref[0, 0], g2_ref[0, 1])
        z = alpha * x_ref[0, sl, :] + g2 * f_ref[0, sl, :].astype(F32)
        zm = jnp.mean(z, axis=-1, keepdims=True)
        dz = z - zm
        zv = jnp.mean(dz * dz, axis=-1, keepdims=True)
        o_ref[0, sl, :] = dz * lax.rsqrt(zv + LN_EPS) * lng_ref[...] + lnb_ref[...]


def _final_norm(xmid, f, modall, lng, lnb, t_lat, alpha, rows):
    B, _, D = xmid.shape
    tile = next(t for t in LN2_TILES if rows % t == 0)
    tok = pl.BlockSpec((1, tile, D), lambda b, i: (b, i, 0))
    const = lambda a: pl.BlockSpec(a.shape, lambda b, i: (0,) * a.ndim)
    return pl.pallas_call(
        functools.partial(_ln2_kernel, alpha=alpha, t_lat=t_lat),
        grid=(B, rows // tile),
        in_specs=[tok, tok, pl.BlockSpec((1, 2, 1, D), lambda b, i: (b, 0, 0, 5)),
                  const(lng), const(lnb)],
        out_specs=tok,
        out_shape=jax.ShapeDtypeStruct((B, rows, D), F32),
        compiler_params=pltpu.CompilerParams(
            dimension_semantics=("parallel", "parallel"), vmem_limit_bytes=VMEM_LIMIT_BYTES),
        name="ln2",
    )(xmid, f, modall, lng, lnb)


def _rope_tables(t_lat, t_ctx):
    rows = t_lat // GRID_W
    row = jnp.repeat(jnp.arange(rows, dtype=F32), GRID_W)
    col = jnp.tile(jnp.arange(GRID_W, dtype=F32), rows)
    half = DA_QK_DIM // 2
    inv_freq = ROPE_BASE ** (-jnp.arange(0, half, 2, dtype=F32) / half)
    ar = row[:, None] * inv_freq
    ac = col[:, None] * inv_freq
    ang = jnp.concatenate([ar, ar, ac, ac], axis=-1)
    sign = jnp.where((jnp.arange(DA_QK_DIM) % 32) < 16, -1.0, 1.0).astype(F32)
    reps = DA_QK_COLS // DA_QK_DIM
    cos = jnp.tile(jnp.cos(ang), (1, reps))
    sin = jnp.tile(jnp.sin(ang) * sign, (1, reps))
    cos = jnp.concatenate([cos, jnp.ones((t_ctx, DA_QK_COLS), F32)], axis=0)
    sin = jnp.concatenate([sin, jnp.zeros((t_ctx, DA_QK_COLS), F32)], axis=0)
    return cos, sin


def kernel(x, c, ctx, c_ctx, w_mod, b_mod, w_in, da_lambda, da_norm_g, rw_shift_mu, rw_w0, rw_w2, rw_a0, rw_a2, rw_k_k, rw_k_a, rw_r_k, rw_ln_g, rw_ln_b, rw_g2, sg_norm_g, sg_norm_b, sg_w, sg_b, w_branch, w_out, ln1_g, ln1_b, w_router, w_e_gate, w_e_up, w_e_down, ln2_g, ln2_b):
    B, T, D = x.shape
    Tc = ctx.shape[1]
    depth = w_mod.shape[0]
    tm = TOKEN_TILE
    assert D == D_MODEL and T % tm == 0 and Tc == tm and T % MERGE_TILE == 0
    ntl = T // tm
    alpha = (2 * depth) ** 0.25
    cap_lat = EC_CAPACITY * T // N_EXPERTS
    cap_ctx = EC_CAPACITY * Tc // N_EXPERTS

    cos, sin = _rope_tables(T, Tc)
    lane = jnp.arange(SCAN_HEADS_PER_PASS * RW_HEAD)
    ones_bd = (lane[:, None] // RW_HEAD == lane[None, :] // RW_HEAD).astype(BF16)
    kb = math.gcd(Tc, RANK_BLOCK)
    tri = (jnp.arange(kb)[:, None] < jnp.arange(kb)[None, :]).astype(BF16)
    rows = ((B + 1 + 7) // 8) * 8
    cc = jnp.concatenate([c, c_ctx[None, :], jnp.zeros((rows - B - 1, D), F32)], axis=0)
    row2 = lambda a: a.reshape(1, -1)

    w_in_bf = w_in.astype(BF16)
    wg, wu, wd = w_e_gate.astype(BF16), w_e_up.astype(BF16), w_e_down.astype(BF16)
    x_lat, x_ctx, ctx_block0 = x, ctx, 0
    for l in range(depth):
        last = l == depth - 1
        lam_init = 0.8 - 0.6 * math.exp(-0.3 * l)
        mod = _modulation(cc, w_mod[l], b_mod[l])
        modall = jnp.stack([mod[:B], jnp.broadcast_to(mod[B], (B, 6 * D))], axis=1)
        modall = modall.reshape(B, 2, 1, 6 * D)

        sgbias = jnp.repeat(sg_b[l].T, SG_WIDTH // SG_GROUPS, axis=1)
        cat2 = lambda a: jnp.transpose(a, (1, 0, 2)).reshape(a.shape[1], 2 * RW_WIDTH)
        rw_params = [rw_shift_mu[l], row2(rw_w0[l]), cat2(rw_w2[l]).astype(BF16), row2(rw_a0[l]),
                     cat2(rw_a2[l]).astype(BF16), row2(rw_k_k[l]), row2(rw_k_a[l]), row2(rw_r_k[l]),
                     rw_g2[l].astype(BF16), ones_bd]
        (q, k, v, ysg, gates, lw0, lw1, kd0, kd1, b0, b1, kk, vv, rr, bonus, gg) = _input_projection(
            x_lat, x_ctx, ctx_block0, T + Tc, modall, w_in_bf, l, cos, sin, row2(sg_norm_g[l]),
            row2(sg_norm_b[l]), sg_w[l].astype(BF16), sgbias, rw_params, ntl)

        nt_out = ntl if last else (T + Tc) // tm
        yda = _diff_attention(q, k, v, da_lambda[l], da_norm_g[l], ntl, lam_init, nt_out)

        y0, y1 = _rwkv_scan(lw0, lw1, kd0, kd1, b0, b1, kk, vv, rr, T)

        merge_args = (yda, y0, y1, bonus, gg, ysg, gates, modall, w_branch[l].astype(BF16),
                      w_out[l].astype(BF16), row2(ln1_g[l]), row2(ln1_b[l]), row2(rw_ln_g[l]),
                      row2(rw_ln_b[l]), ones_bd, w_router[l].T, alpha)
        outs = _merge(x_lat, 0, MERGE_TILE, 0, T // MERGE_TILE, 0, None, *merge_args)
        if not last:
            outs = _merge(x_ctx, ctx_block0, tm, T, Tc // tm, 1, outs, *merge_args)
        xmid, h2, aff = outs

        sets = ((0, T, cap_lat),) if last else ((0, T, cap_lat), (T, Tc, cap_ctx))
        ranks = [_topk_ranks(aff, tri, 0, T, cap_lat)]
        if not last:
            ranks.append(_topk_ranks(aff, tri, T, Tc, cap_ctx))
        f = _expert_ffn(h2, aff, ranks, wg, wu, wd, l, sets)
        x_lat = _final_norm(xmid, f, modall, row2(ln2_g[l]), row2(ln2_b[l]), T, alpha,
                            T if last else T + Tc)
        x_ctx, ctx_block0 = x_lat, ntl
    return x_lat
```

```python
import functools
import math

import jax
import jax.numpy as jnp
from jax import lax
from jax.experimental import pallas as pl
from jax.experimental.pallas import tpu as pltpu

F32 = jnp.float32
BF16 = jnp.bfloat16
HIGHEST = lax.Precision.HIGHEST

D_MODEL = 1024
GRID_W = 64
DA_HEADS = 4
DA_QK_DIM = 64
DA_V_DIM = 128
DA_WIDTH = 512
DA_QK_COLS = 512
ROPE_BASE = 10000.0
DA_EPS = 1e-5
RW_HEAD = 64
RW_HEADS = 8
RW_WIDTH = 512
RW_COLS = 1792
RW_GN_EPS = 64e-5
SG_CHUNK = 128
SG_GROUPS = 4
SG_WIDTH = 512
DA_K0 = 512
DA_V0 = 1024
RW_0 = 1536
SG_0 = RW_0 + RW_COLS
GATE_0 = SG_0 + 2 * SG_WIDTH
N_EXPERTS = 16
EC_CAPACITY = 2
LN_EPS = 1e-5
LOG2_E = math.log2(math.e)

TOKEN_TILE = 256
SCAN_CHUNK = 64
ATTN_KEY_TILE = 256
ATTN_HEADS_PER_STEP = 4
SCAN_HEADS_PER_PASS = 4
SCAN_BATCH = 4
MERGE_TILE = 512
MERGE_ROWS = 128
RANK_BLOCK = 256
RANK_BATCH = 4
LN2_TILES = (1024, 768, 512, 256)
VMEM_LIMIT_BYTES = 58 * 1024 * 1024


def _dot(a, b):
    return jnp.dot(a, b, preferred_element_type=F32)


def _dot_hi(a, b):
    return jnp.dot(a, b, preferred_element_type=F32, precision=HIGHEST)


def _dot_nt(a, b):
    return lax.dot_general(a, b, (((1,), (1,)), ((), ())), preferred_element_type=F32)


def _dot_tn(a, b):
    return lax.dot_general(a, b, (((0,), (0,)), ((), ())), preferred_element_type=F32)


def _sigmoid(z):
    return 1.0 / (1.0 + jnp.exp(-z))


def _seg_sum(z, ones_bd):
    rows = z.shape[0]
    w = ones_bd.shape[0]
    hi = z.astype(BF16)
    lo = (z - hi.astype(F32)).astype(BF16)
    cols = []
    for c in range(0, z.shape[1], w):
        st = _dot(jnp.concatenate([hi[:, c:c + w], lo[:, c:c + w]], axis=0), ones_bd)
        cols.append(st[:rows] + st[rows:])
    return jnp.concatenate(cols, axis=1)


def _mod_kernel(c_ref, w_ref, b_ref, o_ref):
    cc = c_ref[...]
    o_ref[...] = _dot_hi(cc * _sigmoid(cc), w_ref[...]) + b_ref[...]


def _modulation(cc, w_mod, b_mod):
    rows, d = cc.shape
    n = w_mod.shape[1]
    tn = 1024
    return pl.pallas_call(
        _mod_kernel,
        grid=(n // tn,),
        in_specs=[pl.BlockSpec((rows, d), lambda j: (0, 0)),
                  pl.BlockSpec((d, tn), lambda j: (0, j)),
                  pl.BlockSpec((1, tn), lambda j: (0, j))],
        out_specs=pl.BlockSpec((rows, tn), lambda j: (0, j)),
        out_shape=jax.ShapeDtypeStruct((rows, n), F32),
        name="adaln_mod",
    )(cc, w_mod, b_mod.reshape(1, n))


def _inproj_kernel(xl_ref, xc_ref, lp_ref, ln_ref, sh_ref, sc_ref, w_ref, cos_ref, sin_ref, sgg_ref, sgb_ref,
                   sgw_ref, sgbias_ref, *rest, ntl, nt):
    rw_params, (q_ref, k_ref, v_ref, sg_ref, gate_ref), rw_outs = rest[:10], rest[10:15], rest[15:]
    tm = xl_ref.shape[1]
    i = pl.program_id(1)
    x = jnp.where(i < ntl, xl_ref[0], xc_ref[0])
    h = (x * (1.0 + sc_ref[0, 0]) + sh_ref[0, 0]).astype(BF16)

    def proj(c0, c1):
        return _dot(h, w_ref[0, :, c0:c1])

    cos = cos_ref[...]
    sin = sin_ref[...]
    lane = lax.broadcasted_iota(jnp.int32, (tm, DA_QK_COLS), 1)
    first = (lane % 32) < 16

    def rope(z):
        zr = jnp.where(first, pltpu.roll(z, DA_QK_COLS - 16, 1), pltpu.roll(z, 16, 1))
        return z * cos + zr * sin

    halo = jnp.concatenate([lp_ref[0], ln_ref[0]], axis=0)
    h_halo = (halo * (1.0 + sc_ref[0, 0]) + sh_ref[0, 0]).astype(BF16)
    p_ext = _dot(jnp.concatenate([h, h_halo], axis=0), w_ref[0, :, RW_0:SG_0])
    prev_ok = jnp.logical_and(i != 0, i != ntl)
    next_ok = jnp.logical_and(i != ntl - 1, i != nt - 1)
    _rwkv_feature_math(p_ext[:tm], jnp.where(prev_ok, p_ext[tm + 7:tm + 8], 0.0),
                       jnp.where(next_ok, p_ext[tm + 8:tm + 9], 0.0), rw_params, rw_outs)
    q_ref[0] = (rope(proj(0, DA_K0)) * (DA_QK_DIM ** -0.5 * LOG2_E)).astype(BF16)
    k_ref[0] = rope(proj(DA_K0, DA_V0)).astype(BF16)
    v_ref[0] = proj(DA_V0, RW_0).astype(BF16)
    for j in range(3):
        gate_ref[0, :, j * D_MODEL:(j + 1) * D_MODEL] = _sigmoid(
            proj(GATE_0 + j * D_MODEL, GATE_0 + (j + 1) * D_MODEL)).astype(BF16)

    ps = proj(SG_0, GATE_0)
    gl = ps * (0.5 * (1.0 + jnp.tanh(math.sqrt(2.0 / math.pi) * (ps + 0.044715 * (ps * ps * ps)))))
    u = gl[:, :SG_WIDTH]
    vv = gl[:, SG_WIDTH:]
    mu = jnp.mean(vv, axis=-1, keepdims=True)
    dv = vv - mu
    var = jnp.mean(dv * dv, axis=-1, keepdims=True)
    vn = (dv * lax.rsqrt(var + LN_EPS) * sgg_ref[...] + sgb_ref[...]).astype(BF16)
    gd = SG_WIDTH // SG_GROUPS
    for n in range(tm // SG_CHUNK):
        r0 = n * SG_CHUNK
        for g in range(SG_GROUPS):
            c0 = g * gd
            vm = _dot(sgw_ref[g], vn[r0:r0 + SG_CHUNK, c0:c0 + gd]) + sgbias_ref[:, c0:c0 + gd]
            sg_ref[0, r0:r0 + SG_CHUNK, c0:c0 + gd] = (u[r0:r0 + SG_CHUNK, c0:c0 + gd] * vm).astype(BF16)


def _stream_specs(d, ntl, ctx_block0):
    tm = TOKEN_TILE
    lat = pl.BlockSpec((1, tm, d), lambda b, i: (b, jnp.minimum(i, ntl - 1), 0))
    ctx = pl.BlockSpec((1, tm, d), lambda b, i: (b, ctx_block0 + jnp.maximum(i - ntl, 0), 0))
    return [lat, ctx]


def _input_projection(x_lat, x_ctx, ctx_block0, TT, modall, w_in, layer, cos, sin, sgg, sgb, sgw, sgbias,
                      rw_params, ntl):
    B, lat_rows, D = x_lat.shape
    tm = TOKEN_TILE
    nt = TT // tm
    r8 = tm // 8
    last8 = lat_rows // 8 - 1
    W = RW_WIDTH
    tok = lambda w: pl.BlockSpec((1, tm, w), lambda b, i: (b, i, 0))
    modspec = lambda j: pl.BlockSpec((1, 1, 1, D), lambda b, i: (b, i // ntl, 0, j))
    const = lambda a: pl.BlockSpec(a.shape, lambda b, i: (0,) * a.ndim)
    halo_prev = pl.BlockSpec((1, 8, D), lambda b, i: (b, jnp.maximum(jnp.minimum(i, ntl - 1) * r8 - 1, 0), 0))
    halo_next = pl.BlockSpec((1, 8, D), lambda b, i: (b, jnp.minimum((jnp.minimum(i, ntl - 1) + 1) * r8, last8), 0))
    return pl.pallas_call(
        functools.partial(_inproj_kernel, ntl=ntl, nt=nt),
        grid=(B, nt),
        in_specs=_stream_specs(D, ntl, ctx_block0) + [halo_prev, halo_next, modspec(0), modspec(1),
                  pl.BlockSpec((1,) + w_in.shape[1:], lambda b, i: (layer, 0, 0),
                               pipeline_mode=pl.Buffered(1)),
                  pl.BlockSpec((tm, DA_QK_COLS), lambda b, i: (i, 0)),
                  pl.BlockSpec((tm, DA_QK_COLS), lambda b, i: (i, 0)),
                  const(sgg), const(sgb), const(sgw), const(sgbias)]
                 + [const(a) for a in rw_params],
        out_specs=[tok(DA_QK_COLS), tok(DA_QK_COLS), tok(DA_WIDTH), tok(SG_WIDTH), tok(3 * D)] + [tok(W)] * 11,
        out_shape=[jax.ShapeDtypeStruct((B, TT, DA_QK_COLS), BF16),
                   jax.ShapeDtypeStruct((B, TT, DA_QK_COLS), BF16),
                   jax.ShapeDtypeStruct((B, TT, DA_WIDTH), BF16),
                   jax.ShapeDtypeStruct((B, TT, SG_WIDTH), BF16),
                   jax.ShapeDtypeStruct((B, TT, 3 * D), BF16)]
                  + [jax.ShapeDtypeStruct((B, TT, W), F32)] * 2 + [jax.ShapeDtypeStruct((B, TT, W), BF16)] * 9,
        compiler_params=pltpu.CompilerParams(
            dimension_semantics=("parallel", "parallel"), vmem_limit_bytes=VMEM_LIMIT_BYTES),
        name="in_proj",
    )(x_lat, x_ctx, x_lat, x_lat, modall, modall, w_in, cos, sin, sgg, sgb, sgw, sgbias, *rw_params)


def _attn_kernel(q_ref, k_ref, v_ref, lam_ref, g_ref, o_ref, *, ntl, t_lat, lam_init):
    i = pl.program_id(2)
    lp = lam_ref[...]
    lam = (jnp.exp(jnp.sum(lp[0:1] * lp[1:2], axis=-1, keepdims=True))
           - jnp.exp(jnp.sum(lp[2:3] * lp[3:4], axis=-1, keepdims=True)) + lam_init)
    dv = DA_V_DIM
    heads = [slice(h * dv, (h + 1) * dv) for h in range(q_ref.shape[2] // dv)]
    lane = lax.broadcasted_iota(jnp.int32, (q_ref.shape[1], dv), 1)
    qs = []
    for hs in heads:
        q = q_ref[0, :, hs]
        zero = jnp.zeros_like(q)
        qs.append((jnp.where(lane < DA_QK_DIM, q, zero), jnp.where(lane >= DA_QK_DIM, q, zero)))

    def attend(k0, nk):
        kt = ATTN_KEY_TILE
        tiles = [slice(k0 + t * kt, k0 + (t + 1) * kt) for t in range(nk // kt)]

        def row_max(ss):
            m = ss[0]
            for s in ss[1:]:
                m = jnp.maximum(m, s)
            return jnp.max(m, axis=-1, keepdims=True)

        def pv(es, hs):
            acc = None
            for e, sl in zip(es, tiles):
                v = v_ref[0, sl, hs]
                d = _dot(e, jnp.concatenate([v, jnp.ones_like(v)], axis=1))
                acc = d if acc is None else acc + d
            return acc[:, :dv] / acc[:, dv:]

        ks = [[k_ref[0, sl, hs] for sl in tiles] for hs in heads]
        s0 = [[_dot_nt(q0, kk) for kk in kh] for (q0, _), kh in zip(qs, ks)]
        m0 = [row_max(s) for s in s0]
        s1, e0 = [], []
        for (_, q1), kh, sh, mh in zip(qs, ks, s0, m0):
            s1.append([])
            e0.append([])
            for kk, s in zip(kh, sh):
                s1[-1].append(_dot_nt(q1, kk))
                e0[-1].append(jnp.exp2(s - mh).astype(BF16))
        m1 = [row_max(s) for s in s1]
        o0 = [pv(e, hs) for e, hs in zip(e0, heads)]
        e1 = [[jnp.exp2(s - mh).astype(BF16) for s in sh] for sh, mh in zip(s1, m1)]
        for hs, oa, e in zip(heads, o0, e1):
            o = oa - lam * pv(e, hs)
            o = o * lax.rsqrt(jnp.mean(o * o, axis=-1, keepdims=True) + DA_EPS) * g_ref[...]
            o_ref[0, :, hs] = (o * (1.0 - lam_init)).astype(BF16)

    @pl.when(i < ntl)
    def _():
        attend(0, k_ref.shape[1])

    @pl.when(i >= ntl)
    def _():
        attend(t_lat, k_ref.shape[1] - t_lat)


def _diff_attention(q, k, v, lam_p, norm_g, ntl, lam_init, nt):
    B, TT, _ = q.shape
    tm = TOKEN_TILE
    w = ATTN_HEADS_PER_STEP * DA_V_DIM
    kv = pl.BlockSpec((1, TT, w), lambda b, h, i: (b, 0, h))
    qo = pl.BlockSpec((1, tm, w), lambda b, h, i: (b, i, h))
    return pl.pallas_call(
        functools.partial(_attn_kernel, ntl=ntl, t_lat=ntl * tm, lam_init=lam_init),
        grid=(B, DA_HEADS // ATTN_HEADS_PER_STEP, nt),
        in_specs=[qo, kv, kv,
                  pl.BlockSpec(lam_p.shape, lambda b, h, i: (0, 0)),
                  pl.BlockSpec((1, DA_V_DIM), lambda b, h, i: (0, 0))],
        out_specs=qo,
        out_shape=jax.ShapeDtypeStruct((B, TT, DA_WIDTH), BF16),
        compiler_params=pltpu.CompilerParams(
            dimension_semantics=("parallel", "parallel", "parallel"),
            vmem_limit_bytes=VMEM_LIMIT_BYTES),
        name="diff_attn",
    )(q, k, v, lam_p, norm_g.reshape(1, DA_V_DIM))


def _rwkv_feature_math(p, prev_row, next_row, params, outs):
    mu_ref, w0_ref, w2_ref, a0_ref, a2_ref, kk_ref, ka_ref, rk_ref, g2_ref, ones_ref = params
    lw0_ref, lw1_ref, kd0_ref, kd1_ref, b0_ref, b1_ref, kko_ref, v_ref, r_ref, bonus_ref, g_ref = outs
    tm = p.shape[0]
    row = lax.broadcasted_iota(jnp.int32, p.shape, 0)
    prev = jnp.where(row == 0, prev_row, pltpu.roll(p, 1, 0))
    nxt = jnp.where(row == tm - 1, next_row, pltpu.roll(p, tm - 1, 0))
    ps = p + mu_ref[0:1, :] * (prev - p) + mu_ref[1:2, :] * (nxt - p)

    W = RW_WIDTH
    r = ps[:, 0:W]
    k = ps[:, W:2 * W]
    v = ps[:, 2 * W:3 * W]
    xw = ps[:, 3 * W:3 * W + 64]
    xa = ps[:, 3 * W + 64:3 * W + 128]
    xg = ps[:, 3 * W + 128:3 * W + 256]
    ones_bd = ones_ref[...]

    w_pre = _dot(jnp.tanh(xw).astype(BF16), w2_ref[...]) + w0_ref[...]
    logw = -math.exp(-0.5) * _sigmoid(w_pre)
    a = _sigmoid(_dot(xa.astype(BF16), a2_ref[...]) + a0_ref[...])

    kx = k * kk_ref[...]
    nrm = jnp.sqrt(_seg_sum(kx * kx, ones_bd))
    kk = kx / jnp.maximum(nrm, 1e-12)
    ka = ka_ref[...]
    kd0 = k * (1.0 + (a[:, :W] - 1.0) * ka)
    kd1 = k * (1.0 + (a[:, W:] - 1.0) * ka)
    g = _dot(_sigmoid(xg).astype(BF16), g2_ref[...])
    k_b = 0.5 * (kd0 + kd1)
    bonus = _seg_sum(r * k_b * rk_ref[...], ones_bd) * v

    lw0_ref[0] = logw[:, :W]
    lw1_ref[0] = logw[:, W:]
    kd0_ref[0] = kd0.astype(BF16)
    kd1_ref[0] = kd1.astype(BF16)
    b0_ref[0] = (kk * a[:, :W]).astype(BF16)
    b1_ref[0] = (kk * a[:, W:]).astype(BF16)
    kko_ref[0] = kk.astype(BF16)
    v_ref[0] = v.astype(BF16)
    r_ref[0] = r.astype(BF16)
    bonus_ref[0] = bonus.astype(BF16)
    g_ref[0] = g.astype(BF16)


def _split_bf16(x, pieces):
    out = []
    for _ in range(pieces - 1):
        hi = x.astype(BF16)
        out.append(hi)
        x = x - hi.astype(F32)
    out.append(x.astype(BF16))
    return out


def _block_diag(x, groups):
    xb = x.astype(BF16)
    rows, lanes = xb.shape
    t = jnp.concatenate([xb] * groups, axis=0)
    ri = lax.broadcasted_iota(jnp.int32, t.shape, 0) // rows
    li = lax.broadcasted_iota(jnp.int32, t.shape, 1) // (lanes // groups)
    return jnp.where(ri == li, t, jnp.zeros_like(t))


def _scan_kernel(lw0, kd0, b0, kkf, vf, rf, lw1, kd1, b1, kkr, vr, rr, wg_ref, wu_ref, wd_ref,
                 y0_ref, y1_ref, wgo_ref, wuo_ref, wdo_ref, s_ref):
    step = pl.program_id(1)

    @pl.when(step == 0)
    def _():
        s_ref[...] = jnp.zeros_like(s_ref)

    for src, dst in ((wg_ref, wgo_ref), (wu_ref, wuo_ref), (wd_ref, wdo_ref)):
        dst[...] = src[...].astype(dst.dtype)

    C = lw0.shape[1]
    G = SCAN_HEADS_PER_PASS
    L = G * RW_HEAD
    assert C == RW_HEAD
    ti = lax.broadcasted_iota(jnp.int32, (C, L), 0)
    si = lax.broadcasted_iota(jnp.int32, (C, L), 1) % C
    tc = lax.broadcasted_iota(jnp.int32, (C, C), 0)
    sc = lax.broadcasted_iota(jnp.int32, (C, C), 1)
    same_head = (lax.broadcasted_iota(jnp.int32, (L, L), 0) // RW_HEAD
                 == lax.broadcasted_iota(jnp.int32, (L, L), 1) // RW_HEAD)

    groups = []
    for z, (lw, kd, b, kk, v, r) in enumerate(((lw0, kd0, b0, kkf, vf, rf), (lw1, kd1, b1, kkr, vr, rr))):
        rev = z == 1
        incl = (si >= ti) if rev else (si <= ti)
        strict = (si > ti) if rev else (si < ti)
        tri = jnp.where((sc >= tc) if rev else (sc <= tc), 1.0, 0.0).astype(BF16)
        for n in range(lw.shape[0]):
            logw = lw[n]
            cum = sum(_dot(tri, piece) for piece in _split_bf16(logw, 3))
            tot = jnp.sum(logw, axis=0, keepdims=True)
            p_in = jnp.exp(-cum)
            p_end = jnp.exp(tot - cum)
            kkz, rz, bz, kdz = (t[n].astype(F32) for t in (kk, r, b, kd))
            lhs = jnp.concatenate([-kkz * jnp.exp(cum - logw), rz * jnp.exp(cum)], axis=0)
            rhs_b = bz * p_in
            rhs_k = kdz * p_in
            upd = jnp.concatenate([bz * p_end, kdz * p_end], axis=0)
            dec = jnp.exp(tot)
            vz = v[n]
            for j in range(RW_HEADS // G):
                c = slice(j * L, (j + 1) * L)
                groups.append(dict(z=z, smp=n, j=j, lhs=lhs[:, c].astype(BF16), rhs_b=rhs_b[:, c],
                                   rhs_k=rhs_k[:, c], upd=upd[:, c].astype(BF16), dec=dec[:, c],
                                   v=vz[:, c], strict=strict, incl=incl))

    for g in groups:
        aa_b = _dot_nt(g["lhs"], _block_diag(g["rhs_b"], G))
        aa_k = _dot_nt(g["lhs"], _block_diag(g["rhs_k"], G))
        g["a_k"] = jnp.concatenate([jnp.where(g["strict"], aa_k[:C], 0.0),
                                    jnp.where(g["incl"], aa_k[C:], 0.0)], axis=0).astype(BF16)
        g["a_rb"] = jnp.where(g["incl"], aa_b[C:], 0.0).astype(BF16)
        g["n"] = jnp.where(g["strict"], aa_b[:C], 0.0)
        g["apow"] = _dot(g["n"].astype(BF16), _block_diag(g["n"], G))
    levels = int(math.log2(C)) - 1
    for lv in range(levels):
        for g in groups:
            pbd = _block_diag(g["apow"], G)
            if lv < levels - 1:
                st = _dot(jnp.concatenate([g["n"], g["apow"]], axis=0).astype(BF16), pbd)
                g["n"] = g["n"] + g["apow"] + st[:C]
                g["apow"] = st[C:]
            else:
                g["n"] = g["n"] + g["apow"] + _dot(g["n"].astype(BF16), pbd)
    for g in groups:
        g["s0"] = s_ref[g["smp"], g["z"], g["j"]]
        g["gh"] = _dot_nt(g["lhs"], g["s0"].astype(BF16))
    for g in groups:
        st = _dot(g["a_k"], _block_diag(g["v"], G))
        g["w"] = g["gh"][:C] + st[:C]
        g["yk"] = st[C:]
    for g in groups:
        g["u"] = g["w"] + _dot(g["n"].astype(BF16), _block_diag(g["w"], G))
    for g in groups:
        g["y"] = g["gh"][C:] + g["yk"] + _dot(g["a_rb"], _block_diag(g["u"], G))
    for g in groups:
        uv = jnp.concatenate([g["u"].astype(BF16), g["v"]], axis=0)
        s_ref[g["smp"], g["z"], g["j"]] = g["s0"] * g["dec"] + jnp.where(same_head, _dot_tn(uv, g["upd"]), 0.0)
    for z, y_ref in enumerate((y0_ref, y1_ref)):
        for n in range(y_ref.shape[0]):
            y_ref[n] = jnp.concatenate([g["y"] for g in groups if g["z"] == z and g["smp"] == n],
                                       axis=-1).astype(y_ref.dtype)


def _pow2_at_least(n):
    return 1 << max(n - 1, 0).bit_length()


def _rwkv_scan(lw0, lw1, kd0, kd1, b0, b1, kk, v, r, t_lat, expert_weights, layer):
    B, TT, W = kk.shape
    C = SCAN_CHUNK
    ncl = t_lat // C
    nch = TT // C
    ncc = nch - ncl

    def fwd(s):
        return jnp.where(s < ncc, ncl + s, s - ncc)

    nb = math.gcd(B, SCAN_BATCH)
    fs = pl.BlockSpec((nb, C, W), lambda b, s: (b, fwd(s), 0))
    rs = pl.BlockSpec((nb, C, W), lambda b, s: (b, nch - 1 - s, 0))

    n_steps = (B // nb) * nch
    w_views, w_in_specs, w_out_specs, w_out_shapes, w_shapes = [], [], [], [], []
    for w in expert_weights:
        n_layers, n_exp, d_in, d_out = w.shape
        rows = n_exp * d_in
        slab = _pow2_at_least(-(-rows // n_steps))
        assert rows % slab == 0
        n_slabs = rows // slab
        w_views.append(w.reshape(n_layers * n_slabs, slab, d_out))
        pick = lambda b, s, n=n_slabs: jnp.minimum(b * nch + s, n - 1)
        w_in_specs.append(pl.BlockSpec((1, slab, d_out), lambda b, s, p=pick, n=n_slabs: (layer * n + p(b, s), 0, 0)))
        w_out_specs.append(pl.BlockSpec((1, slab, d_out), lambda b, s, p=pick: (p(b, s), 0, 0)))
        w_out_shapes.append(jax.ShapeDtypeStruct((n_slabs, slab, d_out), BF16))
        w_shapes.append((n_exp, d_in, d_out))
    outs = pl.pallas_call(
        _scan_kernel,
        grid=(B // nb, nch),
        in_specs=[fs] * 6 + [rs] * 6 + w_in_specs,
        out_specs=[fs, rs] + w_out_specs,
        out_shape=[jax.ShapeDtypeStruct((B, TT, W), BF16)] * 2 + w_out_shapes,
        scratch_shapes=[pltpu.VMEM((nb, 2, RW_HEADS // SCAN_HEADS_PER_PASS, SCAN_HEADS_PER_PASS * RW_HEAD,
                                    SCAN_HEADS_PER_PASS * RW_HEAD), F32)],
        compiler_params=pltpu.CompilerParams(
            dimension_semantics=("arbitrary", "arbitrary"), vmem_limit_bytes=VMEM_LIMIT_BYTES),
        name="rwkv_scan",
    )(lw0, kd0, b0, kk, v, r, lw1, kd1, b1, kk, v, r, *w_views)
    return outs[0], outs[1], [o.reshape(shp) for o, shp in zip(outs[2:], w_shapes)]


def _merge_kernel(x_ref, yda_ref, y0_ref, y1_ref, bonus_ref, g_ref, ysg_ref, gate_ref,
                  g1_ref, sh2_ref, sc2_ref, wb_ref, wo_ref, ln1g_ref, ln1b_ref, rlng_ref, rlnb_ref,
                  ones_ref, wr_ref, *rest, alpha):
    xmid_ref, h2_ref, aff_ref = rest[-3:]
    D = D_MODEL
    tm = x_ref.shape[1]
    subs = [slice(r, r + MERGE_ROWS) for r in range(0, tm, MERGE_ROWS)]
    ones_bd = ones_ref[...]
    wr_hi, wr_lo = _split_bf16(wr_ref[...], 2)
    wr_hl = jnp.concatenate([wr_hi, wr_lo], axis=0)

    yrw = []
    for sl in subs:
        y = y0_ref[0, sl, :].astype(F32) + y1_ref[0, sl, :].astype(F32)
        mu = _seg_sum(y, ones_bd) * (1.0 / RW_HEAD)
        dy = y - mu
        var = _seg_sum(dy * dy, ones_bd) * (1.0 / RW_HEAD)
        gn = dy * lax.rsqrt(var + RW_GN_EPS) * rlng_ref[...] + rlnb_ref[...]
        yrw.append(((gn + bonus_ref[0, sl, :].astype(F32)) * g_ref[0, sl, :].astype(F32)).astype(BF16))
    ms = []
    for sl, yr in zip(subs, yrw):
        m = gate_ref[0, sl, 0:D].astype(F32) * _dot(yda_ref[0, sl, :], wb_ref[0:DA_WIDTH, :])
        m = m + gate_ref[0, sl, D:2 * D].astype(F32) * _dot(yr, wb_ref[DA_WIDTH:DA_WIDTH + RW_WIDTH, :])
        m = m + gate_ref[0, sl, 2 * D:3 * D].astype(F32) * _dot(ysg_ref[0, sl, :], wb_ref[DA_WIDTH + RW_WIDTH:, :])
        ms.append(m.astype(BF16))
    mixes = [_dot(m, wo_ref[...]) for m in ms]
    h2s = []
    for sl, mix in zip(subs, mixes):
        z = alpha * x_ref[0, sl, :] + g1_ref[0, 0] * mix
        zm = jnp.mean(z, axis=-1, keepdims=True)
        dz = z - zm
        zv = jnp.mean(dz * dz, axis=-1, keepdims=True)
        xmid = dz * lax.rsqrt(zv + LN_EPS) * ln1g_ref[...] + ln1b_ref[...]
        xmid_ref[0, sl, :] = xmid
        h2 = xmid * (1.0 + sc2_ref[0, 0]) + sh2_ref[0, 0]
        h2_ref[0, sl, :] = h2.astype(BF16)
        h2s.append(h2)
    ne = wr_hi.shape[0]
    for sl, h2 in zip(subs, h2s):
        h_hi, h_lo = _split_bf16(h2, 2)
        part = _dot_nt(wr_hl, h_hi)
        logits = part[:ne] + part[ne:] + _dot_nt(wr_hi, h_lo)
        e = jnp.exp(logits - jnp.max(logits, axis=0, keepdims=True))
        aff_ref[0, :, sl] = e / jnp.sum(e, axis=0, keepdims=True)


def _merge(x_src, x_block0, tile, row0, n_tiles, mod_row, prev_outs, yda, y0, y1, bonus, g, ysg, gates, modall,
           w_branch, w_out, ln1g, ln1b, rlng, rlnb, ones_bd, w_router_t, alpha):
    B, TT, _ = yda.shape
    D = x_src.shape[2]
    blk0 = row0 // tile
    tok = lambda w: pl.BlockSpec((1, tile, w), lambda b, i: (b, blk0 + i, 0))
    modspec = lambda j: pl.BlockSpec((1, 1, 1, D), lambda b, i: (b, mod_row, 0, j))
    const = lambda a: pl.BlockSpec(a.shape, lambda b, i: (0,) * a.ndim)
    consts = [w_branch, w_out, ln1g, ln1b, rlng, rlnb, ones_bd, w_router_t]
    n_in = 11 + len(consts)
    prev = list(prev_outs) if prev_outs is not None else []
    return pl.pallas_call(
        functools.partial(_merge_kernel, alpha=alpha),
        grid=(B, n_tiles),
        in_specs=[pl.BlockSpec((1, tile, D), lambda b, i: (b, x_block0 + i, 0)),
                  tok(DA_WIDTH), tok(RW_WIDTH), tok(RW_WIDTH), tok(RW_WIDTH), tok(RW_WIDTH),
                  tok(SG_WIDTH), tok(3 * D), modspec(2), modspec(3), modspec(4)]
                 + [const(a) for a in consts] + [pl.BlockSpec(memory_space=pl.ANY)] * len(prev),
        out_specs=[tok(D), tok(D), pl.BlockSpec((1, N_EXPERTS, tile), lambda b, i: (b, 0, blk0 + i))],
        out_shape=[jax.ShapeDtypeStruct((B, TT, D), F32),
                   jax.ShapeDtypeStruct((B, TT, D), BF16),
                   jax.ShapeDtypeStruct((B, N_EXPERTS, TT), F32)],
        input_output_aliases={n_in + j: j for j in range(len(prev))},
        compiler_params=pltpu.CompilerParams(
            dimension_semantics=("parallel", "parallel"), vmem_limit_bytes=VMEM_LIMIT_BYTES),
        name="merge_ln1_router",
    )(x_src, yda, y0, y1, bonus, g, ysg, gates, modall, modall, modall, *consts, *prev)


def _topk_kernel(aff_ref, tri_ref, rank_ref, *, cap):
    a = aff_ref[...].reshape(-1, aff_ref.shape[2])
    bits = pltpu.bitcast(a, jnp.int32)
    thr = jnp.zeros((a.shape[0], 1), jnp.int32)
    for bit in range(30, -1, -1):
        cand = thr | (1 << bit)
        cnt = jnp.sum(jnp.where(bits >= cand, 1.0, 0.0), axis=-1, keepdims=True)
        thr = jnp.where(cnt >= cap, cand, thr)
    gt = bits > thr
    eq = bits == thr
    need = cap - jnp.sum(jnp.where(gt, 1.0, 0.0), axis=-1, keepdims=True)
    tri = tri_ref[...]
    kb = tri.shape[0]

    def excl_cumsum(mask):
        m = jnp.where(mask, 1.0, 0.0)
        outs = []
        carry = jnp.zeros((m.shape[0], 1), F32)
        for j in range(m.shape[1] // kb):
            blk = m[:, j * kb:(j + 1) * kb]
            outs.append(_dot(blk.astype(BF16), tri) + carry)
            carry = carry + jnp.sum(blk, axis=-1, keepdims=True)
        return jnp.concatenate(outs, axis=1)

    sel = jnp.logical_or(gt, jnp.logical_and(eq, excl_cumsum(eq) < need))
    rank_ref[...] = jnp.where(sel, excl_cumsum(sel), -1.0).astype(jnp.int32).reshape(rank_ref.shape)


def _topk_ranks(aff, tri, t_off, t_len, cap):
    B, E, _ = aff.shape
    blk = t_off // t_len
    nb = math.gcd(B, RANK_BATCH)
    return pl.pallas_call(
        functools.partial(_topk_kernel, cap=cap),
        grid=(B // nb,),
        in_specs=[pl.BlockSpec((nb, E, t_len), lambda b: (b, 0, blk)),
                  pl.BlockSpec(tri.shape, lambda b: (0, 0))],
        out_specs=pl.BlockSpec((nb, E, t_len), lambda b: (b, 0, 0)),
        out_shape=jax.ShapeDtypeStruct((B, E, t_len), jnp.int32),
        compiler_params=pltpu.CompilerParams(
            dimension_semantics=("parallel",), vmem_limit_bytes=VMEM_LIMIT_BYTES),
        name="expert_choice_ranks",
    )(aff, tri)


def _moe_kernel(*refs, sets):
    n = len(sets)
    h_ref, aff_ref = refs[0], refs[1]
    rank_refs = refs[2:2 + n]
    wg_ref, wu_ref, wd_ref, f_ref, acc_ref = refs[2 + n:]
    e = pl.program_id(1)

    @pl.when(e == 0)
    def _():
        acc_ref[...] = jnp.zeros_like(acc_ref)

    onehots, gates, xs = [], [], []
    for (t0, tn, cap), rank_ref in zip(sets, rank_refs):
        rank = rank_ref[0, pl.ds(e, 1), :]
        aff = aff_ref[0, pl.ds(e, 1), t0:t0 + tn]
        slot = lax.broadcasted_iota(jnp.int32, (cap, tn), 0)
        hit = rank == slot
        onehot = jnp.where(hit, 1.0, 0.0).astype(BF16)
        onehots.append(onehot)
        gates.append(jnp.sum(jnp.where(hit, aff, 0.0), axis=-1, keepdims=True))
        xs.append(_dot(onehot, h_ref[0, t0:t0 + tn, :]).astype(BF16))
    xe = jnp.concatenate(xs, axis=0) if n > 1 else xs[0]
    gate = jnp.concatenate(gates, axis=0) if n > 1 else gates[0]
    hg = _dot(xe, wg_ref[0])
    hid = (hg * _sigmoid(hg)) * _dot(xe, wu_ref[0])
    ye = (_dot(hid.astype(BF16), wd_ref[0]) * gate).astype(BF16)
    r0 = 0
    for (t0, tn, cap), onehot in zip(sets, onehots):
        acc_ref[t0:t0 + tn, :] += _dot_tn(onehot, ye[r0:r0 + cap])
        r0 += cap

    @pl.when(e == pl.num_programs(1) - 1)
    def _():
        f_ref[0] = acc_ref[...].astype(f_ref.dtype)


def _expert_ffn(h2, aff, ranks, wg, wu, wd, sets):
    B, TT, D = h2.shape
    E = aff.shape[1]
    F = wg.shape[2]
    return pl.pallas_call(
        functools.partial(_moe_kernel, sets=sets),
        grid=(B, E),
        in_specs=[pl.BlockSpec((1, TT, D), lambda b, e: (b, 0, 0), pipeline_mode=pl.Buffered(1)),
                  pl.BlockSpec((1, E, TT), lambda b, e: (b, 0, 0))]
                 + [pl.BlockSpec((1, E, r.shape[2]), lambda b, e: (b, 0, 0)) for r in ranks]
                 + [pl.BlockSpec((1, D, F), lambda b, e: (e, 0, 0)),
                    pl.BlockSpec((1, D, F), lambda b, e: (e, 0, 0)),
                    pl.BlockSpec((1, F, D), lambda b, e: (e, 0, 0))],
        out_specs=pl.BlockSpec((1, TT, D), lambda b, e: (b, 0, 0)),
        out_shape=jax.ShapeDtypeStruct((B, TT, D), BF16),
        scratch_shapes=[pltpu.VMEM((TT, D), F32)],
        compiler_params=pltpu.CompilerParams(
            dimension_semantics=("parallel", "arbitrary"), vmem_limit_bytes=VMEM_LIMIT_BYTES),
        name="expert_ffn",
    )(h2, aff, *ranks, wg, wu, wd)


def _ln2_kernel(x_ref, f_ref, g2_ref, lng_ref, lnb_ref, o_ref, *, alpha, t_lat):
    tile = x_ref.shape[1]
    row0 = pl.program_id(1) * tile
    for r in range(0, tile, TOKEN_TILE):
        sl = slice(r, r + TOKEN_TILE)
        g2 = jnp.where(row0 + r < t_lat, g2_ref[0, 0], g2_ref[0, 1])
        z = alpha * x_ref[0, sl, :] + g2 * f_ref[0, sl, :].astype(F32)
        zm = jnp.mean(z, axis=-1, keepdims=True)
        dz = z - zm
        zv = jnp.mean(dz * dz, axis=-1, keepdims=True)
        o_ref[0, sl, :] = dz * lax.rsqrt(zv + LN_EPS) * lng_ref[...] + lnb_ref[...]


def _final_norm(xmid, f, modall, lng, lnb, t_lat, alpha, rows):
    B, _, D = xmid.shape
    tile = next(t for t in LN2_TILES if rows % t == 0)
    tok = pl.BlockSpec((1, tile, D), lambda b, i: (b, i, 0))
    const = lambda a: pl.BlockSpec(a.shape, lambda b, i: (0,) * a.ndim)
    return pl.pallas_call(
        functools.partial(_ln2_kernel, alpha=alpha, t_lat=t_lat),
        grid=(B, rows // tile),
        in_specs=[tok, tok, pl.BlockSpec((1, 2, 1, D), lambda b, i: (b, 0, 0, 5)),
                  const(lng), const(lnb)],
        out_specs=tok,
        out_shape=jax.ShapeDtypeStruct((B, rows, D), F32),
        compiler_params=pltpu.CompilerParams(
            dimension_semantics=("parallel", "parallel"), vmem_limit_bytes=VMEM_LIMIT_BYTES),
        name="ln2",
    )(xmid, f, modall, lng, lnb)


def _rope_tables(t_lat, t_ctx):
    rows = t_lat // GRID_W
    row = jnp.repeat(jnp.arange(rows, dtype=F32), GRID_W)
    col = jnp.tile(jnp.arange(GRID_W, dtype=F32), rows)
    half = DA_QK_DIM // 2
    inv_freq = ROPE_BASE ** (-jnp.arange(0, half, 2, dtype=F32) / half)
    ar = row[:, None] * inv_freq
    ac = col[:, None] * inv_freq
    ang = jnp.concatenate([ar, ar, ac, ac], axis=-1)
    sign = jnp.where((jnp.arange(DA_QK_DIM) % 32) < 16, -1.0, 1.0).astype(F32)
    reps = DA_QK_COLS // DA_QK_DIM
    cos = jnp.tile(jnp.cos(ang), (1, reps))
    sin = jnp.tile(jnp.sin(ang) * sign, (1, reps))
    cos = jnp.concatenate([cos, jnp.ones((t_ctx, DA_QK_COLS), F32)], axis=0)
    sin = jnp.concatenate([sin, jnp.zeros((t_ctx, DA_QK_COLS), F32)], axis=0)
    return cos, sin


def kernel(x, c, ctx, c_ctx, w_mod, b_mod, w_in, da_lambda, da_norm_g, rw_shift_mu, rw_w0, rw_w2, rw_a0, rw_a2, rw_k_k, rw_k_a, rw_r_k, rw_ln_g, rw_ln_b, rw_g2, sg_norm_g, sg_norm_b, sg_w, sg_b, w_branch, w_out, ln1_g, ln1_b, w_router, w_e_gate, w_e_up, w_e_down, ln2_g, ln2_b):
    B, T, D = x.shape
    Tc = ctx.shape[1]
    depth = w_mod.shape[0]
    tm = TOKEN_TILE
    assert D == D_MODEL and T % tm == 0 and Tc == tm and T % MERGE_TILE == 0
    ntl = T // tm
    alpha = (2 * depth) ** 0.25
    cap_lat = EC_CAPACITY * T // N_EXPERTS
    cap_ctx = EC_CAPACITY * Tc // N_EXPERTS

    cos, sin = _rope_tables(T, Tc)
    lane = jnp.arange(SCAN_HEADS_PER_PASS * RW_HEAD)
    ones_bd = (lane[:, None] // RW_HEAD == lane[None, :] // RW_HEAD).astype(BF16)
    kb = math.gcd(Tc, RANK_BLOCK)
    tri = (jnp.arange(kb)[:, None] < jnp.arange(kb)[None, :]).astype(BF16)
    rows = ((B + 1 + 7) // 8) * 8
    cc = jnp.concatenate([c, c_ctx[None, :], jnp.zeros((rows - B - 1, D), F32)], axis=0)
    row2 = lambda a: a.reshape(1, -1)

    w_in_bf = w_in.astype(BF16)
    x_lat, x_ctx, ctx_block0 = x, ctx, 0
    for l in range(depth):
        last = l == depth - 1
        lam_init = 0.8 - 0.6 * math.exp(-0.3 * l)
        mod = _modulation(cc, w_mod[l], b_mod[l])
        modall = jnp.stack([mod[:B], jnp.broadcast_to(mod[B], (B, 6 * D))], axis=1)
        modall = modall.reshape(B, 2, 1, 6 * D)

        sgbias = jnp.repeat(sg_b[l].T, SG_WIDTH // SG_GROUPS, axis=1)
        cat2 = lambda a: jnp.transpose(a, (1, 0, 2)).reshape(a.shape[1], 2 * RW_WIDTH)
        rw_params = [rw_shift_mu[l], row2(rw_w0[l]), cat2(rw_w2[l]).astype(BF16), row2(rw_a0[l]),
                     cat2(rw_a2[l]).astype(BF16), row2(rw_k_k[l]), row2(rw_k_a[l]), row2(rw_r_k[l]),
                     rw_g2[l].astype(BF16), ones_bd]
        (q, k, v, ysg, gates, lw0, lw1, kd0, kd1, b0, b1, kk, vv, rr, bonus, gg) = _input_projection(
            x_lat, x_ctx, ctx_block0, T + Tc, modall, w_in_bf, l, cos, sin, row2(sg_norm_g[l]),
            row2(sg_norm_b[l]), sg_w[l].astype(BF16), sgbias, rw_params, ntl)

        nt_out = ntl if last else (T + Tc) // tm
        yda = _diff_attention(q, k, v, da_lambda[l], da_norm_g[l], ntl, lam_init, nt_out)

        y0, y1, (wg, wu, wd) = _rwkv_scan(lw0, lw1, kd0, kd1, b0, b1, kk, vv, rr, T,
                                          (w_e_gate, w_e_up, w_e_down), l)

        merge_args = (yda, y0, y1, bonus, gg, ysg, gates, modall, w_branch[l].astype(BF16),
                      w_out[l].astype(BF16), row2(ln1_g[l]), row2(ln1_b[l]), row2(rw_ln_g[l]),
                      row2(rw_ln_b[l]), ones_bd, w_router[l].T, alpha)
        outs = _merge(x_lat, 0, MERGE_TILE, 0, T // MERGE_TILE, 0, None, *merge_args)
        if not last:
            outs = _merge(x_ctx, ctx_block0, tm, T, Tc // tm, 1, outs, *merge_args)
        xmid, h2, aff = outs

        sets = ((0, T, cap_lat),) if last else ((0, T, cap_lat), (T, Tc, cap_ctx))
        ranks = [_topk_ranks(aff, tri, 0, T, cap_lat)]
        if not last:
            ranks.append(_topk_ranks(aff, tri, T, Tc, cap_ctx))
        f = _expert_ffn(h2, aff, ranks, wg, wu, wd, sets)
        x_lat = _final_norm(xmid, f, modall, row2(ln2_g[l]), row2(ln2_b[l]), T, alpha,
                            T if last else T + Tc)
        x_ctx, ctx_block0 = x_lat, ntl
    return x_lat
```

```python
import functools
import math

import jax
import jax.numpy as jnp
from jax import lax
from jax.experimental import pallas as pl
from jax.experimental.pallas import tpu as pltpu

F32 = jnp.float32
BF16 = jnp.bfloat16
HIGHEST = lax.Precision.HIGHEST

D_MODEL = 1024
GRID_W = 64
DA_HEADS = 4
DA_QK_DIM = 64
DA_V_DIM = 128
DA_WIDTH = 512
DA_QK_COLS = 512
ROPE_BASE = 10000.0
DA_EPS = 1e-5
RW_HEAD = 64
RW_HEADS = 8
RW_WIDTH = 512
RW_COLS = 1792
RW_GN_EPS = 64e-5
SG_CHUNK = 128
SG_GROUPS = 4
SG_WIDTH = 512
DA_K0 = 512
DA_V0 = 1024
RW_0 = 1536
SG_0 = RW_0 + RW_COLS
GATE_0 = SG_0 + 2 * SG_WIDTH
N_EXPERTS = 16
EC_CAPACITY = 2
LN_EPS = 1e-5
LOG2_E = math.log2(math.e)

TOKEN_TILE = 256
SCAN_CHUNK = 64
ATTN_KEY_TILE = 256
ATTN_HEADS_PER_STEP = 4
SCAN_HEADS_PER_PASS = 4
SCAN_BATCH = 4
MERGE_TILE = 512
MERGE_ROWS = 128
RANK_BLOCK = 256
RANK_BATCH = 4
LN2_TILES = (1024, 768, 512, 256)
VMEM_LIMIT_BYTES = 58 * 1024 * 1024


def _dot(a, b):
    return jnp.dot(a, b, preferred_element_type=F32)


def _dot_hi(a, b):
    return jnp.dot(a, b, preferred_element_type=F32, precision=HIGHEST)


def _dot_nt(a, b):
    return lax.dot_general(a, b, (((1,), (1,)), ((), ())), preferred_element_type=F32)


def _dot_tn(a, b):
    return lax.dot_general(a, b, (((0,), (0,)), ((), ())), preferred_element_type=F32)


def _sigmoid(z):
    return 1.0 / (1.0 + jnp.exp(-z))


def _seg_sum(z, ones_bd):
    rows = z.shape[0]
    w = ones_bd.shape[0]
    hi = z.astype(BF16)
    lo = (z - hi.astype(F32)).astype(BF16)
    cols = []
    for c in range(0, z.shape[1], w):
        st = _dot(jnp.concatenate([hi[:, c:c + w], lo[:, c:c + w]], axis=0), ones_bd)
        cols.append(st[:rows] + st[rows:])
    return jnp.concatenate(cols, axis=1)


def _mod_kernel(c_ref, w_ref, b_ref, o_ref):
    cc = c_ref[...]
    o_ref[...] = _dot_hi(cc * _sigmoid(cc), w_ref[0]) + b_ref[0]


def _modulation(cc, w_mod, b_mod, layer):
    rows, d = cc.shape
    n = w_mod.shape[2]
    tn = 1024
    return pl.pallas_call(
        _mod_kernel,
        grid=(n // tn,),
        in_specs=[pl.BlockSpec((rows, d), lambda j: (0, 0)),
                  pl.BlockSpec((1, d, tn), lambda j: (layer, 0, j)),
                  pl.BlockSpec((1, 1, tn), lambda j: (layer, 0, j))],
        out_specs=pl.BlockSpec((rows, tn), lambda j: (0, j)),
        out_shape=jax.ShapeDtypeStruct((rows, n), F32),
        name="adaln_mod",
    )(cc, w_mod, b_mod.reshape(b_mod.shape[0], 1, n))


def _inproj_kernel(xl_ref, xc_ref, lp_ref, ln_ref, sh_ref, sc_ref, w_ref, cos_ref, sin_ref, sgg_ref, sgb_ref,
                   sgw_ref, sgbias_ref, *rest, ntl, nt):
    rw_params, (q_ref, k_ref, v_ref, sg_ref, gate_ref), rw_outs = rest[:10], rest[10:15], rest[15:]
    tm = xl_ref.shape[1]
    i = pl.program_id(1)
    x = jnp.where(i < ntl, xl_ref[0], xc_ref[0])
    h = (x * (1.0 + sc_ref[0, 0]) + sh_ref[0, 0]).astype(BF16)

    def proj(c0, c1):
        return _dot(h, w_ref[0, :, c0:c1])

    cos = cos_ref[...]
    sin = sin_ref[...]
    lane = lax.broadcasted_iota(jnp.int32, (tm, DA_QK_COLS), 1)
    first = (lane % 32) < 16

    def rope(z):
        zr = jnp.where(first, pltpu.roll(z, DA_QK_COLS - 16, 1), pltpu.roll(z, 16, 1))
        return z * cos + zr * sin

    halo = jnp.concatenate([lp_ref[0], ln_ref[0]], axis=0)
    h_halo = (halo * (1.0 + sc_ref[0, 0]) + sh_ref[0, 0]).astype(BF16)
    p_ext = _dot(jnp.concatenate([h, h_halo], axis=0), w_ref[0, :, RW_0:SG_0])
    prev_ok = jnp.logical_and(i != 0, i != ntl)
    next_ok = jnp.logical_and(i != ntl - 1, i != nt - 1)
    _rwkv_feature_math(p_ext[:tm], jnp.where(prev_ok, p_ext[tm + 7:tm + 8], 0.0),
                       jnp.where(next_ok, p_ext[tm + 8:tm + 9], 0.0), rw_params, rw_outs)
    q_ref[0] = (rope(proj(0, DA_K0)) * (DA_QK_DIM ** -0.5 * LOG2_E)).astype(BF16)
    k_ref[0] = rope(proj(DA_K0, DA_V0)).astype(BF16)
    v_ref[0] = proj(DA_V0, RW_0).astype(BF16)
    for j in range(3):
        gate_ref[0, :, j * D_MODEL:(j + 1) * D_MODEL] = _sigmoid(
            proj(GATE_0 + j * D_MODEL, GATE_0 + (j + 1) * D_MODEL)).astype(BF16)

    ps = proj(SG_0, GATE_0)
    gl = ps * (0.5 * (1.0 + jnp.tanh(math.sqrt(2.0 / math.pi) * (ps + 0.044715 * (ps * ps * ps)))))
    u = gl[:, :SG_WIDTH]
    vv = gl[:, SG_WIDTH:]
    mu = jnp.mean(vv, axis=-1, keepdims=True)
    dv = vv - mu
    var = jnp.mean(dv * dv, axis=-1, keepdims=True)
    vn = (dv * lax.rsqrt(var + LN_EPS) * sgg_ref[...] + sgb_ref[...]).astype(BF16)
    gd = SG_WIDTH // SG_GROUPS
    for n in range(tm // SG_CHUNK):
        r0 = n * SG_CHUNK
        for g in range(SG_GROUPS):
            c0 = g * gd
            vm = _dot(sgw_ref[g], vn[r0:r0 + SG_CHUNK, c0:c0 + gd]) + sgbias_ref[:, c0:c0 + gd]
            sg_ref[0, r0:r0 + SG_CHUNK, c0:c0 + gd] = (u[r0:r0 + SG_CHUNK, c0:c0 + gd] * vm).astype(BF16)


def _stream_specs(d, ntl, ctx_block0):
    tm = TOKEN_TILE
    lat = pl.BlockSpec((1, tm, d), lambda b, i: (b, jnp.minimum(i, ntl - 1), 0))
    ctx = pl.BlockSpec((1, tm, d), lambda b, i: (b, ctx_block0 + jnp.maximum(i - ntl, 0), 0))
    return [lat, ctx]


def _input_projection(x_lat, x_ctx, ctx_block0, TT, modall, w_in, layer, cos, sin, sgg, sgb, sgw, sgbias,
                      rw_params, ntl):
    B, lat_rows, D = x_lat.shape
    tm = TOKEN_TILE
    nt = TT // tm
    r8 = tm // 8
    last8 = lat_rows // 8 - 1
    W = RW_WIDTH
    tok = lambda w: pl.BlockSpec((1, tm, w), lambda b, i: (b, i, 0))
    modspec = lambda j: pl.BlockSpec((1, 1, 1, D), lambda b, i: (b, i // ntl, 0, j))
    const = lambda a: pl.BlockSpec(a.shape, lambda b, i: (0,) * a.ndim)
    halo_prev = pl.BlockSpec((1, 8, D), lambda b, i: (b, jnp.maximum(jnp.minimum(i, ntl - 1) * r8 - 1, 0), 0))
    halo_next = pl.BlockSpec((1, 8, D), lambda b, i: (b, jnp.minimum((jnp.minimum(i, ntl - 1) + 1) * r8, last8), 0))
    return pl.pallas_call(
        functools.partial(_inproj_kernel, ntl=ntl, nt=nt),
        grid=(B, nt),
        in_specs=_stream_specs(D, ntl, ctx_block0) + [halo_prev, halo_next, modspec(0), modspec(1),
                  pl.BlockSpec((1,) + w_in.shape[1:], lambda b, i: (layer, 0, 0),
                               pipeline_mode=pl.Buffered(1)),
                  pl.BlockSpec((tm, DA_QK_COLS), lambda b, i: (i, 0)),
                  pl.BlockSpec((tm, DA_QK_COLS), lambda b, i: (i, 0)),
                  const(sgg), const(sgb), const(sgw), const(sgbias)]
                 + [const(a) for a in rw_params],
        out_specs=[tok(DA_QK_COLS), tok(DA_QK_COLS), tok(DA_WIDTH), tok(SG_WIDTH), tok(3 * D)] + [tok(W)] * 11,
        out_shape=[jax.ShapeDtypeStruct((B, TT, DA_QK_COLS), BF16),
                   jax.ShapeDtypeStruct((B, TT, DA_QK_COLS), BF16),
                   jax.ShapeDtypeStruct((B, TT, DA_WIDTH), BF16),
                   jax.ShapeDtypeStruct((B, TT, SG_WIDTH), BF16),
                   jax.ShapeDtypeStruct((B, TT, 3 * D), BF16)]
                  + [jax.ShapeDtypeStruct((B, TT, W), F32)] * 2 + [jax.ShapeDtypeStruct((B, TT, W), BF16)] * 9,
        compiler_params=pltpu.CompilerParams(
            dimension_semantics=("parallel", "parallel"), vmem_limit_bytes=VMEM_LIMIT_BYTES),
        name="in_proj",
    )(x_lat, x_ctx, x_lat, x_lat, modall, modall, w_in, cos, sin, sgg, sgb, sgw, sgbias, *rw_params)


def _attn_kernel(q_ref, k_ref, v_ref, lam_ref, g_ref, o_ref, *, ntl, t_lat, lam_init):
    i = pl.program_id(2)
    lp = lam_ref[...]
    lam = (jnp.exp(jnp.sum(lp[0:1] * lp[1:2], axis=-1, keepdims=True))
           - jnp.exp(jnp.sum(lp[2:3] * lp[3:4], axis=-1, keepdims=True)) + lam_init)
    dv = DA_V_DIM
    heads = [slice(h * dv, (h + 1) * dv) for h in range(q_ref.shape[2] // dv)]
    lane = lax.broadcasted_iota(jnp.int32, (q_ref.shape[1], dv), 1)
    qs = []
    for hs in heads:
        q = q_ref[0, :, hs]
        zero = jnp.zeros_like(q)
        qs.append((jnp.where(lane < DA_QK_DIM, q, zero), jnp.where(lane >= DA_QK_DIM, q, zero)))

    def attend(k0, nk):
        kt = ATTN_KEY_TILE
        tiles = [slice(k0 + t * kt, k0 + (t + 1) * kt) for t in range(nk // kt)]

        def row_max(ss):
            m = ss[0]
            for s in ss[1:]:
                m = jnp.maximum(m, s)
            return jnp.max(m, axis=-1, keepdims=True)

        def pv(es, hs):
            acc = None
            for e, sl in zip(es, tiles):
                v = v_ref[0, sl, hs]
                d = _dot(e, jnp.concatenate([v, jnp.ones_like(v)], axis=1))
                acc = d if acc is None else acc + d
            return acc[:, :dv] / acc[:, dv:]

        ks = [[k_ref[0, sl, hs] for sl in tiles] for hs in heads]
        s0 = [[_dot_nt(q0, kk) for kk in kh] for (q0, _), kh in zip(qs, ks)]
        m0 = [row_max(s) for s in s0]
        s1, e0 = [], []
        for (_, q1), kh, sh, mh in zip(qs, ks, s0, m0):
            s1.append([])
            e0.append([])
            for kk, s in zip(kh, sh):
                s1[-1].append(_dot_nt(q1, kk))
                e0[-1].append(jnp.exp2(s - mh).astype(BF16))
        m1 = [row_max(s) for s in s1]
        o0 = [pv(e, hs) for e, hs in zip(e0, heads)]
        e1 = [[jnp.exp2(s - mh).astype(BF16) for s in sh] for sh, mh in zip(s1, m1)]
        for hs, oa, e in zip(heads, o0, e1):
            o = oa - lam * pv(e, hs)
            o = o * lax.rsqrt(jnp.mean(o * o, axis=-1, keepdims=True) + DA_EPS) * g_ref[...]
            o_ref[0, :, hs] = (o * (1.0 - lam_init)).astype(BF16)

    @pl.when(i < ntl)
    def _():
        attend(0, k_ref.shape[1])

    @pl.when(i >= ntl)
    def _():
        attend(t_lat, k_ref.shape[1] - t_lat)


def _diff_attention(q, k, v, lam_p, norm_g, ntl, lam_init, nt):
    B, TT, _ = q.shape
    tm = TOKEN_TILE
    w = ATTN_HEADS_PER_STEP * DA_V_DIM
    kv = pl.BlockSpec((1, TT, w), lambda b, h, i: (b, 0, h))
    qo = pl.BlockSpec((1, tm, w), lambda b, h, i: (b, i, h))
    return pl.pallas_call(
        functools.partial(_attn_kernel, ntl=ntl, t_lat=ntl * tm, lam_init=lam_init),
        grid=(B, DA_HEADS // ATTN_HEADS_PER_STEP, nt),
        in_specs=[qo, kv, kv,
                  pl.BlockSpec(lam_p.shape, lambda b, h, i: (0, 0)),
                  pl.BlockSpec((1, DA_V_DIM), lambda b, h, i: (0, 0))],
        out_specs=qo,
        out_shape=jax.ShapeDtypeStruct((B, TT, DA_WIDTH), BF16),
        compiler_params=pltpu.CompilerParams(
            dimension_semantics=("parallel", "parallel", "parallel"),
            vmem_limit_bytes=VMEM_LIMIT_BYTES),
        name="diff_attn",
    )(q, k, v, lam_p, norm_g.reshape(1, DA_V_DIM))


def _rwkv_feature_math(p, prev_row, next_row, params, outs):
    mu_ref, w0_ref, w2_ref, a0_ref, a2_ref, kk_ref, ka_ref, rk_ref, g2_ref, ones_ref = params
    lw0_ref, lw1_ref, kd0_ref, kd1_ref, b0_ref, b1_ref, kko_ref, v_ref, r_ref, bonus_ref, g_ref = outs
    tm = p.shape[0]
    row = lax.broadcasted_iota(jnp.int32, p.shape, 0)
    prev = jnp.where(row == 0, prev_row, pltpu.roll(p, 1, 0))
    nxt = jnp.where(row == tm - 1, next_row, pltpu.roll(p, tm - 1, 0))
    ps = p + mu_ref[0:1, :] * (prev - p) + mu_ref[1:2, :] * (nxt - p)

    W = RW_WIDTH
    r = ps[:, 0:W]
    k = ps[:, W:2 * W]
    v = ps[:, 2 * W:3 * W]
    xw = ps[:, 3 * W:3 * W + 64]
    xa = ps[:, 3 * W + 64:3 * W + 128]
    xg = ps[:, 3 * W + 128:3 * W + 256]
    ones_bd = ones_ref[...]

    w_pre = _dot(jnp.tanh(xw).astype(BF16), w2_ref[...]) + w0_ref[...]
    logw = -math.exp(-0.5) * _sigmoid(w_pre)
    a = _sigmoid(_dot(xa.astype(BF16), a2_ref[...]) + a0_ref[...])

    kx = k * kk_ref[...]
    nrm = jnp.sqrt(_seg_sum(kx * kx, ones_bd))
    kk = kx / jnp.maximum(nrm, 1e-12)
    ka = ka_ref[...]
    kd0 = k * (1.0 + (a[:, :W] - 1.0) * ka)
    kd1 = k * (1.0 + (a[:, W:] - 1.0) * ka)
    g = _dot(_sigmoid(xg).astype(BF16), g2_ref[...])
    k_b = 0.5 * (kd0 + kd1)
    bonus = _seg_sum(r * k_b * rk_ref[...], ones_bd) * v

    lw0_ref[0] = logw[:, :W]
    lw1_ref[0] = logw[:, W:]
    kd0_ref[0] = kd0.astype(BF16)
    kd1_ref[0] = kd1.astype(BF16)
    b0_ref[0] = (kk * a[:, :W]).astype(BF16)
    b1_ref[0] = (kk * a[:, W:]).astype(BF16)
    kko_ref[0] = kk.astype(BF16)
    v_ref[0] = v.astype(BF16)
    r_ref[0] = r.astype(BF16)
    bonus_ref[0] = bonus.astype(BF16)
    g_ref[0] = g.astype(BF16)


def _split_bf16(x, pieces):
    out = []
    for _ in range(pieces - 1):
        hi = x.astype(BF16)
        out.append(hi)
        x = x - hi.astype(F32)
    out.append(x.astype(BF16))
    return out


def _block_diag(x, groups):
    xb = x.astype(BF16)
    rows, lanes = xb.shape
    t = jnp.concatenate([xb] * groups, axis=0)
    ri = lax.broadcasted_iota(jnp.int32, t.shape, 0) // rows
    li = lax.broadcasted_iota(jnp.int32, t.shape, 1) // (lanes // groups)
    return jnp.where(ri == li, t, jnp.zeros_like(t))


def _scan_kernel(lw0, kd0, b0, kkf, vf, rf, lw1, kd1, b1, kkr, vr, rr, wg_ref, wu_ref, wd_ref,
                 y0_ref, y1_ref, wgo_ref, wuo_ref, wdo_ref, s_ref):
    step = pl.program_id(1)

    @pl.when(step == 0)
    def _():
        s_ref[...] = jnp.zeros_like(s_ref)

    for src, dst in ((wg_ref, wgo_ref), (wu_ref, wuo_ref), (wd_ref, wdo_ref)):
        dst[...] = src[...].astype(dst.dtype)

    C = lw0.shape[1]
    G = SCAN_HEADS_PER_PASS
    L = G * RW_HEAD
    assert C == RW_HEAD
    ti = lax.broadcasted_iota(jnp.int32, (C, L), 0)
    si = lax.broadcasted_iota(jnp.int32, (C, L), 1) % C
    tc = lax.broadcasted_iota(jnp.int32, (C, C), 0)
    sc = lax.broadcasted_iota(jnp.int32, (C, C), 1)
    same_head = (lax.broadcasted_iota(jnp.int32, (L, L), 0) // RW_HEAD
                 == lax.broadcasted_iota(jnp.int32, (L, L), 1) // RW_HEAD)

    groups = []
    for z, (lw, kd, b, kk, v, r) in enumerate(((lw0, kd0, b0, kkf, vf, rf), (lw1, kd1, b1, kkr, vr, rr))):
        rev = z == 1
        incl = (si >= ti) if rev else (si <= ti)
        strict = (si > ti) if rev else (si < ti)
        tri = jnp.where((sc >= tc) if rev else (sc <= tc), 1.0, 0.0).astype(BF16)
        for n in range(lw.shape[0]):
            logw = lw[n]
            cum = sum(_dot(tri, piece) for piece in _split_bf16(logw, 3))
            tot = jnp.sum(logw, axis=0, keepdims=True)
            p_in = jnp.exp(-cum)
            p_end = jnp.exp(tot - cum)
            kkz, rz, bz, kdz = (t[n].astype(F32) for t in (kk, r, b, kd))
            lhs = jnp.concatenate([-kkz * jnp.exp(cum - logw), rz * jnp.exp(cum)], axis=0)
            rhs_b = bz * p_in
            rhs_k = kdz * p_in
            upd = jnp.concatenate([bz * p_end, kdz * p_end], axis=0)
            dec = jnp.exp(tot)
            vz = v[n]
            for j in range(RW_HEADS // G):
                c = slice(j * L, (j + 1) * L)
                groups.append(dict(z=z, smp=n, j=j, lhs=lhs[:, c].astype(BF16), rhs_b=rhs_b[:, c],
                                   rhs_k=rhs_k[:, c], upd=upd[:, c].astype(BF16), dec=dec[:, c],
                                   v=vz[:, c], strict=strict, incl=incl))

    for g in groups:
        aa_b = _dot_nt(g["lhs"], _block_diag(g["rhs_b"], G))
        aa_k = _dot_nt(g["lhs"], _block_diag(g["rhs_k"], G))
        g["a_k"] = jnp.concatenate([jnp.where(g["strict"], aa_k[:C], 0.0),
                                    jnp.where(g["incl"], aa_k[C:], 0.0)], axis=0).astype(BF16)
        g["a_rb"] = jnp.where(g["incl"], aa_b[C:], 0.0).astype(BF16)
        g["n"] = jnp.where(g["strict"], aa_b[:C], 0.0)
        g["apow"] = _dot(g["n"].astype(BF16), _block_diag(g["n"], G))
    levels = int(math.log2(C)) - 1
    for lv in range(levels):
        for g in groups:
            pbd = _block_diag(g["apow"], G)
            if lv < levels - 1:
                st = _dot(jnp.concatenate([g["n"], g["apow"]], axis=0).astype(BF16), pbd)
                g["n"] = g["n"] + g["apow"] + st[:C]
                g["apow"] = st[C:]
            else:
                g["n"] = g["n"] + g["apow"] + _dot(g["n"].astype(BF16), pbd)
    for g in groups:
        g["s0"] = s_ref[g["smp"], g["z"], g["j"]]
        g["gh"] = _dot_nt(g["lhs"], g["s0"].astype(BF16))
    for g in groups:
        st = _dot(g["a_k"], _block_diag(g["v"], G))
        g["w"] = g["gh"][:C] + st[:C]
        g["yk"] = st[C:]
    for g in groups:
        g["u"] = g["w"] + _dot(g["n"].astype(BF16), _block_diag(g["w"], G))
    for g in groups:
        g["y"] = g["gh"][C:] + g["yk"] + _dot(g["a_rb"], _block_diag(g["u"], G))
    for g in groups:
        uv = jnp.concatenate([g["u"].astype(BF16), g["v"]], axis=0)
        s_ref[g["smp"], g["z"], g["j"]] = g["s0"] * g["dec"] + jnp.where(same_head, _dot_tn(uv, g["upd"]), 0.0)
    for z, y_ref in enumerate((y0_ref, y1_ref)):
        for n in range(y_ref.shape[0]):
            y_ref[n] = jnp.concatenate([g["y"] for g in groups if g["z"] == z and g["smp"] == n],
                                       axis=-1).astype(y_ref.dtype)


def _pow2_at_least(n):
    return 1 << max(n - 1, 0).bit_length()


def _rwkv_scan(lw0, lw1, kd0, kd1, b0, b1, kk, v, r, t_lat, expert_weights, layer):
    B, TT, W = kk.shape
    C = SCAN_CHUNK
    ncl = t_lat // C
    nch = TT // C
    ncc = nch - ncl

    def fwd(s):
        return jnp.where(s < ncc, ncl + s, s - ncc)

    nb = math.gcd(B, SCAN_BATCH)
    fs = pl.BlockSpec((nb, C, W), lambda b, s: (b, fwd(s), 0))
    rs = pl.BlockSpec((nb, C, W), lambda b, s: (b, nch - 1 - s, 0))

    n_steps = (B // nb) * nch
    w_views, w_in_specs, w_out_specs, w_out_shapes, w_shapes = [], [], [], [], []
    for w in expert_weights:
        n_layers, n_exp, d_in, d_out = w.shape
        rows = n_exp * d_in
        slab = _pow2_at_least(-(-rows // n_steps))
        assert rows % slab == 0
        n_slabs = rows // slab
        w_views.append(w.reshape(n_layers * n_slabs, slab, d_out))
        pick = lambda b, s, n=n_slabs: jnp.minimum(b * nch + s, n - 1)
        w_in_specs.append(pl.BlockSpec((1, slab, d_out), lambda b, s, p=pick, n=n_slabs: (layer * n + p(b, s), 0, 0)))
        w_out_specs.append(pl.BlockSpec((1, slab, d_out), lambda b, s, p=pick: (p(b, s), 0, 0)))
        w_out_shapes.append(jax.ShapeDtypeStruct((n_slabs, slab, d_out), BF16))
        w_shapes.append((n_exp, d_in, d_out))
    outs = pl.pallas_call(
        _scan_kernel,
        grid=(B // nb, nch),
        in_specs=[fs] * 6 + [rs] * 6 + w_in_specs,
        out_specs=[fs, rs] + w_out_specs,
        out_shape=[jax.ShapeDtypeStruct((B, TT, W), BF16)] * 2 + w_out_shapes,
        scratch_shapes=[pltpu.VMEM((nb, 2, RW_HEADS // SCAN_HEADS_PER_PASS, SCAN_HEADS_PER_PASS * RW_HEAD,
                                    SCAN_HEADS_PER_PASS * RW_HEAD), F32)],
        compiler_params=pltpu.CompilerParams(
            dimension_semantics=("arbitrary", "arbitrary"), vmem_limit_bytes=VMEM_LIMIT_BYTES),
        name="rwkv_scan",
    )(lw0, kd0, b0, kk, v, r, lw1, kd1, b1, kk, v, r, *w_views)
    return outs[0], outs[1], [o.reshape(shp) for o, shp in zip(outs[2:], w_shapes)]


def _merge_kernel(x_ref, yda_ref, y0_ref, y1_ref, bonus_ref, g_ref, ysg_ref, gate_ref,
                  g1_ref, sh2_ref, sc2_ref, wb_ref, wo_ref, ln1g_ref, ln1b_ref, rlng_ref, rlnb_ref,
                  ones_ref, wr_ref, *rest, alpha):
    xmid_ref, h2_ref, aff_ref = rest[-3:]
    D = D_MODEL
    tm = x_ref.shape[1]
    subs = [slice(r, r + MERGE_ROWS) for r in range(0, tm, MERGE_ROWS)]
    ones_bd = ones_ref[...]
    wr_hi, wr_lo = _split_bf16(wr_ref[...], 2)
    wr_hl = jnp.concatenate([wr_hi, wr_lo], axis=0)

    yrw = []
    for sl in subs:
        y = y0_ref[0, sl, :].astype(F32) + y1_ref[0, sl, :].astype(F32)
        mu = _seg_sum(y, ones_bd) * (1.0 / RW_HEAD)
        dy = y - mu
        var = _seg_sum(dy * dy, ones_bd) * (1.0 / RW_HEAD)
        gn = dy * lax.rsqrt(var + RW_GN_EPS) * rlng_ref[...] + rlnb_ref[...]
        yrw.append(((gn + bonus_ref[0, sl, :].astype(F32)) * g_ref[0, sl, :].astype(F32)).astype(BF16))
    ms = []
    for sl, yr in zip(subs, yrw):
        m = gate_ref[0, sl, 0:D].astype(F32) * _dot(yda_ref[0, sl, :], wb_ref[0:DA_WIDTH, :])
        m = m + gate_ref[0, sl, D:2 * D].astype(F32) * _dot(yr, wb_ref[DA_WIDTH:DA_WIDTH + RW_WIDTH, :])
        m = m + gate_ref[0, sl, 2 * D:3 * D].astype(F32) * _dot(ysg_ref[0, sl, :], wb_ref[DA_WIDTH + RW_WIDTH:, :])
        ms.append(m.astype(BF16))
    mixes = [_dot(m, wo_ref[...]) for m in ms]
    h2s = []
    for sl, mix in zip(subs, mixes):
        z = alpha * x_ref[0, sl, :] + g1_ref[0, 0] * mix
        zm = jnp.mean(z, axis=-1, keepdims=True)
        dz = z - zm
        zv = jnp.mean(dz * dz, axis=-1, keepdims=True)
        xmid = dz * lax.rsqrt(zv + LN_EPS) * ln1g_ref[...] + ln1b_ref[...]
        xmid_ref[0, sl, :] = xmid
        h2 = xmid * (1.0 + sc2_ref[0, 0]) + sh2_ref[0, 0]
        h2_ref[0, sl, :] = h2.astype(BF16)
        h2s.append(h2)
    ne = wr_hi.shape[0]
    for sl, h2 in zip(subs, h2s):
        h_hi, h_lo = _split_bf16(h2, 2)
        part = _dot_nt(wr_hl, h_hi)
        logits = part[:ne] + part[ne:] + _dot_nt(wr_hi, h_lo)
        e = jnp.exp(logits - jnp.max(logits, axis=0, keepdims=True))
        aff_ref[0, :, sl] = e / jnp.sum(e, axis=0, keepdims=True)


def _merge(x_src, x_block0, tile, row0, n_tiles, mod_row, prev_outs, yda, y0, y1, bonus, g, ysg, gates, modall,
           w_branch, w_out, ln1g, ln1b, rlng, rlnb, ones_bd, w_router_t, alpha):
    B, TT, _ = yda.shape
    D = x_src.shape[2]
    blk0 = row0 // tile
    tok = lambda w: pl.BlockSpec((1, tile, w), lambda b, i: (b, blk0 + i, 0))
    modspec = lambda j: pl.BlockSpec((1, 1, 1, D), lambda b, i: (b, mod_row, 0, j))
    const = lambda a: pl.BlockSpec(a.shape, lambda b, i: (0,) * a.ndim)
    consts = [w_branch, w_out, ln1g, ln1b, rlng, rlnb, ones_bd, w_router_t]
    n_in = 11 + len(consts)
    prev = list(prev_outs) if prev_outs is not None else []
    return pl.pallas_call(
        functools.partial(_merge_kernel, alpha=alpha),
        grid=(B, n_tiles),
        in_specs=[pl.BlockSpec((1, tile, D), lambda b, i: (b, x_block0 + i, 0)),
                  tok(DA_WIDTH), tok(RW_WIDTH), tok(RW_WIDTH), tok(RW_WIDTH), tok(RW_WIDTH),
                  tok(SG_WIDTH), tok(3 * D), modspec(2), modspec(3), modspec(4)]
                 + [const(a) for a in consts] + [pl.BlockSpec(memory_space=pl.ANY)] * len(prev),
        out_specs=[tok(D), tok(D), pl.BlockSpec((1, N_EXPERTS, tile), lambda b, i: (b, 0, blk0 + i))],
        out_shape=[jax.ShapeDtypeStruct((B, TT, D), F32),
                   jax.ShapeDtypeStruct((B, TT, D), BF16),
                   jax.ShapeDtypeStruct((B, N_EXPERTS, TT), F32)],
        input_output_aliases={n_in + j: j for j in range(len(prev))},
        compiler_params=pltpu.CompilerParams(
            dimension_semantics=("parallel", "parallel"), vmem_limit_bytes=VMEM_LIMIT_BYTES),
        name="merge_ln1_router",
    )(x_src, yda, y0, y1, bonus, g, ysg, gates, modall, modall, modall, *consts, *prev)


def _topk_kernel(aff_ref, tri_ref, rank_ref, *, cap):
    a = aff_ref[...].reshape(-1, aff_ref.shape[2])
    bits = pltpu.bitcast(a, jnp.int32)
    thr = jnp.zeros((a.shape[0], 1), jnp.int32)
    for bit in range(30, -1, -1):
        cand = thr | (1 << bit)
        cnt = jnp.sum(jnp.where(bits >= cand, 1.0, 0.0), axis=-1, keepdims=True)
        thr = jnp.where(cnt >= cap, cand, thr)
    gt = bits > thr
    eq = bits == thr
    need = cap - jnp.sum(jnp.where(gt, 1.0, 0.0), axis=-1, keepdims=True)
    tri = tri_ref[...]
    kb = tri.shape[0]

    def excl_cumsum(mask):
        m = jnp.where(mask, 1.0, 0.0)
        outs = []
        carry = jnp.zeros((m.shape[0], 1), F32)
        for j in range(m.shape[1] // kb):
            blk = m[:, j * kb:(j + 1) * kb]
            outs.append(_dot(blk.astype(BF16), tri) + carry)
            carry = carry + jnp.sum(blk, axis=-1, keepdims=True)
        return jnp.concatenate(outs, axis=1)

    sel = jnp.logical_or(gt, jnp.logical_and(eq, excl_cumsum(eq) < need))
    rank_ref[...] = jnp.where(sel, excl_cumsum(sel), -1.0).astype(jnp.int32).reshape(rank_ref.shape)


def _topk_ranks(aff, tri, t_off, t_len, cap):
    B, E, _ = aff.shape
    blk = t_off // t_len
    nb = math.gcd(B, RANK_BATCH)
    return pl.pallas_call(
        functools.partial(_topk_kernel, cap=cap),
        grid=(B // nb,),
        in_specs=[pl.BlockSpec((nb, E, t_len), lambda b: (b, 0, blk)),
                  pl.BlockSpec(tri.shape, lambda b: (0, 0))],
        out_specs=pl.BlockSpec((nb, E, t_len), lambda b: (b, 0, 0)),
        out_shape=jax.ShapeDtypeStruct((B, E, t_len), jnp.int32),
        compiler_params=pltpu.CompilerParams(
            dimension_semantics=("parallel",), vmem_limit_bytes=VMEM_LIMIT_BYTES),
        name="expert_choice_ranks",
    )(aff, tri)


def _moe_kernel(*refs, sets):
    n = len(sets)
    h_ref, aff_ref = refs[0], refs[1]
    rank_refs = refs[2:2 + n]
    wg_ref, wu_ref, wd_ref, f_ref, acc_ref = refs[2 + n:]
    e = pl.program_id(1)

    @pl.when(e == 0)
    def _():
        acc_ref[...] = jnp.zeros_like(acc_ref)

    onehots, gates, xs = [], [], []
    for (t0, tn, cap), rank_ref in zip(sets, rank_refs):
        rank = rank_ref[0, pl.ds(e, 1), :]
        aff = aff_ref[0, pl.ds(e, 1), t0:t0 + tn]
        slot = lax.broadcasted_iota(jnp.int32, (cap, tn), 0)
        hit = rank == slot
        onehot = jnp.where(hit, 1.0, 0.0).astype(BF16)
        onehots.append(onehot)
        gates.append(jnp.sum(jnp.where(hit, aff, 0.0), axis=-1, keepdims=True))
        xs.append(_dot(onehot, h_ref[0, t0:t0 + tn, :]).astype(BF16))
    xe = jnp.concatenate(xs, axis=0) if n > 1 else xs[0]
    gate = jnp.concatenate(gates, axis=0) if n > 1 else gates[0]
    hg = _dot(xe, wg_ref[0])
    hid = (hg * _sigmoid(hg)) * _dot(xe, wu_ref[0])
    ye = (_dot(hid.astype(BF16), wd_ref[0]) * gate).astype(BF16)
    r0 = 0
    for (t0, tn, cap), onehot in zip(sets, onehots):
        acc_ref[t0:t0 + tn, :] += _dot_tn(onehot, ye[r0:r0 + cap])
        r0 += cap

    @pl.when(e == pl.num_programs(1) - 1)
    def _():
        f_ref[0] = acc_ref[...].astype(f_ref.dtype)


def _expert_ffn(h2, aff, ranks, wg, wu, wd, sets):
    B, TT, D = h2.shape
    E = aff.shape[1]
    F = wg.shape[2]
    return pl.pallas_call(
        functools.partial(_moe_kernel, sets=sets),
        grid=(B, E),
        in_specs=[pl.BlockSpec((1, TT, D), lambda b, e: (b, 0, 0), pipeline_mode=pl.Buffered(1)),
                  pl.BlockSpec((1, E, TT), lambda b, e: (b, 0, 0))]
                 + [pl.BlockSpec((1, E, r.shape[2]), lambda b, e: (b, 0, 0)) for r in ranks]
                 + [pl.BlockSpec((1, D, F), lambda b, e: (e, 0, 0)),
                    pl.BlockSpec((1, D, F), lambda b, e: (e, 0, 0)),
                    pl.BlockSpec((1, F, D), lambda b, e: (e, 0, 0))],
        out_specs=pl.BlockSpec((1, TT, D), lambda b, e: (b, 0, 0)),
        out_shape=jax.ShapeDtypeStruct((B, TT, D), BF16),
        scratch_shapes=[pltpu.VMEM((TT, D), F32)],
        compiler_params=pltpu.CompilerParams(
            dimension_semantics=("parallel", "arbitrary"), vmem_limit_bytes=VMEM_LIMIT_BYTES),
        name="expert_ffn",
    )(h2, aff, *ranks, wg, wu, wd)


def _ln2_kernel(x_ref, f_ref, g2_ref, lng_ref, lnb_ref, o_ref, *, alpha, t_lat):
    tile = x_ref.shape[1]
    row0 = pl.program_id(1) * tile
    for r in range(0, tile, TOKEN_TILE):
        sl = slice(r, r + TOKEN_TILE)
        g2 = jnp.where(row0 + r < t_lat, g2_ref[0, 0], g2_ref[0, 1])
        z = alpha * x_ref[0, sl, :] + g2 * f_ref[0, sl, :].astype(F32)
        zm = jnp.mean(z, axis=-1, keepdims=True)
        dz = z - zm
        zv = jnp.mean(dz * dz, axis=-1, keepdims=True)
        o_ref[0, sl, :] = dz * lax.rsqrt(zv + LN_EPS) * lng_ref[...] + lnb_ref[...]


def _final_norm(xmid, f, modall, lng, lnb, t_lat, alpha, rows):
    B, _, D = xmid.shape
    tile = next(t for t in LN2_TILES if rows % t == 0)
    tok = pl.BlockSpec((1, tile, D), lambda b, i: (b, i, 0))
    const = lambda a: pl.BlockSpec(a.shape, lambda b, i: (0,) * a.ndim)
    return pl.pallas_call(
        functools.partial(_ln2_kernel, alpha=alpha, t_lat=t_lat),
        grid=(B, rows // tile),
        in_specs=[tok, tok, pl.BlockSpec((1, 2, 1, D), lambda b, i: (b, 0, 0, 5)),
                  const(lng), const(lnb)],
        out_specs=tok,
        out_shape=jax.ShapeDtypeStruct((B, rows, D), F32),
        compiler_params=pltpu.CompilerParams(
            dimension_semantics=("parallel", "parallel"), vmem_limit_bytes=VMEM_LIMIT_BYTES),
        name="ln2",
    )(xmid, f, modall, lng, lnb)


def _rope_tables(t_lat, t_ctx):
    rows = t_lat // GRID_W
    row = jnp.repeat(jnp.arange(rows, dtype=F32), GRID_W)
    col = jnp.tile(jnp.arange(GRID_W, dtype=F32), rows)
    half = DA_QK_DIM // 2
    inv_freq = ROPE_BASE ** (-jnp.arange(0, half, 2, dtype=F32) / half)
    ar = row[:, None] * inv_freq
    ac = col[:, None] * inv_freq
    ang = jnp.concatenate([ar, ar, ac, ac], axis=-1)
    sign = jnp.where((jnp.arange(DA_QK_DIM) % 32) < 16, -1.0, 1.0).astype(F32)
    reps = DA_QK_COLS // DA_QK_DIM
    cos = jnp.tile(jnp.cos(ang), (1, reps))
    sin = jnp.tile(jnp.sin(ang) * sign, (1, reps))
    cos = jnp.concatenate([cos, jnp.ones((t_ctx, DA_QK_COLS), F32)], axis=0)
    sin = jnp.concatenate([sin, jnp.zeros((t_ctx, DA_QK_COLS), F32)], axis=0)
    return cos, sin


def kernel(x, c, ctx, c_ctx, w_mod, b_mod, w_in, da_lambda, da_norm_g, rw_shift_mu, rw_w0, rw_w2, rw_a0, rw_a2, rw_k_k, rw_k_a, rw_r_k, rw_ln_g, rw_ln_b, rw_g2, sg_norm_g, sg_norm_b, sg_w, sg_b, w_branch, w_out, ln1_g, ln1_b, w_router, w_e_gate, w_e_up, w_e_down, ln2_g, ln2_b):
    B, T, D = x.shape
    Tc = ctx.shape[1]
    depth = w_mod.shape[0]
    tm = TOKEN_TILE
    assert D == D_MODEL and T % tm == 0 and Tc == tm and T % MERGE_TILE == 0
    ntl = T // tm
    alpha = (2 * depth) ** 0.25
    cap_lat = EC_CAPACITY * T // N_EXPERTS
    cap_ctx = EC_CAPACITY * Tc // N_EXPERTS

    cos, sin = _rope_tables(T, Tc)
    lane = jnp.arange(SCAN_HEADS_PER_PASS * RW_HEAD)
    ones_bd = (lane[:, None] // RW_HEAD == lane[None, :] // RW_HEAD).astype(BF16)
    kb = math.gcd(Tc, RANK_BLOCK)
    tri = (jnp.arange(kb)[:, None] < jnp.arange(kb)[None, :]).astype(BF16)
    rows = ((B + 1 + 7) // 8) * 8
    cc = jnp.concatenate([c, c_ctx[None, :], jnp.zeros((rows - B - 1, D), F32)], axis=0)
    row2 = lambda a: a.reshape(1, -1)

    w_in_bf = w_in.astype(BF16)
    x_lat, x_ctx, ctx_block0 = x, ctx, 0
    for l in range(depth):
        last = l == depth - 1
        lam_init = 0.8 - 0.6 * math.exp(-0.3 * l)
        mod = _modulation(cc, w_mod, b_mod, l)
        modall = jnp.stack([mod[:B], jnp.broadcast_to(mod[B], (B, 6 * D))], axis=1)
        modall = modall.reshape(B, 2, 1, 6 * D)

        sgbias = jnp.repeat(sg_b[l].T, SG_WIDTH // SG_GROUPS, axis=1)
        cat2 = lambda a: jnp.transpose(a, (1, 0, 2)).reshape(a.shape[1], 2 * RW_WIDTH)
        rw_params = [rw_shift_mu[l], row2(rw_w0[l]), cat2(rw_w2[l]).astype(BF16), row2(rw_a0[l]),
                     cat2(rw_a2[l]).astype(BF16), row2(rw_k_k[l]), row2(rw_k_a[l]), row2(rw_r_k[l]),
                     rw_g2[l].astype(BF16), ones_bd]
        (q, k, v, ysg, gates, lw0, lw1, kd0, kd1, b0, b1, kk, vv, rr, bonus, gg) = _input_projection(
            x_lat, x_ctx, ctx_block0, T + Tc, modall, w_in_bf, l, cos, sin, row2(sg_norm_g[l]),
            row2(sg_norm_b[l]), sg_w[l].astype(BF16), sgbias, rw_params, ntl)

        nt_out = ntl if last else (T + Tc) // tm
        yda = _diff_attention(q, k, v, da_lambda[l], da_norm_g[l], ntl, lam_init, nt_out)

        y0, y1, (wg, wu, wd) = _rwkv_scan(lw0, lw1, kd0, kd1, b0, b1, kk, vv, rr, T,
                                          (w_e_gate, w_e_up, w_e_down), l)

        merge_args = (yda, y0, y1, bonus, gg, ysg, gates, modall, w_branch[l].astype(BF16),
                      w_out[l].astype(BF16), row2(ln1_g[l]), row2(ln1_b[l]), row2(rw_ln_g[l]),
                      row2(rw_ln_b[l]), ones_bd, w_router[l].T, alpha)
        outs = _merge(x_lat, 0, MERGE_TILE, 0, T // MERGE_TILE, 0, None, *merge_args)
        if not last:
            outs = _merge(x_ctx, ctx_block0, tm, T, Tc // tm, 1, outs, *merge_args)
        xmid, h2, aff = outs

        sets = ((0, T, cap_lat),) if last else ((0, T, cap_lat), (T, Tc, cap_ctx))
        ranks = [_topk_ranks(aff, tri, 0, T, cap_lat)]
        if not last:
            ranks.append(_topk_ranks(aff, tri, T, Tc, cap_ctx))
        f = _expert_ffn(h2, aff, ranks, wg, wu, wd, sets)
        x_lat = _final_norm(xmid, f, modall, row2(ln2_g[l]), row2(ln2_b[l]), T, alpha,
                            T if last else T + Tc)
        x_ctx, ctx_block0 = x_lat, ntl
    return x_lat
```

```python
import functools
import math

import jax
import jax.numpy as jnp
from jax import lax
from jax.experimental import pallas as pl
from jax.experimental.pallas import tpu as pltpu

F32 = jnp.float32
BF16 = jnp.bfloat16
HIGHEST = lax.Precision.HIGHEST

D_MODEL = 1024
GRID_W = 64
DA_HEADS = 4
DA_QK_DIM = 64
DA_V_DIM = 128
DA_WIDTH = 512
DA_QK_COLS = 512
ROPE_BASE = 10000.0
DA_EPS = 1e-5
RW_HEAD = 64
RW_HEADS = 8
RW_WIDTH = 512
RW_COLS = 1792
RW_GN_EPS = 64e-5
SG_CHUNK = 128
SG_GROUPS = 4
SG_WIDTH = 512
DA_K0 = 512
DA_V0 = 1024
RW_0 = 1536
SG_0 = RW_0 + RW_COLS
GATE_0 = SG_0 + 2 * SG_WIDTH
N_EXPERTS = 16
EC_CAPACITY = 2
LN_EPS = 1e-5
LOG2_E = math.log2(math.e)
ROPE_AXIS_DIM = DA_QK_DIM // 2
ROPE_HALF = ROPE_AXIS_DIM // 2

TOKEN_TILE = 256
MOD_COL_TILE = 1024
SCAN_CHUNK = 64
ATTN_KEY_TILE = 256
ATTN_HEADS_PER_STEP = 4
SCAN_HEADS_PER_PASS = 4
SCAN_BATCH = 4
FEATURE_ROWS = 128
MERGE_TILE = 512
MERGE_ROWS = 128
RANK_BLOCK = 256
RANK_BATCH = 4
LN2_TILES = (1024, 768, 512, 256)
VMEM_LIMIT_BYTES = 58 * 1024 * 1024


def _dot(a, b):
    return jnp.dot(a, b, preferred_element_type=F32)


def _dot_hi(a, b):
    return jnp.dot(a, b, preferred_element_type=F32, precision=HIGHEST)


def _dot_nt(a, b):
    return lax.dot_general(a, b, (((1,), (1,)), ((), ())), preferred_element_type=F32)


def _dot_tn(a, b):
    return lax.dot_general(a, b, (((0,), (0,)), ((), ())), preferred_element_type=F32)


def _sigmoid(z):
    return 1.0 / (1.0 + jnp.exp(-z))


def _seg_sum(z, ones_bd):
    rows = z.shape[0]
    w = ones_bd.shape[0]
    hi = z.astype(BF16)
    lo = (z - hi.astype(F32)).astype(BF16)
    cols = []
    for c in range(0, z.shape[1], w):
        st = _dot(jnp.concatenate([hi[:, c:c + w], lo[:, c:c + w]], axis=0), ones_bd)
        cols.append(st[:rows] + st[rows:])
    return jnp.concatenate(cols, axis=1)


def _mod_kernel(c_ref, w_ref, b_ref, o_ref):
    cc = c_ref[...]
    o_ref[...] = _dot_hi(cc * _sigmoid(cc), w_ref[0]) + b_ref[0]


def _modulation(cc, w_mod, b_mod, layer):
    rows, d = cc.shape
    n = w_mod.shape[2]
    tn = MOD_COL_TILE
    return pl.pallas_call(
        _mod_kernel,
        grid=(n // tn,),
        in_specs=[pl.BlockSpec((rows, d), lambda j: (0, 0)),
                  pl.BlockSpec((1, d, tn), lambda j: (layer, 0, j)),
                  pl.BlockSpec((1, 1, tn), lambda j: (layer, 0, j))],
        out_specs=pl.BlockSpec((rows, tn), lambda j: (0, j)),
        out_shape=jax.ShapeDtypeStruct((rows, n), F32),
        name="adaln_mod",
    )(cc, w_mod, b_mod.reshape(b_mod.shape[0], 1, n))


def _inproj_kernel(xl_ref, xc_ref, lp_ref, ln_ref, sh_ref, sc_ref, w_ref, cos_ref, sin_ref, sgg_ref, sgb_ref,
                   sgw_ref, sgbias_ref, *rest, ntl, nt):
    rw_params, (q_ref, k_ref, v_ref, sg_ref, gate_ref), rw_outs = rest[:10], rest[10:15], rest[15:]
    tm = xl_ref.shape[1]
    i = pl.program_id(1)
    x = jnp.where(i < ntl, xl_ref[0], xc_ref[0])
    h = (x * (1.0 + sc_ref[0, 0]) + sh_ref[0, 0]).astype(BF16)

    def proj(c0, c1):
        return _dot(h, w_ref[0, :, c0:c1])

    cos = cos_ref[...]
    sin = sin_ref[...]
    lane = lax.broadcasted_iota(jnp.int32, (tm, DA_QK_COLS), 1)
    first = (lane % ROPE_AXIS_DIM) < ROPE_HALF

    def rope(z):
        zr = jnp.where(first, pltpu.roll(z, DA_QK_COLS - ROPE_HALF, 1), pltpu.roll(z, ROPE_HALF, 1))
        return z * cos + zr * sin

    halo = jnp.concatenate([lp_ref[0], ln_ref[0]], axis=0)
    h_halo = (halo * (1.0 + sc_ref[0, 0]) + sh_ref[0, 0]).astype(BF16)
    p_ext = _dot(jnp.concatenate([h, h_halo], axis=0), w_ref[0, :, RW_0:SG_0])
    prev_ok = jnp.logical_and(i != 0, i != ntl)
    next_ok = jnp.logical_and(i != ntl - 1, i != nt - 1)
    _rwkv_feature_math(p_ext[:tm], jnp.where(prev_ok, p_ext[tm + 7:tm + 8], 0.0),
                       jnp.where(next_ok, p_ext[tm + 8:tm + 9], 0.0), rw_params, rw_outs)
    q_ref[0] = (rope(proj(0, DA_K0)) * (DA_QK_DIM ** -0.5 * LOG2_E)).astype(BF16)
    k_ref[0] = rope(proj(DA_K0, DA_V0)).astype(BF16)
    v_ref[0] = proj(DA_V0, RW_0).astype(BF16)
    for j in range(3):
        gate_ref[0, :, j * D_MODEL:(j + 1) * D_MODEL] = _sigmoid(
            proj(GATE_0 + j * D_MODEL, GATE_0 + (j + 1) * D_MODEL)).astype(BF16)

    ps = proj(SG_0, GATE_0)
    gl = ps * (0.5 * (1.0 + jnp.tanh(math.sqrt(2.0 / math.pi) * (ps + 0.044715 * (ps * ps * ps)))))
    u = gl[:, :SG_WIDTH]
    vv = gl[:, SG_WIDTH:]
    mu = jnp.mean(vv, axis=-1, keepdims=True)
    dv = vv - mu
    var = jnp.mean(dv * dv, axis=-1, keepdims=True)
    vn = (dv * lax.rsqrt(var + LN_EPS) * sgg_ref[...] + sgb_ref[...]).astype(BF16)
    gd = SG_WIDTH // SG_GROUPS
    for n in range(tm // SG_CHUNK):
        r0 = n * SG_CHUNK
        for g in range(SG_GROUPS):
            c0 = g * gd
            vm = _dot(sgw_ref[g], vn[r0:r0 + SG_CHUNK, c0:c0 + gd]) + sgbias_ref[:, c0:c0 + gd]
            sg_ref[0, r0:r0 + SG_CHUNK, c0:c0 + gd] = (u[r0:r0 + SG_CHUNK, c0:c0 + gd] * vm).astype(BF16)


def _stream_specs(d, ntl, ctx_block0):
    tm = TOKEN_TILE
    lat = pl.BlockSpec((1, tm, d), lambda b, i: (b, jnp.minimum(i, ntl - 1), 0))
    ctx = pl.BlockSpec((1, tm, d), lambda b, i: (b, ctx_block0 + jnp.maximum(i - ntl, 0), 0))
    return [lat, ctx]


def _input_projection(x_lat, x_ctx, ctx_block0, TT, modall, w_in, layer, cos, sin, sgg, sgb, sgw, sgbias,
                      rw_params, ntl):
    B, lat_rows, D = x_lat.shape
    tm = TOKEN_TILE
    nt = TT // tm
    r8 = tm // 8
    last8 = lat_rows // 8 - 1
    W = RW_WIDTH
    tok = lambda w: pl.BlockSpec((1, tm, w), lambda b, i: (b, i, 0))
    modspec = lambda j: pl.BlockSpec((1, 1, 1, D), lambda b, i: (b, i // ntl, 0, j))
    const = lambda a: pl.BlockSpec(a.shape, lambda b, i: (0,) * a.ndim)
    halo_prev = pl.BlockSpec((1, 8, D), lambda b, i: (b, jnp.maximum(jnp.minimum(i, ntl - 1) * r8 - 1, 0), 0))
    halo_next = pl.BlockSpec((1, 8, D), lambda b, i: (b, jnp.minimum((jnp.minimum(i, ntl - 1) + 1) * r8, last8), 0))
    return pl.pallas_call(
        functools.partial(_inproj_kernel, ntl=ntl, nt=nt),
        grid=(B, nt),
        in_specs=_stream_specs(D, ntl, ctx_block0) + [halo_prev, halo_next, modspec(0), modspec(1),
                  pl.BlockSpec((1,) + w_in.shape[1:], lambda b, i: (layer, 0, 0),
                               pipeline_mode=pl.Buffered(1)),
                  pl.BlockSpec((tm, DA_QK_COLS), lambda b, i: (i, 0)),
                  pl.BlockSpec((tm, DA_QK_COLS), lambda b, i: (i, 0)),
                  const(sgg), const(sgb), const(sgw), const(sgbias)]
                 + [const(a) for a in rw_params],
        out_specs=[tok(DA_QK_COLS), tok(DA_QK_COLS), tok(DA_WIDTH), tok(SG_WIDTH), tok(3 * D)]
                  + [tok(W), tok(W), tok(2 * W), tok(2 * W), tok(3 * W), tok(W), tok(W)],
        out_shape=[jax.ShapeDtypeStruct((B, TT, DA_QK_COLS), BF16),
                   jax.ShapeDtypeStruct((B, TT, DA_QK_COLS), BF16),
                   jax.ShapeDtypeStruct((B, TT, DA_WIDTH), BF16),
                   jax.ShapeDtypeStruct((B, TT, SG_WIDTH), BF16),
                   jax.ShapeDtypeStruct((B, TT, 3 * D), BF16)]
                  + [jax.ShapeDtypeStruct((B, TT, W), F32)] * 2
                  + [jax.ShapeDtypeStruct((B, TT, n * W), BF16) for n in (2, 2, 3, 1, 1)],
        compiler_params=pltpu.CompilerParams(
            dimension_semantics=("parallel", "parallel"), vmem_limit_bytes=VMEM_LIMIT_BYTES),
        name="in_proj",
    )(x_lat, x_ctx, x_lat, x_lat, modall, modall, w_in, cos, sin, sgg, sgb, sgw, sgbias, *rw_params)


def _attn_kernel(q_ref, k_ref, v_ref, lam_ref, g_ref, o_ref, *, ntl, t_lat, lam_init):
    i = pl.program_id(2)
    lp = lam_ref[...]
    lam = (jnp.exp(jnp.sum(lp[0:1] * lp[1:2], axis=-1, keepdims=True))
           - jnp.exp(jnp.sum(lp[2:3] * lp[3:4], axis=-1, keepdims=True)) + lam_init)
    dv = DA_V_DIM
    heads = [slice(h * dv, (h + 1) * dv) for h in range(q_ref.shape[2] // dv)]
    lane = lax.broadcasted_iota(jnp.int32, (q_ref.shape[1], dv), 1)
    qs = []
    for hs in heads:
        q = q_ref[0, :, hs]
        zero = jnp.zeros_like(q)
        qs.append((jnp.where(lane < DA_QK_DIM, q, zero), jnp.where(lane >= DA_QK_DIM, q, zero)))

    def attend(k0, nk):
        kt = ATTN_KEY_TILE
        tiles = [slice(k0 + t * kt, k0 + (t + 1) * kt) for t in range(nk // kt)]

        def row_max(ss):
            m = ss[0]
            for s in ss[1:]:
                m = jnp.maximum(m, s)
            return jnp.max(m, axis=-1, keepdims=True)

        def pv(es, hs):
            acc = None
            for e, sl in zip(es, tiles):
                v = v_ref[0, sl, hs]
                d = _dot(e, jnp.concatenate([v, jnp.ones_like(v)], axis=1))
                acc = d if acc is None else acc + d
            return acc[:, :dv] / acc[:, dv:]

        ks = [[k_ref[0, sl, hs] for sl in tiles] for hs in heads]
        s0 = [[_dot_nt(q0, kk) for kk in kh] for (q0, _), kh in zip(qs, ks)]
        m0 = [row_max(s) for s in s0]
        s1, e0 = [], []
        for (_, q1), kh, sh, mh in zip(qs, ks, s0, m0):
            s1.append([])
            e0.append([])
            for kk, s in zip(kh, sh):
                s1[-1].append(_dot_nt(q1, kk))
                e0[-1].append(jnp.exp2(s - mh).astype(BF16))
        m1 = [row_max(s) for s in s1]
        o0 = [pv(e, hs) for e, hs in zip(e0, heads)]
        e1 = [[jnp.exp2(s - mh).astype(BF16) for s in sh] for sh, mh in zip(s1, m1)]
        for hs, oa, e in zip(heads, o0, e1):
            o = oa - lam * pv(e, hs)
            o = o * lax.rsqrt(jnp.mean(o * o, axis=-1, keepdims=True) + DA_EPS) * g_ref[...]
            o_ref[0, :, hs] = (o * (1.0 - lam_init)).astype(BF16)

    @pl.when(i < ntl)
    def _():
        attend(0, k_ref.shape[1])

    @pl.when(i >= ntl)
    def _():
        attend(t_lat, k_ref.shape[1] - t_lat)


def _diff_attention(q, k, v, lam_p, norm_g, ntl, lam_init, nt):
    B, TT, _ = q.shape
    tm = TOKEN_TILE
    w = ATTN_HEADS_PER_STEP * DA_V_DIM
    kv = pl.BlockSpec((1, TT, w), lambda b, h, i: (b, 0, h))
    qo = pl.BlockSpec((1, tm, w), lambda b, h, i: (b, i, h))
    return pl.pallas_call(
        functools.partial(_attn_kernel, ntl=ntl, t_lat=ntl * tm, lam_init=lam_init),
        grid=(B, DA_HEADS // ATTN_HEADS_PER_STEP, nt),
        in_specs=[qo, kv, kv,
                  pl.BlockSpec(lam_p.shape, lambda b, h, i: (0, 0)),
                  pl.BlockSpec((1, DA_V_DIM), lambda b, h, i: (0, 0))],
        out_specs=qo,
        out_shape=jax.ShapeDtypeStruct((B, TT, DA_WIDTH), BF16),
        compiler_params=pltpu.CompilerParams(
            dimension_semantics=("parallel", "parallel", "parallel"),
            vmem_limit_bytes=VMEM_LIMIT_BYTES),
        name="diff_attn",
    )(q, k, v, lam_p, norm_g.reshape(1, DA_V_DIM))


def _rwkv_feature_math(p, prev_row, next_row, params, outs):
    mu_ref, w0_ref, w2_ref, a0_ref, a2_ref, kk_ref, ka_ref, rk_ref, g2_ref, ones_ref = params
    lw0_ref, lw1_ref, fwd_ref, rev_ref, shared_ref, bonus_ref, g_ref = outs
    tm = p.shape[0]
    row = lax.broadcasted_iota(jnp.int32, p.shape, 0)
    prev = jnp.where(row == 0, prev_row, pltpu.roll(p, 1, 0))
    nxt = jnp.where(row == tm - 1, next_row, pltpu.roll(p, tm - 1, 0))
    ps_all = p + mu_ref[0:1, :] * (prev - p) + mu_ref[1:2, :] * (nxt - p)

    W = RW_WIDTH
    ones_bd = ones_ref[...]
    ka = ka_ref[...]
    for r0 in range(0, tm, FEATURE_ROWS):
        rs = slice(r0, r0 + FEATURE_ROWS)
        ps = ps_all[rs]
        r = ps[:, 0:W]
        k = ps[:, W:2 * W]
        v = ps[:, 2 * W:3 * W]
        xw = ps[:, 3 * W:3 * W + 64]
        xa = ps[:, 3 * W + 64:3 * W + 128]
        xg = ps[:, 3 * W + 128:3 * W + 256]

        w_pre = _dot(jnp.tanh(xw).astype(BF16), w2_ref[...]) + w0_ref[...]
        logw = -math.exp(-0.5) * _sigmoid(w_pre)
        a = _sigmoid(_dot(xa.astype(BF16), a2_ref[...]) + a0_ref[...])

        kx = k * kk_ref[...]
        nrm = jnp.sqrt(_seg_sum(kx * kx, ones_bd))
        kk = kx / jnp.maximum(nrm, 1e-12)
        kd0 = k * (1.0 + (a[:, :W] - 1.0) * ka)
        kd1 = k * (1.0 + (a[:, W:] - 1.0) * ka)
        g = _dot(_sigmoid(xg).astype(BF16), g2_ref[...])
        k_b = 0.5 * (kd0 + kd1)
        bonus = _seg_sum(r * k_b * rk_ref[...], ones_bd) * v

        lw0_ref[0, rs, :] = logw[:, :W]
        lw1_ref[0, rs, :] = logw[:, W:]
        fwd_ref[0, rs, 0:W] = kd0.astype(BF16)
        fwd_ref[0, rs, W:2 * W] = (kk * a[:, :W]).astype(BF16)
        rev_ref[0, rs, 0:W] = kd1.astype(BF16)
        rev_ref[0, rs, W:2 * W] = (kk * a[:, W:]).astype(BF16)
        shared_ref[0, rs, 0:W] = kk.astype(BF16)
        shared_ref[0, rs, W:2 * W] = v.astype(BF16)
        shared_ref[0, rs, 2 * W:3 * W] = r.astype(BF16)
        bonus_ref[0, rs, :] = bonus.astype(BF16)
        g_ref[0, rs, :] = g.astype(BF16)


def _split_bf16(x, pieces):
    out = []
    for _ in range(pieces - 1):
        hi = x.astype(BF16)
        out.append(hi)
        x = x - hi.astype(F32)
    out.append(x.astype(BF16))
    return out


def _block_diag(x, groups):
    xb = x.astype(BF16)
    rows, lanes = xb.shape
    t = jnp.concatenate([xb] * groups, axis=0)
    ri = lax.broadcasted_iota(jnp.int32, t.shape, 0) // rows
    li = lax.broadcasted_iota(jnp.int32, t.shape, 1) // (lanes // groups)
    return jnp.where(ri == li, t, jnp.zeros_like(t))


def _scan_kernel(lw0, dir0, shared0, lw1, dir1, shared1, wg_ref, wu_ref, wd_ref,
                 y0_ref, y1_ref, wgo_ref, wuo_ref, wdo_ref, s_ref):
    step = pl.program_id(1)

    @pl.when(step == 0)
    def _():
        s_ref[...] = jnp.zeros_like(s_ref)

    for src, dst in ((wg_ref, wgo_ref), (wu_ref, wuo_ref), (wd_ref, wdo_ref)):
        dst[...] = src[...].astype(dst.dtype)

    C = lw0.shape[1]
    G = SCAN_HEADS_PER_PASS
    L = G * RW_HEAD
    assert C == RW_HEAD
    ti = lax.broadcasted_iota(jnp.int32, (C, L), 0)
    si = lax.broadcasted_iota(jnp.int32, (C, L), 1) % C
    tc = lax.broadcasted_iota(jnp.int32, (C, C), 0)
    sc = lax.broadcasted_iota(jnp.int32, (C, C), 1)
    same_head = (lax.broadcasted_iota(jnp.int32, (L, L), 0) // RW_HEAD
                 == lax.broadcasted_iota(jnp.int32, (L, L), 1) // RW_HEAD)

    groups = []
    W = RW_WIDTH
    for z, (lw, dirp, shared) in enumerate(((lw0, dir0, shared0), (lw1, dir1, shared1))):
        rev = z == 1
        incl = (si >= ti) if rev else (si <= ti)
        strict = (si > ti) if rev else (si < ti)
        tri = jnp.where((sc >= tc) if rev else (sc <= tc), 1.0, 0.0).astype(BF16)
        for n in range(lw.shape[0]):
            logw = lw[n]
            cum = sum(_dot(tri, piece) for piece in _split_bf16(logw, 3))
            tot = jnp.sum(logw, axis=0, keepdims=True)
            p_in = jnp.exp(-cum)
            p_end = jnp.exp(tot - cum)
            kdz, bz = dirp[n, :, 0:W].astype(F32), dirp[n, :, W:2 * W].astype(F32)
            kkz, rz = shared[n, :, 0:W].astype(F32), shared[n, :, 2 * W:3 * W].astype(F32)
            lhs = jnp.concatenate([-kkz * jnp.exp(cum - logw), rz * jnp.exp(cum)], axis=0)
            rhs_b = bz * p_in
            rhs_k = kdz * p_in
            upd = jnp.concatenate([bz * p_end, kdz * p_end], axis=0)
            dec = jnp.exp(tot)
            vz = shared[n, :, W:2 * W]
            for j in range(RW_HEADS // G):
                c = slice(j * L, (j + 1) * L)
                groups.append(dict(z=z, smp=n, j=j, lhs=lhs[:, c].astype(BF16), rhs_b=rhs_b[:, c],
                                   rhs_k=rhs_k[:, c], upd=upd[:, c].astype(BF16), dec=dec[:, c],
                                   v=vz[:, c], strict=strict, incl=incl))

    for g in groups:
        aa_b = _dot_nt(g["lhs"], _block_diag(g["rhs_b"], G))
        aa_k = _dot_nt(g["lhs"], _block_diag(g["rhs_k"], G))
        g["a_k"] = jnp.concatenate([jnp.where(g["strict"], aa_k[:C], 0.0),
                                    jnp.where(g["incl"], aa_k[C:], 0.0)], axis=0).astype(BF16)
        g["a_rb"] = jnp.where(g["incl"], aa_b[C:], 0.0).astype(BF16)
        g["n"] = jnp.where(g["strict"], aa_b[:C], 0.0)
        g["apow"] = _dot(g["n"].astype(BF16), _block_diag(g["n"], G))
    levels = int(math.log2(C)) - 1
    for lv in range(levels):
        for g in groups:
            pbd = _block_diag(g["apow"], G)
            if lv < levels - 1:
                st = _dot(jnp.concatenate([g["n"], g["apow"]], axis=0).astype(BF16), pbd)
                g["n"] = g["n"] + g["apow"] + st[:C]
                g["apow"] = st[C:]
            else:
                g["n"] = g["n"] + g["apow"] + _dot(g["n"].astype(BF16), pbd)
    for g in groups:
        g["s0"] = s_ref[g["smp"], g["z"], g["j"]]
        g["gh"] = _dot_nt(g["lhs"], g["s0"].astype(BF16))
    for g in groups:
        st = _dot(g["a_k"], _block_diag(g["v"], G))
        g["w"] = g["gh"][:C] + st[:C]
        g["yk"] = st[C:]
    for g in groups:
        g["u"] = g["w"] + _dot(g["n"].astype(BF16), _block_diag(g["w"], G))
    for g in groups:
        g["y"] = g["gh"][C:] + g["yk"] + _dot(g["a_rb"], _block_diag(g["u"], G))
    for g in groups:
        uv = jnp.concatenate([g["u"].astype(BF16), g["v"]], axis=0)
        s_ref[g["smp"], g["z"], g["j"]] = g["s0"] * g["dec"] + jnp.where(same_head, _dot_tn(uv, g["upd"]), 0.0)
    for z, y_ref in enumerate((y0_ref, y1_ref)):
        for n in range(y_ref.shape[0]):
            y_ref[n] = jnp.concatenate([g["y"] for g in groups if g["z"] == z and g["smp"] == n],
                                       axis=-1).astype(y_ref.dtype)


def _pow2_at_least(n):
    return 1 << max(n - 1, 0).bit_length()


def _rwkv_scan(lw0, lw1, dir0, dir1, shared, t_lat, expert_weights, layer):
    B, TT, W = lw0.shape
    C = SCAN_CHUNK
    ncl = t_lat // C
    nch = TT // C
    ncc = nch - ncl

    def fwd(s):
        return jnp.where(s < ncc, ncl + s, s - ncc)

    nb = math.gcd(B, SCAN_BATCH)
    fs = lambda n: pl.BlockSpec((nb, C, n * W), lambda b, s: (b, fwd(s), 0))
    rs = lambda n: pl.BlockSpec((nb, C, n * W), lambda b, s: (b, nch - 1 - s, 0))

    n_steps = (B // nb) * nch
    w_views, w_in_specs, w_out_specs, w_out_shapes, w_shapes = [], [], [], [], []
    for w in expert_weights:
        n_layers, n_exp, d_in, d_out = w.shape
        rows = n_exp * d_in
        slab = _pow2_at_least(-(-rows // n_steps))
        assert rows % slab == 0
        n_slabs = rows // slab
        w_views.append(w.reshape(n_layers * n_slabs, slab, d_out))
        pick = lambda b, s, n=n_slabs: jnp.minimum(b * nch + s, n - 1)
        w_in_specs.append(pl.BlockSpec((1, slab, d_out), lambda b, s, p=pick, n=n_slabs: (layer * n + p(b, s), 0, 0)))
        w_out_specs.append(pl.BlockSpec((1, slab, d_out), lambda b, s, p=pick: (p(b, s), 0, 0)))
        w_out_shapes.append(jax.ShapeDtypeStruct((n_slabs, slab, d_out), BF16))
        w_shapes.append((n_exp, d_in, d_out))
    outs = pl.pallas_call(
        _scan_kernel,
        grid=(B // nb, nch),
        in_specs=[fs(1), fs(2), fs(3), rs(1), rs(2), rs(3)] + w_in_specs,
        out_specs=[fs(1), rs(1)] + w_out_specs,
        out_shape=[jax.ShapeDtypeStruct((B, TT, W), BF16)] * 2 + w_out_shapes,
        scratch_shapes=[pltpu.VMEM((nb, 2, RW_HEADS // SCAN_HEADS_PER_PASS, SCAN_HEADS_PER_PASS * RW_HEAD,
                                    SCAN_HEADS_PER_PASS * RW_HEAD), F32)],
        compiler_params=pltpu.CompilerParams(
            dimension_semantics=("arbitrary", "arbitrary"), vmem_limit_bytes=VMEM_LIMIT_BYTES),
        name="rwkv_scan",
    )(lw0, dir0, shared, lw1, dir1, shared, *w_views)
    return outs[0], outs[1], [o.reshape(shp) for o, shp in zip(outs[2:], w_shapes)]


def _merge_kernel(x_ref, yda_ref, y0_ref, y1_ref, bonus_ref, g_ref, ysg_ref, gate_ref,
                  g1_ref, sh2_ref, sc2_ref, wb_ref, wo_ref, ln1g_ref, ln1b_ref, rlng_ref, rlnb_ref,
                  ones_ref, wr_ref, *rest, alpha):
    xmid_ref, h2_ref, aff_ref = rest[-3:]
    D = D_MODEL
    tm = x_ref.shape[1]
    subs = [slice(r, r + MERGE_ROWS) for r in range(0, tm, MERGE_ROWS)]
    ones_bd = ones_ref[...]
    wr_hi, wr_lo = _split_bf16(wr_ref[...], 2)
    wr_hl = jnp.concatenate([wr_hi, wr_lo], axis=0)

    yrw = []
    for sl in subs:
        y = y0_ref[0, sl, :].astype(F32) + y1_ref[0, sl, :].astype(F32)
        mu = _seg_sum(y, ones_bd) * (1.0 / RW_HEAD)
        dy = y - mu
        var = _seg_sum(dy * dy, ones_bd) * (1.0 / RW_HEAD)
        gn = dy * lax.rsqrt(var + RW_GN_EPS) * rlng_ref[...] + rlnb_ref[...]
        yrw.append(((gn + bonus_ref[0, sl, :].astype(F32)) * g_ref[0, sl, :].astype(F32)).astype(BF16))
    ms = []
    for sl, yr in zip(subs, yrw):
        m = gate_ref[0, sl, 0:D].astype(F32) * _dot(yda_ref[0, sl, :], wb_ref[0:DA_WIDTH, :])
        m = m + gate_ref[0, sl, D:2 * D].astype(F32) * _dot(yr, wb_ref[DA_WIDTH:DA_WIDTH + RW_WIDTH, :])
        m = m + gate_ref[0, sl, 2 * D:3 * D].astype(F32) * _dot(ysg_ref[0, sl, :], wb_ref[DA_WIDTH + RW_WIDTH:, :])
        ms.append(m.astype(BF16))
    mixes = [_dot(m, wo_ref[...]) for m in ms]
    h2s = []
    for sl, mix in zip(subs, mixes):
        z = alpha * x_ref[0, sl, :] + g1_ref[0, 0] * mix
        zm = jnp.mean(z, axis=-1, keepdims=True)
        dz = z - zm
        zv = jnp.mean(dz * dz, axis=-1, keepdims=True)
        xmid = dz * lax.rsqrt(zv + LN_EPS) * ln1g_ref[...] + ln1b_ref[...]
        xmid_ref[0, sl, :] = xmid
        h2 = xmid * (1.0 + sc2_ref[0, 0]) + sh2_ref[0, 0]
        h2_ref[0, sl, :] = h2.astype(BF16)
        h2s.append(h2)
    ne = wr_hi.shape[0]
    for sl, h2 in zip(subs, h2s):
        h_hi, h_lo = _split_bf16(h2, 2)
        part = _dot_nt(wr_hl, h_hi)
        logits = part[:ne] + part[ne:] + _dot_nt(wr_hi, h_lo)
        e = jnp.exp(logits - jnp.max(logits, axis=0, keepdims=True))
        aff_ref[0, :, sl] = e / jnp.sum(e, axis=0, keepdims=True)


def _merge(x_src, x_block0, tile, row0, n_tiles, mod_row, prev_outs, yda, y0, y1, bonus, g, ysg, gates, modall,
           w_branch, w_out, ln1g, ln1b, rlng, rlnb, ones_bd, w_router_t, alpha):
    B, TT, _ = yda.shape
    D = x_src.shape[2]
    blk0 = row0 // tile
    tok = lambda w: pl.BlockSpec((1, tile, w), lambda b, i: (b, blk0 + i, 0))
    modspec = lambda j: pl.BlockSpec((1, 1, 1, D), lambda b, i: (b, mod_row, 0, j))
    const = lambda a: pl.BlockSpec(a.shape, lambda b, i: (0,) * a.ndim)
    consts = [w_branch, w_out, ln1g, ln1b, rlng, rlnb, ones_bd, w_router_t]
    n_in = 11 + len(consts)
    prev = list(prev_outs) if prev_outs is not None else []
    return pl.pallas_call(
        functools.partial(_merge_kernel, alpha=alpha),
        grid=(B, n_tiles),
        in_specs=[pl.BlockSpec((1, tile, D), lambda b, i: (b, x_block0 + i, 0)),
                  tok(DA_WIDTH), tok(RW_WIDTH), tok(RW_WIDTH), tok(RW_WIDTH), tok(RW_WIDTH),
                  tok(SG_WIDTH), tok(3 * D), modspec(2), modspec(3), modspec(4)]
                 + [const(a) for a in consts] + [pl.BlockSpec(memory_space=pl.ANY)] * len(prev),
        out_specs=[tok(D), tok(D), pl.BlockSpec((1, N_EXPERTS, tile), lambda b, i: (b, 0, blk0 + i))],
        out_shape=[jax.ShapeDtypeStruct((B, TT, D), F32),
                   jax.ShapeDtypeStruct((B, TT, D), BF16),
                   jax.ShapeDtypeStruct((B, N_EXPERTS, TT), F32)],
        input_output_aliases={n_in + j: j for j in range(len(prev))},
        compiler_params=pltpu.CompilerParams(
            dimension_semantics=("parallel", "parallel"), vmem_limit_bytes=VMEM_LIMIT_BYTES),
        name="merge_ln1_router",
    )(x_src, yda, y0, y1, bonus, g, ysg, gates, modall, modall, modall, *consts, *prev)


def _topk_kernel(aff_ref, tri_ref, rank_ref, *, cap):
    a = aff_ref[...].reshape(-1, aff_ref.shape[2])
    bits = pltpu.bitcast(a, jnp.int32)
    thr = jnp.zeros((a.shape[0], 1), jnp.int32)
    for bit in range(30, -1, -1):
        cand = thr | (1 << bit)
        cnt = jnp.sum(jnp.where(bits >= cand, 1.0, 0.0), axis=-1, keepdims=True)
        thr = jnp.where(cnt >= cap, cand, thr)
    gt = bits > thr
    eq = bits == thr
    need = cap - jnp.sum(jnp.where(gt, 1.0, 0.0), axis=-1, keepdims=True)
    tri = tri_ref[...]
    kb = tri.shape[0]

    def excl_cumsum(mask):
        m = jnp.where(mask, 1.0, 0.0)
        outs = []
        carry = jnp.zeros((m.shape[0], 1), F32)
        for j in range(m.shape[1] // kb):
            blk = m[:, j * kb:(j + 1) * kb]
            outs.append(_dot(blk.astype(BF16), tri) + carry)
            carry = carry + jnp.sum(blk, axis=-1, keepdims=True)
        return jnp.concatenate(outs, axis=1)

    sel = jnp.logical_or(gt, jnp.logical_and(eq, excl_cumsum(eq) < need))
    rank_ref[...] = jnp.where(sel, excl_cumsum(sel), -1.0).astype(jnp.int32).reshape(rank_ref.shape)


def _topk_ranks(aff, tri, t_off, t_len, cap):
    B, E, _ = aff.shape
    blk = t_off // t_len
    nb = math.gcd(B, RANK_BATCH)
    return pl.pallas_call(
        functools.partial(_topk_kernel, cap=cap),
        grid=(B // nb,),
        in_specs=[pl.BlockSpec((nb, E, t_len), lambda b: (b, 0, blk)),
                  pl.BlockSpec(tri.shape, lambda b: (0, 0))],
        out_specs=pl.BlockSpec((nb, E, t_len), lambda b: (b, 0, 0)),
        out_shape=jax.ShapeDtypeStruct((B, E, t_len), jnp.int32),
        compiler_params=pltpu.CompilerParams(
            dimension_semantics=("parallel",), vmem_limit_bytes=VMEM_LIMIT_BYTES),
        name="expert_choice_ranks",
    )(aff, tri)


def _moe_kernel(*refs, sets):
    n = len(sets)
    h_ref, aff_ref = refs[0], refs[1]
    rank_refs = refs[2:2 + n]
    wg_ref, wu_ref, wd_ref, f_ref, acc_ref = refs[2 + n:]
    e = pl.program_id(1)

    @pl.when(e == 0)
    def _():
        acc_ref[...] = jnp.zeros_like(acc_ref)

    onehots, gates, xs = [], [], []
    for (t0, tn, cap), rank_ref in zip(sets, rank_refs):
        rank = rank_ref[0, pl.ds(e, 1), :]
        aff = aff_ref[0, pl.ds(e, 1), t0:t0 + tn]
        slot = lax.broadcasted_iota(jnp.int32, (cap, tn), 0)
        hit = rank == slot
        onehot = jnp.where(hit, 1.0, 0.0).astype(BF16)
        onehots.append(onehot)
        gates.append(jnp.sum(jnp.where(hit, aff, 0.0), axis=-1, keepdims=True))
        xs.append(_dot(onehot, h_ref[0, t0:t0 + tn, :]).astype(BF16))
    xe = jnp.concatenate(xs, axis=0) if n > 1 else xs[0]
    gate = jnp.concatenate(gates, axis=0) if n > 1 else gates[0]
    hg = _dot(xe, wg_ref[0])
    hid = (hg * _sigmoid(hg)) * _dot(xe, wu_ref[0])
    ye = (_dot(hid.astype(BF16), wd_ref[0]) * gate).astype(BF16)
    r0 = 0
    for (t0, tn, cap), onehot in zip(sets, onehots):
        acc_ref[t0:t0 + tn, :] += _dot_tn(onehot, ye[r0:r0 + cap])
        r0 += cap

    @pl.when(e == pl.num_programs(1) - 1)
    def _():
        f_ref[0] = acc_ref[...].astype(f_ref.dtype)


def _expert_ffn(h2, aff, ranks, wg, wu, wd, sets):
    B, TT, D = h2.shape
    E = aff.shape[1]
    F = wg.shape[2]
    return pl.pallas_call(
        functools.partial(_moe_kernel, sets=sets),
        grid=(B, E),
        in_specs=[pl.BlockSpec((1, TT, D), lambda b, e: (b, 0, 0), pipeline_mode=pl.Buffered(1)),
                  pl.BlockSpec((1, E, TT), lambda b, e: (b, 0, 0))]
                 + [pl.BlockSpec((1, E, r.shape[2]), lambda b, e: (b, 0, 0)) for r in ranks]
                 + [pl.BlockSpec((1, D, F), lambda b, e: (e, 0, 0)),
                    pl.BlockSpec((1, D, F), lambda b, e: (e, 0, 0)),
                    pl.BlockSpec((1, F, D), lambda b, e: (e, 0, 0))],
        out_specs=pl.BlockSpec((1, TT, D), lambda b, e: (b, 0, 0)),
        out_shape=jax.ShapeDtypeStruct((B, TT, D), BF16),
        scratch_shapes=[pltpu.VMEM((TT, D), F32)],
        compiler_params=pltpu.CompilerParams(
            dimension_semantics=("parallel", "arbitrary"), vmem_limit_bytes=VMEM_LIMIT_BYTES),
        name="expert_ffn",
    )(h2, aff, *ranks, wg, wu, wd)


def _ln2_kernel(x_ref, f_ref, g2_ref, lng_ref, lnb_ref, o_ref, *, alpha, t_lat):
    tile = x_ref.shape[1]
    row0 = pl.program_id(1) * tile
    for r in range(0, tile, TOKEN_TILE):
        sl = slice(r, r + TOKEN_TILE)
        g2 = jnp.where(row0 + r < t_lat, g2_ref[0, 0], g2_ref[0, 1])
        z = alpha * x_ref[0, sl, :] + g2 * f_ref[0, sl, :].astype(F32)
        zm = jnp.mean(z, axis=-1, keepdims=True)
        dz = z - zm
        zv = jnp.mean(dz * dz, axis=-1, keepdims=True)
        o_ref[0, sl, :] = dz * lax.rsqrt(zv + LN_EPS) * lng_ref[...] + lnb_ref[...]


def _final_norm(xmid, f, modall, lng, lnb, t_lat, alpha, rows):
    B, _, D = xmid.shape
    tile = next(t for t in LN2_TILES if rows % t == 0)
    tok = pl.BlockSpec((1, tile, D), lambda b, i: (b, i, 0))
    const = lambda a: pl.BlockSpec(a.shape, lambda b, i: (0,) * a.ndim)
    return pl.pallas_call(
        functools.partial(_ln2_kernel, alpha=alpha, t_lat=t_lat),
        grid=(B, rows // tile),
        in_specs=[tok, tok, pl.BlockSpec((1, 2, 1, D), lambda b, i: (b, 0, 0, 5)),
                  const(lng), const(lnb)],
        out_specs=tok,
        out_shape=jax.ShapeDtypeStruct((B, rows, D), F32),
        compiler_params=pltpu.CompilerParams(
            dimension_semantics=("parallel", "parallel"), vmem_limit_bytes=VMEM_LIMIT_BYTES),
        name="ln2",
    )(xmid, f, modall, lng, lnb)


def _rope_tables(t_lat, t_ctx):
    rows = t_lat // GRID_W
    row = jnp.repeat(jnp.arange(rows, dtype=F32), GRID_W)
    col = jnp.tile(jnp.arange(GRID_W, dtype=F32), rows)
    half = DA_QK_DIM // 2
    inv_freq = ROPE_BASE ** (-jnp.arange(0, half, 2, dtype=F32) / half)
    ar = row[:, None] * inv_freq
    ac = col[:, None] * inv_freq
    ang = jnp.concatenate([ar, ar, ac, ac], axis=-1)
    sign = jnp.where((jnp.arange(DA_QK_DIM) % ROPE_AXIS_DIM) < ROPE_HALF, -1.0, 1.0).astype(F32)
    reps = DA_QK_COLS // DA_QK_DIM
    cos = jnp.tile(jnp.cos(ang), (1, reps))
    sin = jnp.tile(jnp.sin(ang) * sign, (1, reps))
    cos = jnp.concatenate([cos, jnp.ones((t_ctx, DA_QK_COLS), F32)], axis=0)
    sin = jnp.concatenate([sin, jnp.zeros((t_ctx, DA_QK_COLS), F32)], axis=0)
    return cos, sin


def kernel(x, c, ctx, c_ctx, w_mod, b_mod, w_in, da_lambda, da_norm_g, rw_shift_mu, rw_w0, rw_w2, rw_a0, rw_a2, rw_k_k, rw_k_a, rw_r_k, rw_ln_g, rw_ln_b, rw_g2, sg_norm_g, sg_norm_b, sg_w, sg_b, w_branch, w_out, ln1_g, ln1_b, w_router, w_e_gate, w_e_up, w_e_down, ln2_g, ln2_b):
    B, T, D = x.shape
    Tc = ctx.shape[1]
    depth = w_mod.shape[0]
    tm = TOKEN_TILE
    assert D == D_MODEL and T % tm == 0 and Tc == tm and T % MERGE_TILE == 0
    ntl = T // tm
    alpha = (2 * depth) ** 0.25
    cap_lat = EC_CAPACITY * T // N_EXPERTS
    cap_ctx = EC_CAPACITY * Tc // N_EXPERTS

    cos, sin = _rope_tables(T, Tc)
    lane = jnp.arange(SCAN_HEADS_PER_PASS * RW_HEAD)
    ones_bd = (lane[:, None] // RW_HEAD == lane[None, :] // RW_HEAD).astype(BF16)
    kb = math.gcd(Tc, RANK_BLOCK)
    tri = (jnp.arange(kb)[:, None] < jnp.arange(kb)[None, :]).astype(BF16)
    rows = ((B + 1 + 7) // 8) * 8
    cc = jnp.concatenate([c, c_ctx[None, :], jnp.zeros((rows - B - 1, D), F32)], axis=0)
    row2 = lambda a: a.reshape(1, -1)

    w_in_bf = w_in.astype(BF16)
    x_lat, x_ctx, ctx_block0 = x, ctx, 0
    for l in range(depth):
        last = l == depth - 1
        lam_init = 0.8 - 0.6 * math.exp(-0.3 * l)
        mod = _modulation(cc, w_mod, b_mod, l)
        modall = jnp.stack([mod[:B], jnp.broadcast_to(mod[B], (B, 6 * D))], axis=1)
        modall = modall.reshape(B, 2, 1, 6 * D)

        sgbias = jnp.repeat(sg_b[l].T, SG_WIDTH // SG_GROUPS, axis=1)
        cat2 = lambda a: jnp.transpose(a, (1, 0, 2)).reshape(a.shape[1], 2 * RW_WIDTH)
        rw_params = [rw_shift_mu[l], row2(rw_w0[l]), cat2(rw_w2[l]).astype(BF16), row2(rw_a0[l]),
                     cat2(rw_a2[l]).astype(BF16), row2(rw_k_k[l]), row2(rw_k_a[l]), row2(rw_r_k[l]),
                     rw_g2[l].astype(BF16), ones_bd]
        (q, k, v, ysg, gates, lw0, lw1, dir0, dir1, shared, bonus, gg) = _input_projection(
            x_lat, x_ctx, ctx_block0, T + Tc, modall, w_in_bf, l, cos, sin, row2(sg_norm_g[l]),
            row2(sg_norm_b[l]), sg_w[l].astype(BF16), sgbias, rw_params, ntl)

        nt_out = ntl if last else (T + Tc) // tm
        yda = _diff_attention(q, k, v, da_lambda[l], da_norm_g[l], ntl, lam_init, nt_out)

        y0, y1, (wg, wu, wd) = _rwkv_scan(lw0, lw1, dir0, dir1, shared, T,
                                          (w_e_gate, w_e_up, w_e_down), l)

        merge_args = (yda, y0, y1, bonus, gg, ysg, gates, modall, w_branch[l].astype(BF16),
                      w_out[l].astype(BF16), row2(ln1_g[l]), row2(ln1_b[l]), row2(rw_ln_g[l]),
                      row2(rw_ln_b[l]), ones_bd, w_router[l].T, alpha)
        outs = _merge(x_lat, 0, MERGE_TILE, 0, T // MERGE_TILE, 0, None, *merge_args)
        if not last:
            outs = _merge(x_ctx, ctx_block0, tm, T, Tc // tm, 1, outs, *merge_args)
        xmid, h2, aff = outs

        sets = ((0, T, cap_lat),) if last else ((0, T, cap_lat), (T, Tc, cap_ctx))
        ranks = [_topk_ranks(aff, tri, 0, T, cap_lat)]
        if not last:
            ranks.append(_topk_ranks(aff, tri, T, Tc, cap_ctx))
        f = _expert_ffn(h2, aff, ranks, wg, wu, wd, sets)
        x_lat = _final_norm(xmid, f, modall, row2(ln2_g[l]), row2(ln2_b[l]), T, alpha,
                            T if last else T + Tc)
        x_ctx, ctx_block0 = x_lat, ntl
    return x_lat
```

```python
import functools
import math

import jax
import jax.numpy as jnp
from jax import lax
from jax.experimental import pallas as pl
from jax.experimental.pallas import tpu as pltpu

F32 = jnp.float32
BF16 = jnp.bfloat16
HIGHEST = lax.Precision.HIGHEST

D_MODEL = 1024
GRID_W = 64
DA_HEADS = 4
DA_QK_DIM = 64
DA_V_DIM = 128
DA_WIDTH = 512
DA_QK_COLS = 512
ROPE_BASE = 10000.0
DA_EPS = 1e-5
RW_HEAD = 64
RW_HEADS = 8
RW_WIDTH = 512
RW_COLS = 1792
RW_GN_EPS = 64e-5
SG_CHUNK = 128
SG_GROUPS = 4
SG_WIDTH = 512
DA_K0 = 512
DA_V0 = 1024
RW_0 = 1536
SG_0 = RW_0 + RW_COLS
GATE_0 = SG_0 + 2 * SG_WIDTH
N_EXPERTS = 16
EC_CAPACITY = 2
LN_EPS = 1e-5
LOG2_E = math.log2(math.e)
ROPE_AXIS_DIM = DA_QK_DIM // 2
ROPE_HALF = ROPE_AXIS_DIM // 2

TOKEN_TILE = 256
MOD_COL_TILE = 1024
SCAN_CHUNK = 64
ATTN_KEY_TILE = 256
ATTN_HEADS_PER_STEP = 4
SCAN_HEADS_PER_PASS = 4
SCAN_BATCH = 4
FEATURE_ROWS = 128
MERGE_TILE = 512
MERGE_ROWS = 128
RANK_BLOCK = 256
RANK_BATCH = 16
LN2_TILES = (1152, 1024, 768, 512, 256)
VMEM_LIMIT_BYTES = 58 * 1024 * 1024


def _dot(a, b):
    return jnp.dot(a, b, preferred_element_type=F32)


def _dot_hi(a, b):
    return jnp.dot(a, b, preferred_element_type=F32, precision=HIGHEST)


def _dot_nt(a, b):
    return lax.dot_general(a, b, (((1,), (1,)), ((), ())), preferred_element_type=F32)


def _dot_tn(a, b):
    return lax.dot_general(a, b, (((0,), (0,)), ((), ())), preferred_element_type=F32)


def _sigmoid(z):
    return 1.0 / (1.0 + jnp.exp(-z))


def _seg_sum(z, ones_bd):
    rows = z.shape[0]
    w = ones_bd.shape[0]
    hi = z.astype(BF16)
    lo = (z - hi.astype(F32)).astype(BF16)
    cols = []
    for c in range(0, z.shape[1], w):
        st = _dot(jnp.concatenate([hi[:, c:c + w], lo[:, c:c + w]], axis=0), ones_bd)
        cols.append(st[:rows] + st[rows:])
    return jnp.concatenate(cols, axis=1)


def _mod_kernel(c_ref, w_ref, b_ref, o_ref):
    cc = c_ref[...]
    o_ref[...] = _dot_hi(cc * _sigmoid(cc), w_ref[0]) + b_ref[0]


def _modulation(cc, w_mod, b_mod, layer):
    rows, d = cc.shape
    n = w_mod.shape[2]
    tn = MOD_COL_TILE
    return pl.pallas_call(
        _mod_kernel,
        grid=(n // tn,),
        in_specs=[pl.BlockSpec((rows, d), lambda j: (0, 0)),
                  pl.BlockSpec((1, d, tn), lambda j: (layer, 0, j)),
                  pl.BlockSpec((1, 1, tn), lambda j: (layer, 0, j))],
        out_specs=pl.BlockSpec((rows, tn), lambda j: (0, j)),
        out_shape=jax.ShapeDtypeStruct((rows, n), F32),
        name="adaln_mod",
    )(cc, w_mod, b_mod.reshape(b_mod.shape[0], 1, n))


def _inproj_kernel(xl_ref, xc_ref, lp_ref, ln_ref, sh_ref, sc_ref, w_ref, cos_ref, sin_ref, sgg_ref, sgb_ref,
                   sgw_ref, sgbias_ref, *rest, ntl, nt):
    rw_params, (q_ref, k_ref, v_ref, sg_ref, gate_ref), rw_outs = rest[:10], rest[10:15], rest[15:]
    tm = xl_ref.shape[1]
    i = pl.program_id(1)
    x = jnp.where(i < ntl, xl_ref[0], xc_ref[0])
    h = (x * (1.0 + sc_ref[0, 0]) + sh_ref[0, 0]).astype(BF16)

    def proj(c0, c1):
        return _dot(h, w_ref[0, :, c0:c1])

    cos = cos_ref[...]
    sin = sin_ref[...]
    lane = lax.broadcasted_iota(jnp.int32, (tm, DA_QK_COLS), 1)
    first = (lane % ROPE_AXIS_DIM) < ROPE_HALF

    def rope(z):
        zr = jnp.where(first, pltpu.roll(z, DA_QK_COLS - ROPE_HALF, 1), pltpu.roll(z, ROPE_HALF, 1))
        return z * cos + zr * sin

    halo = jnp.concatenate([lp_ref[0], ln_ref[0]], axis=0)
    h_halo = (halo * (1.0 + sc_ref[0, 0]) + sh_ref[0, 0]).astype(BF16)
    p_ext = _dot(jnp.concatenate([h, h_halo], axis=0), w_ref[0, :, RW_0:SG_0])
    prev_ok = jnp.logical_and(i != 0, i != ntl)
    next_ok = jnp.logical_and(i != ntl - 1, i != nt - 1)
    _rwkv_feature_math(p_ext[:tm], jnp.where(prev_ok, p_ext[tm + 7:tm + 8], 0.0),
                       jnp.where(next_ok, p_ext[tm + 8:tm + 9], 0.0), rw_params, rw_outs)
    q_ref[0] = (rope(proj(0, DA_K0)) * (DA_QK_DIM ** -0.5 * LOG2_E)).astype(BF16)
    k_ref[0] = rope(proj(DA_K0, DA_V0)).astype(BF16)
    v_ref[0] = proj(DA_V0, RW_0).astype(BF16)
    for j in range(3):
        gate_ref[0, :, j * D_MODEL:(j + 1) * D_MODEL] = _sigmoid(
            proj(GATE_0 + j * D_MODEL, GATE_0 + (j + 1) * D_MODEL)).astype(BF16)

    ps = proj(SG_0, GATE_0)
    gl = ps * (0.5 * (1.0 + jnp.tanh(math.sqrt(2.0 / math.pi) * (ps + 0.044715 * (ps * ps * ps)))))
    u = gl[:, :SG_WIDTH]
    vv = gl[:, SG_WIDTH:]
    mu = jnp.mean(vv, axis=-1, keepdims=True)
    dv = vv - mu
    var = jnp.mean(dv * dv, axis=-1, keepdims=True)
    vn = (dv * lax.rsqrt(var + LN_EPS) * sgg_ref[...] + sgb_ref[...]).astype(BF16)
    gd = SG_WIDTH // SG_GROUPS
    for n in range(tm // SG_CHUNK):
        r0 = n * SG_CHUNK
        for g in range(SG_GROUPS):
            c0 = g * gd
            vm = _dot(sgw_ref[g], vn[r0:r0 + SG_CHUNK, c0:c0 + gd]) + sgbias_ref[:, c0:c0 + gd]
            sg_ref[0, r0:r0 + SG_CHUNK, c0:c0 + gd] = (u[r0:r0 + SG_CHUNK, c0:c0 + gd] * vm).astype(BF16)


def _stream_specs(d, ntl, ctx_block0):
    tm = TOKEN_TILE
    lat = pl.BlockSpec((1, tm, d), lambda b, i: (b, jnp.minimum(i, ntl - 1), 0))
    ctx = pl.BlockSpec((1, tm, d), lambda b, i: (b, ctx_block0 + jnp.maximum(i - ntl, 0), 0))
    return [lat, ctx]


def _input_projection(x_lat, x_ctx, ctx_block0, TT, modall, w_in, layer, cos, sin, sgg, sgb, sgw, sgbias,
                      rw_params, ntl):
    B, lat_rows, D = x_lat.shape
    tm = TOKEN_TILE
    nt = TT // tm
    r8 = tm // 8
    last8 = lat_rows // 8 - 1
    W = RW_WIDTH
    tok = lambda w: pl.BlockSpec((1, tm, w), lambda b, i: (b, i, 0))
    modspec = lambda j: pl.BlockSpec((1, 1, 1, D), lambda b, i: (b, i // ntl, 0, j))
    const = lambda a: pl.BlockSpec(a.shape, lambda b, i: (0,) * a.ndim)
    halo_prev = pl.BlockSpec((1, 8, D), lambda b, i: (b, jnp.maximum(jnp.minimum(i, ntl - 1) * r8 - 1, 0), 0))
    halo_next = pl.BlockSpec((1, 8, D), lambda b, i: (b, jnp.minimum((jnp.minimum(i, ntl - 1) + 1) * r8, last8), 0))
    return pl.pallas_call(
        functools.partial(_inproj_kernel, ntl=ntl, nt=nt),
        grid=(B, nt),
        in_specs=_stream_specs(D, ntl, ctx_block0) + [halo_prev, halo_next, modspec(0), modspec(1),
                  pl.BlockSpec((1,) + w_in.shape[1:], lambda b, i: (layer, 0, 0),
                               pipeline_mode=pl.Buffered(1)),
                  pl.BlockSpec((tm, DA_QK_COLS), lambda b, i: (i, 0)),
                  pl.BlockSpec((tm, DA_QK_COLS), lambda b, i: (i, 0)),
                  const(sgg), const(sgb), const(sgw), const(sgbias)]
                 + [const(a) for a in rw_params],
        out_specs=[tok(DA_QK_COLS), tok(DA_QK_COLS), tok(DA_WIDTH), tok(SG_WIDTH), tok(3 * D)]
                  + [tok(W), tok(W), tok(2 * W), tok(2 * W), tok(3 * W), tok(W), tok(W)],
        out_shape=[jax.ShapeDtypeStruct((B, TT, DA_QK_COLS), BF16),
                   jax.ShapeDtypeStruct((B, TT, DA_QK_COLS), BF16),
                   jax.ShapeDtypeStruct((B, TT, DA_WIDTH), BF16),
                   jax.ShapeDtypeStruct((B, TT, SG_WIDTH), BF16),
                   jax.ShapeDtypeStruct((B, TT, 3 * D), BF16)]
                  + [jax.ShapeDtypeStruct((B, TT, W), F32)] * 2
                  + [jax.ShapeDtypeStruct((B, TT, n * W), BF16) for n in (2, 2, 3, 1, 1)],
        compiler_params=pltpu.CompilerParams(
            dimension_semantics=("parallel", "parallel"), vmem_limit_bytes=VMEM_LIMIT_BYTES),
        name="in_proj",
    )(x_lat, x_ctx, x_lat, x_lat, modall, modall, w_in, cos, sin, sgg, sgb, sgw, sgbias, *rw_params)


def _attn_kernel(q_ref, k_ref, v_ref, lam_ref, g_ref, o_ref, *, ntl, t_lat, lam_init):
    i = pl.program_id(2)
    lp = lam_ref[...]
    lam = (jnp.exp(jnp.sum(lp[0:1] * lp[1:2], axis=-1, keepdims=True))
           - jnp.exp(jnp.sum(lp[2:3] * lp[3:4], axis=-1, keepdims=True)) + lam_init)
    dv = DA_V_DIM
    heads = [slice(h * dv, (h + 1) * dv) for h in range(q_ref.shape[2] // dv)]
    lane = lax.broadcasted_iota(jnp.int32, (q_ref.shape[1], dv), 1)
    qs = []
    for hs in heads:
        q = q_ref[0, :, hs]
        zero = jnp.zeros_like(q)
        qs.append((jnp.where(lane < DA_QK_DIM, q, zero), jnp.where(lane >= DA_QK_DIM, q, zero)))

    def attend(k0, nk):
        kt = ATTN_KEY_TILE
        tiles = [slice(k0 + t * kt, k0 + (t + 1) * kt) for t in range(nk // kt)]

        def row_max(ss):
            m = ss[0]
            for s in ss[1:]:
                m = jnp.maximum(m, s)
            return jnp.max(m, axis=-1, keepdims=True)

        def pv(es, hs):
            acc = None
            for e, sl in zip(es, tiles):
                v = v_ref[0, sl, hs]
                d = _dot(e, jnp.concatenate([v, jnp.ones_like(v)], axis=1))
                acc = d if acc is None else acc + d
            return acc[:, :dv] / acc[:, dv:]

        ks = [[k_ref[0, sl, hs] for sl in tiles] for hs in heads]
        s0 = [[_dot_nt(q0, kk) for kk in kh] for (q0, _), kh in zip(qs, ks)]
        m0 = [row_max(s) for s in s0]
        s1, e0 = [], []
        for (_, q1), kh, sh, mh in zip(qs, ks, s0, m0):
            s1.append([])
            e0.append([])
            for kk, s in zip(kh, sh):
                s1[-1].append(_dot_nt(q1, kk))
                e0[-1].append(jnp.exp2(s - mh).astype(BF16))
        m1 = [row_max(s) for s in s1]
        o0 = [pv(e, hs) for e, hs in zip(e0, heads)]
        e1 = [[jnp.exp2(s - mh).astype(BF16) for s in sh] for sh, mh in zip(s1, m1)]
        for hs, oa, e in zip(heads, o0, e1):
            o = oa - lam * pv(e, hs)
            o = o * lax.rsqrt(jnp.mean(o * o, axis=-1, keepdims=True) + DA_EPS) * g_ref[...]
            o_ref[0, :, hs] = (o * (1.0 - lam_init)).astype(BF16)

    @pl.when(i < ntl)
    def _():
        attend(0, k_ref.shape[1])

    @pl.when(i >= ntl)
    def _():
        attend(t_lat, k_ref.shape[1] - t_lat)


def _diff_attention(q, k, v, lam_p, norm_g, ntl, lam_init, nt):
    B, TT, _ = q.shape
    tm = TOKEN_TILE
    w = ATTN_HEADS_PER_STEP * DA_V_DIM
    kv = pl.BlockSpec((1, TT, w), lambda b, h, i: (b, 0, h))
    qo = pl.BlockSpec((1, tm, w), lambda b, h, i: (b, i, h))
    return pl.pallas_call(
        functools.partial(_attn_kernel, ntl=ntl, t_lat=ntl * tm, lam_init=lam_init),
        grid=(B, DA_HEADS // ATTN_HEADS_PER_STEP, nt),
        in_specs=[qo, kv, kv,
                  pl.BlockSpec(lam_p.shape, lambda b, h, i: (0, 0)),
                  pl.BlockSpec((1, DA_V_DIM), lambda b, h, i: (0, 0))],
        out_specs=qo,
        out_shape=jax.ShapeDtypeStruct((B, TT, DA_WIDTH), BF16),
        compiler_params=pltpu.CompilerParams(
            dimension_semantics=("parallel", "parallel", "parallel"),
            vmem_limit_bytes=VMEM_LIMIT_BYTES),
        name="diff_attn",
    )(q, k, v, lam_p, norm_g.reshape(1, DA_V_DIM))


def _rwkv_feature_math(p, prev_row, next_row, params, outs):
    mu_ref, w0_ref, w2_ref, a0_ref, a2_ref, kk_ref, ka_ref, rk_ref, g2_ref, ones_ref = params
    lw0_ref, lw1_ref, fwd_ref, rev_ref, shared_ref, bonus_ref, g_ref = outs
    tm = p.shape[0]
    row = lax.broadcasted_iota(jnp.int32, p.shape, 0)
    prev = jnp.where(row == 0, prev_row, pltpu.roll(p, 1, 0))
    nxt = jnp.where(row == tm - 1, next_row, pltpu.roll(p, tm - 1, 0))
    ps_all = p + mu_ref[0:1, :] * (prev - p) + mu_ref[1:2, :] * (nxt - p)

    W = RW_WIDTH
    ones_bd = ones_ref[...]
    ka = ka_ref[...]
    for r0 in range(0, tm, FEATURE_ROWS):
        rs = slice(r0, r0 + FEATURE_ROWS)
        ps = ps_all[rs]
        r = ps[:, 0:W]
        k = ps[:, W:2 * W]
        v = ps[:, 2 * W:3 * W]
        xw = ps[:, 3 * W:3 * W + 64]
        xa = ps[:, 3 * W + 64:3 * W + 128]
        xg = ps[:, 3 * W + 128:3 * W + 256]

        w_pre = _dot(jnp.tanh(xw).astype(BF16), w2_ref[...]) + w0_ref[...]
        logw = -math.exp(-0.5) * _sigmoid(w_pre)
        a = _sigmoid(_dot(xa.astype(BF16), a2_ref[...]) + a0_ref[...])

        kx = k * kk_ref[...]
        nrm = jnp.sqrt(_seg_sum(kx * kx, ones_bd))
        kk = kx / jnp.maximum(nrm, 1e-12)
        kd0 = k * (1.0 + (a[:, :W] - 1.0) * ka)
        kd1 = k * (1.0 + (a[:, W:] - 1.0) * ka)
        g = _dot(_sigmoid(xg).astype(BF16), g2_ref[...])
        k_b = 0.5 * (kd0 + kd1)
        bonus = _seg_sum(r * k_b * rk_ref[...], ones_bd) * v

        lw0_ref[0, rs, :] = logw[:, :W]
        lw1_ref[0, rs, :] = logw[:, W:]
        fwd_ref[0, rs, 0:W] = kd0.astype(BF16)
        fwd_ref[0, rs, W:2 * W] = (kk * a[:, :W]).astype(BF16)
        rev_ref[0, rs, 0:W] = kd1.astype(BF16)
        rev_ref[0, rs, W:2 * W] = (kk * a[:, W:]).astype(BF16)
        shared_ref[0, rs, 0:W] = kk.astype(BF16)
        shared_ref[0, rs, W:2 * W] = v.astype(BF16)
        shared_ref[0, rs, 2 * W:3 * W] = r.astype(BF16)
        bonus_ref[0, rs, :] = bonus.astype(BF16)
        g_ref[0, rs, :] = g.astype(BF16)


def _split_bf16(x, pieces):
    out = []
    for _ in range(pieces - 1):
        hi = x.astype(BF16)
        out.append(hi)
        x = x - hi.astype(F32)
    out.append(x.astype(BF16))
    return out


def _block_diag(x, groups):
    xb = x.astype(BF16)
    rows, lanes = xb.shape
    t = jnp.concatenate([xb] * groups, axis=0)
    ri = lax.broadcasted_iota(jnp.int32, t.shape, 0) // rows
    li = lax.broadcasted_iota(jnp.int32, t.shape, 1) // (lanes // groups)
    return jnp.where(ri == li, t, jnp.zeros_like(t))


def _scan_kernel(lw0, dir0, shared0, lw1, dir1, shared1, wg_ref, wu_ref, wd_ref,
                 y0_ref, y1_ref, wgo_ref, wuo_ref, wdo_ref, s_ref):
    step = pl.program_id(1)

    @pl.when(step == 0)
    def _():
        s_ref[...] = jnp.zeros_like(s_ref)

    for src, dst in ((wg_ref, wgo_ref), (wu_ref, wuo_ref), (wd_ref, wdo_ref)):
        dst[...] = src[...].astype(dst.dtype)

    C = lw0.shape[1]
    G = SCAN_HEADS_PER_PASS
    L = G * RW_HEAD
    assert C == RW_HEAD
    ti = lax.broadcasted_iota(jnp.int32, (C, L), 0)
    si = lax.broadcasted_iota(jnp.int32, (C, L), 1) % C
    tc = lax.broadcasted_iota(jnp.int32, (C, C), 0)
    sc = lax.broadcasted_iota(jnp.int32, (C, C), 1)
    same_head = (lax.broadcasted_iota(jnp.int32, (L, L), 0) // RW_HEAD
                 == lax.broadcasted_iota(jnp.int32, (L, L), 1) // RW_HEAD)

    groups = []
    W = RW_WIDTH
    for z, (lw, dirp, shared) in enumerate(((lw0, dir0, shared0), (lw1, dir1, shared1))):
        rev = z == 1
        incl = (si >= ti) if rev else (si <= ti)
        strict = (si > ti) if rev else (si < ti)
        tri = jnp.where((sc >= tc) if rev else (sc <= tc), 1.0, 0.0).astype(BF16)
        for n in range(lw.shape[0]):
            logw = lw[n]
            cum = sum(_dot(tri, piece) for piece in _split_bf16(logw, 3))
            tot = jnp.sum(logw, axis=0, keepdims=True)
            p_in = jnp.exp(-cum)
            p_end = jnp.exp(tot - cum)
            kdz, bz = dirp[n, :, 0:W].astype(F32), dirp[n, :, W:2 * W].astype(F32)
            kkz, rz = shared[n, :, 0:W].astype(F32), shared[n, :, 2 * W:3 * W].astype(F32)
            lhs = jnp.concatenate([-kkz * jnp.exp(cum - logw), rz * jnp.exp(cum)], axis=0)
            rhs_b = bz * p_in
            rhs_k = kdz * p_in
            upd = jnp.concatenate([bz * p_end, kdz * p_end], axis=0)
            dec = jnp.exp(tot)
            vz = shared[n, :, W:2 * W]
            for j in range(RW_HEADS // G):
                c = slice(j * L, (j + 1) * L)
                groups.append(dict(z=z, smp=n, j=j, lhs=lhs[:, c].astype(BF16), rhs_b=rhs_b[:, c],
                                   rhs_k=rhs_k[:, c], upd=upd[:, c].astype(BF16), dec=dec[:, c],
                                   v=vz[:, c], strict=strict, incl=incl))

    for g in groups:
        aa_b = _dot_nt(g["lhs"], _block_diag(g["rhs_b"], G))
        aa_k = _dot_nt(g["lhs"], _block_diag(g["rhs_k"], G))
        g["a_k"] = jnp.concatenate([jnp.where(g["strict"], aa_k[:C], 0.0),
                                    jnp.where(g["incl"], aa_k[C:], 0.0)], axis=0).astype(BF16)
        g["a_rb"] = jnp.where(g["incl"], aa_b[C:], 0.0).astype(BF16)
        g["n"] = jnp.where(g["strict"], aa_b[:C], 0.0)
        g["apow"] = _dot(g["n"].astype(BF16), _block_diag(g["n"], G))
    levels = int(math.log2(C)) - 1
    for lv in range(levels):
        for g in groups:
            pbd = _block_diag(g["apow"], G)
            if lv < levels - 1:
                st = _dot(jnp.concatenate([g["n"], g["apow"]], axis=0).astype(BF16), pbd)
                g["n"] = g["n"] + g["apow"] + st[:C]
                g["apow"] = st[C:]
            else:
                g["n"] = g["n"] + g["apow"] + _dot(g["n"].astype(BF16), pbd)
    for g in groups:
        g["s0"] = s_ref[g["smp"], g["z"], g["j"]]
        g["gh"] = _dot_nt(g["lhs"], g["s0"].astype(BF16))
    for g in groups:
        st = _dot(g["a_k"], _block_diag(g["v"], G))
        g["w"] = g["gh"][:C] + st[:C]
        g["yk"] = st[C:]
    for g in groups:
        g["u"] = g["w"] + _dot(g["n"].astype(BF16), _block_diag(g["w"], G))
    for g in groups:
        g["y"] = g["gh"][C:] + g["yk"] + _dot(g["a_rb"], _block_diag(g["u"], G))
    for g in groups:
        uv = jnp.concatenate([g["u"].astype(BF16), g["v"]], axis=0)
        s_ref[g["smp"], g["z"], g["j"]] = g["s0"] * g["dec"] + jnp.where(same_head, _dot_tn(uv, g["upd"]), 0.0)
    for z, y_ref in enumerate((y0_ref, y1_ref)):
        for n in range(y_ref.shape[0]):
            y_ref[n] = jnp.concatenate([g["y"] for g in groups if g["z"] == z and g["smp"] == n],
                                       axis=-1).astype(y_ref.dtype)


def _pow2_at_least(n):
    return 1 << max(n - 1, 0).bit_length()


def _rwkv_scan(lw0, lw1, dir0, dir1, shared, t_lat, expert_weights, layer):
    B, TT, W = lw0.shape
    C = SCAN_CHUNK
    ncl = t_lat // C
    nch = TT // C
    ncc = nch - ncl

    def fwd(s):
        return jnp.where(s < ncc, ncl + s, s - ncc)

    nb = math.gcd(B, SCAN_BATCH)
    fs = lambda n: pl.BlockSpec((nb, C, n * W), lambda b, s: (b, fwd(s), 0))
    rs = lambda n: pl.BlockSpec((nb, C, n * W), lambda b, s: (b, nch - 1 - s, 0))

    n_steps = (B // nb) * nch
    w_views, w_in_specs, w_out_specs, w_out_shapes, w_shapes = [], [], [], [], []
    for w in expert_weights:
        n_layers, n_exp, d_in, d_out = w.shape
        rows = n_exp * d_in
        slab = _pow2_at_least(-(-rows // n_steps))
        assert rows % slab == 0
        n_slabs = rows // slab
        w_views.append(w.reshape(n_layers * n_slabs, slab, d_out))
        pick = lambda b, s, n=n_slabs: jnp.minimum(b * nch + s, n - 1)
        w_in_specs.append(pl.BlockSpec((1, slab, d_out), lambda b, s, p=pick, n=n_slabs: (layer * n + p(b, s), 0, 0)))
        w_out_specs.append(pl.BlockSpec((1, slab, d_out), lambda b, s, p=pick: (p(b, s), 0, 0)))
        w_out_shapes.append(jax.ShapeDtypeStruct((n_slabs, slab, d_out), BF16))
        w_shapes.append((n_exp, d_in, d_out))
    outs = pl.pallas_call(
        _scan_kernel,
        grid=(B // nb, nch),
        in_specs=[fs(1), fs(2), fs(3), rs(1), rs(2), rs(3)] + w_in_specs,
        out_specs=[fs(1), rs(1)] + w_out_specs,
        out_shape=[jax.ShapeDtypeStruct((B, TT, W), BF16)] * 2 + w_out_shapes,
        scratch_shapes=[pltpu.VMEM((nb, 2, RW_HEADS // SCAN_HEADS_PER_PASS, SCAN_HEADS_PER_PASS * RW_HEAD,
                                    SCAN_HEADS_PER_PASS * RW_HEAD), F32)],
        compiler_params=pltpu.CompilerParams(
            dimension_semantics=("arbitrary", "arbitrary"), vmem_limit_bytes=VMEM_LIMIT_BYTES),
        name="rwkv_scan",
    )(lw0, dir0, shared, lw1, dir1, shared, *w_views)
    return outs[0], outs[1], [o.reshape(shp) for o, shp in zip(outs[2:], w_shapes)]


def _merge_kernel(x_ref, yda_ref, y0_ref, y1_ref, bonus_ref, g_ref, ysg_ref, gate_ref,
                  g1_ref, sh2_ref, sc2_ref, wb_ref, wo_ref, ln1g_ref, ln1b_ref, rlng_ref, rlnb_ref,
                  ones_ref, wr_ref, *rest, alpha):
    xmid_ref, h2_ref, aff_ref = rest[-3:]
    D = D_MODEL
    tm = x_ref.shape[1]
    subs = [slice(r, r + MERGE_ROWS) for r in range(0, tm, MERGE_ROWS)]
    ones_bd = ones_ref[...]
    wr_hi, wr_lo = _split_bf16(wr_ref[...], 2)
    wr_hl = jnp.concatenate([wr_hi, wr_lo], axis=0)

    yrw = []
    for sl in subs:
        y = y0_ref[0, sl, :].astype(F32) + y1_ref[0, sl, :].astype(F32)
        mu = _seg_sum(y, ones_bd) * (1.0 / RW_HEAD)
        dy = y - mu
        var = _seg_sum(dy * dy, ones_bd) * (1.0 / RW_HEAD)
        gn = dy * lax.rsqrt(var + RW_GN_EPS) * rlng_ref[...] + rlnb_ref[...]
        yrw.append(((gn + bonus_ref[0, sl, :].astype(F32)) * g_ref[0, sl, :].astype(F32)).astype(BF16))
    ms = []
    for sl, yr in zip(subs, yrw):
        m = gate_ref[0, sl, 0:D].astype(F32) * _dot(yda_ref[0, sl, :], wb_ref[0:DA_WIDTH, :])
        m = m + gate_ref[0, sl, D:2 * D].astype(F32) * _dot(yr, wb_ref[DA_WIDTH:DA_WIDTH + RW_WIDTH, :])
        m = m + gate_ref[0, sl, 2 * D:3 * D].astype(F32) * _dot(ysg_ref[0, sl, :], wb_ref[DA_WIDTH + RW_WIDTH:, :])
        ms.append(m.astype(BF16))
    mixes = [_dot(m, wo_ref[...]) for m in ms]
    h2s = []
    for sl, mix in zip(subs, mixes):
        z = alpha * x_ref[0, sl, :] + g1_ref[0, 0] * mix
        zm = jnp.mean(z, axis=-1, keepdims=True)
        dz = z - zm
        zv = jnp.mean(dz * dz, axis=-1, keepdims=True)
        xmid = dz * lax.rsqrt(zv + LN_EPS) * ln1g_ref[...] + ln1b_ref[...]
        xmid_ref[0, sl, :] = xmid
        h2 = xmid * (1.0 + sc2_ref[0, 0]) + sh2_ref[0, 0]
        h2_ref[0, sl, :] = h2.astype(BF16)
        h2s.append(h2)
    ne = wr_hi.shape[0]
    for sl, h2 in zip(subs, h2s):
        h_hi, h_lo = _split_bf16(h2, 2)
        part = _dot_nt(wr_hl, h_hi)
        logits = part[:ne] + part[ne:] + _dot_nt(wr_hi, h_lo)
        e = jnp.exp(logits - jnp.max(logits, axis=0, keepdims=True))
        aff_ref[0, :, sl] = e / jnp.sum(e, axis=0, keepdims=True)


def _merge(x_src, x_block0, tile, row0, n_tiles, mod_row, prev_outs, yda, y0, y1, bonus, g, ysg, gates, modall,
           w_branch, w_out, ln1g, ln1b, rlng, rlnb, ones_bd, w_router_t, alpha):
    B, TT, _ = yda.shape
    D = x_src.shape[2]
    blk0 = row0 // tile
    tok = lambda w: pl.BlockSpec((1, tile, w), lambda b, i: (b, blk0 + i, 0))
    modspec = lambda j: pl.BlockSpec((1, 1, 1, D), lambda b, i: (b, mod_row, 0, j))
    const = lambda a: pl.BlockSpec(a.shape, lambda b, i: (0,) * a.ndim)
    consts = [w_branch, w_out, ln1g, ln1b, rlng, rlnb, ones_bd, w_router_t]
    n_in = 11 + len(consts)
    prev = list(prev_outs) if prev_outs is not None else []
    return pl.pallas_call(
        functools.partial(_merge_kernel, alpha=alpha),
        grid=(B, n_tiles),
        in_specs=[pl.BlockSpec((1, tile, D), lambda b, i: (b, x_block0 + i, 0)),
                  tok(DA_WIDTH), tok(RW_WIDTH), tok(RW_WIDTH), tok(RW_WIDTH), tok(RW_WIDTH),
                  tok(SG_WIDTH), tok(3 * D), modspec(2), modspec(3), modspec(4)]
                 + [const(a) for a in consts] + [pl.BlockSpec(memory_space=pl.ANY)] * len(prev),
        out_specs=[tok(D), tok(D), pl.BlockSpec((1, N_EXPERTS, tile), lambda b, i: (b, 0, blk0 + i))],
        out_shape=[jax.ShapeDtypeStruct((B, TT, D), F32),
                   jax.ShapeDtypeStruct((B, TT, D), BF16),
                   jax.ShapeDtypeStruct((B, N_EXPERTS, TT), F32)],
        input_output_aliases={n_in + j: j for j in range(len(prev))},
        compiler_params=pltpu.CompilerParams(
            dimension_semantics=("parallel", "parallel"), vmem_limit_bytes=VMEM_LIMIT_BYTES),
        name="merge_ln1_router",
    )(x_src, yda, y0, y1, bonus, g, ysg, gates, modall, modall, modall, *consts, *prev)


def _topk_kernel(aff_ref, tri_ref, rank_ref, *, cap):
    a = aff_ref[...].reshape(-1, aff_ref.shape[2])
    bits = pltpu.bitcast(a, jnp.int32)
    thr = jnp.zeros((a.shape[0], 1), jnp.int32)
    for bit in range(30, -1, -1):
        cand = thr | (1 << bit)
        cnt = jnp.sum(jnp.where(bits >= cand, 1.0, 0.0), axis=-1, keepdims=True)
        thr = jnp.where(cnt >= cap, cand, thr)
    gt = bits > thr
    eq = bits == thr
    need = cap - jnp.sum(jnp.where(gt, 1.0, 0.0), axis=-1, keepdims=True)
    tri = tri_ref[...]
    kb = tri.shape[0]

    def excl_cumsum(mask):
        m = jnp.where(mask, 1.0, 0.0)
        outs = []
        carry = jnp.zeros((m.shape[0], 1), F32)
        for j in range(m.shape[1] // kb):
            blk = m[:, j * kb:(j + 1) * kb]
            outs.append(_dot(blk.astype(BF16), tri) + carry)
            carry = carry + jnp.sum(blk, axis=-1, keepdims=True)
        return jnp.concatenate(outs, axis=1)

    sel = jnp.logical_or(gt, jnp.logical_and(eq, excl_cumsum(eq) < need))
    rank_ref[...] = jnp.where(sel, excl_cumsum(sel), -1.0).astype(jnp.int32).reshape(rank_ref.shape)


def _topk_ranks(aff, tri, t_off, t_len, cap):
    B, E, _ = aff.shape
    blk = t_off // t_len
    nb = math.gcd(B, RANK_BATCH)
    return pl.pallas_call(
        functools.partial(_topk_kernel, cap=cap),
        grid=(B // nb,),
        in_specs=[pl.BlockSpec((nb, E, t_len), lambda b: (b, 0, blk)),
                  pl.BlockSpec(tri.shape, lambda b: (0, 0))],
        out_specs=pl.BlockSpec((nb, E, t_len), lambda b: (b, 0, 0)),
        out_shape=jax.ShapeDtypeStruct((B, E, t_len), jnp.int32),
        compiler_params=pltpu.CompilerParams(
            dimension_semantics=("parallel",), vmem_limit_bytes=VMEM_LIMIT_BYTES),
        name="expert_choice_ranks",
    )(aff, tri)


def _moe_kernel(*refs, sets):
    n = len(sets)
    h_ref, aff_ref = refs[0], refs[1]
    rank_refs = refs[2:2 + n]
    wg_ref, wu_ref, wd_ref, f_ref, acc_ref = refs[2 + n:]
    e = pl.program_id(1)

    @pl.when(e == 0)
    def _():
        acc_ref[...] = jnp.zeros_like(acc_ref)

    onehots, gates, xs = [], [], []
    for (t0, tn, cap), rank_ref in zip(sets, rank_refs):
        rank = rank_ref[0, pl.ds(e, 1), :]
        aff = aff_ref[0, pl.ds(e, 1), t0:t0 + tn]
        slot = lax.broadcasted_iota(jnp.int32, (cap, tn), 0)
        hit = rank == slot
        onehot = jnp.where(hit, 1.0, 0.0).astype(BF16)
        onehots.append(onehot)
        gates.append(jnp.sum(jnp.where(hit, aff, 0.0), axis=-1, keepdims=True))
        xs.append(_dot(onehot, h_ref[0, t0:t0 + tn, :]).astype(BF16))
    xe = jnp.concatenate(xs, axis=0) if n > 1 else xs[0]
    gate = jnp.concatenate(gates, axis=0) if n > 1 else gates[0]
    hg = _dot(xe, wg_ref[0])
    hid = (hg * _sigmoid(hg)) * _dot(xe, wu_ref[0])
    ye = (_dot(hid.astype(BF16), wd_ref[0]) * gate).astype(BF16)
    r0 = 0
    for (t0, tn, cap), onehot in zip(sets, onehots):
        acc_ref[t0:t0 + tn, :] += _dot_tn(onehot, ye[r0:r0 + cap])
        r0 += cap

    @pl.when(e == pl.num_programs(1) - 1)
    def _():
        f_ref[0] = acc_ref[...].astype(f_ref.dtype)


def _expert_ffn(h2, aff, ranks, wg, wu, wd, sets):
    B, TT, D = h2.shape
    E = aff.shape[1]
    F = wg.shape[2]
    return pl.pallas_call(
        functools.partial(_moe_kernel, sets=sets),
        grid=(B, E),
        in_specs=[pl.BlockSpec((1, TT, D), lambda b, e: (b, 0, 0), pipeline_mode=pl.Buffered(1)),
                  pl.BlockSpec((1, E, TT), lambda b, e: (b, 0, 0))]
                 + [pl.BlockSpec((1, E, r.shape[2]), lambda b, e: (b, 0, 0)) for r in ranks]
                 + [pl.BlockSpec((1, D, F), lambda b, e: (e, 0, 0)),
                    pl.BlockSpec((1, D, F), lambda b, e: (e, 0, 0)),
                    pl.BlockSpec((1, F, D), lambda b, e: (e, 0, 0))],
        out_specs=pl.BlockSpec((1, TT, D), lambda b, e: (b, 0, 0)),
        out_shape=jax.ShapeDtypeStruct((B, TT, D), BF16),
        scratch_shapes=[pltpu.VMEM((TT, D), F32)],
        compiler_params=pltpu.CompilerParams(
            dimension_semantics=("parallel", "arbitrary"), vmem_limit_bytes=VMEM_LIMIT_BYTES),
        name="expert_ffn",
    )(h2, aff, *ranks, wg, wu, wd)


def _ln2_kernel(x_ref, f_ref, g2_ref, lng_ref, lnb_ref, o_ref, *, alpha, t_lat):
    tile = x_ref.shape[1]
    row0 = pl.program_id(1) * tile
    for r in range(0, tile, TOKEN_TILE):
        sl = slice(r, r + TOKEN_TILE)
        g2 = jnp.where(row0 + r < t_lat, g2_ref[0, 0], g2_ref[0, 1])
        z = alpha * x_ref[0, sl, :] + g2 * f_ref[0, sl, :].astype(F32)
        zm = jnp.mean(z, axis=-1, keepdims=True)
        dz = z - zm
        zv = jnp.mean(dz * dz, axis=-1, keepdims=True)
        o_ref[0, sl, :] = dz * lax.rsqrt(zv + LN_EPS) * lng_ref[...] + lnb_ref[...]


def _final_norm(xmid, f, modall, lng, lnb, t_lat, alpha, rows):
    B, _, D = xmid.shape
    tile = next(t for t in LN2_TILES if rows % t == 0)
    tok = pl.BlockSpec((1, tile, D), lambda b, i: (b, i, 0))
    const = lambda a: pl.BlockSpec(a.shape, lambda b, i: (0,) * a.ndim)
    return pl.pallas_call(
        functools.partial(_ln2_kernel, alpha=alpha, t_lat=t_lat),
        grid=(B, rows // tile),
        in_specs=[tok, tok, pl.BlockSpec((1, 2, 1, D), lambda b, i: (b, 0, 0, 5)),
                  const(lng), const(lnb)],
        out_specs=tok,
        out_shape=jax.ShapeDtypeStruct((B, rows, D), F32),
        compiler_params=pltpu.CompilerParams(
            dimension_semantics=("parallel", "parallel"), vmem_limit_bytes=VMEM_LIMIT_BYTES),
        name="ln2",
    )(xmid, f, modall, lng, lnb)


def _rope_tables(t_lat, t_ctx):
    rows = t_lat // GRID_W
    row = jnp.repeat(jnp.arange(rows, dtype=F32), GRID_W)
    col = jnp.tile(jnp.arange(GRID_W, dtype=F32), rows)
    half = DA_QK_DIM // 2
    inv_freq = ROPE_BASE ** (-jnp.arange(0, half, 2, dtype=F32) / half)
    ar = row[:, None] * inv_freq
    ac = col[:, None] * inv_freq
    ang = jnp.concatenate([ar, ar, ac, ac], axis=-1)
    sign = jnp.where((jnp.arange(DA_QK_DIM) % ROPE_AXIS_DIM) < ROPE_HALF, -1.0, 1.0).astype(F32)
    reps = DA_QK_COLS // DA_QK_DIM
    cos = jnp.tile(jnp.cos(ang), (1, reps))
    sin = jnp.tile(jnp.sin(ang) * sign, (1, reps))
    cos = jnp.concatenate([cos, jnp.ones((t_ctx, DA_QK_COLS), F32)], axis=0)
    sin = jnp.concatenate([sin, jnp.zeros((t_ctx, DA_QK_COLS), F32)], axis=0)
    return cos, sin


def kernel(x, c, ctx, c_ctx, w_mod, b_mod, w_in, da_lambda, da_norm_g, rw_shift_mu, rw_w0, rw_w2, rw_a0, rw_a2, rw_k_k, rw_k_a, rw_r_k, rw_ln_g, rw_ln_b, rw_g2, sg_norm_g, sg_norm_b, sg_w, sg_b, w_branch, w_out, ln1_g, ln1_b, w_router, w_e_gate, w_e_up, w_e_down, ln2_g, ln2_b):
    B, T, D = x.shape
    Tc = ctx.shape[1]
    depth = w_mod.shape[0]
    tm = TOKEN_TILE
    assert D == D_MODEL and T % tm == 0 and Tc == tm and T % MERGE_TILE == 0
    ntl = T // tm
    alpha = (2 * depth) ** 0.25
    cap_lat = EC_CAPACITY * T // N_EXPERTS
    cap_ctx = EC_CAPACITY * Tc // N_EXPERTS

    cos, sin = _rope_tables(T, Tc)
    lane = jnp.arange(SCAN_HEADS_PER_PASS * RW_HEAD)
    ones_bd = (lane[:, None] // RW_HEAD == lane[None, :] // RW_HEAD).astype(BF16)
    kb = math.gcd(Tc, RANK_BLOCK)
    tri = (jnp.arange(kb)[:, None] < jnp.arange(kb)[None, :]).astype(BF16)
    rows = ((B + 1 + 7) // 8) * 8
    cc = jnp.concatenate([c, c_ctx[None, :], jnp.zeros((rows - B - 1, D), F32)], axis=0)
    row2 = lambda a: a.reshape(1, -1)

    w_in_bf = w_in.astype(BF16)
    x_lat, x_ctx, ctx_block0 = x, ctx, 0
    for l in range(depth):
        last = l == depth - 1
        lam_init = 0.8 - 0.6 * math.exp(-0.3 * l)
        mod = _modulation(cc, w_mod, b_mod, l)
        modall = jnp.stack([mod[:B], jnp.broadcast_to(mod[B], (B, 6 * D))], axis=1)
        modall = modall.reshape(B, 2, 1, 6 * D)

        sgbias = jnp.repeat(sg_b[l].T, SG_WIDTH // SG_GROUPS, axis=1)
        cat2 = lambda a: jnp.transpose(a, (1, 0, 2)).reshape(a.shape[1], 2 * RW_WIDTH)
        rw_params = [rw_shift_mu[l], row2(rw_w0[l]), cat2(rw_w2[l]).astype(BF16), row2(rw_a0[l]),
                     cat2(rw_a2[l]).astype(BF16), row2(rw_k_k[l]), row2(rw_k_a[l]), row2(rw_r_k[l]),
                     rw_g2[l].astype(BF16), ones_bd]
        (q, k, v, ysg, gates, lw0, lw1, dir0, dir1, shared, bonus, gg) = _input_projection(
            x_lat, x_ctx, ctx_block0, T + Tc, modall, w_in_bf, l, cos, sin, row2(sg_norm_g[l]),
            row2(sg_norm_b[l]), sg_w[l].astype(BF16), sgbias, rw_params, ntl)

        nt_out = ntl if last else (T + Tc) // tm
        yda = _diff_attention(q, k, v, da_lambda[l], da_norm_g[l], ntl, lam_init, nt_out)

        y0, y1, (wg, wu, wd) = _rwkv_scan(lw0, lw1, dir0, dir1, shared, T,
                                          (w_e_gate, w_e_up, w_e_down), l)

        merge_args = (yda, y0, y1, bonus, gg, ysg, gates, modall, w_branch[l].astype(BF16),
                      w_out[l].astype(BF16), row2(ln1_g[l]), row2(ln1_b[l]), row2(rw_ln_g[l]),
                      row2(rw_ln_b[l]), ones_bd, w_router[l].T, alpha)
        outs = _merge(x_lat, 0, MERGE_TILE, 0, T // MERGE_TILE, 0, None, *merge_args)
        if not last:
            outs = _merge(x_ctx, ctx_block0, tm, T, Tc // tm, 1, outs, *merge_args)
        xmid, h2, aff = outs

        sets = ((0, T, cap_lat),) if last else ((0, T, cap_lat), (T, Tc, cap_ctx))
        ranks = [_topk_ranks(aff, tri, 0, T, cap_lat)]
        if not last:
            ranks.append(_topk_ranks(aff, tri, T, Tc, cap_ctx))
        f = _expert_ffn(h2, aff, ranks, wg, wu, wd, sets)
        x_lat = _final_norm(xmid, f, modall, row2(ln2_g[l]), row2(ln2_b[l]), T, alpha,
                            T if last else T + Tc)
        x_ctx, ctx_block0 = x_lat, ntl
    return x_lat
```
